```python
import math
import jax, jax.numpy as jnp
from jax import lax
import numpy as np

D_MODEL = 1024
BATCH = 2
SEQ = 8192
DEPTH = 1

GDN_HEADS = 4
GDN_HEAD_DIM = 128
GDN_WIDTH = GDN_HEADS * GDN_HEAD_DIM
GDN_CHUNK = 64
CONV_WIDTH = 4
SGU_GROUPS = 4
SGU_GROUP_DIM = 128
SGU_WIDTH = SGU_GROUPS * SGU_GROUP_DIM
SGU_CHUNK = 128
N_MEM = 256
XATTN_HEADS = 4
XATTN_HEAD_DIM = 128
XATTN_WIDTH = XATTN_HEADS * XATTN_HEAD_DIM
N_BRANCHES = 3
SPLITS = [3 * GDN_WIDTH,
          4 * GDN_WIDTH,
          4 * GDN_WIDTH + 2 * GDN_HEADS,
          4 * GDN_WIDTH + 2 * GDN_HEADS + 2 * SGU_WIDTH,
          4 * GDN_WIDTH + 2 * GDN_HEADS + 2 * SGU_WIDTH + XATTN_WIDTH]
D_IN_PROJ = SPLITS[-1] + N_BRANCHES * D_MODEL
N_EXPERTS = 256
TOP_K = 8
N_GROUPS = 8
TOPK_GROUPS = 4
D_EXPERT = 256
D_SHARED = 256
ROUTED_SCALE = 2.5
MOE_BLOCK = 128
LN_EPS = 1e-5
RMS_EPS = 1e-6
DEEPNORM_ALPHA = (2.0 * DEPTH) ** 0.25
DEEPNORM_BETA = (8.0 * DEPTH) ** -0.25

kernel_name = "hybrid_gdn_sgu_memxattn_moe_deepnorm"


def _layer_norm(x, g, b):
    xf = x.astype(jnp.float32)
    mu = xf.mean(-1, keepdims=True)
    var = jnp.square(xf - mu).mean(-1, keepdims=True)
    return ((xf - mu) * lax.rsqrt(var + LN_EPS) * g.astype(jnp.float32) + b.astype(jnp.float32)).astype(x.dtype)


def _causal_conv_silu(x, w):
    c = x.shape[-1]
    y = lax.conv_general_dilated(x, w[:, None, :].astype(x.dtype), window_strides=(1,),
                                 padding=[(CONV_WIDTH - 1, 0)],
                                 dimension_numbers=('NWC', 'WIO', 'NWC'), feature_group_count=c)
    return jax.nn.silu(y)


def _gated_delta_net(q, k, v, a, b, a_log, dt_bias):
    f32 = jnp.float32
    bsz, s, h, dk = q.shape
    dv = v.shape[-1]
    q, k, v = q.astype(f32), k.astype(f32), v.astype(f32)
    q = q * lax.rsqrt(jnp.sum(q * q, -1, keepdims=True) + RMS_EPS) * (dk ** -0.5)
    k = k * lax.rsqrt(jnp.sum(k * k, -1, keepdims=True) + RMS_EPS)
    beta = jax.nn.sigmoid(b.astype(f32))
    g = -jnp.exp(a_log.astype(f32)) * jax.nn.softplus(a.astype(f32) + dt_bias.astype(f32))
    n, c = s // GDN_CHUNK, GDN_CHUNK

    def chunk(t):
        return t.reshape(bsz, n, c, h, -1).transpose(0, 1, 3, 2, 4)

    q, k, v = chunk(q), chunk(k), chunk(v)
    beta = beta.reshape(bsz, n, c, h).transpose(0, 1, 3, 2)
    gc = jnp.cumsum(g.reshape(bsz, n, c, h).transpose(0, 1, 3, 2), axis=-1)
    causal = jnp.tril(jnp.ones((c, c), bool))
    strict = jnp.tril(jnp.ones((c, c), bool), -1)
    decay = jnp.exp(jnp.where(causal, gc[..., :, None] - gc[..., None, :], -jnp.inf))
    kb = k * beta[..., None]
    lmat = jnp.where(strict, jnp.einsum('bnhid,bnhjd->bnhij', kb, k) * decay, 0.0)
    eye = jnp.eye(c, dtype=f32)
    rhs = jnp.concatenate([v * beta[..., None], kb * jnp.exp(gc)[..., None]], axis=-1)
    sol = lax.linalg.triangular_solve(eye + lmat, rhs, left_side=True, lower=True, unit_diagonal=True)
    u, w = sol[..., :dv], sol[..., dv:]
    attn = jnp.einsum('bnhid,bnhjd->bnhij', q, k) * decay
    q_dec = q * jnp.exp(gc)[..., None]
    k_dec = k * jnp.exp(gc[..., -1:] - gc)[..., None]
    g_last = jnp.exp(gc[..., -1])

    def step(state, xs):
        q_c, k_c, u_c, w_c, a_c, gl = xs
        v_new = u_c - jnp.einsum('bhcd,bhde->bhce', w_c, state)
        o = jnp.einsum('bhcd,bhde->bhce', q_c, state) + jnp.einsum('bhij,bhje->bhie', a_c, v_new)
        state = state * gl[..., None, None] + jnp.einsum('bhcd,bhce->bhde', k_c, v_new)
        return state, o

    xs = tuple(jnp.moveaxis(t, 1, 0) for t in (q_dec, k_dec, u, w, attn, g_last))
    _, o = lax.scan(step, jnp.zeros((bsz, h, dk, dv), f32), xs)
    return o.transpose(1, 0, 3, 2, 4).reshape(bsz, s, h, dv)


def _spatial_gating(uv, ln_g, ln_b, w_s, b_s):
    uv = jax.nn.gelu(uv)
    u, v = jnp.split(uv, 2, axis=-1)
    v = _layer_norm(v, ln_g, ln_b)
    bsz, s, _ = v.shape
    v = v.reshape(bsz, s // SGU_CHUNK, SGU_CHUNK, SGU_GROUPS, SGU_GROUP_DIM)
    w = jnp.where(jnp.tril(jnp.ones((SGU_CHUNK, SGU_CHUNK), bool)), w_s, 0.0).astype(v.dtype)
    sg = jnp.einsum('gij,bnjgc->bnigc', w, v) + b_s.T[:, :, None].astype(v.dtype)
    return u * sg.reshape(bsz, s, SGU_WIDTH)


def _memory_attention(xq, mem, w_mem_kv):
    bsz, s, _ = xq.shape
    q = xq.reshape(bsz, s, XATTN_HEADS, XATTN_HEAD_DIM)
    k, v = jnp.split(mem @ w_mem_kv, 2, axis=-1)
    k = k.reshape(bsz, -1, XATTN_HEADS, XATTN_HEAD_DIM)
    v = v.reshape(bsz, -1, XATTN_HEADS, XATTN_HEAD_DIM)
    sc = jnp.einsum('bshd,bmhd->bhsm', q, k).astype(jnp.float32) * (XATTN_HEAD_DIM ** -0.5)
    p = jax.nn.softmax(sc, axis=-1).astype(v.dtype)
    return jnp.einsum('bhsm,bmhd->bshd', p, v).reshape(bsz, s, XATTN_WIDTH)


def _route(h, router_w, router_bias):
    t = h.shape[0]
    scores = jax.nn.sigmoid((h @ router_w).astype(jnp.float32))
    biased = scores + router_bias.astype(jnp.float32)
    grp = biased.reshape(t, N_GROUPS, N_EXPERTS // N_GROUPS)
    grp_score = lax.top_k(grp, 2)[0].sum(-1)
    _, grp_idx = lax.top_k(grp_score, TOPK_GROUPS)
    grp_mask = jnp.any(grp_idx[:, :, None] == jnp.arange(N_GROUPS)[None, None, :], axis=1)
    expert_mask = jnp.repeat(grp_mask, N_EXPERTS // N_GROUPS, axis=1)
    _, idx = lax.top_k(jnp.where(expert_mask, biased, -jnp.inf), TOP_K)
    wts = jnp.take_along_axis(scores, idx, axis=-1)
    wts = wts / wts.sum(-1, keepdims=True) * ROUTED_SCALE
    return idx, wts


def _routed_experts(h, idx, wts, w_gate, w_up, w_down):
    t, d = h.shape
    n_assign = t * TOP_K
    flat_e = idx.reshape(-1)
    flat_w = wts.reshape(-1)
    order = jnp.argsort(flat_e)
    sorted_e = flat_e[order]
    counts = jnp.bincount(flat_e, length=N_EXPERTS)
    padded = (counts + MOE_BLOCK - 1) // MOE_BLOCK * MOE_BLOCK
    pad_end = jnp.cumsum(padded)
    pad_start = pad_end - padded
    start = jnp.cumsum(counts) - counts
    dest = pad_start[sorted_e] + jnp.arange(n_assign, dtype=jnp.int32) - start[sorted_e]
    n_blocks = (n_assign + N_EXPERTS * (MOE_BLOCK - 1) + MOE_BLOCK - 1) // MOE_BLOCK
    n_slots = n_blocks * MOE_BLOCK
    slot_tok = jnp.full((n_slots,), t, jnp.int32).at[dest].set((order // TOP_K).astype(jnp.int32))
    slot_w = jnp.zeros((n_slots,), h.dtype).at[dest].set(flat_w[order].astype(h.dtype))
    block_starts = jnp.arange(n_blocks, dtype=jnp.int32) * MOE_BLOCK
    block_e = jnp.minimum(jnp.searchsorted(pad_end, block_starts, side='right'), N_EXPERTS - 1)
    h_pad = jnp.concatenate([h, jnp.zeros((1, d), h.dtype)], axis=0)

    def step(acc, xs):
        tok, wb, e = xs
        xb = h_pad[tok]
        act = jax.nn.silu(xb @ w_gate[e]) * (xb @ w_up[e])
        yb = ((act @ w_down[e]) * wb[:, None]).astype(acc.dtype)
        return acc.at[tok].add(yb), None

    acc, _ = lax.scan(step, jnp.zeros((t + 1, d), h.dtype),
                      (slot_tok.reshape(n_blocks, MOE_BLOCK), slot_w.reshape(n_blocks, MOE_BLOCK), block_e))
    return acc[:t]


def setup_inputs(seed: int = 0) -> dict:
    key = jax.random.key(seed)
    ks = iter(jax.random.split(key, 40))
    f32 = jnp.float32
    L = DEPTH

    def nrm(shape, scale):
        return jax.random.normal(next(ks), shape, f32) * scale

    x = nrm((BATCH, SEQ, D_MODEL), 1.0)
    mem = nrm((BATCH, N_MEM, D_MODEL), 1.0)
    w_in = nrm((L, D_MODEL, D_IN_PROJ), D_MODEL ** -0.5)
    conv_w = nrm((L, CONV_WIDTH, 3 * GDN_WIDTH), 0.5)
    a_log = jnp.log(jax.random.uniform(next(ks), (L, GDN_HEADS), f32, 1.0, 16.0))
    dt = jnp.exp(jax.random.uniform(next(ks), (L, GDN_HEADS), f32) * (math.log(0.1) - math.log(0.001)) + math.log(0.001))
    dt_bias = dt + jnp.log(-jnp.expm1(-dt))
    gdn_norm_g = 1.0 + nrm((L, GDN_HEAD_DIM), 0.02)
    w_a = nrm((L, GDN_WIDTH, D_MODEL), GDN_WIDTH ** -0.5)
    sgu_ln_g = 1.0 + nrm((L, SGU_WIDTH), 0.02)
    sgu_ln_b = nrm((L, SGU_WIDTH), 0.02)
    sgu_w = nrm((L, SGU_GROUPS, SGU_CHUNK, SGU_CHUNK), SGU_CHUNK ** -0.5)
    sgu_b = 1.0 + nrm((L, SGU_GROUPS, SGU_CHUNK), 0.1)
    w_b = nrm((L, SGU_WIDTH, D_MODEL), SGU_WIDTH ** -0.5)
    w_mem_kv = nrm((L, D_MODEL, 2 * XATTN_WIDTH), D_MODEL ** -0.5)
    w_c = nrm((L, XATTN_WIDTH, D_MODEL), XATTN_WIDTH ** -0.5)
    w_o = nrm((L, D_MODEL, D_MODEL), D_MODEL ** -0.5 * DEEPNORM_BETA)
    ln1_g = 1.0 + nrm((L, D_MODEL), 0.02)
    ln1_b = nrm((L, D_MODEL), 0.02)
    router_w = nrm((L, D_MODEL, N_EXPERTS), D_MODEL ** -0.5)
    router_bias = nrm((L, N_EXPERTS), 0.01)
    w_gate = nrm((L, N_EXPERTS, D_MODEL, D_EXPERT), D_MODEL ** -0.5)
    w_up = nrm((L, N_EXPERTS, D_MODEL, D_EXPERT), D_MODEL ** -0.5)
    w_down = nrm((L, N_EXPERTS, D_EXPERT, D_MODEL), D_EXPERT ** -0.5 * DEEPNORM_BETA)
    ws_gate = nrm((L, D_MODEL, D_SHARED), D_MODEL ** -0.5)
    ws_up = nrm((L, D_MODEL, D_SHARED), D_MODEL ** -0.5)
    ws_down = nrm((L, D_SHARED, D_MODEL), D_SHARED ** -0.5 * DEEPNORM_BETA)
    ln2_g = 1.0 + nrm((L, D_MODEL), 0.02)
    ln2_b = nrm((L, D_MODEL), 0.02)
    return {"x": x, "mem": mem, "w_in": w_in, "conv_w": conv_w, "a_log": a_log, "dt_bias": dt_bias,
            "gdn_norm_g": gdn_norm_g, "w_a": w_a, "sgu_ln_g": sgu_ln_g, "sgu_ln_b": sgu_ln_b,
            "sgu_w": sgu_w, "sgu_b": sgu_b, "w_b": w_b, "w_mem_kv": w_mem_kv, "w_c": w_c, "w_o": w_o,
            "ln1_g": ln1_g, "ln1_b": ln1_b, "router_w": router_w, "router_bias": router_bias,
            "w_gate": w_gate, "w_up": w_up, "w_down": w_down, "ws_gate": ws_gate, "ws_up": ws_up,
            "ws_down": ws_down, "ln2_g": ln2_g, "ln2_b": ln2_b}


def reference(x, mem, w_in, conv_w, a_log, dt_bias, gdn_norm_g, w_a, sgu_ln_g, sgu_ln_b, sgu_w, sgu_b, w_b,
              w_mem_kv, w_c, w_o, ln1_g, ln1_b, router_w, router_bias, w_gate, w_up, w_down, ws_gate, ws_up,
              ws_down, ln2_g, ln2_b):
    bsz, s, d = x.shape
    for l in range(DEPTH):
        proj = x @ w_in[l]
        qkv, z, ab, uv, xq, gate_logits = jnp.split(proj, SPLITS, axis=-1)
        qkv = _causal_conv_silu(qkv, conv_w[l])
        q, k, v = (t.reshape(bsz, s, GDN_HEADS, GDN_HEAD_DIM) for t in jnp.split(qkv, 3, axis=-1))
        o = _gated_delta_net(q, k, v, ab[..., :GDN_HEADS], ab[..., GDN_HEADS:], a_log[l], dt_bias[l])
        zf = z.reshape(bsz, s, GDN_HEADS, GDN_HEAD_DIM).astype(jnp.float32)
        o = o * lax.rsqrt(jnp.mean(o * o, -1, keepdims=True) + RMS_EPS) * gdn_norm_g[l].astype(jnp.float32) * jax.nn.silu(zf)
        br_a = o.astype(x.dtype).reshape(bsz, s, GDN_WIDTH) @ w_a[l]
        br_b = _spatial_gating(uv, sgu_ln_g[l], sgu_ln_b[l], sgu_w[l], sgu_b[l]) @ w_b[l]
        br_c = _memory_attention(xq, mem, w_mem_kv[l]) @ w_c[l]
        gates = jax.nn.sigmoid(gate_logits).reshape(bsz, s, N_BRANCHES, d)
        merged = gates[:, :, 0] * br_a + gates[:, :, 1] * br_b + gates[:, :, 2] * br_c
        x = _layer_norm(DEEPNORM_ALPHA * x + merged @ w_o[l], ln1_g[l], ln1_b[l])
        h = x.reshape(bsz * s, d)
        idx, wts = _route(h, router_w[l], router_bias[l])
        routed = _routed_experts(h, idx, wts, w_gate[l], w_up[l], w_down[l])
        shared = (jax.nn.silu(h @ ws_gate[l]) * (h @ ws_up[l])) @ ws_down[l]
        moe = (routed + shared).reshape(bsz, s, d)
        x = _layer_norm(DEEPNORM_ALPHA * x + moe, ln2_g[l], ln2_b[l])
    return x
```

```python
import functools

import jax
import jax.numpy as jnp
from jax import lax
from jax.experimental import pallas as pl
from jax.experimental.pallas import tpu as pltpu

F32 = jnp.float32
BF16 = jnp.bfloat16
I32 = jnp.int32
U32 = jnp.uint32

D_MODEL = 1024
DEPTH = 1
GDN_HEADS = 4
HEAD_DIM = 128
GDN_WIDTH = GDN_HEADS * HEAD_DIM
CONV_WIDTH = 4
SGU_GROUPS = 4
SGU_WIDTH = 512
SGU_CHUNK = 128
N_MEM = 256
XATTN_HEADS = 4
XATTN_WIDTH = 512
N_EXPERTS = 256
TOP_K = 8
N_GROUPS = 8
GROUP_SIZE = N_EXPERTS // N_GROUPS
TOPK_GROUPS = 4
D_EXPERT = 256
D_SHARED = 256
ROUTED_SCALE = 2.5
LN_EPS = 1e-5
RMS_EPS = 1e-6
DEEPNORM_ALPHA = (2.0 * DEPTH) ** 0.25

_C_QKV = 0
_C_Z = 3 * GDN_WIDTH
_C_AB = 4 * GDN_WIDTH
_C_UV = _C_AB + 2 * GDN_HEADS
_C_XQ = _C_UV + 2 * SGU_WIDTH
_C_GATE = _C_XQ + XATTN_WIDTH

CHUNK = 128
TILE = 256
HALO = 8
EXPERT_BLOCK = 256
COMBINE_TILE = 128
VMEM_LIMIT = 56 * 1024 * 1024


def _dot(a, b):
    return jnp.dot(a, b, preferred_element_type=F32)


def _dot_nt(a, b):
    return lax.dot_general(a, b, (((1,), (1,)), ((), ())), preferred_element_type=F32)


def _split(a):
    hi = a.astype(BF16)
    lo = (a - hi.astype(F32)).astype(BF16)
    return hi, lo


def _dot3(a, b):
    ah, al = _split(a)
    bh, bl = _split(b)
    return _dot(ah, bh) + (_dot(ah, bl) + _dot(al, bh))


def _softplus(x):
    return jnp.maximum(x, 0.0) + jnp.log1p(jnp.exp(-jnp.abs(x)))


def _layer_norm(x, g, b):
    mu = jnp.mean(x, axis=-1, keepdims=True)
    xc = x - mu
    var = jnp.mean(xc * xc, axis=-1, keepdims=True)
    return xc * lax.rsqrt(var + LN_EPS) * g + b


def _const_spec(shape):
    nd = len(shape)
    return pl.BlockSpec(shape, lambda *_: (0,) * nd)


def _params(semantics):
    return pltpu.CompilerParams(dimension_semantics=semantics, vmem_limit_bytes=VMEM_LIMIT)


def _mem_kv_kernel(mem_ref, w_ref, k_ref, v_ref):
    kv = _dot(mem_ref[0].astype(BF16), w_ref[...])
    k_ref[0] = kv[:, :XATTN_WIDTH].astype(BF16)
    v_ref[0] = kv[:, XATTN_WIDTH:].astype(BF16)


def _mem_kv(mem, w_kv):
    bsz, n_mem, d = mem.shape
    out = jax.ShapeDtypeStruct((bsz, n_mem, XATTN_WIDTH), BF16)
    return pl.pallas_call(
        _mem_kv_kernel,
        grid=(bsz,),
        in_specs=[pl.BlockSpec((1, n_mem, d), lambda b: (b, 0, 0)), _const_spec(w_kv.shape)],
        out_specs=[pl.BlockSpec((1, n_mem, XATTN_WIDTH), lambda b: (b, 0, 0))] * 2,
        out_shape=[out, out],
        compiler_params=_params(("arbitrary",)),
        name="mem_kv",
    )(mem, w_kv)


def _mix_bc_kernel(x_ref, wuv_ref, wg1_ref, wb_ref, lng_ref, lnb_ref, sw_ref, sb_ref,
                   wxq_ref, k_ref, v_ref, wg2_ref, wc_ref, o_ref):
    xb = x_ref[0].astype(BF16)
    uv = jax.nn.gelu(_dot(xb, wuv_ref[...]), approximate=True)
    u = uv[:, :SGU_WIDTH]
    v = _layer_norm(uv[:, SGU_WIDTH:], lng_ref[...], lnb_ref[...])
    ii = lax.broadcasted_iota(I32, (SGU_CHUNK, SGU_CHUNK), 0)
    jj = lax.broadcasted_iota(I32, (SGU_CHUNK, SGU_CHUNK), 1)
    rows = []
    for n in range(TILE // SGU_CHUNK):
        cols = []
        for g in range(SGU_GROUPS):
            wt = jnp.where(ii >= jj, sw_ref[g], 0.0).astype(BF16)
            vg = v[n * SGU_CHUNK:(n + 1) * SGU_CHUNK, g * 128:(g + 1) * 128].astype(BF16)
            cols.append(_dot(wt, vg))
        rows.append(jnp.concatenate(cols, axis=1) + sb_ref[...])
    sg = jnp.concatenate(rows, axis=0)
    br_b = _dot((u * sg).astype(BF16), wb_ref[...])
    acc = jax.nn.sigmoid(_dot(xb, wg1_ref[...])) * br_b
    xq = _dot(xb, wxq_ref[...])
    heads = []
    for h in range(XATTN_HEADS):
        qh = xq[:, h * 128:(h + 1) * 128].astype(BF16)
        kh = k_ref[0, :, h * 128:(h + 1) * 128]
        vh = v_ref[0, :, h * 128:(h + 1) * 128]
        sc = _dot_nt(qh, kh) * (128 ** -0.5)
        p = jnp.exp(sc - jnp.max(sc, axis=-1, keepdims=True))
        p = p / jnp.sum(p, axis=-1, keepdims=True)
        heads.append(_dot(p.astype(BF16), vh))
    att = jnp.concatenate(heads, axis=1).astype(BF16)
    br_c = _dot(att, wc_ref[...])
    o_ref[0] = acc + jax.nn.sigmoid(_dot(xb, wg2_ref[...])) * br_c


def _mix_bc(x, wts, k, v):
    bsz, s, d = x.shape
    tok = pl.BlockSpec((1, TILE, d), lambda b, i: (b, i, 0))
    kvs = pl.BlockSpec((1, N_MEM, XATTN_WIDTH), lambda b, i: (b, 0, 0))
    names = ("w_uv", "w_g1", "w_b", "sgu_ln_g", "sgu_ln_b", "sgu_w", "sgu_bias", "w_xq")
    consts = [wts[n] for n in names]
    tail = [wts["w_g2"], wts["w_c"]]
    return pl.pallas_call(
        _mix_bc_kernel,
        grid=(bsz, s // TILE),
        in_specs=[tok] + [_const_spec(c.shape) for c in consts] + [kvs, kvs]
        + [_const_spec(c.shape) for c in tail],
        out_specs=tok,
        out_shape=jax.ShapeDtypeStruct((bsz, s, d), F32),
        compiler_params=_params(("arbitrary", "arbitrary")),
        name="mix_bc",
    )(x, *consts, k, v, *tail)


def _unit_lower_inverse(lmat, ii, jj):
    eye = (ii == jj).astype(F32)
    x = ii ^ jj
    t = eye - jnp.where(x < 2, lmat, 0.0)
    s = 2
    while s < CHUNK:
        off = (x < 2 * s) & ((ii & s) != 0) & ((jj & s) == 0)
        loff = jnp.where(off, lmat, 0.0)
        t = t - _dot3(_dot3(t, loff), t)
        s *= 2
    return t


def _gdn_prep_kernel(x_ref, wqkv_ref, wz_ref, wab_ref, convw_ref, prm_ref,
                     u_ref, w_ref, qd_ref, kd_ref, a_ref, gl_ref, z_ref, hist_ref):
    @pl.when(pl.program_id(1) == 0)
    def _():
        hist_ref[0:HALO, :] = jnp.zeros((HALO, 3 * GDN_WIDTH), F32)

    xb = x_ref[0].astype(BF16)
    z_ref[0] = _dot(xb, wz_ref[...])
    qkv = _dot(xb, wqkv_ref[...])
    hist_ref[HALO:HALO + TILE, :] = qkv
    conv = qkv * convw_ref[CONV_WIDTH - 1:CONV_WIDTH, :]
    for j in range(CONV_WIDTH - 1):
        shift = CONV_WIDTH - 1 - j
        conv = conv + hist_ref[HALO - shift:HALO - shift + TILE, :] * convw_ref[j:j + 1, :]
    hist_ref[0:HALO, :] = qkv[TILE - HALO:, :]
    qkv = conv * jax.nn.sigmoid(conv)

    ab = _dot(xb, wab_ref[...])
    g = -jnp.exp(prm_ref[0:1, :]) * _softplus(ab + prm_ref[1:2, :])
    beta = jax.nn.sigmoid(ab)

    ii = lax.broadcasted_iota(I32, (CHUNK, CHUNK), 0)
    jj = lax.broadcasted_iota(I32, (CHUNK, CHUNK), 1)
    tri = (ii >= jj).astype(BF16)
    for c in range(TILE // CHUNK):
        r0 = c * CHUNK
        gch = g[r0:r0 + CHUNK, :]
        g1 = gch.astype(BF16)
        r1 = gch - g1.astype(F32)
        g2 = r1.astype(BF16)
        g3 = (r1 - g2.astype(F32)).astype(BF16)
        gc = _dot(tri, g1) + (_dot(tri, g2) + _dot(tri, g3))
        gct = gc.T
        for h in range(GDN_HEADS):
            lanes = slice(h * HEAD_DIM, (h + 1) * HEAD_DIM)
            qh = qkv[r0:r0 + CHUNK, h * HEAD_DIM:(h + 1) * HEAD_DIM]
            kh = qkv[r0:r0 + CHUNK, GDN_WIDTH + h * HEAD_DIM:GDN_WIDTH + (h + 1) * HEAD_DIM]
            vh = qkv[r0:r0 + CHUNK, 2 * GDN_WIDTH + h * HEAD_DIM:2 * GDN_WIDTH + (h + 1) * HEAD_DIM]
            qn = qh * lax.rsqrt(jnp.sum(qh * qh, axis=-1, keepdims=True) + RMS_EPS) * (HEAD_DIM ** -0.5)
            kn = kh * lax.rsqrt(jnp.sum(kh * kh, axis=-1, keepdims=True) + RMS_EPS)
            gcol = gc[:, h:h + 1]
            grow = gct[h:h + 1, :]
            bcol = beta[r0:r0 + CHUNK, GDN_HEADS + h:GDN_HEADS + h + 1]
            glast = gc[CHUNK - 1:CHUNK, h:h + 1]
            decay = jnp.exp(jnp.where(ii >= jj, gcol - grow, -jnp.inf))
            knb = kn.astype(BF16)
            kk = _dot_nt(knb, knb)
            qk = _dot_nt(qn.astype(BF16), knb)
            lmat = jnp.where(ii > jj, kk * bcol * decay, 0.0)
            tinv = _unit_lower_inverse(lmat, ii, jj)
            egc = jnp.exp(gcol)
            rhs = jnp.concatenate([vh * bcol, kn * (bcol * egc)], axis=1)
            sol = _dot3(tinv, rhs)
            u_ref[0, r0:r0 + CHUNK, lanes] = sol[:, :HEAD_DIM]
            w_ref[0, r0:r0 + CHUNK, lanes] = sol[:, HEAD_DIM:].astype(BF16)
            qd_ref[0, r0:r0 + CHUNK, lanes] = (qn * egc).astype(BF16)
            kd_ref[0, r0:r0 + CHUNK, lanes] = (kn * jnp.exp(glast - gcol)).astype(BF16)
            a_ref[0, r0:r0 + CHUNK, lanes] = (qk * decay).astype(BF16)
            gl_ref[0, c * 8:(c + 1) * 8, lanes] = jnp.broadcast_to(jnp.exp(glast), (8, HEAD_DIM))


def _gdn_prep(x, wts):
    bsz, s, d = x.shape
    tok = pl.BlockSpec((1, TILE, d), lambda b, i: (b, i, 0))
    hw = pl.BlockSpec((1, TILE, GDN_WIDTH), lambda b, i: (b, i, 0))
    gls = pl.BlockSpec((1, 8 * TILE // CHUNK, GDN_WIDTH), lambda b, i: (b, i, 0))
    consts = [wts[n] for n in ("w_qkv", "w_z", "w_ab", "conv_w", "gdn_prm")]
    f32o = jax.ShapeDtypeStruct((bsz, s, GDN_WIDTH), F32)
    b16o = jax.ShapeDtypeStruct((bsz, s, GDN_WIDTH), BF16)
    glo = jax.ShapeDtypeStruct((bsz, 8 * s // CHUNK, GDN_WIDTH), F32)
    return pl.pallas_call(
        _gdn_prep_kernel,
        grid=(bsz, s // TILE),
        in_specs=[tok] + [_const_spec(c.shape) for c in consts],
        out_specs=[hw, hw, hw, hw, hw, gls, hw],
        out_shape=[f32o, b16o, b16o, b16o, b16o, glo, f32o],
        scratch_shapes=[pltpu.VMEM((HALO + TILE, 3 * GDN_WIDTH), F32)],
        compiler_params=_params(("arbitrary", "arbitrary")),
        name="gdn_prep",
    )(x, *consts)


def _gdn_scan_kernel(u_ref, w_ref, qd_ref, kd_ref, a_ref, gl_ref, z_ref, gn_ref, o_ref, state_ref):
    @pl.when(pl.program_id(0) == 0)
    def _():
        state_ref[...] = jnp.zeros(state_ref.shape, F32)

    bsz = u_ref.shape[0]
    for c in range(TILE // CHUNK):
        rows = slice(c * CHUNK, (c + 1) * CHUNK)
        for b in range(bsz):
            for h in range(GDN_HEADS):
                lanes = slice(h * HEAD_DIM, (h + 1) * HEAD_DIM)
                st = state_ref[b, h]
                wq = jnp.concatenate([w_ref[b, rows, lanes], qd_ref[b, rows, lanes]], axis=0)
                ws = _dot(wq, st.astype(BF16))
                v_new = u_ref[b, rows, lanes] - ws[:CHUNK]
                vb = v_new.astype(BF16)
                o = ws[CHUNK:] + _dot(a_ref[b, rows, lanes], vb)
                kdt = kd_ref[b, rows, lanes].astype(F32).T.astype(BF16)
                state_ref[b, h] = st * gl_ref[b, c * 8:c * 8 + 1, lanes] + _dot(kdt, vb)
                zz = z_ref[b, rows, lanes]
                o = o * lax.rsqrt(jnp.mean(o * o, axis=-1, keepdims=True) + RMS_EPS) * gn_ref[...]
                o_ref[b, rows, lanes] = (o * (zz * jax.nn.sigmoid(zz))).astype(BF16)


def _gdn_scan(u, w, qd, kd, a, gl, z, gn):
    bsz, s, _ = u.shape
    hw = pl.BlockSpec((bsz, TILE, GDN_WIDTH), lambda i: (0, i, 0))
    gls = pl.BlockSpec((bsz, 8 * TILE // CHUNK, GDN_WIDTH), lambda i: (0, i, 0))
    return pl.pallas_call(
        _gdn_scan_kernel,
        grid=(s // TILE,),
        in_specs=[hw, hw, hw, hw, hw, gls, hw, _const_spec(gn.shape)],
        out_specs=hw,
        out_shape=jax.ShapeDtypeStruct((bsz, s, GDN_WIDTH), BF16),
        scratch_shapes=[pltpu.VMEM((bsz, GDN_HEADS, HEAD_DIM, HEAD_DIM), F32)],
        compiler_params=_params(("arbitrary",)),
        name="gdn_scan",
    )(u, w, qd, kd, a, gl, z, gn)


def _merge_kernel(x_ref, og_ref, mbc_ref, wa_ref, wg0_ref, wo_ref, ln1g_ref, ln1b_ref,
                  rwt_ref, rb_ref, wsgu_ref, wsd_ref,
                  res_ref, hp_ref, idx_ref, wt_ref, rank_ref, cnt_ref, carry_ref):
    first = (pl.program_id(0) == 0) & (pl.program_id(1) == 0)

    @pl.when(first)
    def _():
        carry_ref[...] = jnp.zeros(carry_ref.shape, F32)

    x = x_ref[0]
    xb = x.astype(BF16)
    br_a = _dot(og_ref[0], wa_ref[...])
    merged = jax.nn.sigmoid(_dot(xb, wg0_ref[...])) * br_a + mbc_ref[0]
    y = _dot(merged.astype(BF16), wo_ref[...])
    h = _layer_norm(DEEPNORM_ALPHA * x + y, ln1g_ref[...], ln1b_ref[...])
    hb = hb16 = h.astype(BF16)

    bits = lax.bitcast_convert_type(hb16.astype(F32), U32)
    half = D_MODEL // 2
    hp_ref[...] = (bits[:, :half] >> 16) | (bits[:, half:] & jnp.uint32(0xFFFF0000))

    gu = _dot(hb, wsgu_ref[...])
    gt = gu[:, :D_SHARED]
    act = gt * jax.nn.sigmoid(gt) * gu[:, D_SHARED:]
    res_ref[0] = DEEPNORM_ALPHA * h + _dot(act.astype(BF16), wsd_ref[...])

    scores = jax.nn.sigmoid(_dot_nt(rwt_ref[...], hb))
    biased = scores + rb_ref[...]
    g3 = biased.reshape(N_GROUPS, GROUP_SIZE, TILE)
    m1 = jnp.max(g3, axis=1)
    m1b = m1[:, None, :]
    n_top = jnp.sum((g3 == m1b).astype(F32), axis=1)
    m2 = jnp.max(jnp.where(g3 < m1b, g3, -jnp.inf), axis=1)
    gs = m1 + jnp.where(n_top >= 2.0, m1, m2)
    gidx = lax.broadcasted_iota(I32, (N_GROUPS, TILE), 0)
    beaten = jnp.zeros((N_GROUPS, TILE), F32)
    for g in range(N_GROUPS):
        row = gs[g:g + 1, :]
        beaten = beaten + ((row > gs) | ((row == gs) & (g < gidx))).astype(F32)
    sel = (beaten < float(TOPK_GROUPS)).astype(F32)
    sel_e = jnp.broadcast_to(sel[:, None, :], (N_GROUPS, GROUP_SIZE, TILE)).reshape(N_EXPERTS, TILE)
    masked = jnp.where(sel_e > 0.5, biased, -jnp.inf)
    eidx = lax.broadcasted_iota(I32, (N_EXPERTS, TILE), 0)
    chosen = jnp.zeros((N_EXPERTS, TILE), F32)
    picks, pick_scores = [], []
    for _k in range(TOP_K):
        mx = jnp.max(masked, axis=0, keepdims=True)
        pick = jnp.min(jnp.where(masked == mx, eidx, N_EXPERTS), axis=0, keepdims=True)
        hot = eidx == pick
        picks.append(pick)
        pick_scores.append(jnp.sum(jnp.where(hot, scores, 0.0), axis=0, keepdims=True))
        chosen = jnp.where(hot, 1.0, chosen)
        masked = jnp.where(hot, -jnp.inf, masked)
    total = pick_scores[0]
    for sck in pick_scores[1:]:
        total = total + sck
    idx_ref[...] = jnp.concatenate(picks, axis=0)
    wt_ref[...] = jnp.concatenate([sck / total * ROUTED_SCALE for sck in pick_scores], axis=0)

    ti = lax.broadcasted_iota(I32, (TILE, TILE), 0)
    tj = lax.broadcasted_iota(I32, (TILE, TILE), 1)
    before = _dot(chosen.astype(BF16), (ti < tj).astype(BF16))
    carry = carry_ref[...]
    before = before + jnp.concatenate([carry] * (TILE // 128), axis=1)
    rank_ref[...] = jnp.concatenate(
        [jnp.sum(jnp.where(eidx == p, before, 0.0), axis=0, keepdims=True) for p in picks],
        axis=0).astype(I32)
    carry = carry + _dot(chosen.astype(BF16), jnp.ones((TILE, 128), BF16))
    carry_ref[...] = carry
    cnt_ref[...] = carry.astype(I32)


def _merge(x, og, mbc, wts):
    bsz, s, d = x.shape
    t = bsz * s
    nt = s // TILE
    tok = pl.BlockSpec((1, TILE, d), lambda b, i: (b, i, 0))
    ogs = pl.BlockSpec((1, TILE, GDN_WIDTH), lambda b, i: (b, i, 0))
    flat = lambda b, i: (b * nt + i, 0)
    lane = lambda b, i: (0, b * nt + i)
    consts = [wts[n] for n in ("w_a", "w_g0", "w_o", "ln1_g", "ln1_b", "router_wt", "router_bias",
                               "ws_gu", "ws_down")]
    return pl.pallas_call(
        _merge_kernel,
        grid=(bsz, nt),
        in_specs=[tok, ogs, tok] + [_const_spec(c.shape) for c in consts],
        out_specs=[tok,
                   pl.BlockSpec((TILE, d // 2), flat),
                   pl.BlockSpec((TOP_K, TILE), lane),
                   pl.BlockSpec((TOP_K, TILE), lane),
                   pl.BlockSpec((TOP_K, TILE), lane),
                   _const_spec((N_EXPERTS, 128))],
        out_shape=[jax.ShapeDtypeStruct((bsz, s, d), F32),
                   jax.ShapeDtypeStruct((t, d // 2), U32),
                   jax.ShapeDtypeStruct((TOP_K, t), I32),
                   jax.ShapeDtypeStruct((TOP_K, t), F32),
                   jax.ShapeDtypeStruct((TOP_K, t), I32),
                   jax.ShapeDtypeStruct((N_EXPERTS, 128), I32)],
        scratch_shapes=[pltpu.VMEM((N_EXPERTS, 128), F32)],
        compiler_params=_params(("arbitrary", "arbitrary")),
        name="merge",
    )(x, og, mbc, *consts)


def _row_copy(src_ref, src_row, dst_ref, dst_row, sem):
    return pltpu.make_async_copy(src_ref.at[pl.ds(src_row, 1), :], dst_ref.at[pl.ds(dst_row, 1), :], sem)


def _dispatch_kernel(pos_ref, hp_ref, xs_ref, sem):
    def copies(r):
        return [_row_copy(hp_ref, r, xs_ref, pos_ref[0, 0, r * TOP_K + k], sem) for k in range(TOP_K)]

    def start(r, carry):
        for cp in copies(r):
            cp.start()
        return carry

    lax.fori_loop(0, hp_ref.shape[0], start, 0)

    def wait(r, carry):
        for cp in copies(r):
            cp.wait()
        return carry

    lax.fori_loop(0, hp_ref.shape[0], wait, 0)


def _dispatch(hp, pos_tok, n_slots):
    t, half = hp.shape
    nt = t // TILE
    pos3 = pos_tok.reshape(nt, 1, TILE * TOP_K)
    return pl.pallas_call(
        _dispatch_kernel,
        grid=(nt,),
        in_specs=[pl.BlockSpec((1, 1, TILE * TOP_K), lambda i: (i, 0, 0), memory_space=pltpu.SMEM),
                  pl.BlockSpec((TILE, half), lambda i: (i, 0))],
        out_specs=pl.BlockSpec(memory_space=pl.ANY),
        out_shape=jax.ShapeDtypeStruct((n_slots, half), U32),
        scratch_shapes=[pltpu.SemaphoreType.DMA(())],
        compiler_params=_params(("arbitrary",)),
        name="dispatch",
    )(pos3, hp)


def _experts_kernel(be_ref, nv_ref, nu_ref, xs_ref, wg_ref, wu_ref, wd_ref, y_ref):
    i = pl.program_id(0)

    @pl.when(i < nu_ref[0])
    def _():
        words = xs_ref[...]
        lo = lax.bitcast_convert_type(words << 16, F32)
        hi = lax.bitcast_convert_type(words & jnp.uint32(0xFFFF0000), F32)
        x = jnp.concatenate([lo, hi], axis=1)
        live = lax.broadcasted_iota(I32, (EXPERT_BLOCK, 1), 0) < nv_ref[i]
        xb = jnp.where(live, x, 0.0).astype(BF16)
        gt = _dot(xb, wg_ref[0].astype(BF16))
        up = _dot(xb, wu_ref[0].astype(BF16))
        act = (gt * jax.nn.sigmoid(gt) * up).astype(BF16)
        y_ref[...] = _dot(act, wd_ref[0].astype(BF16))


def _experts(xs, block_e, block_valid, n_used, w_gate, w_up, w_down):
    n_slots, half = xs.shape
    nb = n_slots // EXPERT_BLOCK
    d = 2 * half

    def row(i, be, nv, nu):
        return (jnp.minimum(i, nu[0] - 1), 0)

    def wsel(i, be, nv, nu):
        return (be[i], 0, 0)

    grid_spec = pltpu.PrefetchScalarGridSpec(
        num_scalar_prefetch=3,
        grid=(nb,),
        in_specs=[pl.BlockSpec((EXPERT_BLOCK, half), row),
                  pl.BlockSpec((1, d, D_EXPERT), wsel),
                  pl.BlockSpec((1, d, D_EXPERT), wsel),
                  pl.BlockSpec((1, D_EXPERT, d), wsel)],
        out_specs=pl.BlockSpec((EXPERT_BLOCK, d), row),
    )
    return pl.pallas_call(
        _experts_kernel,
        grid_spec=grid_spec,
        out_shape=jax.ShapeDtypeStruct((n_slots, d), F32),
        compiler_params=_params(("arbitrary",)),
        name="experts",
    )(block_e, block_valid, n_used, xs, w_gate, w_up, w_down)


def _combine_kernel(pos_ref, y_ref, wt_ref, res_ref, g_ref, b_ref, o_ref, buf_ref, sem):
    n = COMBINE_TILE * TOP_K

    def start(j, carry):
        _row_copy(y_ref, pos_ref[0, 0, j], buf_ref, j, sem).start()
        return carry

    lax.fori_loop(0, n, start, 0)

    def wait(j, carry):
        _row_copy(y_ref, pos_ref[0, 0, j], buf_ref, j, sem).wait()
        return carry

    lax.fori_loop(0, n, wait, 0)

    acc = res_ref[...]
    for k in range(TOP_K):
        acc = acc + buf_ref[k * COMBINE_TILE:(k + 1) * COMBINE_TILE, :] * wt_ref[:, k:k + 1]
    o_ref[...] = _layer_norm(acc, g_ref[...], b_ref[...])


def _combine(y, pos_slot, wts_tok, res, ln_g, ln_b):
    t, d = res.shape
    nt = t // COMBINE_TILE
    pos3 = pos_slot.reshape(nt, 1, COMBINE_TILE * TOP_K)
    tok = pl.BlockSpec((COMBINE_TILE, d), lambda i: (i, 0))
    return pl.pallas_call(
        _combine_kernel,
        grid=(nt,),
        in_specs=[pl.BlockSpec((1, 1, COMBINE_TILE * TOP_K), lambda i: (i, 0, 0), memory_space=pltpu.SMEM),
                  pl.BlockSpec(memory_space=pl.ANY),
                  pl.BlockSpec((COMBINE_TILE, TOP_K), lambda i: (i, 0)),
                  tok, _const_spec(ln_g.shape), _const_spec(ln_b.shape)],
        out_specs=tok,
        out_shape=jax.ShapeDtypeStruct((t, d), F32),
        scratch_shapes=[pltpu.VMEM((COMBINE_TILE * TOP_K, d), F32), pltpu.SemaphoreType.DMA(())],
        compiler_params=_params(("arbitrary",)),
        name="combine",
    )(pos3, y, wts_tok, res, ln_g, ln_b)


def _prepare(l, w_in, conv_w, a_log, dt_bias, gdn_norm_g, w_a, sgu_ln_g, sgu_ln_b, sgu_w, sgu_b, w_b,
             w_mem_kv, w_c, w_o, ln1_g, ln1_b, router_w, router_bias, ws_gate, ws_up, ws_down, ln2_g, ln2_b):
    wi = w_in[l]
    d = D_MODEL
    bf = lambda a: a.astype(BF16)
    row = lambda a: a.reshape(1, -1).astype(F32)
    w_ab = jnp.zeros((d, 128), F32).at[:, :2 * GDN_HEADS].set(wi[:, _C_AB:_C_UV])
    prm = jnp.zeros((8, 128), F32).at[0, :GDN_HEADS].set(a_log[l]).at[1, :GDN_HEADS].set(dt_bias[l])
    sgu_bias = jnp.repeat(sgu_b[l].T, SGU_WIDTH // SGU_GROUPS, axis=1)
    return {
        "w_qkv": bf(wi[:, _C_QKV:_C_Z]), "w_z": bf(wi[:, _C_Z:_C_AB]), "w_ab": bf(w_ab),
        "w_uv": bf(wi[:, _C_UV:_C_XQ]), "w_xq": bf(wi[:, _C_XQ:_C_GATE]),
        "w_g0": bf(wi[:, _C_GATE:_C_GATE + d]), "w_g1": bf(wi[:, _C_GATE + d:_C_GATE + 2 * d]),
        "w_g2": bf(wi[:, _C_GATE + 2 * d:_C_GATE + 3 * d]),
        "conv_w": conv_w[l].astype(F32), "gdn_prm": prm, "gdn_norm_g": row(gdn_norm_g[l]),
        "w_a": bf(w_a[l]), "sgu_ln_g": row(sgu_ln_g[l]), "sgu_ln_b": row(sgu_ln_b[l]),
        "sgu_w": sgu_w[l].astype(F32), "sgu_bias": sgu_bias.astype(F32), "w_b": bf(w_b[l]),
        "w_mem_kv": bf(w_mem_kv[l]), "w_c": bf(w_c[l]), "w_o": bf(w_o[l]),
        "ln1_g": row(ln1_g[l]), "ln1_b": row(ln1_b[l]),
        "router_wt": bf(router_w[l].T), "router_bias": router_bias[l].reshape(-1, 1).astype(F32),
        "ws_gu": bf(jnp.concatenate([ws_gate[l], ws_up[l]], axis=1)), "ws_down": bf(ws_down[l]),
        "ln2_g": row(ln2_g[l]), "ln2_b": row(ln2_b[l]),
    }


def _layer(x, mem, wts, w_gate, w_up, w_down):
    bsz, s, d = x.shape
    t = bsz * s
    k, v = _mem_kv(mem, wts["w_mem_kv"])
    mbc = _mix_bc(x, wts, k, v)
    u, w, qd, kd, a, gl, z = _gdn_prep(x, wts)
    og = _gdn_scan(u, w, qd, kd, a, gl, z, wts["gdn_norm_g"])
    res, hp, idx, rw, rank, cnt = _merge(x, og, mbc, wts)
    counts = cnt[:, 0]
    padded = (counts + EXPERT_BLOCK - 1) // EXPERT_BLOCK * EXPERT_BLOCK
    pad_end = jnp.cumsum(padded)
    pad_start = pad_end - padded
    pos = pad_start[idx] + rank
    nb = t * TOP_K // EXPERT_BLOCK + N_EXPERTS
    n_used = (pad_end[-1] // EXPERT_BLOCK).astype(I32)
    starts = jnp.arange(nb, dtype=I32) * EXPERT_BLOCK
    last = jnp.maximum(n_used - 1, 0) * EXPERT_BLOCK
    block_e = jnp.minimum(jnp.searchsorted(pad_end, jnp.minimum(starts, last), side="right"),
                          N_EXPERTS - 1).astype(I32)
    block_valid = jnp.clip(counts[block_e] - (starts - pad_start[block_e]), 0, EXPERT_BLOCK).astype(I32)
    pos_tok = pos.T.reshape(-1)
    xs = _dispatch(hp, pos_tok, nb * EXPERT_BLOCK)
    y = _experts(xs, block_e, block_valid, n_used.reshape(1), w_gate, w_up, w_down)
    pos_slot = pos.reshape(TOP_K, t // COMBINE_TILE, COMBINE_TILE).transpose(1, 0, 2).reshape(-1)
    out = _combine(y, pos_slot, rw.T, res.reshape(t, d), wts["ln2_g"], wts["ln2_b"])
    return out.reshape(bsz, s, d)


def kernel(x, mem, w_in, conv_w, a_log, dt_bias, gdn_norm_g, w_a, sgu_ln_g, sgu_ln_b, sgu_w, sgu_b, w_b,
           w_mem_kv, w_c, w_o, ln1_g, ln1_b, router_w, router_bias, w_gate, w_up, w_down, ws_gate, ws_up,
           ws_down, ln2_g, ln2_b):
    assert x.shape[1] % TILE == 0 and x.shape[2] == D_MODEL
    for l in range(DEPTH):
        wts = _prepare(l, w_in, conv_w, a_log, dt_bias, gdn_norm_g, w_a, sgu_ln_g, sgu_ln_b, sgu_w, sgu_b,
                       w_b, w_mem_kv, w_c, w_o, ln1_g, ln1_b, router_w, router_bias, ws_gate, ws_up,
                       ws_down, ln2_g, ln2_b)
        x = _layer(x, mem, wts, w_gate[l], w_up[l], w_down[l])
    return x
```

```python
import functools

import jax
import jax.numpy as jnp
from jax import lax
from jax.experimental import pallas as pl
from jax.experimental.pallas import tpu as pltpu

F32 = jnp.float32
BF16 = jnp.bfloat16
I32 = jnp.int32
U32 = jnp.uint32

D_MODEL = 1024
DEPTH = 1
GDN_HEADS = 4
HEAD_DIM = 128
GDN_WIDTH = GDN_HEADS * HEAD_DIM
CONV_WIDTH = 4
SGU_GROUPS = 4
SGU_WIDTH = 512
SGU_CHUNK = 128
N_MEM = 256
XATTN_HEADS = 4
XATTN_WIDTH = 512
N_EXPERTS = 256
TOP_K = 8
N_GROUPS = 8
GROUP_SIZE = N_EXPERTS // N_GROUPS
TOPK_GROUPS = 4
D_EXPERT = 256
D_SHARED = 256
ROUTED_SCALE = 2.5
LN_EPS = 1e-5
RMS_EPS = 1e-6
DEEPNORM_ALPHA = (2.0 * DEPTH) ** 0.25

_C_QKV = 0
_C_Z = 3 * GDN_WIDTH
_C_AB = 4 * GDN_WIDTH
_C_UV = _C_AB + 2 * GDN_HEADS
_C_XQ = _C_UV + 2 * SGU_WIDTH
_C_GATE = _C_XQ + XATTN_WIDTH

CHUNK = 128
TILE = 256
HALO = 8
EXPERT_BLOCK = 256
COMBINE_TILE = 128
VMEM_LIMIT = 56 * 1024 * 1024


def _dot(a, b):
    return jnp.dot(a, b, preferred_element_type=F32)


def _dot_nt(a, b):
    return lax.dot_general(a, b, (((1,), (1,)), ((), ())), preferred_element_type=F32)


def _split(a):
    hi = a.astype(BF16)
    lo = (a - hi.astype(F32)).astype(BF16)
    return hi, lo


def _dot3(a, b):
    ah, al = _split(a)
    bh, bl = _split(b)
    return _dot(ah, bh) + (_dot(ah, bl) + _dot(al, bh))


def _softplus(x):
    return jnp.maximum(x, 0.0) + jnp.log1p(jnp.exp(-jnp.abs(x)))


def _layer_norm(x, g, b):
    mu = jnp.mean(x, axis=-1, keepdims=True)
    xc = x - mu
    var = jnp.mean(xc * xc, axis=-1, keepdims=True)
    return xc * lax.rsqrt(var + LN_EPS) * g + b


def _const_spec(shape):
    nd = len(shape)
    return pl.BlockSpec(shape, lambda *_: (0,) * nd)


def _params(semantics):
    return pltpu.CompilerParams(dimension_semantics=semantics, vmem_limit_bytes=VMEM_LIMIT)


def _mem_kv_kernel(mem_ref, w_ref, k_ref, v_ref):
    kv = _dot(mem_ref[0].astype(BF16), w_ref[...])
    k_ref[0] = kv[:, :XATTN_WIDTH].astype(BF16)
    v_ref[0] = kv[:, XATTN_WIDTH:].astype(BF16)


def _mem_kv(mem, w_kv):
    bsz, n_mem, d = mem.shape
    out = jax.ShapeDtypeStruct((bsz, n_mem, XATTN_WIDTH), BF16)
    return pl.pallas_call(
        _mem_kv_kernel,
        grid=(bsz,),
        in_specs=[pl.BlockSpec((1, n_mem, d), lambda b: (b, 0, 0)), _const_spec(w_kv.shape)],
        out_specs=[pl.BlockSpec((1, n_mem, XATTN_WIDTH), lambda b: (b, 0, 0))] * 2,
        out_shape=[out, out],
        compiler_params=_params(("arbitrary",)),
        name="mem_kv",
    )(mem, w_kv)


def _mix_bc_kernel(x_ref, wuv_ref, wg1_ref, wb_ref, lng_ref, lnb_ref, sw_ref, sb_ref,
                   wxq_ref, k_ref, v_ref, wg2_ref, wc_ref, o_ref):
    xb = x_ref[0].astype(BF16)
    uv = jax.nn.gelu(_dot(xb, wuv_ref[...]), approximate=True)
    u = uv[:, :SGU_WIDTH]
    v = _layer_norm(uv[:, SGU_WIDTH:], lng_ref[...], lnb_ref[...])
    ii = lax.broadcasted_iota(I32, (SGU_CHUNK, SGU_CHUNK), 0)
    jj = lax.broadcasted_iota(I32, (SGU_CHUNK, SGU_CHUNK), 1)
    rows = []
    for n in range(TILE // SGU_CHUNK):
        cols = []
        for g in range(SGU_GROUPS):
            wt = jnp.where(ii >= jj, sw_ref[g], 0.0).astype(BF16)
            vg = v[n * SGU_CHUNK:(n + 1) * SGU_CHUNK, g * 128:(g + 1) * 128].astype(BF16)
            cols.append(_dot(wt, vg))
        rows.append(jnp.concatenate(cols, axis=1) + sb_ref[...])
    sg = jnp.concatenate(rows, axis=0)
    br_b = _dot((u * sg).astype(BF16), wb_ref[...])
    acc = jax.nn.sigmoid(_dot(xb, wg1_ref[...])) * br_b
    xq = _dot(xb, wxq_ref[...])
    heads = []
    for h in range(XATTN_HEADS):
        qh = xq[:, h * 128:(h + 1) * 128].astype(BF16)
        kh = k_ref[0, :, h * 128:(h + 1) * 128]
        vh = v_ref[0, :, h * 128:(h + 1) * 128]
        sc = _dot_nt(qh, kh) * (128 ** -0.5)
        p = jnp.exp(sc - jnp.max(sc, axis=-1, keepdims=True))
        p = p / jnp.sum(p, axis=-1, keepdims=True)
        heads.append(_dot(p.astype(BF16), vh))
    att = jnp.concatenate(heads, axis=1).astype(BF16)
    br_c = _dot(att, wc_ref[...])
    o_ref[0] = acc + jax.nn.sigmoid(_dot(xb, wg2_ref[...])) * br_c


def _mix_bc(x, wts, k, v):
    bsz, s, d = x.shape
    tok = pl.BlockSpec((1, TILE, d), lambda b, i: (b, i, 0))
    kvs = pl.BlockSpec((1, N_MEM, XATTN_WIDTH), lambda b, i: (b, 0, 0))
    names = ("w_uv", "w_g1", "w_b", "sgu_ln_g", "sgu_ln_b", "sgu_w", "sgu_bias", "w_xq")
    consts = [wts[n] for n in names]
    tail = [wts["w_g2"], wts["w_c"]]
    return pl.pallas_call(
        _mix_bc_kernel,
        grid=(bsz, s // TILE),
        in_specs=[tok] + [_const_spec(c.shape) for c in consts] + [kvs, kvs]
        + [_const_spec(c.shape) for c in tail],
        out_specs=tok,
        out_shape=jax.ShapeDtypeStruct((bsz, s, d), F32),
        compiler_params=_params(("arbitrary", "arbitrary")),
        name="mix_bc",
    )(x, *consts, k, v, *tail)


def _unit_lower_inverses(lmats, ii, jj):
    eye = (ii == jj).astype(F32)
    x = ii ^ jj
    lsplit = [_split(l) for l in lmats]
    ts = [eye - jnp.where(x < 2, l, 0.0) for l in lmats]
    zero = jnp.zeros((CHUNK, CHUNK), BF16)
    s = 2
    while s < CHUNK:
        off = (x < 2 * s) & ((ii & s) != 0) & ((jj & s) == 0)
        tsplit = [_split(t) for t in ts]
        prods = []
        for (th, tl), (lh, ll) in zip(tsplit, lsplit):
            oh = jnp.where(off, lh, zero)
            ol = jnp.where(off, ll, zero)
            prods.append(_dot(th, oh) + (_dot(th, ol) + _dot(tl, oh)))
        nxt = []
        for t, p, (th, tl) in zip(ts, prods, tsplit):
            ph, plo = _split(p)
            nxt.append(t - (_dot(ph, th) + (_dot(ph, tl) + _dot(plo, th))))
        ts = nxt
        s *= 2
    return ts


def _gdn_prep_kernel(x_ref, wqkv_ref, wz_ref, wab_ref, convw_ref, prm_ref,
                     u_ref, w_ref, qd_ref, kd_ref, a_ref, gl_ref, z_ref, hist_ref):
    @pl.when(pl.program_id(1) == 0)
    def _():
        hist_ref[0:HALO, :] = jnp.zeros((HALO, 3 * GDN_WIDTH), F32)

    xb = x_ref[0].astype(BF16)
    z_ref[0] = _dot(xb, wz_ref[...])
    qkv = _dot(xb, wqkv_ref[...])
    hist_ref[HALO:HALO + TILE, :] = qkv
    conv = qkv * convw_ref[CONV_WIDTH - 1:CONV_WIDTH, :]
    for j in range(CONV_WIDTH - 1):
        shift = CONV_WIDTH - 1 - j
        conv = conv + hist_ref[HALO - shift:HALO - shift + TILE, :] * convw_ref[j:j + 1, :]
    hist_ref[0:HALO, :] = qkv[TILE - HALO:, :]
    qkv = conv * jax.nn.sigmoid(conv)

    ab = _dot(xb, wab_ref[...])
    g = -jnp.exp(prm_ref[0:1, :]) * _softplus(ab + prm_ref[1:2, :])
    beta = jax.nn.sigmoid(ab)

    ii = lax.broadcasted_iota(I32, (CHUNK, CHUNK), 0)
    jj = lax.broadcasted_iota(I32, (CHUNK, CHUNK), 1)
    tri = (ii >= jj).astype(BF16)
    items = []
    for c in range(TILE // CHUNK):
        r0 = c * CHUNK
        gch = g[r0:r0 + CHUNK, :]
        g1 = gch.astype(BF16)
        r1 = gch - g1.astype(F32)
        g2 = r1.astype(BF16)
        g3 = (r1 - g2.astype(F32)).astype(BF16)
        gc = _dot(tri, g1) + (_dot(tri, g2) + _dot(tri, g3))
        gct = gc.T
        for h in range(GDN_HEADS):
            qh = qkv[r0:r0 + CHUNK, h * HEAD_DIM:(h + 1) * HEAD_DIM]
            kh = qkv[r0:r0 + CHUNK, GDN_WIDTH + h * HEAD_DIM:GDN_WIDTH + (h + 1) * HEAD_DIM]
            vh = qkv[r0:r0 + CHUNK, 2 * GDN_WIDTH + h * HEAD_DIM:2 * GDN_WIDTH + (h + 1) * HEAD_DIM]
            qn = qh * lax.rsqrt(jnp.sum(qh * qh, axis=-1, keepdims=True) + RMS_EPS) * (HEAD_DIM ** -0.5)
            kn = kh * lax.rsqrt(jnp.sum(kh * kh, axis=-1, keepdims=True) + RMS_EPS)
            gcol = gc[:, h:h + 1]
            grow = gct[h:h + 1, :]
            bcol = beta[r0:r0 + CHUNK, GDN_HEADS + h:GDN_HEADS + h + 1]
            glast = gc[CHUNK - 1:CHUNK, h:h + 1]
            decay = jnp.exp(jnp.where(ii >= jj, gcol - grow, -jnp.inf))
            knb = kn.astype(BF16)
            kk = _dot_nt(knb, knb)
            qk = _dot_nt(qn.astype(BF16), knb)
            egc = jnp.exp(gcol)
            items.append(dict(
                c=c, h=h, lmat=jnp.where(ii > jj, kk * bcol * decay, 0.0),
                rhs=jnp.concatenate([vh * bcol, kn * (bcol * egc)], axis=1),
                qd=(qn * egc).astype(BF16), kd=(kn * jnp.exp(glast - gcol)).astype(BF16),
                a=(qk * decay).astype(BF16), gl=jnp.broadcast_to(jnp.exp(glast), (8, HEAD_DIM))))
    tinvs = _unit_lower_inverses([it["lmat"] for it in items], ii, jj)
    sols = [_dot3(tinv, it["rhs"]) for tinv, it in zip(tinvs, items)]
    for it, sol in zip(items, sols):
        rows = slice(it["c"] * CHUNK, (it["c"] + 1) * CHUNK)
        lanes = slice(it["h"] * HEAD_DIM, (it["h"] + 1) * HEAD_DIM)
        u_ref[0, rows, lanes] = sol[:, :HEAD_DIM]
        w_ref[0, rows, lanes] = sol[:, HEAD_DIM:].astype(BF16)
        qd_ref[0, rows, lanes] = it["qd"]
        kd_ref[0, rows, lanes] = it["kd"]
        a_ref[0, rows, lanes] = it["a"]
        gl_ref[0, it["c"] * 8:(it["c"] + 1) * 8, lanes] = it["gl"]


def _gdn_prep(x, wts):
    bsz, s, d = x.shape
    tok = pl.BlockSpec((1, TILE, d), lambda b, i: (b, i, 0))
    hw = pl.BlockSpec((1, TILE, GDN_WIDTH), lambda b, i: (b, i, 0))
    gls = pl.BlockSpec((1, 8 * TILE // CHUNK, GDN_WIDTH), lambda b, i: (b, i, 0))
    consts = [wts[n] for n in ("w_qkv", "w_z", "w_ab", "conv_w", "gdn_prm")]
    f32o = jax.ShapeDtypeStruct((bsz, s, GDN_WIDTH), F32)
    b16o = jax.ShapeDtypeStruct((bsz, s, GDN_WIDTH), BF16)
    glo = jax.ShapeDtypeStruct((bsz, 8 * s // CHUNK, GDN_WIDTH), F32)
    return pl.pallas_call(
        _gdn_prep_kernel,
        grid=(bsz, s // TILE),
        in_specs=[tok] + [_const_spec(c.shape) for c in consts],
        out_specs=[hw, hw, hw, hw, hw, gls, hw],
        out_shape=[f32o, b16o, b16o, b16o, b16o, glo, f32o],
        scratch_shapes=[pltpu.VMEM((HALO + TILE, 3 * GDN_WIDTH), F32)],
        compiler_params=_params(("arbitrary", "arbitrary")),
        name="gdn_prep",
    )(x, *consts)


def _gdn_scan_kernel(u_ref, w_ref, qd_ref, kd_ref, a_ref, gl_ref, z_ref, gn_ref, o_ref, state_ref):
    @pl.when(pl.program_id(0) == 0)
    def _():
        state_ref[...] = jnp.zeros(state_ref.shape, F32)

    bsz = u_ref.shape[0]
    for c in range(TILE // CHUNK):
        rows = slice(c * CHUNK, (c + 1) * CHUNK)
        for b in range(bsz):
            for h in range(GDN_HEADS):
                lanes = slice(h * HEAD_DIM, (h + 1) * HEAD_DIM)
                st = state_ref[b, h]
                wq = jnp.concatenate([w_ref[b, rows, lanes], qd_ref[b, rows, lanes]], axis=0)
                ws = _dot(wq, st.astype(BF16))
                v_new = u_ref[b, rows, lanes] - ws[:CHUNK]
                vb = v_new.astype(BF16)
                o = ws[CHUNK:] + _dot(a_ref[b, rows, lanes], vb)
                kdt = kd_ref[b, rows, lanes].astype(F32).T.astype(BF16)
                state_ref[b, h] = st * gl_ref[b, c * 8:c * 8 + 1, lanes] + _dot(kdt, vb)
                zz = z_ref[b, rows, lanes]
                o = o * lax.rsqrt(jnp.mean(o * o, axis=-1, keepdims=True) + RMS_EPS) * gn_ref[...]
                o_ref[b, rows, lanes] = (o * (zz * jax.nn.sigmoid(zz))).astype(BF16)


def _gdn_scan(u, w, qd, kd, a, gl, z, gn):
    bsz, s, _ = u.shape
    hw = pl.BlockSpec((bsz, TILE, GDN_WIDTH), lambda i: (0, i, 0))
    gls = pl.BlockSpec((bsz, 8 * TILE // CHUNK, GDN_WIDTH), lambda i: (0, i, 0))
    return pl.pallas_call(
        _gdn_scan_kernel,
        grid=(s // TILE,),
        in_specs=[hw, hw, hw, hw, hw, gls, hw, _const_spec(gn.shape)],
        out_specs=hw,
        out_shape=jax.ShapeDtypeStruct((bsz, s, GDN_WIDTH), BF16),
        scratch_shapes=[pltpu.VMEM((bsz, GDN_HEADS, HEAD_DIM, HEAD_DIM), F32)],
        compiler_params=_params(("arbitrary",)),
        name="gdn_scan",
    )(u, w, qd, kd, a, gl, z, gn)


def _merge_kernel(x_ref, og_ref, mbc_ref, wa_ref, wg0_ref, wo_ref, ln1g_ref, ln1b_ref,
                  rwt_ref, rb_ref, wsgu_ref, wsd_ref,
                  res_ref, hp_ref, idx_ref, wt_ref, rank_ref, cnt_ref, carry_ref):
    first = (pl.program_id(0) == 0) & (pl.program_id(1) == 0)

    @pl.when(first)
    def _():
        carry_ref[...] = jnp.zeros(carry_ref.shape, F32)

    x = x_ref[0]
    xb = x.astype(BF16)
    br_a = _dot(og_ref[0], wa_ref[...])
    merged = jax.nn.sigmoid(_dot(xb, wg0_ref[...])) * br_a + mbc_ref[0]
    y = _dot(merged.astype(BF16), wo_ref[...])
    h = _layer_norm(DEEPNORM_ALPHA * x + y, ln1g_ref[...], ln1b_ref[...])
    hb = hb16 = h.astype(BF16)

    bits = lax.bitcast_convert_type(hb16.astype(F32), U32)
    half = D_MODEL // 2
    hp_ref[...] = (bits[:, :half] >> 16) | (bits[:, half:] & jnp.uint32(0xFFFF0000))

    gu = _dot(hb, wsgu_ref[...])
    gt = gu[:, :D_SHARED]
    act = gt * jax.nn.sigmoid(gt) * gu[:, D_SHARED:]
    res_ref[0] = DEEPNORM_ALPHA * h + _dot(act.astype(BF16), wsd_ref[...])

    scores = jax.nn.sigmoid(_dot_nt(rwt_ref[...], hb))
    biased = scores + rb_ref[...]
    g3 = biased.reshape(N_GROUPS, GROUP_SIZE, TILE)
    m1 = jnp.max(g3, axis=1)
    m1b = m1[:, None, :]
    n_top = jnp.sum((g3 == m1b).astype(F32), axis=1)
    m2 = jnp.max(jnp.where(g3 < m1b, g3, -jnp.inf), axis=1)
    gs = m1 + jnp.where(n_top >= 2.0, m1, m2)
    gidx = lax.broadcasted_iota(I32, (N_GROUPS, TILE), 0)
    beaten = jnp.zeros((N_GROUPS, TILE), F32)
    for g in range(N_GROUPS):
        row = gs[g:g + 1, :]
        beaten = beaten + ((row > gs) | ((row == gs) & (g < gidx))).astype(F32)
    sel = (beaten < float(TOPK_GROUPS)).astype(F32)
    sel_e = jnp.broadcast_to(sel[:, None, :], (N_GROUPS, GROUP_SIZE, TILE)).reshape(N_EXPERTS, TILE)
    masked = jnp.where(sel_e > 0.5, biased, -jnp.inf)
    eidx = lax.broadcasted_iota(I32, (N_EXPERTS, TILE), 0)
    chosen = jnp.zeros((N_EXPERTS, TILE), F32)
    picks, pick_scores = [], []
    for _k in range(TOP_K):
        mx = jnp.max(masked, axis=0, keepdims=True)
        pick = jnp.min(jnp.where(masked == mx, eidx, N_EXPERTS), axis=0, keepdims=True)
        hot = eidx == pick
        picks.append(pick)
        pick_scores.append(jnp.sum(jnp.where(hot, scores, 0.0), axis=0, keepdims=True))
        chosen = jnp.where(hot, 1.0, chosen)
        masked = jnp.where(hot, -jnp.inf, masked)
    total = pick_scores[0]
    for sck in pick_scores[1:]:
        total = total + sck
    idx_ref[...] = jnp.concatenate(picks, axis=0)
    wt_ref[...] = jnp.concatenate([sck / total * ROUTED_SCALE for sck in pick_scores], axis=0)

    ti = lax.broadcasted_iota(I32, (TILE, TILE), 0)
    tj = lax.broadcasted_iota(I32, (TILE, TILE), 1)
    before = _dot(chosen.astype(BF16), (ti < tj).astype(BF16))
    carry = carry_ref[...]
    before = before + jnp.concatenate([carry] * (TILE // 128), axis=1)
    rank_ref[...] = jnp.concatenate(
        [jnp.sum(jnp.where(eidx == p, before, 0.0), axis=0, keepdims=True) for p in picks],
        axis=0).astype(I32)
    carry = carry + _dot(chosen.astype(BF16), jnp.ones((TILE, 128), BF16))
    carry_ref[...] = carry
    cnt_ref[...] = carry.astype(I32)


def _merge(x, og, mbc, wts):
    bsz, s, d = x.shape
    t = bsz * s
    nt = s // TILE
    tok = pl.BlockSpec((1, TILE, d), lambda b, i: (b, i, 0))
    ogs = pl.BlockSpec((1, TILE, GDN_WIDTH), lambda b, i: (b, i, 0))
    flat = lambda b, i: (b * nt + i, 0)
    lane = lambda b, i: (0, b * nt + i)
    consts = [wts[n] for n in ("w_a", "w_g0", "w_o", "ln1_g", "ln1_b", "router_wt", "router_bias",
                               "ws_gu", "ws_down")]
    return pl.pallas_call(
        _merge_kernel,
        grid=(bsz, nt),
        in_specs=[tok, ogs, tok] + [_const_spec(c.shape) for c in consts],
        out_specs=[tok,
                   pl.BlockSpec((TILE, d // 2), flat),
                   pl.BlockSpec((TOP_K, TILE), lane),
                   pl.BlockSpec((TOP_K, TILE), lane),
                   pl.BlockSpec((TOP_K, TILE), lane),
                   _const_spec((N_EXPERTS, 128))],
        out_shape=[jax.ShapeDtypeStruct((bsz, s, d), F32),
                   jax.ShapeDtypeStruct((t, d // 2), U32),
                   jax.ShapeDtypeStruct((TOP_K, t), I32),
                   jax.ShapeDtypeStruct((TOP_K, t), F32),
                   jax.ShapeDtypeStruct((TOP_K, t), I32),
                   jax.ShapeDtypeStruct((N_EXPERTS, 128), I32)],
        scratch_shapes=[pltpu.VMEM((N_EXPERTS, 128), F32)],
        compiler_params=_params(("arbitrary", "arbitrary")),
        name="merge",
    )(x, og, mbc, *consts)


def _plan_kernel(idx_ref, rank_ref, cnt_ref, pos_ref, blk_ref):
    e = N_EXPERTS
    shift = EXPERT_BLOCK.bit_length() - 1
    cnt = cnt_ref[...]
    nblk = lax.shift_right_logical(cnt + (EXPERT_BLOCK - 1), shift)
    ei = lax.broadcasted_iota(I32, (e, e), 0)
    ej = lax.broadcasted_iota(I32, (e, e), 1)
    incl = _dot((ei >= ej).astype(BF16), nblk.astype(F32).astype(BF16)).astype(I32)
    excl = incl - nblk
    start = (excl * EXPERT_BLOCK).astype(F32)
    n_used = incl[e - 1:e, :]

    nbp = blk_ref.shape[1]
    rep = nbp // 128
    wide = lambda a: jnp.concatenate([a] * rep, axis=1)
    j = lax.broadcasted_iota(I32, (1, nbp), 1)
    jc = jnp.minimum(j, wide(n_used) - 1)
    be = jnp.sum((wide(incl) <= jc).astype(F32), axis=0, keepdims=True).astype(I32)
    be = jnp.minimum(be, e - 1)
    hot = lax.broadcasted_iota(I32, (e, nbp), 0) == be
    cnt_j = jnp.sum(jnp.where(hot, wide(cnt).astype(F32), 0.0), axis=0, keepdims=True).astype(I32)
    excl_j = jnp.sum(jnp.where(hot, wide(excl).astype(F32), 0.0), axis=0, keepdims=True).astype(I32)
    valid = jnp.clip(cnt_j - (j - excl_j) * EXPERT_BLOCK, 0, EXPERT_BLOCK)
    blk_ref[...] = jnp.concatenate([be, valid, wide(n_used), jnp.zeros((5, nbp), I32)], axis=0)

    e128 = lax.broadcasted_iota(I32, (e, 128), 0)

    def body(i, carry):
        off = pl.multiple_of(i * 128, 128)
        ids = idx_ref[:, pl.ds(off, 128)]
        rows = [jnp.sum(jnp.where(e128 == ids[k:k + 1, :], start, 0.0), axis=0, keepdims=True)
                for k in range(TOP_K)]
        pos_ref[:, pl.ds(off, 128)] = jnp.concatenate(rows, axis=0).astype(I32) + rank_ref[:, pl.ds(off, 128)]
        return carry

    lax.fori_loop(0, idx_ref.shape[1] // 128, body, 0)


def _plan(idx, rank, cnt, nb):
    t = idx.shape[1]
    nbp = (nb + 127) // 128 * 128
    return pl.pallas_call(
        _plan_kernel,
        grid=(1,),
        in_specs=[_const_spec(idx.shape), _const_spec(rank.shape), _const_spec(cnt.shape)],
        out_specs=[_const_spec((TOP_K, t)), _const_spec((8, nbp))],
        out_shape=[jax.ShapeDtypeStruct((TOP_K, t), I32), jax.ShapeDtypeStruct((8, nbp), I32)],
        compiler_params=_params(("arbitrary",)),
        name="plan",
    )(idx, rank, cnt)


def _row_copy(src_ref, src_row, dst_ref, dst_row, sem):
    return pltpu.make_async_copy(src_ref.at[pl.ds(src_row, 1), :], dst_ref.at[pl.ds(dst_row, 1), :], sem)


def _dispatch_kernel(pos_ref, hp_ref, xs_ref, sem):
    def copies(r):
        return [_row_copy(hp_ref, r, xs_ref, pos_ref[0, 0, r * TOP_K + k], sem) for k in range(TOP_K)]

    def start(r, carry):
        for cp in copies(r):
            cp.start()
        return carry

    lax.fori_loop(0, hp_ref.shape[0], start, 0)

    def wait(r, carry):
        for cp in copies(r):
            cp.wait()
        return carry

    lax.fori_loop(0, hp_ref.shape[0], wait, 0)


def _dispatch(hp, pos_tok, n_slots):
    t, half = hp.shape
    nt = t // TILE
    pos3 = pos_tok.reshape(nt, 1, TILE * TOP_K)
    return pl.pallas_call(
        _dispatch_kernel,
        grid=(nt,),
        in_specs=[pl.BlockSpec((1, 1, TILE * TOP_K), lambda i: (i, 0, 0), memory_space=pltpu.SMEM),
                  pl.BlockSpec((TILE, half), lambda i: (i, 0))],
        out_specs=pl.BlockSpec(memory_space=pl.ANY),
        out_shape=jax.ShapeDtypeStruct((n_slots, half), U32),
        scratch_shapes=[pltpu.SemaphoreType.DMA(())],
        compiler_params=_params(("arbitrary",)),
        name="dispatch",
    )(pos3, hp)


def _experts_kernel(be_ref, nv_ref, nu_ref, xs_ref, wg_ref, wu_ref, wd_ref, y_ref):
    i = pl.program_id(0)

    @pl.when(i < nu_ref[0])
    def _():
        words = xs_ref[...]
        lo = lax.bitcast_convert_type(words << 16, F32)
        hi = lax.bitcast_convert_type(words & jnp.uint32(0xFFFF0000), F32)
        x = jnp.concatenate([lo, hi], axis=1)
        live = lax.broadcasted_iota(I32, (EXPERT_BLOCK, 1), 0) < nv_ref[i]
        xb = jnp.where(live, x, 0.0).astype(BF16)
        gt = _dot(xb, wg_ref[0].astype(BF16))
        up = _dot(xb, wu_ref[0].astype(BF16))
        act = (gt * jax.nn.sigmoid(gt) * up).astype(BF16)
        y_ref[...] = _dot(act, wd_ref[0].astype(BF16))


def _experts(xs, block_e, block_valid, n_used, w_gate, w_up, w_down):
    n_slots, half = xs.shape
    nb = n_slots // EXPERT_BLOCK
    d = 2 * half

    def row(i, be, nv, nu):
        return (jnp.minimum(i, nu[0] - 1), 0)

    def wsel(i, be, nv, nu):
        return (be[i], 0, 0)

    grid_spec = pltpu.PrefetchScalarGridSpec(
        num_scalar_prefetch=3,
        grid=(nb,),
        in_specs=[pl.BlockSpec((EXPERT_BLOCK, half), row),
                  pl.BlockSpec((1, d, D_EXPERT), wsel),
                  pl.BlockSpec((1, d, D_EXPERT), wsel),
                  pl.BlockSpec((1, D_EXPERT, d), wsel)],
        out_specs=pl.BlockSpec((EXPERT_BLOCK, d), row),
    )
    return pl.pallas_call(
        _experts_kernel,
        grid_spec=grid_spec,
        out_shape=jax.ShapeDtypeStruct((n_slots, d), F32),
        compiler_params=_params(("arbitrary",)),
        name="experts",
    )(block_e, block_valid, n_used, xs, w_gate, w_up, w_down)


def _combine_kernel(pos_ref, y_ref, wt_ref, res_ref, g_ref, b_ref, o_ref, buf_ref, sem):
    n = COMBINE_TILE * TOP_K

    def start(j, carry):
        _row_copy(y_ref, pos_ref[0, 0, j], buf_ref, j, sem).start()
        return carry

    lax.fori_loop(0, n, start, 0, unroll=8)

    def wait(j, carry):
        _row_copy(y_ref, pos_ref[0, 0, j], buf_ref, j, sem).wait()
        return carry

    lax.fori_loop(0, n, wait, 0, unroll=8)

    acc = res_ref[...]
    for k in range(TOP_K):
        acc = acc + buf_ref[k * COMBINE_TILE:(k + 1) * COMBINE_TILE, :] * wt_ref[:, k:k + 1]
    o_ref[...] = _layer_norm(acc, g_ref[...], b_ref[...])


def _combine(y, pos_slot, wts_tok, res, ln_g, ln_b):
    t, d = res.shape
    nt = t // COMBINE_TILE
    pos3 = pos_slot.reshape(nt, 1, COMBINE_TILE * TOP_K)
    tok = pl.BlockSpec((COMBINE_TILE, d), lambda i: (i, 0))
    return pl.pallas_call(
        _combine_kernel,
        grid=(nt,),
        in_specs=[pl.BlockSpec((1, 1, COMBINE_TILE * TOP_K), lambda i: (i, 0, 0), memory_space=pltpu.SMEM),
                  pl.BlockSpec(memory_space=pl.ANY),
                  pl.BlockSpec((COMBINE_TILE, TOP_K), lambda i: (i, 0)),
                  tok, _const_spec(ln_g.shape), _const_spec(ln_b.shape)],
        out_specs=tok,
        out_shape=jax.ShapeDtypeStruct((t, d), F32),
        scratch_shapes=[pltpu.VMEM((COMBINE_TILE * TOP_K, d), F32), pltpu.SemaphoreType.DMA(())],
        compiler_params=_params(("arbitrary",)),
        name="combine",
    )(pos3, y, wts_tok, res, ln_g, ln_b)


def _prepare(l, w_in, conv_w, a_log, dt_bias, gdn_norm_g, w_a, sgu_ln_g, sgu_ln_b, sgu_w, sgu_b, w_b,
             w_mem_kv, w_c, w_o, ln1_g, ln1_b, router_w, router_bias, ws_gate, ws_up, ws_down, ln2_g, ln2_b):
    wi = w_in[l]
    d = D_MODEL
    bf = lambda a: a.astype(BF16)
    row = lambda a: a.reshape(1, -1).astype(F32)
    w_ab = jnp.zeros((d, 128), F32).at[:, :2 * GDN_HEADS].set(wi[:, _C_AB:_C_UV])
    prm = jnp.zeros((8, 128), F32).at[0, :GDN_HEADS].set(a_log[l]).at[1, :GDN_HEADS].set(dt_bias[l])
    sgu_bias = jnp.repeat(sgu_b[l].T, SGU_WIDTH // SGU_GROUPS, axis=1)
    return {
        "w_qkv": bf(wi[:, _C_QKV:_C_Z]), "w_z": bf(wi[:, _C_Z:_C_AB]), "w_ab": bf(w_ab),
        "w_uv": bf(wi[:, _C_UV:_C_XQ]), "w_xq": bf(wi[:, _C_XQ:_C_GATE]),
        "w_g0": bf(wi[:, _C_GATE:_C_GATE + d]), "w_g1": bf(wi[:, _C_GATE + d:_C_GATE + 2 * d]),
        "w_g2": bf(wi[:, _C_GATE + 2 * d:_C_GATE + 3 * d]),
        "conv_w": conv_w[l].astype(F32), "gdn_prm": prm, "gdn_norm_g": row(gdn_norm_g[l]),
        "w_a": bf(w_a[l]), "sgu_ln_g": row(sgu_ln_g[l]), "sgu_ln_b": row(sgu_ln_b[l]),
        "sgu_w": sgu_w[l].astype(F32), "sgu_bias": sgu_bias.astype(F32), "w_b": bf(w_b[l]),
        "w_mem_kv": bf(w_mem_kv[l]), "w_c": bf(w_c[l]), "w_o": bf(w_o[l]),
        "ln1_g": row(ln1_g[l]), "ln1_b": row(ln1_b[l]),
        "router_wt": bf(router_w[l].T), "router_bias": router_bias[l].reshape(-1, 1).astype(F32),
        "ws_gu": bf(jnp.concatenate([ws_gate[l], ws_up[l]], axis=1)), "ws_down": bf(ws_down[l]),
        "ln2_g": row(ln2_g[l]), "ln2_b": row(ln2_b[l]),
    }


def _layer(x, mem, wts, w_gate, w_up, w_down):
    bsz, s, d = x.shape
    t = bsz * s
    k, v = _mem_kv(mem, wts["w_mem_kv"])
    mbc = _mix_bc(x, wts, k, v)
    u, w, qd, kd, a, gl, z = _gdn_prep(x, wts)
    og = _gdn_scan(u, w, qd, kd, a, gl, z, wts["gdn_norm_g"])
    res, hp, idx, rw, rank, cnt = _merge(x, og, mbc, wts)
    nb = t * TOP_K // EXPERT_BLOCK + N_EXPERTS
    pos, blk = _plan(idx, rank, cnt, nb)
    pos_tok = pos.T.reshape(-1)
    xs = _dispatch(hp, pos_tok, nb * EXPERT_BLOCK)
    y = _experts(xs, blk[0, :nb], blk[1, :nb], blk[2, :1], w_gate, w_up, w_down)
    pos_slot = pos.reshape(TOP_K, t // COMBINE_TILE, COMBINE_TILE).transpose(1, 0, 2).reshape(-1)
    out = _combine(y, pos_slot, rw.T, res.reshape(t, d), wts["ln2_g"], wts["ln2_b"])
    return out.reshape(bsz, s, d)


def kernel(x, mem, w_in, conv_w, a_log, dt_bias, gdn_norm_g, w_a, sgu_ln_g, sgu_ln_b, sgu_w, sgu_b, w_b,
           w_mem_kv, w_c, w_o, ln1_g, ln1_b, router_w, router_bias, w_gate, w_up, w_down, ws_gate, ws_up,
           ws_down, ln2_g, ln2_b):
    assert x.shape[1] % TILE == 0 and x.shape[2] == D_MODEL
    for l in range(DEPTH):
        wts = _prepare(l, w_in, conv_w, a_log, dt_bias, gdn_norm_g, w_a, sgu_ln_g, sgu_ln_b, sgu_w, sgu_b,
                       w_b, w_mem_kv, w_c, w_o, ln1_g, ln1_b, router_w, router_bias, ws_gate, ws_up,
                       ws_down, ln2_g, ln2_b)
        x = _layer(x, mem, wts, w_gate[l], w_up[l], w_down[l])
    return x
```

```python
import functools

import jax
import jax.numpy as jnp
from jax import lax
from jax.experimental import pallas as pl
from jax.experimental.pallas import tpu as pltpu

F32 = jnp.float32
BF16 = jnp.bfloat16
I32 = jnp.int32
U32 = jnp.uint32

D_MODEL = 1024
DEPTH = 1
GDN_HEADS = 4
HEAD_DIM = 128
GDN_WIDTH = GDN_HEADS * HEAD_DIM
CONV_WIDTH = 4
SGU_GROUPS = 4
SGU_WIDTH = 512
SGU_CHUNK = 128
N_MEM = 256
XATTN_HEADS = 4
XATTN_WIDTH = 512
N_EXPERTS = 256
TOP_K = 8
N_GROUPS = 8
GROUP_SIZE = N_EXPERTS // N_GROUPS
TOPK_GROUPS = 4
D_EXPERT = 256
D_SHARED = 256
ROUTED_SCALE = 2.5
LN_EPS = 1e-5
RMS_EPS = 1e-6
DEEPNORM_ALPHA = (2.0 * DEPTH) ** 0.25

_C_QKV = 0
_C_Z = 3 * GDN_WIDTH
_C_AB = 4 * GDN_WIDTH
_C_UV = _C_AB + 2 * GDN_HEADS
_C_XQ = _C_UV + 2 * SGU_WIDTH
_C_GATE = _C_XQ + XATTN_WIDTH

CHUNK = 128
TILE = 256
HALO = 8
EXPERT_BLOCK = 256
COMBINE_TILE = 128
VMEM_LIMIT = 56 * 1024 * 1024


def _dot(a, b):
    return jnp.dot(a, b, preferred_element_type=F32)


def _dot_nt(a, b):
    return lax.dot_general(a, b, (((1,), (1,)), ((), ())), preferred_element_type=F32)


def _split(a):
    hi = a.astype(BF16)
    lo = (a - hi.astype(F32)).astype(BF16)
    return hi, lo


def _dot3(a, b):
    ah, al = _split(a)
    bh, bl = _split(b)
    return _dot(ah, bh) + (_dot(ah, bl) + _dot(al, bh))


def _softplus(x):
    return jnp.maximum(x, 0.0) + jnp.log1p(jnp.exp(-jnp.abs(x)))


def _layer_norm(x, g, b):
    mu = jnp.mean(x, axis=-1, keepdims=True)
    xc = x - mu
    var = jnp.mean(xc * xc, axis=-1, keepdims=True)
    return xc * lax.rsqrt(var + LN_EPS) * g + b


def _const_spec(shape):
    nd = len(shape)
    return pl.BlockSpec(shape, lambda *_: (0,) * nd)


def _params(semantics):
    return pltpu.CompilerParams(dimension_semantics=semantics, vmem_limit_bytes=VMEM_LIMIT)


def _mem_kv_kernel(mem_ref, w_ref, k_ref, v_ref):
    kv = _dot(mem_ref[0].astype(BF16), w_ref[...])
    k_ref[0] = kv[:, :XATTN_WIDTH].astype(BF16)
    v_ref[0] = kv[:, XATTN_WIDTH:].astype(BF16)


def _mem_kv(mem, w_kv):
    bsz, n_mem, d = mem.shape
    out = jax.ShapeDtypeStruct((bsz, n_mem, XATTN_WIDTH), BF16)
    return pl.pallas_call(
        _mem_kv_kernel,
        grid=(bsz,),
        in_specs=[pl.BlockSpec((1, n_mem, d), lambda b: (b, 0, 0)), _const_spec(w_kv.shape)],
        out_specs=[pl.BlockSpec((1, n_mem, XATTN_WIDTH), lambda b: (b, 0, 0))] * 2,
        out_shape=[out, out],
        compiler_params=_params(("arbitrary",)),
        name="mem_kv",
    )(mem, w_kv)


def _mix_bc_kernel(x_ref, wuv_ref, wg1_ref, wb_ref, lng_ref, lnb_ref, sw_ref, sb_ref,
                   wxq_ref, k_ref, v_ref, wg2_ref, wc_ref, o_ref):
    xb = x_ref[0].astype(BF16)
    uv = jax.nn.gelu(_dot(xb, wuv_ref[...]), approximate=True)
    u = uv[:, :SGU_WIDTH]
    v = _layer_norm(uv[:, SGU_WIDTH:], lng_ref[...], lnb_ref[...])
    ii = lax.broadcasted_iota(I32, (SGU_CHUNK, SGU_CHUNK), 0)
    jj = lax.broadcasted_iota(I32, (SGU_CHUNK, SGU_CHUNK), 1)
    rows = []
    for n in range(TILE // SGU_CHUNK):
        cols = []
        for g in range(SGU_GROUPS):
            wt = jnp.where(ii >= jj, sw_ref[g], 0.0).astype(BF16)
            vg = v[n * SGU_CHUNK:(n + 1) * SGU_CHUNK, g * 128:(g + 1) * 128].astype(BF16)
            cols.append(_dot(wt, vg))
        rows.append(jnp.concatenate(cols, axis=1) + sb_ref[...])
    sg = jnp.concatenate(rows, axis=0)
    br_b = _dot((u * sg).astype(BF16), wb_ref[...])
    acc = jax.nn.sigmoid(_dot(xb, wg1_ref[...])) * br_b
    xq = _dot(xb, wxq_ref[...])
    heads = []
    for h in range(XATTN_HEADS):
        qh = xq[:, h * 128:(h + 1) * 128].astype(BF16)
        kh = k_ref[0, :, h * 128:(h + 1) * 128]
        vh = v_ref[0, :, h * 128:(h + 1) * 128]
        sc = _dot_nt(qh, kh) * (128 ** -0.5)
        p = jnp.exp(sc - jnp.max(sc, axis=-1, keepdims=True))
        p = p / jnp.sum(p, axis=-1, keepdims=True)
        heads.append(_dot(p.astype(BF16), vh))
    att = jnp.concatenate(heads, axis=1).astype(BF16)
    br_c = _dot(att, wc_ref[...])
    o_ref[0] = acc + jax.nn.sigmoid(_dot(xb, wg2_ref[...])) * br_c


def _mix_bc(x, wts, k, v):
    bsz, s, d = x.shape
    tok = pl.BlockSpec((1, TILE, d), lambda b, i: (b, i, 0))
    kvs = pl.BlockSpec((1, N_MEM, XATTN_WIDTH), lambda b, i: (b, 0, 0))
    names = ("w_uv", "w_g1", "w_b", "sgu_ln_g", "sgu_ln_b", "sgu_w", "sgu_bias", "w_xq")
    consts = [wts[n] for n in names]
    tail = [wts["w_g2"], wts["w_c"]]
    return pl.pallas_call(
        _mix_bc_kernel,
        grid=(bsz, s // TILE),
        in_specs=[tok] + [_const_spec(c.shape) for c in consts] + [kvs, kvs]
        + [_const_spec(c.shape) for c in tail],
        out_specs=tok,
        out_shape=jax.ShapeDtypeStruct((bsz, s, d), F32),
        compiler_params=_params(("arbitrary", "arbitrary")),
        name="mix_bc",
    )(x, *consts, k, v, *tail)


def _unit_lower_inverses(lmats, ii, jj):
    eye = (ii == jj).astype(F32)
    x = ii ^ jj
    lsplit = [_split(l) for l in lmats]
    ts = [eye - jnp.where(x < 2, l, 0.0) for l in lmats]
    zero = jnp.zeros((CHUNK, CHUNK), BF16)
    s = 2
    while s < CHUNK:
        off = (x < 2 * s) & ((ii & s) != 0) & ((jj & s) == 0)
        tsplit = [_split(t) for t in ts]
        prods = []
        for (th, tl), (lh, ll) in zip(tsplit, lsplit):
            oh = jnp.where(off, lh, zero)
            ol = jnp.where(off, ll, zero)
            prods.append(_dot(th, oh) + (_dot(th, ol) + _dot(tl, oh)))
        nxt = []
        for t, p, (th, tl) in zip(ts, prods, tsplit):
            ph, plo = _split(p)
            nxt.append(t - (_dot(ph, th) + (_dot(ph, tl) + _dot(plo, th))))
        ts = nxt
        s *= 2
    return ts


def _gdn_prep_kernel(x_ref, wqkv_ref, wz_ref, wab_ref, convw_ref, prm_ref,
                     u_ref, w_ref, qd_ref, kd_ref, a_ref, gl_ref, z_ref, hist_ref):
    @pl.when(pl.program_id(1) == 0)
    def _():
        hist_ref[0:HALO, :] = jnp.zeros((HALO, 3 * GDN_WIDTH), F32)

    xb = x_ref[0].astype(BF16)
    z_ref[0] = _dot(xb, wz_ref[...])
    qkv = _dot(xb, wqkv_ref[...])
    hist_ref[HALO:HALO + TILE, :] = qkv
    conv = qkv * convw_ref[CONV_WIDTH - 1:CONV_WIDTH, :]
    for j in range(CONV_WIDTH - 1):
        shift = CONV_WIDTH - 1 - j
        conv = conv + hist_ref[HALO - shift:HALO - shift + TILE, :] * convw_ref[j:j + 1, :]
    hist_ref[0:HALO, :] = qkv[TILE - HALO:, :]
    qkv = conv * jax.nn.sigmoid(conv)

    ab = _dot(xb, wab_ref[...])
    g = -jnp.exp(prm_ref[0:1, :]) * _softplus(ab + prm_ref[1:2, :])
    beta = jax.nn.sigmoid(ab)

    ii = lax.broadcasted_iota(I32, (CHUNK, CHUNK), 0)
    jj = lax.broadcasted_iota(I32, (CHUNK, CHUNK), 1)
    tri = (ii >= jj).astype(BF16)
    items = []
    for c in range(TILE // CHUNK):
        r0 = c * CHUNK
        gch = g[r0:r0 + CHUNK, :]
        g1 = gch.astype(BF16)
        r1 = gch - g1.astype(F32)
        g2 = r1.astype(BF16)
        g3 = (r1 - g2.astype(F32)).astype(BF16)
        gc = _dot(tri, g1) + (_dot(tri, g2) + _dot(tri, g3))
        gct = gc.T
        for h in range(GDN_HEADS):
            qh = qkv[r0:r0 + CHUNK, h * HEAD_DIM:(h + 1) * HEAD_DIM]
            kh = qkv[r0:r0 + CHUNK, GDN_WIDTH + h * HEAD_DIM:GDN_WIDTH + (h + 1) * HEAD_DIM]
            vh = qkv[r0:r0 + CHUNK, 2 * GDN_WIDTH + h * HEAD_DIM:2 * GDN_WIDTH + (h + 1) * HEAD_DIM]
            qn = qh * lax.rsqrt(jnp.sum(qh * qh, axis=-1, keepdims=True) + RMS_EPS) * (HEAD_DIM ** -0.5)
            kn = kh * lax.rsqrt(jnp.sum(kh * kh, axis=-1, keepdims=True) + RMS_EPS)
            gcol = gc[:, h:h + 1]
            grow = gct[h:h + 1, :]
            bcol = beta[r0:r0 + CHUNK, GDN_HEADS + h:GDN_HEADS + h + 1]
            glast = gc[CHUNK - 1:CHUNK, h:h + 1]
            decay = jnp.exp(jnp.where(ii >= jj, gcol - grow, -jnp.inf))
            knb = kn.astype(BF16)
            kk = _dot_nt(knb, knb)
            qk = _dot_nt(qn.astype(BF16), knb)
            egc = jnp.exp(gcol)
            items.append(dict(
                c=c, h=h, lmat=jnp.where(ii > jj, kk * bcol * decay, 0.0),
                rhs=jnp.concatenate([vh * bcol, kn * (bcol * egc)], axis=1),
                qd=(qn * egc).astype(BF16), kd=(kn * jnp.exp(glast - gcol)).astype(BF16),
                a=(qk * decay).astype(BF16), gl=jnp.broadcast_to(jnp.exp(glast), (8, HEAD_DIM))))
    tinvs = _unit_lower_inverses([it["lmat"] for it in items], ii, jj)
    sols = [_dot3(tinv, it["rhs"]) for tinv, it in zip(tinvs, items)]
    for it, sol in zip(items, sols):
        rows = slice(it["c"] * CHUNK, (it["c"] + 1) * CHUNK)
        lanes = slice(it["h"] * HEAD_DIM, (it["h"] + 1) * HEAD_DIM)
        u_ref[0, rows, lanes] = sol[:, :HEAD_DIM]
        w_ref[0, rows, lanes] = sol[:, HEAD_DIM:].astype(BF16)
        qd_ref[0, rows, lanes] = it["qd"]
        kd_ref[0, rows, lanes] = it["kd"]
        a_ref[0, rows, lanes] = it["a"]
        gl_ref[0, it["c"] * 8:(it["c"] + 1) * 8, lanes] = it["gl"]


def _gdn_prep(x, wts):
    bsz, s, d = x.shape
    tok = pl.BlockSpec((1, TILE, d), lambda b, i: (b, i, 0))
    hw = pl.BlockSpec((1, TILE, GDN_WIDTH), lambda b, i: (b, i, 0))
    gls = pl.BlockSpec((1, 8 * TILE // CHUNK, GDN_WIDTH), lambda b, i: (b, i, 0))
    consts = [wts[n] for n in ("w_qkv", "w_z", "w_ab", "conv_w", "gdn_prm")]
    f32o = jax.ShapeDtypeStruct((bsz, s, GDN_WIDTH), F32)
    b16o = jax.ShapeDtypeStruct((bsz, s, GDN_WIDTH), BF16)
    glo = jax.ShapeDtypeStruct((bsz, 8 * s // CHUNK, GDN_WIDTH), F32)
    return pl.pallas_call(
        _gdn_prep_kernel,
        grid=(bsz, s // TILE),
        in_specs=[tok] + [_const_spec(c.shape) for c in consts],
        out_specs=[hw, hw, hw, hw, hw, gls, hw],
        out_shape=[f32o, b16o, b16o, b16o, b16o, glo, f32o],
        scratch_shapes=[pltpu.VMEM((HALO + TILE, 3 * GDN_WIDTH), F32)],
        compiler_params=_params(("arbitrary", "arbitrary")),
        name="gdn_prep",
    )(x, *consts)


def _gdn_scan_kernel(u_ref, w_ref, qd_ref, kd_ref, a_ref, gl_ref, z_ref, gn_ref, o_ref, state_ref):
    @pl.when(pl.program_id(0) == 0)
    def _():
        state_ref[...] = jnp.zeros(state_ref.shape, F32)

    bsz = u_ref.shape[0]
    for c in range(TILE // CHUNK):
        rows = slice(c * CHUNK, (c + 1) * CHUNK)
        for b in range(bsz):
            for h in range(GDN_HEADS):
                lanes = slice(h * HEAD_DIM, (h + 1) * HEAD_DIM)
                st = state_ref[b, h]
                wq = jnp.concatenate([w_ref[b, rows, lanes], qd_ref[b, rows, lanes]], axis=0)
                ws = _dot(wq, st.astype(BF16))
                v_new = u_ref[b, rows, lanes] - ws[:CHUNK]
                vb = v_new.astype(BF16)
                o = ws[CHUNK:] + _dot(a_ref[b, rows, lanes], vb)
                kdt = kd_ref[b, rows, lanes].astype(F32).T.astype(BF16)
                state_ref[b, h] = st * gl_ref[b, c * 8:c * 8 + 1, lanes] + _dot(kdt, vb)
                zz = z_ref[b, rows, lanes]
                o = o * lax.rsqrt(jnp.mean(o * o, axis=-1, keepdims=True) + RMS_EPS) * gn_ref[...]
                o_ref[b, rows, lanes] = (o * (zz * jax.nn.sigmoid(zz))).astype(BF16)


def _gdn_scan(u, w, qd, kd, a, gl, z, gn):
    bsz, s, _ = u.shape
    hw = pl.BlockSpec((bsz, TILE, GDN_WIDTH), lambda i: (0, i, 0))
    gls = pl.BlockSpec((bsz, 8 * TILE // CHUNK, GDN_WIDTH), lambda i: (0, i, 0))
    return pl.pallas_call(
        _gdn_scan_kernel,
        grid=(s // TILE,),
        in_specs=[hw, hw, hw, hw, hw, gls, hw, _const_spec(gn.shape)],
        out_specs=hw,
        out_shape=jax.ShapeDtypeStruct((bsz, s, GDN_WIDTH), BF16),
        scratch_shapes=[pltpu.VMEM((bsz, GDN_HEADS, HEAD_DIM, HEAD_DIM), F32)],
        compiler_params=_params(("arbitrary",)),
        name="gdn_scan",
    )(u, w, qd, kd, a, gl, z, gn)


def _merge_kernel(x_ref, og_ref, mbc_ref, wa_ref, wg0_ref, wo_ref, ln1g_ref, ln1b_ref,
                  rwt_ref, rb_ref, wsgu_ref, wsd_ref,
                  res_ref, hp_ref, idx_ref, wt_ref, rank_ref, cnt_ref, carry_ref):
    first = (pl.program_id(0) == 0) & (pl.program_id(1) == 0)

    @pl.when(first)
    def _():
        carry_ref[...] = jnp.zeros(carry_ref.shape, F32)

    x = x_ref[0]
    xb = x.astype(BF16)
    br_a = _dot(og_ref[0], wa_ref[...])
    merged = jax.nn.sigmoid(_dot(xb, wg0_ref[...])) * br_a + mbc_ref[0]
    y = _dot(merged.astype(BF16), wo_ref[...])
    h = _layer_norm(DEEPNORM_ALPHA * x + y, ln1g_ref[...], ln1b_ref[...])
    hb = h.astype(BF16)
    hp_ref[...] = _pack_bf16_pairs(h)

    gu = _dot(hb, wsgu_ref[...])
    gt = gu[:, :D_SHARED]
    act = gt * jax.nn.sigmoid(gt) * gu[:, D_SHARED:]
    res_ref[0] = DEEPNORM_ALPHA * h + _dot(act.astype(BF16), wsd_ref[...])

    scores = jax.nn.sigmoid(_dot_nt(rwt_ref[...], hb))
    biased = scores + rb_ref[...]
    g3 = biased.reshape(N_GROUPS, GROUP_SIZE, TILE)
    m1 = jnp.max(g3, axis=1)
    m1b = m1[:, None, :]
    n_top = jnp.sum((g3 == m1b).astype(F32), axis=1)
    m2 = jnp.max(jnp.where(g3 < m1b, g3, -jnp.inf), axis=1)
    gs = m1 + jnp.where(n_top >= 2.0, m1, m2)
    gidx = lax.broadcasted_iota(I32, (N_GROUPS, TILE), 0)
    beaten = jnp.zeros((N_GROUPS, TILE), F32)
    for g in range(N_GROUPS):
        row = gs[g:g + 1, :]
        beaten = beaten + ((row > gs) | ((row == gs) & (g < gidx))).astype(F32)
    sel = (beaten < float(TOPK_GROUPS)).astype(F32)
    sel_e = jnp.broadcast_to(sel[:, None, :], (N_GROUPS, GROUP_SIZE, TILE)).reshape(N_EXPERTS, TILE)
    masked = jnp.where(sel_e > 0.5, biased, -jnp.inf)
    eidx = lax.broadcasted_iota(I32, (N_EXPERTS, TILE), 0)
    chosen = jnp.zeros((N_EXPERTS, TILE), F32)
    picks, pick_scores = [], []
    for _k in range(TOP_K):
        mx = jnp.max(masked, axis=0, keepdims=True)
        pick = jnp.min(jnp.where(masked == mx, eidx, N_EXPERTS), axis=0, keepdims=True)
        hot = eidx == pick
        picks.append(pick)
        pick_scores.append(jnp.sum(jnp.where(hot, scores, 0.0), axis=0, keepdims=True))
        chosen = jnp.where(hot, 1.0, chosen)
        masked = jnp.where(hot, -jnp.inf, masked)
    total = pick_scores[0]
    for sck in pick_scores[1:]:
        total = total + sck
    idx_ref[...] = jnp.concatenate(picks, axis=0)
    wt_ref[...] = jnp.concatenate([sck / total * ROUTED_SCALE for sck in pick_scores], axis=0)

    ti = lax.broadcasted_iota(I32, (TILE, TILE), 0)
    tj = lax.broadcasted_iota(I32, (TILE, TILE), 1)
    before = _dot(chosen.astype(BF16), (ti < tj).astype(BF16))
    carry = carry_ref[...]
    before = before + jnp.concatenate([carry] * (TILE // 128), axis=1)
    rank_ref[...] = jnp.concatenate(
        [jnp.sum(jnp.where(eidx == p, before, 0.0), axis=0, keepdims=True) for p in picks],
        axis=0).astype(I32)
    carry = carry + _dot(chosen.astype(BF16), jnp.ones((TILE, 128), BF16))
    carry_ref[...] = carry
    cnt_ref[...] = carry.astype(I32)


def _merge(x, og, mbc, wts):
    bsz, s, d = x.shape
    t = bsz * s
    nt = s // TILE
    tok = pl.BlockSpec((1, TILE, d), lambda b, i: (b, i, 0))
    ogs = pl.BlockSpec((1, TILE, GDN_WIDTH), lambda b, i: (b, i, 0))
    flat = lambda b, i: (b * nt + i, 0)
    lane = lambda b, i: (0, b * nt + i)
    consts = [wts[n] for n in ("w_a", "w_g0", "w_o", "ln1_g", "ln1_b", "router_wt", "router_bias",
                               "ws_gu", "ws_down")]
    return pl.pallas_call(
        _merge_kernel,
        grid=(bsz, nt),
        in_specs=[tok, ogs, tok] + [_const_spec(c.shape) for c in consts],
        out_specs=[tok,
                   pl.BlockSpec((TILE, d // 2), flat),
                   pl.BlockSpec((TOP_K, TILE), lane),
                   pl.BlockSpec((TOP_K, TILE), lane),
                   pl.BlockSpec((TOP_K, TILE), lane),
                   _const_spec((N_EXPERTS, 128))],
        out_shape=[jax.ShapeDtypeStruct((bsz, s, d), F32),
                   jax.ShapeDtypeStruct((t, d // 2), U32),
                   jax.ShapeDtypeStruct((TOP_K, t), I32),
                   jax.ShapeDtypeStruct((TOP_K, t), F32),
                   jax.ShapeDtypeStruct((TOP_K, t), I32),
                   jax.ShapeDtypeStruct((N_EXPERTS, 128), I32)],
        scratch_shapes=[pltpu.VMEM((N_EXPERTS, 128), F32)],
        compiler_params=_params(("arbitrary", "arbitrary")),
        name="merge",
    )(x, og, mbc, *consts)


def _plan_kernel(idx_ref, rank_ref, cnt_ref, pos_ref, meta_ref):
    e = N_EXPERTS
    shift = EXPERT_BLOCK.bit_length() - 1
    cnt = cnt_ref[...]
    nblk = lax.shift_right_logical(cnt + (EXPERT_BLOCK - 1), shift)
    ei = lax.broadcasted_iota(I32, (e, e), 0)
    ej = lax.broadcasted_iota(I32, (e, e), 1)
    incl = _dot((ei >= ej).astype(BF16), nblk.astype(F32).astype(BF16)).astype(I32)
    excl = incl - nblk
    start = (excl * EXPERT_BLOCK).astype(F32)
    lane = lax.broadcasted_iota(I32, (e, 128), 1)
    n_used = jnp.broadcast_to(incl[e - 1:e, :], (e, 128))
    meta_ref[...] = jnp.where(lane == 0, excl, jnp.where(lane == 1, nblk, jnp.where(lane == 2, cnt, n_used)))

    e128 = lax.broadcasted_iota(I32, (e, 128), 0)

    def body(i, carry):
        off = pl.multiple_of(i * 128, 128)
        ids = idx_ref[:, pl.ds(off, 128)]
        rows = [jnp.sum(jnp.where(e128 == ids[k:k + 1, :], start, 0.0), axis=0, keepdims=True)
                for k in range(TOP_K)]
        pos_ref[:, pl.ds(off, 128)] = jnp.concatenate(rows, axis=0).astype(I32) + rank_ref[:, pl.ds(off, 128)]
        return carry

    lax.fori_loop(0, idx_ref.shape[1] // 128, body, 0)


def _plan(idx, rank, cnt):
    t = idx.shape[1]
    return pl.pallas_call(
        _plan_kernel,
        grid=(1,),
        in_specs=[_const_spec(idx.shape), _const_spec(rank.shape), _const_spec(cnt.shape)],
        out_specs=[_const_spec((TOP_K, t)), _const_spec((N_EXPERTS, 128))],
        out_shape=[jax.ShapeDtypeStruct((TOP_K, t), I32), jax.ShapeDtypeStruct((N_EXPERTS, 128), I32)],
        compiler_params=_params(("arbitrary",)),
        name="plan",
    )(idx, rank, cnt)


def _row_copy(src_ref, src_row, dst_ref, dst_row, sem):
    return pltpu.make_async_copy(src_ref.at[pl.ds(src_row, 1), :], dst_ref.at[pl.ds(dst_row, 1), :], sem)


def _dispatch_kernel(pos_ref, hp_ref, xs_ref, sem):
    def copies(r):
        return [_row_copy(hp_ref, r, xs_ref, pos_ref[0, 0, r * TOP_K + k], sem) for k in range(TOP_K)]

    def start(r, carry):
        for k, cp in enumerate(copies(r)):
            cp.start(priority=k % 2)
        return carry

    lax.fori_loop(0, hp_ref.shape[0], start, 0)

    def wait(r, carry):
        for cp in copies(r):
            cp.wait()
        return carry

    lax.fori_loop(0, hp_ref.shape[0], wait, 0)


def _dispatch(hp, pos_tok, n_slots):
    t, half = hp.shape
    nt = t // TILE
    pos3 = pos_tok.reshape(nt, 1, TILE * TOP_K)
    return pl.pallas_call(
        _dispatch_kernel,
        grid=(nt,),
        in_specs=[pl.BlockSpec((1, 1, TILE * TOP_K), lambda i: (i, 0, 0), memory_space=pltpu.SMEM),
                  pl.BlockSpec((TILE, half), lambda i: (i, 0))],
        out_specs=pl.BlockSpec(memory_space=pl.ANY),
        out_shape=jax.ShapeDtypeStruct((n_slots, half), U32),
        scratch_shapes=[pltpu.SemaphoreType.DMA(())],
        compiler_params=_params(("arbitrary",)),
        name="dispatch",
    )(pos3, hp)


def _pack_bf16_pairs(a):
    n = a.shape[1] // 2
    bits = lax.bitcast_convert_type(a.astype(BF16).astype(F32), U32)
    return (bits[:, :n] >> 16) | (bits[:, n:] & jnp.uint32(0xFFFF0000))


def _unpack_bf16_pairs(words):
    lo = lax.bitcast_convert_type(words << 16, F32)
    hi = lax.bitcast_convert_type(words & jnp.uint32(0xFFFF0000), F32)
    return jnp.concatenate([lo, hi], axis=1)


def _experts_kernel(fb_ref, nb_ref, cnt_ref, nu_ref, xs_ref, wg_ref, wu_ref, wd_ref, y_ref,
                    wgu_s, wd_s, xbuf, ybuf, xsem, ysem):
    e = pl.program_id(0)
    n_used = nu_ref[0]
    bm = EXPERT_BLOCK

    def x_copy(g, slot):
        return pltpu.make_async_copy(xs_ref.at[pl.ds(g * bm, bm), :], xbuf.at[slot], xsem.at[slot])

    def y_copy(g, slot):
        return pltpu.make_async_copy(ybuf.at[slot], y_ref.at[pl.ds(g * bm, bm), :], ysem.at[slot])

    @pl.when(e == 0)
    def _():
        x_copy(0, 0).start()

    nb = nb_ref[e]

    @pl.when(nb > 0)
    def _():
        wgu_s[:, :D_EXPERT] = wg_ref[0].astype(BF16)
        wgu_s[:, D_EXPERT:] = wu_ref[0].astype(BF16)
        wd_s[...] = wd_ref[0].astype(BF16)

    def block(b, carry):
        g = fb_ref[e] + b
        slot = g & 1
        x_copy(g, slot).wait()

        @pl.when(g + 1 < n_used)
        def _():
            x_copy(g + 1, 1 - slot).start()

        x = _unpack_bf16_pairs(xbuf[slot])
        live = lax.broadcasted_iota(I32, (bm, 1), 0) < cnt_ref[e] - b * bm
        xb = jnp.where(live, x, 0.0).astype(BF16)
        gu = _dot(xb, wgu_s[...])
        gt = gu[:, :D_EXPERT]
        act = (gt * jax.nn.sigmoid(gt) * gu[:, D_EXPERT:]).astype(BF16)
        y = _dot(act, wd_s[...])

        @pl.when(g >= 2)
        def _():
            y_copy(g - 2, slot).wait()

        ybuf[slot] = _pack_bf16_pairs(y)
        y_copy(g, slot).start()
        return carry

    lax.fori_loop(0, nb, block, 0)

    @pl.when(e == pl.num_programs(0) - 1)
    def _():
        @pl.when(n_used >= 2)
        def _():
            y_copy(n_used - 2, n_used & 1).wait()

        y_copy(n_used - 1, (n_used - 1) & 1).wait()


def _experts(xs, first_block, n_blocks, counts, n_used, w_gate, w_up, w_down):
    n_slots, half = xs.shape
    d = 2 * half
    wsel = lambda e, *_: (e, 0, 0)
    grid_spec = pltpu.PrefetchScalarGridSpec(
        num_scalar_prefetch=4,
        grid=(N_EXPERTS,),
        in_specs=[pl.BlockSpec(memory_space=pl.ANY),
                  pl.BlockSpec((1, d, D_EXPERT), wsel),
                  pl.BlockSpec((1, d, D_EXPERT), wsel),
                  pl.BlockSpec((1, D_EXPERT, d), wsel)],
        out_specs=pl.BlockSpec(memory_space=pl.ANY),
        scratch_shapes=[pltpu.VMEM((d, 2 * D_EXPERT), BF16), pltpu.VMEM((D_EXPERT, d), BF16),
                        pltpu.VMEM((2, EXPERT_BLOCK, half), U32), pltpu.VMEM((2, EXPERT_BLOCK, half), U32),
                        pltpu.SemaphoreType.DMA((2,)), pltpu.SemaphoreType.DMA((2,))],
    )
    return pl.pallas_call(
        _experts_kernel,
        grid_spec=grid_spec,
        out_shape=jax.ShapeDtypeStruct((n_slots, half), U32),
        compiler_params=_params(("arbitrary",)),
        name="experts",
    )(first_block, n_blocks, counts, n_used, xs, w_gate, w_up, w_down)


def _combine_kernel(pos_ref, nxt_ref, y_ref, wt_ref, res_ref, g_ref, b_ref, o_ref, buf_ref, sem):
    i = pl.program_id(0)
    n = COMBINE_TILE * TOP_K

    def issue(p_ref, slot):
        def start(j8, carry):
            for q in range(8):
                j = j8 * 8 + q
                _row_copy(y_ref, p_ref[0, 0, j], buf_ref.at[slot], j, sem.at[slot]).start(priority=q % 2)
            return carry

        lax.fori_loop(0, n // 8, start, 0)

    @pl.when(i == 0)
    def _():
        issue(pos_ref, 0)

    @pl.when(i + 1 < pl.num_programs(0))
    def _():
        issue(nxt_ref, (i + 1) & 1)

    slot = i & 1

    def wait(j, carry):
        _row_copy(y_ref, 0, buf_ref.at[slot], j, sem.at[slot]).wait()
        return carry

    lax.fori_loop(0, n, wait, 0, unroll=8)

    acc = res_ref[...]
    for k in range(TOP_K):
        rows = _unpack_bf16_pairs(buf_ref[slot, k * COMBINE_TILE:(k + 1) * COMBINE_TILE, :])
        acc = acc + rows * wt_ref[:, k:k + 1]
    o_ref[...] = _layer_norm(acc, g_ref[...], b_ref[...])


def _combine(y, pos_slot, wts_tok, res, ln_g, ln_b):
    t, d = res.shape
    nt = t // COMBINE_TILE
    n = COMBINE_TILE * TOP_K
    pos3 = pos_slot.reshape(nt, 1, n)
    tok = pl.BlockSpec((COMBINE_TILE, d), lambda i: (i, 0))
    return pl.pallas_call(
        _combine_kernel,
        grid=(nt,),
        in_specs=[pl.BlockSpec((1, 1, n), lambda i: (i, 0, 0), memory_space=pltpu.SMEM),
                  pl.BlockSpec((1, 1, n), lambda i: (jnp.minimum(i + 1, nt - 1), 0, 0), memory_space=pltpu.SMEM),
                  pl.BlockSpec(memory_space=pl.ANY),
                  pl.BlockSpec((COMBINE_TILE, TOP_K), lambda i: (i, 0)),
                  tok, _const_spec(ln_g.shape), _const_spec(ln_b.shape)],
        out_specs=tok,
        out_shape=jax.ShapeDtypeStruct((t, d), F32),
        scratch_shapes=[pltpu.VMEM((2, n, d // 2), U32), pltpu.SemaphoreType.DMA((2,))],
        compiler_params=_params(("arbitrary",)),
        name="combine",
    )(pos3, pos3, y, wts_tok, res, ln_g, ln_b)


def _prepare(l, w_in, conv_w, a_log, dt_bias, gdn_norm_g, w_a, sgu_ln_g, sgu_ln_b, sgu_w, sgu_b, w_b,
             w_mem_kv, w_c, w_o, ln1_g, ln1_b, router_w, router_bias, ws_gate, ws_up, ws_down, ln2_g, ln2_b):
    wi = w_in[l]
    d = D_MODEL
    bf = lambda a: a.astype(BF16)
    row = lambda a: a.reshape(1, -1).astype(F32)
    w_ab = jnp.zeros((d, 128), F32).at[:, :2 * GDN_HEADS].set(wi[:, _C_AB:_C_UV])
    prm = jnp.zeros((8, 128), F32).at[0, :GDN_HEADS].set(a_log[l]).at[1, :GDN_HEADS].set(dt_bias[l])
    sgu_bias = jnp.repeat(sgu_b[l].T, SGU_WIDTH // SGU_GROUPS, axis=1)
    return {
        "w_qkv": bf(wi[:, _C_QKV:_C_Z]), "w_z": bf(wi[:, _C_Z:_C_AB]), "w_ab": bf(w_ab),
        "w_uv": bf(wi[:, _C_UV:_C_XQ]), "w_xq": bf(wi[:, _C_XQ:_C_GATE]),
        "w_g0": bf(wi[:, _C_GATE:_C_GATE + d]), "w_g1": bf(wi[:, _C_GATE + d:_C_GATE + 2 * d]),
        "w_g2": bf(wi[:, _C_GATE + 2 * d:_C_GATE + 3 * d]),
        "conv_w": conv_w[l].astype(F32), "gdn_prm": prm, "gdn_norm_g": row(gdn_norm_g[l]),
        "w_a": bf(w_a[l]), "sgu_ln_g": row(sgu_ln_g[l]), "sgu_ln_b": row(sgu_ln_b[l]),
        "sgu_w": sgu_w[l].astype(F32), "sgu_bias": sgu_bias.astype(F32), "w_b": bf(w_b[l]),
        "w_mem_kv": bf(w_mem_kv[l]), "w_c": bf(w_c[l]), "w_o": bf(w_o[l]),
        "ln1_g": row(ln1_g[l]), "ln1_b": row(ln1_b[l]),
        "router_wt": bf(router_w[l].T), "router_bias": router_bias[l].reshape(-1, 1).astype(F32),
        "ws_gu": bf(jnp.concatenate([ws_gate[l], ws_up[l]], axis=1)), "ws_down": bf(ws_down[l]),
        "ln2_g": row(ln2_g[l]), "ln2_b": row(ln2_b[l]),
    }


def _layer(x, mem, wts, w_gate, w_up, w_down):
    bsz, s, d = x.shape
    t = bsz * s
    k, v = _mem_kv(mem, wts["w_mem_kv"])
    mbc = _mix_bc(x, wts, k, v)
    u, w, qd, kd, a, gl, z = _gdn_prep(x, wts)
    og = _gdn_scan(u, w, qd, kd, a, gl, z, wts["gdn_norm_g"])
    res, hp, idx, rw, rank, cnt = _merge(x, og, mbc, wts)
    nb = t * TOP_K // EXPERT_BLOCK + N_EXPERTS
    pos, meta = _plan(idx, rank, cnt)
    pos_tok = pos.T.reshape(-1)
    xs = _dispatch(hp, pos_tok, nb * EXPERT_BLOCK)
    y = _experts(xs, meta[:, 0], meta[:, 1], meta[:, 2], meta[:1, 3], w_gate, w_up, w_down)
    pos_slot = pos.reshape(TOP_K, t // COMBINE_TILE, COMBINE_TILE).transpose(1, 0, 2).reshape(-1)
    out = _combine(y, pos_slot, rw.T, res.reshape(t, d), wts["ln2_g"], wts["ln2_b"])
    return out.reshape(bsz, s, d)


def kernel(x, mem, w_in, conv_w, a_log, dt_bias, gdn_norm_g, w_a, sgu_ln_g, sgu_ln_b, sgu_w, sgu_b, w_b,
           w_mem_kv, w_c, w_o, ln1_g, ln1_b, router_w, router_bias, w_gate, w_up, w_down, ws_gate, ws_up,
           ws_down, ln2_g, ln2_b):
    assert x.shape[1] % TILE == 0 and x.shape[2] == D_MODEL
    for l in range(DEPTH):
        wts = _prepare(l, w_in, conv_w, a_log, dt_bias, gdn_norm_g, w_a, sgu_ln_g, sgu_ln_b, sgu_w, sgu_b,
                       w_b, w_mem_kv, w_c, w_o, ln1_g, ln1_b, router_w, router_bias, ws_gate, ws_up,
                       ws_down, ln2_g, ln2_b)
        x = _layer(x, mem, wts, w_gate[l], w_up[l], w_down[l])
    return x
```

```python
import functools

import jax
import jax.numpy as jnp
from jax import lax
from jax.experimental import pallas as pl
from jax.experimental.pallas import tpu as pltpu

F32 = jnp.float32
BF16 = jnp.bfloat16
I32 = jnp.int32
U32 = jnp.uint32

D_MODEL = 1024
DEPTH = 1
GDN_HEADS = 4
HEAD_DIM = 128
GDN_WIDTH = GDN_HEADS * HEAD_DIM
CONV_WIDTH = 4
SGU_GROUPS = 4
SGU_WIDTH = 512
SGU_CHUNK = 128
N_MEM = 256
XATTN_HEADS = 4
XATTN_WIDTH = 512
N_EXPERTS = 256
TOP_K = 8
N_GROUPS = 8
GROUP_SIZE = N_EXPERTS // N_GROUPS
TOPK_GROUPS = 4
D_EXPERT = 256
D_SHARED = 256
ROUTED_SCALE = 2.5
LN_EPS = 1e-5
RMS_EPS = 1e-6
DEEPNORM_ALPHA = (2.0 * DEPTH) ** 0.25

_C_QKV = 0
_C_Z = 3 * GDN_WIDTH
_C_AB = 4 * GDN_WIDTH
_C_UV = _C_AB + 2 * GDN_HEADS
_C_XQ = _C_UV + 2 * SGU_WIDTH
_C_GATE = _C_XQ + XATTN_WIDTH

CHUNK = 128
TILE = 256
HALO = 8
EXPERT_BLOCK = 256
X_SLOTS = 3
COMBINE_TILE = 128
VMEM_LIMIT = 56 * 1024 * 1024


def _dot(a, b):
    return jnp.dot(a, b, preferred_element_type=F32)


def _dot_nt(a, b):
    return lax.dot_general(a, b, (((1,), (1,)), ((), ())), preferred_element_type=F32)


def _split(a):
    hi = a.astype(BF16)
    lo = (a - hi.astype(F32)).astype(BF16)
    return hi, lo


def _dot3(a, b):
    ah, al = _split(a)
    bh, bl = _split(b)
    return _dot(ah, bh) + (_dot(ah, bl) + _dot(al, bh))


def _softplus(x):
    return jnp.maximum(x, 0.0) + jnp.log1p(jnp.exp(-jnp.abs(x)))


def _layer_norm(x, g, b):
    mu = jnp.mean(x, axis=-1, keepdims=True)
    xc = x - mu
    var = jnp.mean(xc * xc, axis=-1, keepdims=True)
    return xc * lax.rsqrt(var + LN_EPS) * g + b


def _const_spec(shape):
    nd = len(shape)
    return pl.BlockSpec(shape, lambda *_: (0,) * nd)


def _params(semantics):
    return pltpu.CompilerParams(dimension_semantics=semantics, vmem_limit_bytes=VMEM_LIMIT)


def _mem_kv_kernel(mem_ref, w_ref, k_ref, v_ref):
    kv = _dot(mem_ref[0].astype(BF16), w_ref[...])
    k_ref[0] = kv[:, :XATTN_WIDTH].astype(BF16)
    v_ref[0] = kv[:, XATTN_WIDTH:].astype(BF16)


def _mem_kv(mem, w_kv):
    bsz, n_mem, d = mem.shape
    out = jax.ShapeDtypeStruct((bsz, n_mem, XATTN_WIDTH), BF16)
    return pl.pallas_call(
        _mem_kv_kernel,
        grid=(bsz,),
        in_specs=[pl.BlockSpec((1, n_mem, d), lambda b: (b, 0, 0)), _const_spec(w_kv.shape)],
        out_specs=[pl.BlockSpec((1, n_mem, XATTN_WIDTH), lambda b: (b, 0, 0))] * 2,
        out_shape=[out, out],
        compiler_params=_params(("arbitrary",)),
        name="mem_kv",
    )(mem, w_kv)


def _mix_bc_kernel(x_ref, wuv_ref, wg1_ref, wb_ref, lng_ref, lnb_ref, sw_ref, sb_ref,
                   wxq_ref, k_ref, v_ref, wg2_ref, wc_ref, o_ref):
    xb = x_ref[0].astype(BF16)
    uv = jax.nn.gelu(_dot(xb, wuv_ref[...]), approximate=True)
    u = uv[:, :SGU_WIDTH]
    v = _layer_norm(uv[:, SGU_WIDTH:], lng_ref[...], lnb_ref[...])
    ii = lax.broadcasted_iota(I32, (SGU_CHUNK, SGU_CHUNK), 0)
    jj = lax.broadcasted_iota(I32, (SGU_CHUNK, SGU_CHUNK), 1)
    rows = []
    for n in range(TILE // SGU_CHUNK):
        cols = []
        for g in range(SGU_GROUPS):
            wt = jnp.where(ii >= jj, sw_ref[g], 0.0).astype(BF16)
            vg = v[n * SGU_CHUNK:(n + 1) * SGU_CHUNK, g * 128:(g + 1) * 128].astype(BF16)
            cols.append(_dot(wt, vg))
        rows.append(jnp.concatenate(cols, axis=1) + sb_ref[...])
    sg = jnp.concatenate(rows, axis=0)
    br_b = _dot((u * sg).astype(BF16), wb_ref[...])
    acc = jax.nn.sigmoid(_dot(xb, wg1_ref[...])) * br_b
    xq = _dot(xb, wxq_ref[...])
    heads = []
    for h in range(XATTN_HEADS):
        qh = xq[:, h * 128:(h + 1) * 128].astype(BF16)
        kh = k_ref[0, :, h * 128:(h + 1) * 128]
        vh = v_ref[0, :, h * 128:(h + 1) * 128]
        sc = _dot_nt(qh, kh) * (128 ** -0.5)
        p = jnp.exp(sc - jnp.max(sc, axis=-1, keepdims=True))
        p = p / jnp.sum(p, axis=-1, keepdims=True)
        heads.append(_dot(p.astype(BF16), vh))
    att = jnp.concatenate(heads, axis=1).astype(BF16)
    br_c = _dot(att, wc_ref[...])
    o_ref[0] = acc + jax.nn.sigmoid(_dot(xb, wg2_ref[...])) * br_c


def _mix_bc(x, wts, k, v):
    bsz, s, d = x.shape
    tok = pl.BlockSpec((1, TILE, d), lambda b, i: (b, i, 0))
    kvs = pl.BlockSpec((1, N_MEM, XATTN_WIDTH), lambda b, i: (b, 0, 0))
    names = ("w_uv", "w_g1", "w_b", "sgu_ln_g", "sgu_ln_b", "sgu_w", "sgu_bias", "w_xq")
    consts = [wts[n] for n in names]
    tail = [wts["w_g2"], wts["w_c"]]
    return pl.pallas_call(
        _mix_bc_kernel,
        grid=(bsz, s // TILE),
        in_specs=[tok] + [_const_spec(c.shape) for c in consts] + [kvs, kvs]
        + [_const_spec(c.shape) for c in tail],
        out_specs=tok,
        out_shape=jax.ShapeDtypeStruct((bsz, s, d), F32),
        compiler_params=_params(("arbitrary", "arbitrary")),
        name="mix_bc",
    )(x, *consts, k, v, *tail)


def _unit_lower_inverses(lmats, ii, jj):
    eye = (ii == jj).astype(F32)
    x = ii ^ jj
    lsplit = [_split(l) for l in lmats]
    ts = [eye - jnp.where(x < 2, l, 0.0) for l in lmats]
    zero = jnp.zeros((CHUNK, CHUNK), BF16)
    s = 2
    while s < CHUNK:
        off = (x < 2 * s) & ((ii & s) != 0) & ((jj & s) == 0)
        tsplit = [_split(t) for t in ts]
        prods = []
        for (th, tl), (lh, ll) in zip(tsplit, lsplit):
            oh = jnp.where(off, lh, zero)
            ol = jnp.where(off, ll, zero)
            prods.append(_dot(th, oh) + (_dot(th, ol) + _dot(tl, oh)))
        nxt = []
        for t, p, (th, tl) in zip(ts, prods, tsplit):
            ph, plo = _split(p)
            nxt.append(t - (_dot(ph, th) + (_dot(ph, tl) + _dot(plo, th))))
        ts = nxt
        s *= 2
    return ts


def _gdn_prep_kernel(x_ref, wqkv_ref, wz_ref, wab_ref, convw_ref, prm_ref,
                     u_ref, w_ref, qd_ref, kd_ref, a_ref, gl_ref, z_ref, hist_ref):
    @pl.when(pl.program_id(1) == 0)
    def _():
        hist_ref[0:HALO, :] = jnp.zeros((HALO, 3 * GDN_WIDTH), F32)

    xb = x_ref[0].astype(BF16)
    z_ref[0] = _dot(xb, wz_ref[...])
    qkv = _dot(xb, wqkv_ref[...])
    hist_ref[HALO:HALO + TILE, :] = qkv
    conv = qkv * convw_ref[CONV_WIDTH - 1:CONV_WIDTH, :]
    for j in range(CONV_WIDTH - 1):
        shift = CONV_WIDTH - 1 - j
        conv = conv + hist_ref[HALO - shift:HALO - shift + TILE, :] * convw_ref[j:j + 1, :]
    hist_ref[0:HALO, :] = qkv[TILE - HALO:, :]
    qkv = conv * jax.nn.sigmoid(conv)

    ab = _dot(xb, wab_ref[...])
    g = -jnp.exp(prm_ref[0:1, :]) * _softplus(ab + prm_ref[1:2, :])
    beta = jax.nn.sigmoid(ab)

    ii = lax.broadcasted_iota(I32, (CHUNK, CHUNK), 0)
    jj = lax.broadcasted_iota(I32, (CHUNK, CHUNK), 1)
    tri = (ii >= jj).astype(BF16)
    items = []
    for c in range(TILE // CHUNK):
        r0 = c * CHUNK
        gch = g[r0:r0 + CHUNK, :]
        g1 = gch.astype(BF16)
        r1 = gch - g1.astype(F32)
        g2 = r1.astype(BF16)
        g3 = (r1 - g2.astype(F32)).astype(BF16)
        gc = _dot(tri, g1) + (_dot(tri, g2) + _dot(tri, g3))
        gct = gc.T
        for h in range(GDN_HEADS):
            qh = qkv[r0:r0 + CHUNK, h * HEAD_DIM:(h + 1) * HEAD_DIM]
            kh = qkv[r0:r0 + CHUNK, GDN_WIDTH + h * HEAD_DIM:GDN_WIDTH + (h + 1) * HEAD_DIM]
            vh = qkv[r0:r0 + CHUNK, 2 * GDN_WIDTH + h * HEAD_DIM:2 * GDN_WIDTH + (h + 1) * HEAD_DIM]
            qn = qh * lax.rsqrt(jnp.sum(qh * qh, axis=-1, keepdims=True) + RMS_EPS) * (HEAD_DIM ** -0.5)
            kn = kh * lax.rsqrt(jnp.sum(kh * kh, axis=-1, keepdims=True) + RMS_EPS)
            gcol = gc[:, h:h + 1]
            grow = gct[h:h + 1, :]
            bcol = beta[r0:r0 + CHUNK, GDN_HEADS + h:GDN_HEADS + h + 1]
            glast = gc[CHUNK - 1:CHUNK, h:h + 1]
            decay = jnp.exp(jnp.where(ii >= jj, gcol - grow, -jnp.inf))
            knb = kn.astype(BF16)
            kk = _dot_nt(knb, knb)
            qk = _dot_nt(qn.astype(BF16), knb)
            egc = jnp.exp(gcol)
            items.append(dict(
                c=c, h=h, lmat=jnp.where(ii > jj, kk * bcol * decay, 0.0),
                rhs=jnp.concatenate([vh * bcol, kn * (bcol * egc)], axis=1),
                qd=(qn * egc).astype(BF16), kd=(kn * jnp.exp(glast - gcol)).astype(BF16),
                a=(qk * decay).astype(BF16), gl=jnp.broadcast_to(jnp.exp(glast), (8, HEAD_DIM))))
    tinvs = _unit_lower_inverses([it["lmat"] for it in items], ii, jj)
    sols = [_dot3(tinv, it["rhs"]) for tinv, it in zip(tinvs, items)]
    for it, sol in zip(items, sols):
        rows = slice(it["c"] * CHUNK, (it["c"] + 1) * CHUNK)
        lanes = slice(it["h"] * HEAD_DIM, (it["h"] + 1) * HEAD_DIM)
        u_ref[0, rows, lanes] = sol[:, :HEAD_DIM]
        w_ref[0, rows, lanes] = sol[:, HEAD_DIM:].astype(BF16)
        qd_ref[0, rows, lanes] = it["qd"]
        kd_ref[0, rows, lanes] = it["kd"]
        a_ref[0, rows, lanes] = it["a"]
        gl_ref[0, it["c"] * 8:(it["c"] + 1) * 8, lanes] = it["gl"]


def _gdn_prep(x, wts):
    bsz, s, d = x.shape
    tok = pl.BlockSpec((1, TILE, d), lambda b, i: (b, i, 0))
    hw = pl.BlockSpec((1, TILE, GDN_WIDTH), lambda b, i: (b, i, 0))
    gls = pl.BlockSpec((1, 8 * TILE // CHUNK, GDN_WIDTH), lambda b, i: (b, i, 0))
    consts = [wts[n] for n in ("w_qkv", "w_z", "w_ab", "conv_w", "gdn_prm")]
    f32o = jax.ShapeDtypeStruct((bsz, s, GDN_WIDTH), F32)
    b16o = jax.ShapeDtypeStruct((bsz, s, GDN_WIDTH), BF16)
    glo = jax.ShapeDtypeStruct((bsz, 8 * s // CHUNK, GDN_WIDTH), F32)
    return pl.pallas_call(
        _gdn_prep_kernel,
        grid=(bsz, s // TILE),
        in_specs=[tok] + [_const_spec(c.shape) for c in consts],
        out_specs=[hw, hw, hw, hw, hw, gls, hw],
        out_shape=[f32o, b16o, b16o, b16o, b16o, glo, f32o],
        scratch_shapes=[pltpu.VMEM((HALO + TILE, 3 * GDN_WIDTH), F32)],
        compiler_params=_params(("arbitrary", "arbitrary")),
        name="gdn_prep",
    )(x, *consts)


def _gdn_scan_kernel(u_ref, w_ref, qd_ref, kd_ref, a_ref, gl_ref, z_ref, gn_ref, o_ref, state_ref):
    @pl.when(pl.program_id(0) == 0)
    def _():
        state_ref[...] = jnp.zeros(state_ref.shape, F32)

    bsz = u_ref.shape[0]
    for c in range(TILE // CHUNK):
        rows = slice(c * CHUNK, (c + 1) * CHUNK)
        for b in range(bsz):
            for h in range(GDN_HEADS):
                lanes = slice(h * HEAD_DIM, (h + 1) * HEAD_DIM)
                st = state_ref[b, h]
                wq = jnp.concatenate([w_ref[b, rows, lanes], qd_ref[b, rows, lanes]], axis=0)
                ws = _dot(wq, st.astype(BF16))
                v_new = u_ref[b, rows, lanes] - ws[:CHUNK]
                vb = v_new.astype(BF16)
                o = ws[CHUNK:] + _dot(a_ref[b, rows, lanes], vb)
                kdt = kd_ref[b, rows, lanes].astype(F32).T.astype(BF16)
                state_ref[b, h] = st * gl_ref[b, c * 8:c * 8 + 1, lanes] + _dot(kdt, vb)
                zz = z_ref[b, rows, lanes]
                o = o * lax.rsqrt(jnp.mean(o * o, axis=-1, keepdims=True) + RMS_EPS) * gn_ref[...]
                o_ref[b, rows, lanes] = (o * (zz * jax.nn.sigmoid(zz))).astype(BF16)


def _gdn_scan(u, w, qd, kd, a, gl, z, gn):
    bsz, s, _ = u.shape
    hw = pl.BlockSpec((bsz, TILE, GDN_WIDTH), lambda i: (0, i, 0))
    gls = pl.BlockSpec((bsz, 8 * TILE // CHUNK, GDN_WIDTH), lambda i: (0, i, 0))
    return pl.pallas_call(
        _gdn_scan_kernel,
        grid=(s // TILE,),
        in_specs=[hw, hw, hw, hw, hw, gls, hw, _const_spec(gn.shape)],
        out_specs=hw,
        out_shape=jax.ShapeDtypeStruct((bsz, s, GDN_WIDTH), BF16),
        scratch_shapes=[pltpu.VMEM((bsz, GDN_HEADS, HEAD_DIM, HEAD_DIM), F32)],
        compiler_params=_params(("arbitrary",)),
        name="gdn_scan",
    )(u, w, qd, kd, a, gl, z, gn)


def _merge_kernel(x_ref, og_ref, mbc_ref, wa_ref, wg0_ref, wo_ref, ln1g_ref, ln1b_ref,
                  rwt_ref, rb_ref, wsgu_ref, wsd_ref,
                  res_ref, hp_ref, idx_ref, wt_ref, rank_ref, cnt_ref, carry_ref):
    first = (pl.program_id(0) == 0) & (pl.program_id(1) == 0)

    @pl.when(first)
    def _():
        carry_ref[...] = jnp.zeros(carry_ref.shape, F32)

    x = x_ref[0]
    xb = x.astype(BF16)
    br_a = _dot(og_ref[0], wa_ref[...])
    merged = jax.nn.sigmoid(_dot(xb, wg0_ref[...])) * br_a + mbc_ref[0]
    y = _dot(merged.astype(BF16), wo_ref[...])
    h = _layer_norm(DEEPNORM_ALPHA * x + y, ln1g_ref[...], ln1b_ref[...])
    hb = h.astype(BF16)
    hp_ref[...] = _pack_bf16_pairs(h)

    gu = _dot(hb, wsgu_ref[...])
    gt = gu[:, :D_SHARED]
    act = gt * jax.nn.sigmoid(gt) * gu[:, D_SHARED:]
    res_ref[0] = DEEPNORM_ALPHA * h + _dot(act.astype(BF16), wsd_ref[...])

    scores = jax.nn.sigmoid(_dot_nt(rwt_ref[...], hb))
    biased = scores + rb_ref[...]
    g3 = biased.reshape(N_GROUPS, GROUP_SIZE, TILE)
    m1 = jnp.max(g3, axis=1)
    m1b = m1[:, None, :]
    n_top = jnp.sum((g3 == m1b).astype(F32), axis=1)
    m2 = jnp.max(jnp.where(g3 < m1b, g3, -jnp.inf), axis=1)
    gs = m1 + jnp.where(n_top >= 2.0, m1, m2)
    gidx = lax.broadcasted_iota(I32, (N_GROUPS, TILE), 0)
    beaten = jnp.zeros((N_GROUPS, TILE), F32)
    for g in range(N_GROUPS):
        row = gs[g:g + 1, :]
        beaten = beaten + ((row > gs) | ((row == gs) & (g < gidx))).astype(F32)
    sel = (beaten < float(TOPK_GROUPS)).astype(F32)
    sel_e = jnp.broadcast_to(sel[:, None, :], (N_GROUPS, GROUP_SIZE, TILE)).reshape(N_EXPERTS, TILE)
    masked = jnp.where(sel_e > 0.5, biased, -jnp.inf)
    eidx = lax.broadcasted_iota(I32, (N_EXPERTS, TILE), 0)
    chosen = jnp.zeros((N_EXPERTS, TILE), F32)
    picks, pick_scores = [], []
    for _k in range(TOP_K):
        mx = jnp.max(masked, axis=0, keepdims=True)
        pick = jnp.min(jnp.where(masked == mx, eidx, N_EXPERTS), axis=0, keepdims=True)
        hot = eidx == pick
        picks.append(pick)
        pick_scores.append(jnp.sum(jnp.where(hot, scores, 0.0), axis=0, keepdims=True))
        chosen = jnp.where(hot, 1.0, chosen)
        masked = jnp.where(hot, -jnp.inf, masked)
    total = pick_scores[0]
    for sck in pick_scores[1:]:
        total = total + sck
    idx_ref[...] = jnp.concatenate(picks, axis=0)
    wt_ref[...] = jnp.concatenate([sck / total * ROUTED_SCALE for sck in pick_scores], axis=0)

    ti = lax.broadcasted_iota(I32, (TILE, TILE), 0)
    tj = lax.broadcasted_iota(I32, (TILE, TILE), 1)
    before = _dot(chosen.astype(BF16), (ti < tj).astype(BF16))
    carry = carry_ref[...]
    before = before + jnp.concatenate([carry] * (TILE // 128), axis=1)
    rank_ref[...] = jnp.concatenate(
        [jnp.sum(jnp.where(eidx == p, before, 0.0), axis=0, keepdims=True) for p in picks],
        axis=0).astype(I32)
    carry = carry + _dot(chosen.astype(BF16), jnp.ones((TILE, 128), BF16))
    carry_ref[...] = carry
    cnt_ref[...] = carry.astype(I32)


def _merge(x, og, mbc, wts):
    bsz, s, d = x.shape
    t = bsz * s
    nt = s // TILE
    tok = pl.BlockSpec((1, TILE, d), lambda b, i: (b, i, 0))
    ogs = pl.BlockSpec((1, TILE, GDN_WIDTH), lambda b, i: (b, i, 0))
    flat = lambda b, i: (b * nt + i, 0)
    lane = lambda b, i: (0, b * nt + i)
    consts = [wts[n] for n in ("w_a", "w_g0", "w_o", "ln1_g", "ln1_b", "router_wt", "router_bias",
                               "ws_gu", "ws_down")]
    return pl.pallas_call(
        _merge_kernel,
        grid=(bsz, nt),
        in_specs=[tok, ogs, tok] + [_const_spec(c.shape) for c in consts],
        out_specs=[tok,
                   pl.BlockSpec((TILE, d // 2), flat),
                   pl.BlockSpec((TOP_K, TILE), lane),
                   pl.BlockSpec((TOP_K, TILE), lane),
                   pl.BlockSpec((TOP_K, TILE), lane),
                   _const_spec((N_EXPERTS, 128))],
        out_shape=[jax.ShapeDtypeStruct((bsz, s, d), F32),
                   jax.ShapeDtypeStruct((t, d // 2), U32),
                   jax.ShapeDtypeStruct((TOP_K, t), I32),
                   jax.ShapeDtypeStruct((TOP_K, t), F32),
                   jax.ShapeDtypeStruct((TOP_K, t), I32),
                   jax.ShapeDtypeStruct((N_EXPERTS, 128), I32)],
        scratch_shapes=[pltpu.VMEM((N_EXPERTS, 128), F32)],
        compiler_params=_params(("arbitrary", "arbitrary")),
        name="merge",
    )(x, og, mbc, *consts)


def _plan_kernel(idx_ref, rank_ref, cnt_ref, pos_ref, meta_ref):
    e = N_EXPERTS
    shift = EXPERT_BLOCK.bit_length() - 1
    cnt = cnt_ref[...]
    nblk = lax.shift_right_logical(cnt + (EXPERT_BLOCK - 1), shift)
    ei = lax.broadcasted_iota(I32, (e, e), 0)
    ej = lax.broadcasted_iota(I32, (e, e), 1)
    incl = _dot((ei >= ej).astype(BF16), nblk.astype(F32).astype(BF16)).astype(I32)
    excl = incl - nblk
    start = (excl * EXPERT_BLOCK).astype(F32)
    lane = lax.broadcasted_iota(I32, (e, 128), 1)
    n_used = jnp.broadcast_to(incl[e - 1:e, :], (e, 128))
    meta_ref[...] = jnp.where(lane == 0, excl, jnp.where(lane == 1, nblk, jnp.where(lane == 2, cnt, n_used)))

    e128 = lax.broadcasted_iota(I32, (e, 128), 0)

    def body(i, carry):
        off = pl.multiple_of(i * 128, 128)
        ids = idx_ref[:, pl.ds(off, 128)]
        rows = [jnp.sum(jnp.where(e128 == ids[k:k + 1, :], start, 0.0), axis=0, keepdims=True)
                for k in range(TOP_K)]
        pos_ref[:, pl.ds(off, 128)] = jnp.concatenate(rows, axis=0).astype(I32) + rank_ref[:, pl.ds(off, 128)]
        return carry

    lax.fori_loop(0, idx_ref.shape[1] // 128, body, 0)


def _plan(idx, rank, cnt):
    t = idx.shape[1]
    return pl.pallas_call(
        _plan_kernel,
        grid=(1,),
        in_specs=[_const_spec(idx.shape), _const_spec(rank.shape), _const_spec(cnt.shape)],
        out_specs=[_const_spec((TOP_K, t)), _const_spec((N_EXPERTS, 128))],
        out_shape=[jax.ShapeDtypeStruct((TOP_K, t), I32), jax.ShapeDtypeStruct((N_EXPERTS, 128), I32)],
        compiler_params=_params(("arbitrary",)),
        name="plan",
    )(idx, rank, cnt)


def _row_copy(src_ref, src_row, dst_ref, dst_row, sem):
    return pltpu.make_async_copy(src_ref.at[pl.ds(src_row, 1), :], dst_ref.at[pl.ds(dst_row, 1), :], sem)


def _dispatch_kernel(pos_ref, hp_ref, xs_ref, sem):
    def copies(r):
        return [_row_copy(hp_ref, r, xs_ref, pos_ref[0, 0, r * TOP_K + k], sem) for k in range(TOP_K)]

    def start(r, carry):
        for k, cp in enumerate(copies(r)):
            cp.start(priority=k % 2)
        return carry

    lax.fori_loop(0, hp_ref.shape[0], start, 0)

    def wait(r, carry):
        for cp in copies(r):
            cp.wait()
        return carry

    lax.fori_loop(0, hp_ref.shape[0], wait, 0)


def _dispatch(hp, pos_tok, n_slots):
    t, half = hp.shape
    nt = t // TILE
    pos3 = pos_tok.reshape(nt, 1, TILE * TOP_K)
    return pl.pallas_call(
        _dispatch_kernel,
        grid=(nt,),
        in_specs=[pl.BlockSpec((1, 1, TILE * TOP_K), lambda i: (i, 0, 0), memory_space=pltpu.SMEM),
                  pl.BlockSpec((TILE, half), lambda i: (i, 0))],
        out_specs=pl.BlockSpec(memory_space=pl.ANY),
        out_shape=jax.ShapeDtypeStruct((n_slots, half), U32),
        scratch_shapes=[pltpu.SemaphoreType.DMA(())],
        compiler_params=_params(("arbitrary",)),
        name="dispatch",
    )(pos3, hp)


def _pack_bf16_pairs(a):
    n = a.shape[1] // 2
    bits = lax.bitcast_convert_type(a.astype(BF16).astype(F32), U32)
    return (bits[:, :n] >> 16) | (bits[:, n:] & jnp.uint32(0xFFFF0000))


def _unpack_bf16_pairs(words):
    lo = lax.bitcast_convert_type(words << 16, F32)
    hi = lax.bitcast_convert_type(words & jnp.uint32(0xFFFF0000), F32)
    return jnp.concatenate([lo, hi], axis=1)


def _experts_kernel(fb_ref, nb_ref, cnt_ref, nu_ref, xs_ref, wg_ref, wu_ref, wd_ref, y_ref,
                    wgu_s, wd_s, xbuf, ybuf, xsem, ysem):
    e = pl.program_id(0)
    n_used = nu_ref[0]
    bm = EXPERT_BLOCK
    row_queue = 1

    def x_copy(g, slot):
        return pltpu.make_async_copy(xs_ref.at[pl.ds(g * bm, bm), :], xbuf.at[slot], xsem.at[slot])

    def y_copy(g, slot):
        return pltpu.make_async_copy(ybuf.at[slot], y_ref.at[pl.ds(g * bm, bm), :], ysem.at[slot])

    @pl.when(e == 0)
    def _():
        for g0 in range(X_SLOTS - 1):
            @pl.when(g0 < n_used)
            def _():
                x_copy(g0, g0).start(priority=row_queue)

    nb = nb_ref[e]

    @pl.when(nb > 0)
    def _():
        wgu_s[:, :D_EXPERT] = wg_ref[0].astype(BF16)
        wgu_s[:, D_EXPERT:] = wu_ref[0].astype(BF16)
        wd_s[...] = wd_ref[0].astype(BF16)

    def block(b, carry):
        g = fb_ref[e] + b
        xslot = lax.rem(g, X_SLOTS)
        slot = g & 1
        x_copy(g, xslot).wait()
        ahead = g + (X_SLOTS - 1)

        @pl.when(ahead < n_used)
        def _():
            x_copy(ahead, lax.rem(ahead, X_SLOTS)).start(priority=row_queue)

        x = _unpack_bf16_pairs(xbuf[xslot])
        live = lax.broadcasted_iota(I32, (bm, 1), 0) < cnt_ref[e] - b * bm
        xb = jnp.where(live, x, 0.0).astype(BF16)
        gu = _dot(xb, wgu_s[...])
        gt = gu[:, :D_EXPERT]
        act = (gt * jax.nn.sigmoid(gt) * gu[:, D_EXPERT:]).astype(BF16)
        y = _dot(act, wd_s[...])

        @pl.when(g >= 2)
        def _():
            y_copy(g - 2, slot).wait()

        ybuf[slot] = _pack_bf16_pairs(y)
        y_copy(g, slot).start(priority=row_queue)
        return carry

    lax.fori_loop(0, nb, block, 0)

    @pl.when(e == pl.num_programs(0) - 1)
    def _():
        @pl.when(n_used >= 2)
        def _():
            y_copy(n_used - 2, n_used & 1).wait()

        y_copy(n_used - 1, (n_used - 1) & 1).wait()


def _experts(xs, first_block, n_blocks, counts, n_used, w_gate, w_up, w_down):
    n_slots, half = xs.shape
    d = 2 * half
    wsel = lambda e, *_: (e, 0, 0)
    grid_spec = pltpu.PrefetchScalarGridSpec(
        num_scalar_prefetch=4,
        grid=(N_EXPERTS,),
        in_specs=[pl.BlockSpec(memory_space=pl.ANY),
                  pl.BlockSpec((1, d, D_EXPERT), wsel),
                  pl.BlockSpec((1, d, D_EXPERT), wsel),
                  pl.BlockSpec((1, D_EXPERT, d), wsel)],
        out_specs=pl.BlockSpec(memory_space=pl.ANY),
        scratch_shapes=[pltpu.VMEM((d, 2 * D_EXPERT), BF16), pltpu.VMEM((D_EXPERT, d), BF16),
                        pltpu.VMEM((X_SLOTS, EXPERT_BLOCK, half), U32), pltpu.VMEM((2, EXPERT_BLOCK, half), U32),
                        pltpu.SemaphoreType.DMA((X_SLOTS,)), pltpu.SemaphoreType.DMA((2,))],
    )
    return pl.pallas_call(
        _experts_kernel,
        grid_spec=grid_spec,
        out_shape=jax.ShapeDtypeStruct((n_slots, half), U32),
        compiler_params=_params(("arbitrary",)),
        name="experts",
    )(first_block, n_blocks, counts, n_used, xs, w_gate, w_up, w_down)


def _combine_kernel(pos_ref, nxt_ref, y_ref, wt_ref, res_ref, g_ref, b_ref, o_ref, buf_ref, sem):
    i = pl.program_id(0)
    n = COMBINE_TILE * TOP_K

    def issue(p_ref, slot):
        def start(j8, carry):
            for q in range(8):
                src = y_ref.at[pl.ds(p_ref[0, 0, j8 * 8 + q], 1), :]
                pltpu.make_async_copy(src, buf_ref.at[slot, j8, pl.ds(q, 1), :], sem.at[slot]).start(priority=q % 2)
            return carry

        lax.fori_loop(0, n // 8, start, 0)

    @pl.when(i == 0)
    def _():
        issue(pos_ref, 0)

    @pl.when(i + 1 < pl.num_programs(0))
    def _():
        issue(nxt_ref, (i + 1) & 1)

    slot = i & 1

    def wait(j8, carry):
        for q in range(8):
            pltpu.make_async_copy(y_ref.at[pl.ds(0, 1), :], buf_ref.at[slot, j8, pl.ds(q, 1), :],
                                  sem.at[slot]).wait()
        return carry

    lax.fori_loop(0, n // 8, wait, 0)

    acc = res_ref[...]
    per_k = COMBINE_TILE // 8
    for k in range(TOP_K):
        rows = buf_ref[slot, k * per_k:(k + 1) * per_k].reshape(COMBINE_TILE, buf_ref.shape[3])
        acc = acc + _unpack_bf16_pairs(rows) * wt_ref[:, k:k + 1]
    o_ref[...] = _layer_norm(acc, g_ref[...], b_ref[...])


def _combine(y, pos_slot, wts_tok, res, ln_g, ln_b):
    t, d = res.shape
    nt = t // COMBINE_TILE
    n = COMBINE_TILE * TOP_K
    pos3 = pos_slot.reshape(nt, 1, n)
    tok = pl.BlockSpec((COMBINE_TILE, d), lambda i: (i, 0))
    return pl.pallas_call(
        _combine_kernel,
        grid=(nt,),
        in_specs=[pl.BlockSpec((1, 1, n), lambda i: (i, 0, 0), memory_space=pltpu.SMEM),
                  pl.BlockSpec((1, 1, n), lambda i: (jnp.minimum(i + 1, nt - 1), 0, 0), memory_space=pltpu.SMEM),
                  pl.BlockSpec(memory_space=pl.ANY),
                  pl.BlockSpec((COMBINE_TILE, TOP_K), lambda i: (i, 0)),
                  tok, _const_spec(ln_g.shape), _const_spec(ln_b.shape)],
        out_specs=tok,
        out_shape=jax.ShapeDtypeStruct((t, d), F32),
        scratch_shapes=[pltpu.VMEM((2, n // 8, 8, d // 2), U32), pltpu.SemaphoreType.DMA((2,))],
        compiler_params=_params(("arbitrary",)),
        name="combine",
    )(pos3, pos3, y, wts_tok, res, ln_g, ln_b)


def _prepare(l, w_in, conv_w, a_log, dt_bias, gdn_norm_g, w_a, sgu_ln_g, sgu_ln_b, sgu_w, sgu_b, w_b,
             w_mem_kv, w_c, w_o, ln1_g, ln1_b, router_w, router_bias, ws_gate, ws_up, ws_down, ln2_g, ln2_b):
    wi = w_in[l]
    d = D_MODEL
    bf = lambda a: a.astype(BF16)
    row = lambda a: a.reshape(1, -1).astype(F32)
    w_ab = jnp.zeros((d, 128), F32).at[:, :2 * GDN_HEADS].set(wi[:, _C_AB:_C_UV])
    prm = jnp.zeros((8, 128), F32).at[0, :GDN_HEADS].set(a_log[l]).at[1, :GDN_HEADS].set(dt_bias[l])
    sgu_bias = jnp.repeat(sgu_b[l].T, SGU_WIDTH // SGU_GROUPS, axis=1)
    return {
        "w_qkv": bf(wi[:, _C_QKV:_C_Z]), "w_z": bf(wi[:, _C_Z:_C_AB]), "w_ab": bf(w_ab),
        "w_uv": bf(wi[:, _C_UV:_C_XQ]), "w_xq": bf(wi[:, _C_XQ:_C_GATE]),
        "w_g0": bf(wi[:, _C_GATE:_C_GATE + d]), "w_g1": bf(wi[:, _C_GATE + d:_C_GATE + 2 * d]),
        "w_g2": bf(wi[:, _C_GATE + 2 * d:_C_GATE + 3 * d]),
        "conv_w": conv_w[l].astype(F32), "gdn_prm": prm, "gdn_norm_g": row(gdn_norm_g[l]),
        "w_a": bf(w_a[l]), "sgu_ln_g": row(sgu_ln_g[l]), "sgu_ln_b": row(sgu_ln_b[l]),
        "sgu_w": sgu_w[l].astype(F32), "sgu_bias": sgu_bias.astype(F32), "w_b": bf(w_b[l]),
        "w_mem_kv": bf(w_mem_kv[l]), "w_c": bf(w_c[l]), "w_o": bf(w_o[l]),
        "ln1_g": row(ln1_g[l]), "ln1_b": row(ln1_b[l]),
        "router_wt": bf(router_w[l].T), "router_bias": router_bias[l].reshape(-1, 1).astype(F32),
        "ws_gu": bf(jnp.concatenate([ws_gate[l], ws_up[l]], axis=1)), "ws_down": bf(ws_down[l]),
        "ln2_g": row(ln2_g[l]), "ln2_b": row(ln2_b[l]),
    }


def _layer(x, mem, wts, w_gate, w_up, w_down):
    bsz, s, d = x.shape
    t = bsz * s
    k, v = _mem_kv(mem, wts["w_mem_kv"])
    mbc = _mix_bc(x, wts, k, v)
    u, w, qd, kd, a, gl, z = _gdn_prep(x, wts)
    og = _gdn_scan(u, w, qd, kd, a, gl, z, wts["gdn_norm_g"])
    res, hp, idx, rw, rank, cnt = _merge(x, og, mbc, wts)
    nb = t * TOP_K // EXPERT_BLOCK + N_EXPERTS
    pos, meta = _plan(idx, rank, cnt)
    pos_tok = pos.T.reshape(-1)
    xs = _dispatch(hp, pos_tok, nb * EXPERT_BLOCK)
    y = _experts(xs, meta[:, 0], meta[:, 1], meta[:, 2], meta[:1, 3], w_gate, w_up, w_down)
    pos_slot = pos.reshape(TOP_K, t // COMBINE_TILE, COMBINE_TILE).transpose(1, 0, 2).reshape(-1)
    out = _combine(y, pos_slot, rw.T, res.reshape(t, d), wts["ln2_g"], wts["ln2_b"])
    return out.reshape(bsz, s, d)


def kernel(x, mem, w_in, conv_w, a_log, dt_bias, gdn_norm_g, w_a, sgu_ln_g, sgu_ln_b, sgu_w, sgu_b, w_b,
           w_mem_kv, w_c, w_o, ln1_g, ln1_b, router_w, router_bias, w_gate, w_up, w_down, ws_gate, ws_up,
           ws_down, ln2_g, ln2_b):
    assert x.shape[1] % TILE == 0 and x.shape[2] == D_MODEL
    for l in range(DEPTH):
        wts = _prepare(l, w_in, conv_w, a_log, dt_bias, gdn_norm_g, w_a, sgu_ln_g, sgu_ln_b, sgu_w, sgu_b,
                       w_b, w_mem_kv, w_c, w_o, ln1_g, ln1_b, router_w, router_bias, ws_gate, ws_up,
                       ws_down, ln2_g, ln2_b)
        x = _layer(x, mem, wts, w_gate[l], w_up[l], w_down[l])
    return x
```

```python
import functools

import jax
import jax.numpy as jnp
from jax import lax
from jax.experimental import pallas as pl
from jax.experimental.pallas import tpu as pltpu
from jax.experimental.pallas import tpu_sc as plsc

F32 = jnp.float32
BF16 = jnp.bfloat16
I32 = jnp.int32
U32 = jnp.uint32

D_MODEL = 1024
DEPTH = 1
GDN_HEADS = 4
HEAD_DIM = 128
GDN_WIDTH = GDN_HEADS * HEAD_DIM
CONV_WIDTH = 4
SGU_GROUPS = 4
SGU_WIDTH = 512
SGU_CHUNK = 128
N_MEM = 256
XATTN_HEADS = 4
XATTN_WIDTH = 512
N_EXPERTS = 256
TOP_K = 8
N_GROUPS = 8
GROUP_SIZE = N_EXPERTS // N_GROUPS
TOPK_GROUPS = 4
D_EXPERT = 256
D_SHARED = 256
ROUTED_SCALE = 2.5
LN_EPS = 1e-5
RMS_EPS = 1e-6
DEEPNORM_ALPHA = (2.0 * DEPTH) ** 0.25

_C_QKV = 0
_C_Z = 3 * GDN_WIDTH
_C_AB = 4 * GDN_WIDTH
_C_UV = _C_AB + 2 * GDN_HEADS
_C_XQ = _C_UV + 2 * SGU_WIDTH
_C_GATE = _C_XQ + XATTN_WIDTH

CHUNK = 128
TILE = 256
HALO = 8
EXPERT_BLOCK = 256
X_SLOTS = 3
SC_CORES = 2
SC_SUBCORES = 16
SC_WINDOW = 64
COMBINE_TILE = 256
VMEM_LIMIT = 56 * 1024 * 1024


def _dot(a, b):
    return jnp.dot(a, b, preferred_element_type=F32)


def _dot_nt(a, b):
    return lax.dot_general(a, b, (((1,), (1,)), ((), ())), preferred_element_type=F32)


def _split(a):
    hi = a.astype(BF16)
    lo = (a - hi.astype(F32)).astype(BF16)
    return hi, lo


def _dot3(a, b):
    ah, al = _split(a)
    bh, bl = _split(b)
    return _dot(ah, bh) + (_dot(ah, bl) + _dot(al, bh))


def _softplus(x):
    return jnp.maximum(x, 0.0) + jnp.log1p(jnp.exp(-jnp.abs(x)))


def _layer_norm(x, g, b):
    mu = jnp.mean(x, axis=-1, keepdims=True)
    xc = x - mu
    var = jnp.mean(xc * xc, axis=-1, keepdims=True)
    return xc * lax.rsqrt(var + LN_EPS) * g + b


def _const_spec(shape):
    nd = len(shape)
    return pl.BlockSpec(shape, lambda *_: (0,) * nd)


def _params(semantics):
    return pltpu.CompilerParams(dimension_semantics=semantics, vmem_limit_bytes=VMEM_LIMIT)


def _mem_kv_kernel(mem_ref, w_ref, k_ref, v_ref):
    kv = _dot(mem_ref[0].astype(BF16), w_ref[...])
    k_ref[0] = kv[:, :XATTN_WIDTH].astype(BF16)
    v_ref[0] = kv[:, XATTN_WIDTH:].astype(BF16)


def _mem_kv(mem, w_kv):
    bsz, n_mem, d = mem.shape
    out = jax.ShapeDtypeStruct((bsz, n_mem, XATTN_WIDTH), BF16)
    return pl.pallas_call(
        _mem_kv_kernel,
        grid=(bsz,),
        in_specs=[pl.BlockSpec((1, n_mem, d), lambda b: (b, 0, 0)), _const_spec(w_kv.shape)],
        out_specs=[pl.BlockSpec((1, n_mem, XATTN_WIDTH), lambda b: (b, 0, 0))] * 2,
        out_shape=[out, out],
        compiler_params=_params(("arbitrary",)),
        name="mem_kv",
    )(mem, w_kv)


def _mix_bc_kernel(x_ref, wuv_ref, wg1_ref, wb_ref, lng_ref, lnb_ref, sw_ref, sb_ref,
                   wxq_ref, k_ref, v_ref, wg2_ref, wc_ref, o_ref):
    xb = x_ref[0].astype(BF16)
    uv = jax.nn.gelu(_dot(xb, wuv_ref[...]), approximate=True)
    u = uv[:, :SGU_WIDTH]
    v = _layer_norm(uv[:, SGU_WIDTH:], lng_ref[...], lnb_ref[...])
    ii = lax.broadcasted_iota(I32, (SGU_CHUNK, SGU_CHUNK), 0)
    jj = lax.broadcasted_iota(I32, (SGU_CHUNK, SGU_CHUNK), 1)
    rows = []
    for n in range(TILE // SGU_CHUNK):
        cols = []
        for g in range(SGU_GROUPS):
            wt = jnp.where(ii >= jj, sw_ref[g], 0.0).astype(BF16)
            vg = v[n * SGU_CHUNK:(n + 1) * SGU_CHUNK, g * 128:(g + 1) * 128].astype(BF16)
            cols.append(_dot(wt, vg))
        rows.append(jnp.concatenate(cols, axis=1) + sb_ref[...])
    sg = jnp.concatenate(rows, axis=0)
    br_b = _dot((u * sg).astype(BF16), wb_ref[...])
    acc = jax.nn.sigmoid(_dot(xb, wg1_ref[...])) * br_b
    xq = _dot(xb, wxq_ref[...])
    heads = []
    for h in range(XATTN_HEADS):
        qh = xq[:, h * 128:(h + 1) * 128].astype(BF16)
        kh = k_ref[0, :, h * 128:(h + 1) * 128]
        vh = v_ref[0, :, h * 128:(h + 1) * 128]
        sc = _dot_nt(qh, kh) * (128 ** -0.5)
        p = jnp.exp(sc - jnp.max(sc, axis=-1, keepdims=True))
        p = p / jnp.sum(p, axis=-1, keepdims=True)
        heads.append(_dot(p.astype(BF16), vh))
    att = jnp.concatenate(heads, axis=1).astype(BF16)
    br_c = _dot(att, wc_ref[...])
    o_ref[0] = acc + jax.nn.sigmoid(_dot(xb, wg2_ref[...])) * br_c


def _mix_bc(x, wts, k, v):
    bsz, s, d = x.shape
    tok = pl.BlockSpec((1, TILE, d), lambda b, i: (b, i, 0))
    kvs = pl.BlockSpec((1, N_MEM, XATTN_WIDTH), lambda b, i: (b, 0, 0))
    names = ("w_uv", "w_g1", "w_b", "sgu_ln_g", "sgu_ln_b", "sgu_w", "sgu_bias", "w_xq")
    consts = [wts[n] for n in names]
    tail = [wts["w_g2"], wts["w_c"]]
    return pl.pallas_call(
        _mix_bc_kernel,
        grid=(bsz, s // TILE),
        in_specs=[tok] + [_const_spec(c.shape) for c in consts] + [kvs, kvs]
        + [_const_spec(c.shape) for c in tail],
        out_specs=tok,
        out_shape=jax.ShapeDtypeStruct((bsz, s, d), F32),
        compiler_params=_params(("arbitrary", "arbitrary")),
        name="mix_bc",
    )(x, *consts, k, v, *tail)


def _unit_lower_inverses(lmats, ii, jj):
    eye = (ii == jj).astype(F32)
    x = ii ^ jj
    lsplit = [_split(l) for l in lmats]
    ts = [eye - jnp.where(x < 2, l, 0.0) for l in lmats]
    zero = jnp.zeros((CHUNK, CHUNK), BF16)
    s = 2
    while s < CHUNK:
        off = (x < 2 * s) & ((ii & s) != 0) & ((jj & s) == 0)
        tsplit = [_split(t) for t in ts]
        prods = []
        for (th, tl), (lh, ll) in zip(tsplit, lsplit):
            oh = jnp.where(off, lh, zero)
            ol = jnp.where(off, ll, zero)
            prods.append(_dot(th, oh) + (_dot(th, ol) + _dot(tl, oh)))
        nxt = []
        for t, p, (th, tl) in zip(ts, prods, tsplit):
            ph, plo = _split(p)
            nxt.append(t - (_dot(ph, th) + (_dot(ph, tl) + _dot(plo, th))))
        ts = nxt
        s *= 2
    return ts


def _gdn_prep_kernel(x_ref, wqkv_ref, wz_ref, wab_ref, convw_ref, prm_ref,
                     u_ref, w_ref, qd_ref, kd_ref, a_ref, gl_ref, z_ref, hist_ref):
    @pl.when(pl.program_id(1) == 0)
    def _():
        hist_ref[0:HALO, :] = jnp.zeros((HALO, 3 * GDN_WIDTH), F32)

    xb = x_ref[0].astype(BF16)
    z_ref[0] = _dot(xb, wz_ref[...])
    qkv = _dot(xb, wqkv_ref[...])
    hist_ref[HALO:HALO + TILE, :] = qkv
    conv = qkv * convw_ref[CONV_WIDTH - 1:CONV_WIDTH, :]
    for j in range(CONV_WIDTH - 1):
        shift = CONV_WIDTH - 1 - j
        conv = conv + hist_ref[HALO - shift:HALO - shift + TILE, :] * convw_ref[j:j + 1, :]
    hist_ref[0:HALO, :] = qkv[TILE - HALO:, :]
    qkv = conv * jax.nn.sigmoid(conv)

    ab = _dot(xb, wab_ref[...])
    g = -jnp.exp(prm_ref[0:1, :]) * _softplus(ab + prm_ref[1:2, :])
    beta = jax.nn.sigmoid(ab)

    ii = lax.broadcasted_iota(I32, (CHUNK, CHUNK), 0)
    jj = lax.broadcasted_iota(I32, (CHUNK, CHUNK), 1)
    tri = (ii >= jj).astype(BF16)
    items = []
    for c in range(TILE // CHUNK):
        r0 = c * CHUNK
        gch = g[r0:r0 + CHUNK, :]
        g1 = gch.astype(BF16)
        r1 = gch - g1.astype(F32)
        g2 = r1.astype(BF16)
        g3 = (r1 - g2.astype(F32)).astype(BF16)
        gc = _dot(tri, g1) + (_dot(tri, g2) + _dot(tri, g3))
        gct = gc.T
        for h in range(GDN_HEADS):
            qh = qkv[r0:r0 + CHUNK, h * HEAD_DIM:(h + 1) * HEAD_DIM]
            kh = qkv[r0:r0 + CHUNK, GDN_WIDTH + h * HEAD_DIM:GDN_WIDTH + (h + 1) * HEAD_DIM]
            vh = qkv[r0:r0 + CHUNK, 2 * GDN_WIDTH + h * HEAD_DIM:2 * GDN_WIDTH + (h + 1) * HEAD_DIM]
            qn = qh * lax.rsqrt(jnp.sum(qh * qh, axis=-1, keepdims=True) + RMS_EPS) * (HEAD_DIM ** -0.5)
            kn = kh * lax.rsqrt(jnp.sum(kh * kh, axis=-1, keepdims=True) + RMS_EPS)
            gcol = gc[:, h:h + 1]
            grow = gct[h:h + 1, :]
            bcol = beta[r0:r0 + CHUNK, GDN_HEADS + h:GDN_HEADS + h + 1]
            glast = gc[CHUNK - 1:CHUNK, h:h + 1]
            decay = jnp.exp(jnp.where(ii >= jj, gcol - grow, -jnp.inf))
            knb = kn.astype(BF16)
            kk = _dot_nt(knb, knb)
            qk = _dot_nt(qn.astype(BF16), knb)
            egc = jnp.exp(gcol)
            items.append(dict(
                c=c, h=h, lmat=jnp.where(ii > jj, kk * bcol * decay, 0.0),
                rhs=jnp.concatenate([vh * bcol, kn * (bcol * egc)], axis=1),
                qd=(qn * egc).astype(BF16), kd=(kn * jnp.exp(glast - gcol)).astype(BF16),
                a=(qk * decay).astype(BF16), gl=jnp.broadcast_to(jnp.exp(glast), (8, HEAD_DIM))))
    tinvs = _unit_lower_inverses([it["lmat"] for it in items], ii, jj)
    sols = [_dot3(tinv, it["rhs"]) for tinv, it in zip(tinvs, items)]
    for it, sol in zip(items, sols):
        rows = slice(it["c"] * CHUNK, (it["c"] + 1) * CHUNK)
        lanes = slice(it["h"] * HEAD_DIM, (it["h"] + 1) * HEAD_DIM)
        u_ref[0, rows, lanes] = sol[:, :HEAD_DIM]
        w_ref[0, rows, lanes] = sol[:, HEAD_DIM:].astype(BF16)
        qd_ref[0, rows, lanes] = it["qd"]
        kd_ref[0, rows, lanes] = it["kd"]
        a_ref[0, rows, lanes] = it["a"]
        gl_ref[0, it["c"] * 8:(it["c"] + 1) * 8, lanes] = it["gl"]


def _gdn_prep(x, wts):
    bsz, s, d = x.shape
    tok = pl.BlockSpec((1, TILE, d), lambda b, i: (b, i, 0))
    hw = pl.BlockSpec((1, TILE, GDN_WIDTH), lambda b, i: (b, i, 0))
    gls = pl.BlockSpec((1, 8 * TILE // CHUNK, GDN_WIDTH), lambda b, i: (b, i, 0))
    consts = [wts[n] for n in ("w_qkv", "w_z", "w_ab", "conv_w", "gdn_prm")]
    f32o = jax.ShapeDtypeStruct((bsz, s, GDN_WIDTH), F32)
    b16o = jax.ShapeDtypeStruct((bsz, s, GDN_WIDTH), BF16)
    glo = jax.ShapeDtypeStruct((bsz, 8 * s // CHUNK, GDN_WIDTH), F32)
    return pl.pallas_call(
        _gdn_prep_kernel,
        grid=(bsz, s // TILE),
        in_specs=[tok] + [_const_spec(c.shape) for c in consts],
        out_specs=[hw, hw, hw, hw, hw, gls, hw],
        out_shape=[f32o, b16o, b16o, b16o, b16o, glo, f32o],
        scratch_shapes=[pltpu.VMEM((HALO + TILE, 3 * GDN_WIDTH), F32)],
        compiler_params=_params(("arbitrary", "arbitrary")),
        name="gdn_prep",
    )(x, *consts)


def _gdn_scan_kernel(u_ref, w_ref, qd_ref, kd_ref, a_ref, gl_ref, z_ref, gn_ref, o_ref, state_ref):
    @pl.when(pl.program_id(0) == 0)
    def _():
        state_ref[...] = jnp.zeros(state_ref.shape, F32)

    bsz = u_ref.shape[0]
    for c in range(TILE // CHUNK):
        rows = slice(c * CHUNK, (c + 1) * CHUNK)
        for b in range(bsz):
            for h in range(GDN_HEADS):
                lanes = slice(h * HEAD_DIM, (h + 1) * HEAD_DIM)
                st = state_ref[b, h]
                wq = jnp.concatenate([w_ref[b, rows, lanes], qd_ref[b, rows, lanes]], axis=0)
                ws = _dot(wq, st.astype(BF16))
                v_new = u_ref[b, rows, lanes] - ws[:CHUNK]
                vb = v_new.astype(BF16)
                o = ws[CHUNK:] + _dot(a_ref[b, rows, lanes], vb)
                kdt = kd_ref[b, rows, lanes].astype(F32).T.astype(BF16)
                state_ref[b, h] = st * gl_ref[b, c * 8:c * 8 + 1, lanes] + _dot(kdt, vb)
                zz = z_ref[b, rows, lanes]
                o = o * lax.rsqrt(jnp.mean(o * o, axis=-1, keepdims=True) + RMS_EPS) * gn_ref[...]
                o_ref[b, rows, lanes] = (o * (zz * jax.nn.sigmoid(zz))).astype(BF16)


def _gdn_scan(u, w, qd, kd, a, gl, z, gn):
    bsz, s, _ = u.shape
    hw = pl.BlockSpec((bsz, TILE, GDN_WIDTH), lambda i: (0, i, 0))
    gls = pl.BlockSpec((bsz, 8 * TILE // CHUNK, GDN_WIDTH), lambda i: (0, i, 0))
    return pl.pallas_call(
        _gdn_scan_kernel,
        grid=(s // TILE,),
        in_specs=[hw, hw, hw, hw, hw, gls, hw, _const_spec(gn.shape)],
        out_specs=hw,
        out_shape=jax.ShapeDtypeStruct((bsz, s, GDN_WIDTH), BF16),
        scratch_shapes=[pltpu.VMEM((bsz, GDN_HEADS, HEAD_DIM, HEAD_DIM), F32)],
        compiler_params=_params(("arbitrary",)),
        name="gdn_scan",
    )(u, w, qd, kd, a, gl, z, gn)


def _merge_kernel(x_ref, og_ref, mbc_ref, wa_ref, wg0_ref, wo_ref, ln1g_ref, ln1b_ref,
                  rwt_ref, rb_ref, wsgu_ref, wsd_ref,
                  res_ref, hp_ref, idx_ref, wt_ref, rank_ref, cnt_ref, carry_ref):
    first = (pl.program_id(0) == 0) & (pl.program_id(1) == 0)

    @pl.when(first)
    def _():
        carry_ref[...] = jnp.zeros(carry_ref.shape, F32)

    x = x_ref[0]
    xb = x.astype(BF16)
    br_a = _dot(og_ref[0], wa_ref[...])
    merged = jax.nn.sigmoid(_dot(xb, wg0_ref[...])) * br_a + mbc_ref[0]
    y = _dot(merged.astype(BF16), wo_ref[...])
    h = _layer_norm(DEEPNORM_ALPHA * x + y, ln1g_ref[...], ln1b_ref[...])
    hb = h.astype(BF16)
    hp_ref[...] = _pack_bf16_pairs(h)

    gu = _dot(hb, wsgu_ref[...])
    gt = gu[:, :D_SHARED]
    act = gt * jax.nn.sigmoid(gt) * gu[:, D_SHARED:]
    res_ref[0] = DEEPNORM_ALPHA * h + _dot(act.astype(BF16), wsd_ref[...])

    scores = jax.nn.sigmoid(_dot_nt(rwt_ref[...], hb))
    biased = scores + rb_ref[...]
    g3 = biased.reshape(N_GROUPS, GROUP_SIZE, TILE)
    m1 = jnp.max(g3, axis=1)
    m1b = m1[:, None, :]
    n_top = jnp.sum((g3 == m1b).astype(F32), axis=1)
    m2 = jnp.max(jnp.where(g3 < m1b, g3, -jnp.inf), axis=1)
    gs = m1 + jnp.where(n_top >= 2.0, m1, m2)
    gidx = lax.broadcasted_iota(I32, (N_GROUPS, TILE), 0)
    beaten = jnp.zeros((N_GROUPS, TILE), F32)
    for g in range(N_GROUPS):
        row = gs[g:g + 1, :]
        beaten = beaten + ((row > gs) | ((row == gs) & (g < gidx))).astype(F32)
    sel = (beaten < float(TOPK_GROUPS)).astype(F32)
    sel_e = jnp.broadcast_to(sel[:, None, :], (N_GROUPS, GROUP_SIZE, TILE)).reshape(N_EXPERTS, TILE)
    masked = jnp.where(sel_e > 0.5, biased, -jnp.inf)
    eidx = lax.broadcasted_iota(I32, (N_EXPERTS, TILE), 0)
    chosen = jnp.zeros((N_EXPERTS, TILE), F32)
    picks, pick_scores = [], []
    for _k in range(TOP_K):
        mx = jnp.max(masked, axis=0, keepdims=True)
        pick = jnp.min(jnp.where(masked == mx, eidx, N_EXPERTS), axis=0, keepdims=True)
        hot = eidx == pick
        picks.append(pick)
        pick_scores.append(jnp.sum(jnp.where(hot, scores, 0.0), axis=0, keepdims=True))
        chosen = jnp.where(hot, 1.0, chosen)
        masked = jnp.where(hot, -jnp.inf, masked)
    total = pick_scores[0]
    for sck in pick_scores[1:]:
        total = total + sck
    idx_ref[...] = jnp.concatenate(picks, axis=0)
    wt_ref[...] = jnp.concatenate([sck / total * ROUTED_SCALE for sck in pick_scores], axis=0)

    ti = lax.broadcasted_iota(I32, (TILE, TILE), 0)
    tj = lax.broadcasted_iota(I32, (TILE, TILE), 1)
    before = _dot(chosen.astype(BF16), (ti < tj).astype(BF16))
    carry = carry_ref[...]
    before = before + jnp.concatenate([carry] * (TILE // 128), axis=1)
    rank_ref[...] = jnp.concatenate(
        [jnp.sum(jnp.where(eidx == p, before, 0.0), axis=0, keepdims=True) for p in picks],
        axis=0).astype(I32)
    carry = carry + _dot(chosen.astype(BF16), jnp.ones((TILE, 128), BF16))
    carry_ref[...] = carry
    cnt_ref[...] = carry.astype(I32)


def _merge(x, og, mbc, wts):
    bsz, s, d = x.shape
    t = bsz * s
    nt = s // TILE
    tok = pl.BlockSpec((1, TILE, d), lambda b, i: (b, i, 0))
    ogs = pl.BlockSpec((1, TILE, GDN_WIDTH), lambda b, i: (b, i, 0))
    flat = lambda b, i: (b * nt + i, 0)
    lane = lambda b, i: (0, b * nt + i)
    consts = [wts[n] for n in ("w_a", "w_g0", "w_o", "ln1_g", "ln1_b", "router_wt", "router_bias",
                               "ws_gu", "ws_down")]
    return pl.pallas_call(
        _merge_kernel,
        grid=(bsz, nt),
        in_specs=[tok, ogs, tok] + [_const_spec(c.shape) for c in consts],
        out_specs=[tok,
                   pl.BlockSpec((TILE, d // 2), flat),
                   pl.BlockSpec((TOP_K, TILE), lane),
                   pl.BlockSpec((TOP_K, TILE), lane),
                   pl.BlockSpec((TOP_K, TILE), lane),
                   _const_spec((N_EXPERTS, 128))],
        out_shape=[jax.ShapeDtypeStruct((bsz, s, d), F32),
                   jax.ShapeDtypeStruct((t, d // 2), U32),
                   jax.ShapeDtypeStruct((TOP_K, t), I32),
                   jax.ShapeDtypeStruct((TOP_K, t), F32),
                   jax.ShapeDtypeStruct((TOP_K, t), I32),
                   jax.ShapeDtypeStruct((N_EXPERTS, 128), I32)],
        scratch_shapes=[pltpu.VMEM((N_EXPERTS, 128), F32)],
        compiler_params=_params(("arbitrary", "arbitrary")),
        name="merge",
    )(x, og, mbc, *consts)


def _plan_kernel(idx_ref, rank_ref, cnt_ref, pos_ref, meta_ref):
    e = N_EXPERTS
    shift = EXPERT_BLOCK.bit_length() - 1
    cnt = cnt_ref[...]
    nblk = lax.shift_right_logical(cnt + (EXPERT_BLOCK - 1), shift)
    ei = lax.broadcasted_iota(I32, (e, e), 0)
    ej = lax.broadcasted_iota(I32, (e, e), 1)
    incl = _dot((ei >= ej).astype(BF16), nblk.astype(F32).astype(BF16)).astype(I32)
    excl = incl - nblk
    start = (excl * EXPERT_BLOCK).astype(F32)
    lane = lax.broadcasted_iota(I32, (e, 128), 1)
    n_used = jnp.broadcast_to(incl[e - 1:e, :], (e, 128))
    meta_ref[...] = jnp.where(lane == 0, excl, jnp.where(lane == 1, nblk, jnp.where(lane == 2, cnt, n_used)))

    e128 = lax.broadcasted_iota(I32, (e, 128), 0)

    def body(i, carry):
        off = pl.multiple_of(i * 128, 128)
        ids = idx_ref[:, pl.ds(off, 128)]
        rows = [jnp.sum(jnp.where(e128 == ids[k:k + 1, :], start, 0.0), axis=0, keepdims=True)
                for k in range(TOP_K)]
        pos_ref[:, pl.ds(off, 128)] = jnp.concatenate(rows, axis=0).astype(I32) + rank_ref[:, pl.ds(off, 128)]
        return carry

    lax.fori_loop(0, idx_ref.shape[1] // 128, body, 0)


def _plan(idx, rank, cnt):
    t = idx.shape[1]
    return pl.pallas_call(
        _plan_kernel,
        grid=(1,),
        in_specs=[_const_spec(idx.shape), _const_spec(rank.shape), _const_spec(cnt.shape)],
        out_specs=[_const_spec((TOP_K, t)), _const_spec((N_EXPERTS, 128))],
        out_shape=[jax.ShapeDtypeStruct((TOP_K, t), I32), jax.ShapeDtypeStruct((N_EXPERTS, 128), I32)],
        compiler_params=_params(("arbitrary",)),
        name="plan",
    )(idx, rank, cnt)


def _row_copy(src_ref, src_row, dst_ref, dst_row, sem):
    return pltpu.make_async_copy(src_ref.at[pl.ds(src_row, 1), :], dst_ref.at[pl.ds(dst_row, 1), :], sem)


def _dispatch_kernel(pos_ref, hp_ref, xs_ref, sem):
    def copies(r):
        return [_row_copy(hp_ref, r, xs_ref, pos_ref[0, 0, r * TOP_K + k], sem) for k in range(TOP_K)]

    def start(r, carry):
        for k, cp in enumerate(copies(r)):
            cp.start(priority=k % 2)
        return carry

    lax.fori_loop(0, hp_ref.shape[0], start, 0)

    def wait(r, carry):
        for cp in copies(r):
            cp.wait()
        return carry

    lax.fori_loop(0, hp_ref.shape[0], wait, 0)


def _dispatch(hp, pos_tok, n_slots):
    t, half = hp.shape
    nt = t // TILE
    pos3 = pos_tok.reshape(nt, 1, TILE * TOP_K)
    return pl.pallas_call(
        _dispatch_kernel,
        grid=(nt,),
        in_specs=[pl.BlockSpec((1, 1, TILE * TOP_K), lambda i: (i, 0, 0), memory_space=pltpu.SMEM),
                  pl.BlockSpec((TILE, half), lambda i: (i, 0))],
        out_specs=pl.BlockSpec(memory_space=pl.ANY),
        out_shape=jax.ShapeDtypeStruct((n_slots, half), U32),
        scratch_shapes=[pltpu.SemaphoreType.DMA(())],
        compiler_params=_params(("arbitrary",)),
        name="dispatch",
    )(pos3, hp)


def _pack_bf16_pairs(a):
    n = a.shape[1] // 2
    bits = lax.bitcast_convert_type(a.astype(BF16).astype(F32), U32)
    return (bits[:, :n] >> 16) | (bits[:, n:] & jnp.uint32(0xFFFF0000))


def _unpack_bf16_pairs(words):
    lo = lax.bitcast_convert_type(words << 16, F32)
    hi = lax.bitcast_convert_type(words & jnp.uint32(0xFFFF0000), F32)
    return jnp.concatenate([lo, hi], axis=1)


def _experts_kernel(fb_ref, nb_ref, cnt_ref, nu_ref, xs_ref, wg_ref, wu_ref, wd_ref, y_ref,
                    wgu_s, wd_s, xbuf, ybuf, xsem, ysem):
    e = pl.program_id(0)
    n_used = nu_ref[0]
    bm = EXPERT_BLOCK
    row_queue = 1

    def x_copy(g, slot):
        return pltpu.make_async_copy(xs_ref.at[pl.ds(g * bm, bm), :], xbuf.at[slot], xsem.at[slot])

    def y_copy(g, slot):
        return pltpu.make_async_copy(ybuf.at[slot], y_ref.at[pl.ds(g * bm, bm), :], ysem.at[slot])

    @pl.when(e == 0)
    def _():
        for g0 in range(X_SLOTS - 1):
            @pl.when(g0 < n_used)
            def _():
                x_copy(g0, g0).start(priority=row_queue)

    nb = nb_ref[e]

    @pl.when(nb > 0)
    def _():
        wgu_s[:, :D_EXPERT] = wg_ref[0].astype(BF16)
        wgu_s[:, D_EXPERT:] = wu_ref[0].astype(BF16)
        wd_s[...] = wd_ref[0].astype(BF16)

    def block(b, carry):
        g = fb_ref[e] + b
        xslot = lax.rem(g, X_SLOTS)
        slot = g & 1
        x_copy(g, xslot).wait()
        ahead = g + (X_SLOTS - 1)

        @pl.when(ahead < n_used)
        def _():
            x_copy(ahead, lax.rem(ahead, X_SLOTS)).start(priority=row_queue)

        x = _unpack_bf16_pairs(xbuf[xslot])
        live = lax.broadcasted_iota(I32, (bm, 1), 0) < cnt_ref[e] - b * bm
        xb = jnp.where(live, x, 0.0).astype(BF16)
        gu = _dot(xb, wgu_s[...])
        gt = gu[:, :D_EXPERT]
        act = (gt * jax.nn.sigmoid(gt) * gu[:, D_EXPERT:]).astype(BF16)
        y = _dot(act, wd_s[...])

        @pl.when(g >= 2)
        def _():
            y_copy(g - 2, slot).wait()

        ybuf[slot] = _pack_bf16_pairs(y)
        y_copy(g, slot).start(priority=row_queue)
        return carry

    lax.fori_loop(0, nb, block, 0)

    @pl.when(e == pl.num_programs(0) - 1)
    def _():
        @pl.when(n_used >= 2)
        def _():
            y_copy(n_used - 2, n_used & 1).wait()

        y_copy(n_used - 1, (n_used - 1) & 1).wait()


def _experts(xs, first_block, n_blocks, counts, n_used, w_gate, w_up, w_down):
    n_slots, half = xs.shape
    d = 2 * half
    wsel = lambda e, *_: (e, 0, 0)
    grid_spec = pltpu.PrefetchScalarGridSpec(
        num_scalar_prefetch=4,
        grid=(N_EXPERTS,),
        in_specs=[pl.BlockSpec(memory_space=pl.ANY),
                  pl.BlockSpec((1, d, D_EXPERT), wsel),
                  pl.BlockSpec((1, d, D_EXPERT), wsel),
                  pl.BlockSpec((1, D_EXPERT, d), wsel)],
        out_specs=pl.BlockSpec(memory_space=pl.ANY),
        scratch_shapes=[pltpu.VMEM((d, 2 * D_EXPERT), BF16), pltpu.VMEM((D_EXPERT, d), BF16),
                        pltpu.VMEM((X_SLOTS, EXPERT_BLOCK, half), U32), pltpu.VMEM((2, EXPERT_BLOCK, half), U32),
                        pltpu.SemaphoreType.DMA((X_SLOTS,)), pltpu.SemaphoreType.DMA((2,))],
    )
    return pl.pallas_call(
        _experts_kernel,
        grid_spec=grid_spec,
        out_shape=jax.ShapeDtypeStruct((n_slots, half), U32),
        compiler_params=_params(("arbitrary",)),
        name="experts",
    )(first_block, n_blocks, counts, n_used, xs, w_gate, w_up, w_down)


def _sc_gather_rows(table, indices):
    n = indices.shape[0]
    d = table.shape[1]
    workers = SC_CORES * SC_SUBCORES
    per_w = n // workers
    nch = per_w // SC_WINDOW
    assert per_w * workers == n and nch * SC_WINDOW == per_w and nch % 2 == 0
    mesh = plsc.VectorSubcoreMesh(core_axis_name="core", subcore_axis_name="subcore",
                                  num_cores=SC_CORES, num_subcores=SC_SUBCORES)

    @functools.partial(
        pl.kernel, out_type=jax.ShapeDtypeStruct((n, d), table.dtype), mesh=mesh, name="sc_gather_rows",
        scratch_types=[pltpu.VMEM((per_w,), I32), pltpu.VMEM((2, SC_WINDOW, d), table.dtype),
                       pltpu.SemaphoreType.DMA((2,)), pltpu.SemaphoreType.DMA((2,))])
    def gather(table_hbm, idx_hbm, out_hbm, idx_v, rows_v, gsem, wsem):
        base = (lax.axis_index("subcore") * SC_CORES + lax.axis_index("core")) * per_w
        pltpu.sync_copy(idx_hbm.at[pl.ds(base, per_w)], idx_v)

        def fetch(c, slot):
            return pltpu.make_async_copy(table_hbm.at[idx_v.at[pl.ds(c * SC_WINDOW, SC_WINDOW)]],
                                         rows_v.at[slot], gsem.at[slot])

        def put(c, slot):
            return pltpu.make_async_copy(rows_v.at[slot], out_hbm.at[pl.ds(base + c * SC_WINDOW, SC_WINDOW)],
                                         wsem.at[slot])

        fetch(0, 0).start()

        @pl.loop(0, nch, step=2)
        def _(c0):
            for b in range(2):
                c = c0 + b
                fetch(c, b).wait()

                @pl.when(c + 1 < nch)
                def _():
                    @pl.when(c >= 1)
                    def _():
                        put(c - 1, 1 - b).wait()

                    fetch(c + 1, 1 - b).start()

                put(c, b).start()

        put(nch - 2, 0).wait()
        put(nch - 1, 1).wait()

    return gather(table, indices)


def _combine_kernel(yt_ref, wt_ref, res_ref, g_ref, b_ref, o_ref):
    acc = res_ref[...]
    for k in range(TOP_K):
        acc = acc + _unpack_bf16_pairs(yt_ref[k]) * wt_ref[:, k:k + 1]
    o_ref[...] = _layer_norm(acc, g_ref[...], b_ref[...])


def _combine(y_tok, wts_tok, res, ln_g, ln_b):
    t, d = res.shape
    tok = pl.BlockSpec((COMBINE_TILE, d), lambda i: (i, 0))
    return pl.pallas_call(
        _combine_kernel,
        grid=(t // COMBINE_TILE,),
        in_specs=[pl.BlockSpec((TOP_K, COMBINE_TILE, d // 2), lambda i: (0, i, 0)),
                  pl.BlockSpec((COMBINE_TILE, TOP_K), lambda i: (i, 0)),
                  tok, _const_spec(ln_g.shape), _const_spec(ln_b.shape)],
        out_specs=tok,
        out_shape=jax.ShapeDtypeStruct((t, d), F32),
        compiler_params=_params(("arbitrary",)),
        name="combine",
    )(y_tok, wts_tok, res, ln_g, ln_b)


def _prepare(l, w_in, conv_w, a_log, dt_bias, gdn_norm_g, w_a, sgu_ln_g, sgu_ln_b, sgu_w, sgu_b, w_b,
             w_mem_kv, w_c, w_o, ln1_g, ln1_b, router_w, router_bias, ws_gate, ws_up, ws_down, ln2_g, ln2_b):
    wi = w_in[l]
    d = D_MODEL
    bf = lambda a: a.astype(BF16)
    row = lambda a: a.reshape(1, -1).astype(F32)
    w_ab = jnp.zeros((d, 128), F32).at[:, :2 * GDN_HEADS].set(wi[:, _C_AB:_C_UV])
    prm = jnp.zeros((8, 128), F32).at[0, :GDN_HEADS].set(a_log[l]).at[1, :GDN_HEADS].set(dt_bias[l])
    sgu_bias = jnp.repeat(sgu_b[l].T, SGU_WIDTH // SGU_GROUPS, axis=1)
    return {
        "w_qkv": bf(wi[:, _C_QKV:_C_Z]), "w_z": bf(wi[:, _C_Z:_C_AB]), "w_ab": bf(w_ab),
        "w_uv": bf(wi[:, _C_UV:_C_XQ]), "w_xq": bf(wi[:, _C_XQ:_C_GATE]),
        "w_g0": bf(wi[:, _C_GATE:_C_GATE + d]), "w_g1": bf(wi[:, _C_GATE + d:_C_GATE + 2 * d]),
        "w_g2": bf(wi[:, _C_GATE + 2 * d:_C_GATE + 3 * d]),
        "conv_w": conv_w[l].astype(F32), "gdn_prm": prm, "gdn_norm_g": row(gdn_norm_g[l]),
        "w_a": bf(w_a[l]), "sgu_ln_g": row(sgu_ln_g[l]), "sgu_ln_b": row(sgu_ln_b[l]),
        "sgu_w": sgu_w[l].astype(F32), "sgu_bias": sgu_bias.astype(F32), "w_b": bf(w_b[l]),
        "w_mem_kv": bf(w_mem_kv[l]), "w_c": bf(w_c[l]), "w_o": bf(w_o[l]),
        "ln1_g": row(ln1_g[l]), "ln1_b": row(ln1_b[l]),
        "router_wt": bf(router_w[l].T), "router_bias": router_bias[l].reshape(-1, 1).astype(F32),
        "ws_gu": bf(jnp.concatenate([ws_gate[l], ws_up[l]], axis=1)), "ws_down": bf(ws_down[l]),
        "ln2_g": row(ln2_g[l]), "ln2_b": row(ln2_b[l]),
    }


def _layer(x, mem, wts, w_gate, w_up, w_down):
    bsz, s, d = x.shape
    t = bsz * s
    k, v = _mem_kv(mem, wts["w_mem_kv"])
    mbc = _mix_bc(x, wts, k, v)
    u, w, qd, kd, a, gl, z = _gdn_prep(x, wts)
    og = _gdn_scan(u, w, qd, kd, a, gl, z, wts["gdn_norm_g"])
    res, hp, idx, rw, rank, cnt = _merge(x, og, mbc, wts)
    nb = t * TOP_K // EXPERT_BLOCK + N_EXPERTS
    pos, meta = _plan(idx, rank, cnt)
    pos_tok = pos.T.reshape(-1)
    xs = _dispatch(hp, pos_tok, nb * EXPERT_BLOCK)
    y = _experts(xs, meta[:, 0], meta[:, 1], meta[:, 2], meta[:1, 3], w_gate, w_up, w_down)
    y_tok = _sc_gather_rows(lax.bitcast_convert_type(y, I32), pos.reshape(-1))
    y_tok = lax.bitcast_convert_type(y_tok, U32).reshape(TOP_K, t, d // 2)
    out = _combine(y_tok, rw.T, res.reshape(t, d), wts["ln2_g"], wts["ln2_b"])
    return out.reshape(bsz, s, d)


def kernel(x, mem, w_in, conv_w, a_log, dt_bias, gdn_norm_g, w_a, sgu_ln_g, sgu_ln_b, sgu_w, sgu_b, w_b,
           w_mem_kv, w_c, w_o, ln1_g, ln1_b, router_w, router_bias, w_gate, w_up, w_down, ws_gate, ws_up,
           ws_down, ln2_g, ln2_b):
    assert x.shape[1] % TILE == 0 and x.shape[2] == D_MODEL
    for l in range(DEPTH):
        wts = _prepare(l, w_in, conv_w, a_log, dt_bias, gdn_norm_g, w_a, sgu_ln_g, sgu_ln_b, sgu_w, sgu_b,
                       w_b, w_mem_kv, w_c, w_o, ln1_g, ln1_b, router_w, router_bias, ws_gate, ws_up,
                       ws_down, ln2_g, ln2_b)
        x = _layer(x, mem, wts, w_gate[l], w_up[l], w_down[l])
    return x
```

```python
import functools

import jax
import jax.numpy as jnp
from jax import lax
from jax.experimental import pallas as pl
from jax.experimental.pallas import tpu as pltpu
from jax.experimental.pallas import tpu_sc as plsc

F32 = jnp.float32
BF16 = jnp.bfloat16
I32 = jnp.int32
U32 = jnp.uint32

D_MODEL = 1024
DEPTH = 1
GDN_HEADS = 4
HEAD_DIM = 128
GDN_WIDTH = GDN_HEADS * HEAD_DIM
CONV_WIDTH = 4
SGU_GROUPS = 4
SGU_WIDTH = 512
SGU_CHUNK = 128
N_MEM = 256
XATTN_HEADS = 4
XATTN_WIDTH = 512
N_EXPERTS = 256
TOP_K = 8
N_GROUPS = 8
GROUP_SIZE = N_EXPERTS // N_GROUPS
TOPK_GROUPS = 4
D_EXPERT = 256
D_SHARED = 256
ROUTED_SCALE = 2.5
LN_EPS = 1e-5
RMS_EPS = 1e-6
DEEPNORM_ALPHA = (2.0 * DEPTH) ** 0.25

_C_QKV = 0
_C_Z = 3 * GDN_WIDTH
_C_AB = 4 * GDN_WIDTH
_C_UV = _C_AB + 2 * GDN_HEADS
_C_XQ = _C_UV + 2 * SGU_WIDTH
_C_GATE = _C_XQ + XATTN_WIDTH

CHUNK = 128
TILE = 256
HALO = 8
EXPERT_BLOCK = 256
X_SLOTS = 3
SC_CORES = 2
SC_SUBCORES = 16
SC_WINDOW = 64
COMBINE_TILE = 256
VMEM_LIMIT = 56 * 1024 * 1024


def _dot(a, b):
    return jnp.dot(a, b, preferred_element_type=F32)


def _dot_nt(a, b):
    return lax.dot_general(a, b, (((1,), (1,)), ((), ())), preferred_element_type=F32)


def _split(a):
    hi = a.astype(BF16)
    lo = (a - hi.astype(F32)).astype(BF16)
    return hi, lo


def _dot3(a, b):
    ah, al = _split(a)
    bh, bl = _split(b)
    return _dot(ah, bh) + (_dot(ah, bl) + _dot(al, bh))


def _softplus(x):
    return jnp.maximum(x, 0.0) + jnp.log1p(jnp.exp(-jnp.abs(x)))


def _layer_norm(x, g, b):
    mu = jnp.mean(x, axis=-1, keepdims=True)
    xc = x - mu
    var = jnp.mean(xc * xc, axis=-1, keepdims=True)
    return xc * lax.rsqrt(var + LN_EPS) * g + b


def _const_spec(shape):
    nd = len(shape)
    return pl.BlockSpec(shape, lambda *_: (0,) * nd)


def _params(semantics):
    return pltpu.CompilerParams(dimension_semantics=semantics, vmem_limit_bytes=VMEM_LIMIT)


def _mem_kv_kernel(mem_ref, w_ref, k_ref, v_ref):
    kv = _dot(mem_ref[0].astype(BF16), w_ref[...])
    k_ref[0] = kv[:, :XATTN_WIDTH].astype(BF16)
    v_ref[0] = kv[:, XATTN_WIDTH:].astype(BF16)


def _mem_kv(mem, w_kv):
    bsz, n_mem, d = mem.shape
    out = jax.ShapeDtypeStruct((bsz, n_mem, XATTN_WIDTH), BF16)
    return pl.pallas_call(
        _mem_kv_kernel,
        grid=(bsz,),
        in_specs=[pl.BlockSpec((1, n_mem, d), lambda b: (b, 0, 0)), _const_spec(w_kv.shape)],
        out_specs=[pl.BlockSpec((1, n_mem, XATTN_WIDTH), lambda b: (b, 0, 0))] * 2,
        out_shape=[out, out],
        compiler_params=_params(("arbitrary",)),
        name="mem_kv",
    )(mem, w_kv)


def _mix_bc_kernel(x_ref, wuv_ref, wg1_ref, wb_ref, lng_ref, lnb_ref, sw_ref, sb_ref,
                   wxq_ref, k_ref, v_ref, wg2_ref, wc_ref, o_ref):
    xb = x_ref[0].astype(BF16)
    uv = jax.nn.gelu(_dot(xb, wuv_ref[...]), approximate=True)
    u = uv[:, :SGU_WIDTH]
    v = _layer_norm(uv[:, SGU_WIDTH:], lng_ref[...], lnb_ref[...])
    ii = lax.broadcasted_iota(I32, (SGU_CHUNK, SGU_CHUNK), 0)
    jj = lax.broadcasted_iota(I32, (SGU_CHUNK, SGU_CHUNK), 1)
    rows = []
    for n in range(TILE // SGU_CHUNK):
        cols = []
        for g in range(SGU_GROUPS):
            wt = jnp.where(ii >= jj, sw_ref[g], 0.0).astype(BF16)
            vg = v[n * SGU_CHUNK:(n + 1) * SGU_CHUNK, g * 128:(g + 1) * 128].astype(BF16)
            cols.append(_dot(wt, vg))
        rows.append(jnp.concatenate(cols, axis=1) + sb_ref[...])
    sg = jnp.concatenate(rows, axis=0)
    br_b = _dot((u * sg).astype(BF16), wb_ref[...])
    acc = jax.nn.sigmoid(_dot(xb, wg1_ref[...])) * br_b
    xq = _dot(xb, wxq_ref[...])
    heads = []
    for h in range(XATTN_HEADS):
        qh = xq[:, h * 128:(h + 1) * 128].astype(BF16)
        kh = k_ref[0, :, h * 128:(h + 1) * 128]
        vh = v_ref[0, :, h * 128:(h + 1) * 128]
        sc = _dot_nt(qh, kh) * (128 ** -0.5)
        p = jnp.exp(sc - jnp.max(sc, axis=-1, keepdims=True))
        p = p / jnp.sum(p, axis=-1, keepdims=True)
        heads.append(_dot(p.astype(BF16), vh))
    att = jnp.concatenate(heads, axis=1).astype(BF16)
    br_c = _dot(att, wc_ref[...])
    o_ref[0] = acc + jax.nn.sigmoid(_dot(xb, wg2_ref[...])) * br_c


def _mix_bc(x, wts, k, v):
    bsz, s, d = x.shape
    tok = pl.BlockSpec((1, TILE, d), lambda b, i: (b, i, 0))
    kvs = pl.BlockSpec((1, N_MEM, XATTN_WIDTH), lambda b, i: (b, 0, 0))
    names = ("w_uv", "w_g1", "w_b", "sgu_ln_g", "sgu_ln_b", "sgu_w", "sgu_bias", "w_xq")
    consts = [wts[n] for n in names]
    tail = [wts["w_g2"], wts["w_c"]]
    return pl.pallas_call(
        _mix_bc_kernel,
        grid=(bsz, s // TILE),
        in_specs=[tok] + [_const_spec(c.shape) for c in consts] + [kvs, kvs]
        + [_const_spec(c.shape) for c in tail],
        out_specs=tok,
        out_shape=jax.ShapeDtypeStruct((bsz, s, d), F32),
        compiler_params=_params(("arbitrary", "arbitrary")),
        name="mix_bc",
    )(x, *consts, k, v, *tail)


def _unit_lower_inverses(lmats, ii, jj):
    eye = (ii == jj).astype(F32)
    x = ii ^ jj
    lsplit = [_split(l) for l in lmats]
    ts = [eye - jnp.where(x < 2, l, 0.0) for l in lmats]
    zero = jnp.zeros((CHUNK, CHUNK), BF16)
    s = 2
    while s < CHUNK:
        off = (x < 2 * s) & ((ii & s) != 0) & ((jj & s) == 0)
        tsplit = [_split(t) for t in ts]
        prods = []
        for (th, tl), (lh, ll) in zip(tsplit, lsplit):
            oh = jnp.where(off, lh, zero)
            ol = jnp.where(off, ll, zero)
            prods.append(_dot(th, oh) + (_dot(th, ol) + _dot(tl, oh)))
        nxt = []
        for t, p, (th, tl) in zip(ts, prods, tsplit):
            ph, plo = _split(p)
            nxt.append(t - (_dot(ph, th) + (_dot(ph, tl) + _dot(plo, th))))
        ts = nxt
        s *= 2
    return ts


def _gdn_prep_kernel(x_ref, wqkv_ref, wz_ref, wab_ref, convw_ref, prm_ref,
                     u_ref, w_ref, qd_ref, kd_ref, a_ref, gl_ref, z_ref, hist_ref):
    @pl.when(pl.program_id(1) == 0)
    def _():
        hist_ref[0:HALO, :] = jnp.zeros((HALO, 3 * GDN_WIDTH), F32)

    xb = x_ref[0].astype(BF16)
    z_ref[0] = _dot(xb, wz_ref[...])
    qkv = _dot(xb, wqkv_ref[...])
    hist_ref[HALO:HALO + TILE, :] = qkv
    conv = qkv * convw_ref[CONV_WIDTH - 1:CONV_WIDTH, :]
    for j in range(CONV_WIDTH - 1):
        shift = CONV_WIDTH - 1 - j
        conv = conv + hist_ref[HALO - shift:HALO - shift + TILE, :] * convw_ref[j:j + 1, :]
    hist_ref[0:HALO, :] = qkv[TILE - HALO:, :]
    qkv = conv * jax.nn.sigmoid(conv)

    ab = _dot(xb, wab_ref[...])
    g = -jnp.exp(prm_ref[0:1, :]) * _softplus(ab + prm_ref[1:2, :])
    beta = jax.nn.sigmoid(ab)

    ii = lax.broadcasted_iota(I32, (CHUNK, CHUNK), 0)
    jj = lax.broadcasted_iota(I32, (CHUNK, CHUNK), 1)
    tri = (ii >= jj).astype(BF16)
    items = []
    for c in range(TILE // CHUNK):
        r0 = c * CHUNK
        gch = g[r0:r0 + CHUNK, :]
        g1 = gch.astype(BF16)
        r1 = gch - g1.astype(F32)
        g2 = r1.astype(BF16)
        g3 = (r1 - g2.astype(F32)).astype(BF16)
        gc = _dot(tri, g1) + (_dot(tri, g2) + _dot(tri, g3))
        gct = gc.T
        for h in range(GDN_HEADS):
            qh = qkv[r0:r0 + CHUNK, h * HEAD_DIM:(h + 1) * HEAD_DIM]
            kh = qkv[r0:r0 + CHUNK, GDN_WIDTH + h * HEAD_DIM:GDN_WIDTH + (h + 1) * HEAD_DIM]
            vh = qkv[r0:r0 + CHUNK, 2 * GDN_WIDTH + h * HEAD_DIM:2 * GDN_WIDTH + (h + 1) * HEAD_DIM]
            qn = qh * lax.rsqrt(jnp.sum(qh * qh, axis=-1, keepdims=True) + RMS_EPS) * (HEAD_DIM ** -0.5)
            kn = kh * lax.rsqrt(jnp.sum(kh * kh, axis=-1, keepdims=True) + RMS_EPS)
            gcol = gc[:, h:h + 1]
            grow = gct[h:h + 1, :]
            bcol = beta[r0:r0 + CHUNK, GDN_HEADS + h:GDN_HEADS + h + 1]
            glast = gc[CHUNK - 1:CHUNK, h:h + 1]
            decay = jnp.exp(jnp.where(ii >= jj, gcol - grow, -jnp.inf))
            knb = kn.astype(BF16)
            kk = _dot_nt(knb, knb)
            qk = _dot_nt(qn.astype(BF16), knb)
            egc = jnp.exp(gcol)
            items.append(dict(
                c=c, h=h, lmat=jnp.where(ii > jj, kk * bcol * decay, 0.0),
                rhs=jnp.concatenate([vh * bcol, kn * (bcol * egc)], axis=1),
                qd=(qn * egc).astype(BF16), kd=(kn * jnp.exp(glast - gcol)).astype(BF16),
                a=(qk * decay).astype(BF16), gl=jnp.broadcast_to(jnp.exp(glast), (8, HEAD_DIM))))
    tinvs = _unit_lower_inverses([it["lmat"] for it in items], ii, jj)
    sols = [_dot3(tinv, it["rhs"]) for tinv, it in zip(tinvs, items)]
    for it, sol in zip(items, sols):
        rows = slice(it["c"] * CHUNK, (it["c"] + 1) * CHUNK)
        lanes = slice(it["h"] * HEAD_DIM, (it["h"] + 1) * HEAD_DIM)
        u_ref[0, rows, lanes] = sol[:, :HEAD_DIM]
        w_ref[0, rows, lanes] = sol[:, HEAD_DIM:].astype(BF16)
        qd_ref[0, rows, lanes] = it["qd"]
        kd_ref[0, rows, lanes] = it["kd"]
        a_ref[0, rows, lanes] = it["a"]
        gl_ref[0, it["c"] * 8:(it["c"] + 1) * 8, lanes] = it["gl"]


def _gdn_prep(x, wts):
    bsz, s, d = x.shape
    tok = pl.BlockSpec((1, TILE, d), lambda b, i: (b, i, 0))
    hw = pl.BlockSpec((1, TILE, GDN_WIDTH), lambda b, i: (b, i, 0))
    gls = pl.BlockSpec((1, 8 * TILE // CHUNK, GDN_WIDTH), lambda b, i: (b, i, 0))
    consts = [wts[n] for n in ("w_qkv", "w_z", "w_ab", "conv_w", "gdn_prm")]
    f32o = jax.ShapeDtypeStruct((bsz, s, GDN_WIDTH), F32)
    b16o = jax.ShapeDtypeStruct((bsz, s, GDN_WIDTH), BF16)
    glo = jax.ShapeDtypeStruct((bsz, 8 * s // CHUNK, GDN_WIDTH), F32)
    return pl.pallas_call(
        _gdn_prep_kernel,
        grid=(bsz, s // TILE),
        in_specs=[tok] + [_const_spec(c.shape) for c in consts],
        out_specs=[hw, hw, hw, hw, hw, gls, hw],
        out_shape=[f32o, b16o, b16o, b16o, b16o, glo, f32o],
        scratch_shapes=[pltpu.VMEM((HALO + TILE, 3 * GDN_WIDTH), F32)],
        compiler_params=_params(("arbitrary", "arbitrary")),
        name="gdn_prep",
    )(x, *consts)


def _gdn_scan_kernel(u_ref, w_ref, qd_ref, kd_ref, a_ref, gl_ref, z_ref, gn_ref, o_ref, state_ref):
    @pl.when(pl.program_id(0) == 0)
    def _():
        state_ref[...] = jnp.zeros(state_ref.shape, F32)

    bsz = u_ref.shape[0]
    for c in range(TILE // CHUNK):
        rows = slice(c * CHUNK, (c + 1) * CHUNK)
        for b in range(bsz):
            for h in range(GDN_HEADS):
                lanes = slice(h * HEAD_DIM, (h + 1) * HEAD_DIM)
                st = state_ref[b, h]
                wq = jnp.concatenate([w_ref[b, rows, lanes], qd_ref[b, rows, lanes]], axis=0)
                ws = _dot(wq, st.astype(BF16))
                v_new = u_ref[b, rows, lanes] - ws[:CHUNK]
                vb = v_new.astype(BF16)
                o = ws[CHUNK:] + _dot(a_ref[b, rows, lanes], vb)
                kdt = kd_ref[b, rows, lanes].astype(F32).T.astype(BF16)
                state_ref[b, h] = st * gl_ref[b, c * 8:c * 8 + 1, lanes] + _dot(kdt, vb)
                zz = z_ref[b, rows, lanes]
                o = o * lax.rsqrt(jnp.mean(o * o, axis=-1, keepdims=True) + RMS_EPS) * gn_ref[...]
                o_ref[b, rows, lanes] = (o * (zz * jax.nn.sigmoid(zz))).astype(BF16)


def _gdn_scan(u, w, qd, kd, a, gl, z, gn):
    bsz, s, _ = u.shape
    hw = pl.BlockSpec((bsz, TILE, GDN_WIDTH), lambda i: (0, i, 0))
    gls = pl.BlockSpec((bsz, 8 * TILE // CHUNK, GDN_WIDTH), lambda i: (0, i, 0))
    return pl.pallas_call(
        _gdn_scan_kernel,
        grid=(s // TILE,),
        in_specs=[hw, hw, hw, hw, hw, gls, hw, _const_spec(gn.shape)],
        out_specs=hw,
        out_shape=jax.ShapeDtypeStruct((bsz, s, GDN_WIDTH), BF16),
        scratch_shapes=[pltpu.VMEM((bsz, GDN_HEADS, HEAD_DIM, HEAD_DIM), F32)],
        compiler_params=_params(("arbitrary",)),
        name="gdn_scan",
    )(u, w, qd, kd, a, gl, z, gn)


def _merge_kernel(x_ref, og_ref, mbc_ref, wa_ref, wg0_ref, wo_ref, ln1g_ref, ln1b_ref,
                  rwt_ref, rb_ref, wsgu_ref, wsd_ref,
                  res_ref, hp_ref, idx_ref, wt_ref, rank_ref, cnt_ref, carry_ref):
    first = (pl.program_id(0) == 0) & (pl.program_id(1) == 0)

    @pl.when(first)
    def _():
        carry_ref[...] = jnp.zeros(carry_ref.shape, F32)

    x = x_ref[0]
    xb = x.astype(BF16)
    br_a = _dot(og_ref[0], wa_ref[...])
    merged = jax.nn.sigmoid(_dot(xb, wg0_ref[...])) * br_a + mbc_ref[0]
    y = _dot(merged.astype(BF16), wo_ref[...])
    h = _layer_norm(DEEPNORM_ALPHA * x + y, ln1g_ref[...], ln1b_ref[...])
    hb = h.astype(BF16)
    hp_ref[...] = _pack_bf16_pairs(h)

    gu = _dot(hb, wsgu_ref[...])
    gt = gu[:, :D_SHARED]
    act = gt * jax.nn.sigmoid(gt) * gu[:, D_SHARED:]
    res_ref[0] = DEEPNORM_ALPHA * h + _dot(act.astype(BF16), wsd_ref[...])

    scores = jax.nn.sigmoid(_dot_nt(rwt_ref[...], hb))
    biased = scores + rb_ref[...]
    g3 = biased.reshape(N_GROUPS, GROUP_SIZE, TILE)
    m1 = jnp.max(g3, axis=1)
    m1b = m1[:, None, :]
    n_top = jnp.sum((g3 == m1b).astype(F32), axis=1)
    m2 = jnp.max(jnp.where(g3 < m1b, g3, -jnp.inf), axis=1)
    gs = m1 + jnp.where(n_top >= 2.0, m1, m2)
    gidx = lax.broadcasted_iota(I32, (N_GROUPS, TILE), 0)
    beaten = jnp.zeros((N_GROUPS, TILE), F32)
    for g in range(N_GROUPS):
        row = gs[g:g + 1, :]
        beaten = beaten + ((row > gs) | ((row == gs) & (g < gidx))).astype(F32)
    sel = (beaten < float(TOPK_GROUPS)).astype(F32)
    sel_e = jnp.broadcast_to(sel[:, None, :], (N_GROUPS, GROUP_SIZE, TILE)).reshape(N_EXPERTS, TILE)
    masked = jnp.where(sel_e > 0.5, biased, -jnp.inf)
    eidx = lax.broadcasted_iota(I32, (N_EXPERTS, TILE), 0)
    chosen = jnp.zeros((N_EXPERTS, TILE), F32)
    picks, pick_scores = [], []
    for _k in range(TOP_K):
        mx = jnp.max(masked, axis=0, keepdims=True)
        pick = jnp.min(jnp.where(masked == mx, eidx, N_EXPERTS), axis=0, keepdims=True)
        hot = eidx == pick
        picks.append(pick)
        pick_scores.append(jnp.sum(jnp.where(hot, scores, 0.0), axis=0, keepdims=True))
        chosen = jnp.where(hot, 1.0, chosen)
        masked = jnp.where(hot, -jnp.inf, masked)
    total = pick_scores[0]
    for sck in pick_scores[1:]:
        total = total + sck
    idx_ref[...] = jnp.concatenate(picks, axis=0)
    wt_ref[...] = jnp.concatenate([sck / total * ROUTED_SCALE for sck in pick_scores], axis=0)

    ti = lax.broadcasted_iota(I32, (TILE, TILE), 0)
    tj = lax.broadcasted_iota(I32, (TILE, TILE), 1)
    before = _dot(chosen.astype(BF16), (ti < tj).astype(BF16))
    carry = carry_ref[...]
    before = before + jnp.concatenate([carry] * (TILE // 128), axis=1)
    rank_ref[...] = jnp.concatenate(
        [jnp.sum(jnp.where(eidx == p, before, 0.0), axis=0, keepdims=True) for p in picks],
        axis=0).astype(I32)
    carry = carry + _dot(chosen.astype(BF16), jnp.ones((TILE, 128), BF16))
    carry_ref[...] = carry
    cnt_ref[...] = carry.astype(I32)


def _merge(x, og, mbc, wts):
    bsz, s, d = x.shape
    t = bsz * s
    nt = s // TILE
    tok = pl.BlockSpec((1, TILE, d), lambda b, i: (b, i, 0))
    ogs = pl.BlockSpec((1, TILE, GDN_WIDTH), lambda b, i: (b, i, 0))
    flat = lambda b, i: (b * nt + i, 0)
    lane = lambda b, i: (0, b * nt + i)
    consts = [wts[n] for n in ("w_a", "w_g0", "w_o", "ln1_g", "ln1_b", "router_wt", "router_bias",
                               "ws_gu", "ws_down")]
    return pl.pallas_call(
        _merge_kernel,
        grid=(bsz, nt),
        in_specs=[tok, ogs, tok] + [_const_spec(c.shape) for c in consts],
        out_specs=[tok,
                   pl.BlockSpec((TILE, d // 2), flat),
                   pl.BlockSpec((TOP_K, TILE), lane),
                   pl.BlockSpec((TOP_K, TILE), lane),
                   pl.BlockSpec((TOP_K, TILE), lane),
                   _const_spec((N_EXPERTS, 128))],
        out_shape=[jax.ShapeDtypeStruct((bsz, s, d), F32),
                   jax.ShapeDtypeStruct((t, d // 2), U32),
                   jax.ShapeDtypeStruct((TOP_K, t), I32),
                   jax.ShapeDtypeStruct((TOP_K, t), F32),
                   jax.ShapeDtypeStruct((TOP_K, t), I32),
                   jax.ShapeDtypeStruct((N_EXPERTS, 128), I32)],
        scratch_shapes=[pltpu.VMEM((N_EXPERTS, 128), F32)],
        compiler_params=_params(("arbitrary", "arbitrary")),
        name="merge",
    )(x, og, mbc, *consts)


def _plan_kernel(idx_ref, rank_ref, cnt_ref, pos_ref, meta_ref):
    e = N_EXPERTS
    shift = EXPERT_BLOCK.bit_length() - 1
    cnt = cnt_ref[...]
    nblk = lax.shift_right_logical(cnt + (EXPERT_BLOCK - 1), shift)
    ei = lax.broadcasted_iota(I32, (e, e), 0)
    ej = lax.broadcasted_iota(I32, (e, e), 1)
    incl = _dot((ei >= ej).astype(BF16), nblk.astype(F32).astype(BF16)).astype(I32)
    excl = incl - nblk
    start = (excl * EXPERT_BLOCK).astype(F32)
    lane = lax.broadcasted_iota(I32, (e, 128), 1)
    n_used = jnp.broadcast_to(incl[e - 1:e, :], (e, 128))
    meta_ref[...] = jnp.where(lane == 0, excl, jnp.where(lane == 1, nblk, jnp.where(lane == 2, cnt, n_used)))

    e128 = lax.broadcasted_iota(I32, (e, 128), 0)

    def body(i, carry):
        off = pl.multiple_of(i * 128, 128)
        ids = idx_ref[:, pl.ds(off, 128)]
        rows = [jnp.sum(jnp.where(e128 == ids[k:k + 1, :], start, 0.0), axis=0, keepdims=True)
                for k in range(TOP_K)]
        pos_ref[:, pl.ds(off, 128)] = jnp.concatenate(rows, axis=0).astype(I32) + rank_ref[:, pl.ds(off, 128)]
        return carry

    lax.fori_loop(0, idx_ref.shape[1] // 128, body, 0)


def _plan(idx, rank, cnt):
    t = idx.shape[1]
    return pl.pallas_call(
        _plan_kernel,
        grid=(1,),
        in_specs=[_const_spec(idx.shape), _const_spec(rank.shape), _const_spec(cnt.shape)],
        out_specs=[_const_spec((TOP_K, t)), _const_spec((N_EXPERTS, 128))],
        out_shape=[jax.ShapeDtypeStruct((TOP_K, t), I32), jax.ShapeDtypeStruct((N_EXPERTS, 128), I32)],
        compiler_params=_params(("arbitrary",)),
        name="plan",
    )(idx, rank, cnt)


def _sc_scatter_rows(rows, pos, n_slots):
    t, d = rows.shape
    workers = SC_CORES * SC_SUBCORES
    per_w = t // workers
    nwin = per_w // SC_WINDOW
    assert per_w * workers == t and nwin * SC_WINDOW == per_w and nwin % 2 == 0
    idx = pos.reshape(TOP_K, workers, nwin, SC_WINDOW).transpose(1, 2, 0, 3).reshape(workers, nwin * TOP_K, SC_WINDOW)
    mesh = plsc.VectorSubcoreMesh(core_axis_name="core", subcore_axis_name="subcore",
                                  num_cores=SC_CORES, num_subcores=SC_SUBCORES)

    @functools.partial(
        pl.kernel, out_type=jax.ShapeDtypeStruct((n_slots, d), rows.dtype), mesh=mesh, name="sc_scatter_rows",
        scratch_types=[pltpu.VMEM((nwin * TOP_K, SC_WINDOW), I32), pltpu.VMEM((2, SC_WINDOW, d), rows.dtype),
                       pltpu.SemaphoreType.DMA((2,)), pltpu.SemaphoreType.DMA((2,))])
    def scatter(rows_hbm, idx_hbm, out_hbm, idx_v, rows_v, lsem, ssem):
        wid = lax.axis_index("subcore") * SC_CORES + lax.axis_index("core")
        base = wid * per_w
        pltpu.sync_copy(idx_hbm.at[wid], idx_v)

        def load(w, slot):
            return pltpu.make_async_copy(rows_hbm.at[pl.ds(base + w * SC_WINDOW, SC_WINDOW)], rows_v.at[slot],
                                         lsem.at[slot])

        def send(w, k, slot):
            return pltpu.make_async_copy(rows_v.at[slot], out_hbm.at[idx_v.at[w * TOP_K + k]], ssem.at[slot])

        load(0, 0).start()

        @pl.loop(0, nwin, step=2)
        def _(w0):
            for b in range(2):
                w = w0 + b
                load(w, b).wait()

                @pl.when(w + 1 < nwin)
                def _():
                    @pl.when(w >= 1)
                    def _():
                        for k in range(TOP_K):
                            send(w - 1, k, 1 - b).wait()

                    load(w + 1, 1 - b).start()

                for k in range(TOP_K):
                    send(w, k, b).start()

        for k in range(TOP_K):
            send(nwin - 2, k, 0).wait()
        for k in range(TOP_K):
            send(nwin - 1, k, 1).wait()

    return scatter(rows, idx)


def _pack_bf16_pairs(a):
    n = a.shape[1] // 2
    bits = lax.bitcast_convert_type(a.astype(BF16).astype(F32), U32)
    return (bits[:, :n] >> 16) | (bits[:, n:] & jnp.uint32(0xFFFF0000))


def _unpack_bf16_pairs(words):
    lo = lax.bitcast_convert_type(words << 16, F32)
    hi = lax.bitcast_convert_type(words & jnp.uint32(0xFFFF0000), F32)
    return jnp.concatenate([lo, hi], axis=1)


def _experts_kernel(fb_ref, nb_ref, cnt_ref, nu_ref, xs_ref, wg_ref, wu_ref, wd_ref, y_ref,
                    wgu_s, wd_s, xbuf, ybuf, xsem, ysem):
    e = pl.program_id(0)
    n_used = nu_ref[0]
    bm = EXPERT_BLOCK
    row_queue = 1

    def x_copy(g, slot):
        return pltpu.make_async_copy(xs_ref.at[pl.ds(g * bm, bm), :], xbuf.at[slot], xsem.at[slot])

    def y_copy(g, slot):
        return pltpu.make_async_copy(ybuf.at[slot], y_ref.at[pl.ds(g * bm, bm), :], ysem.at[slot])

    @pl.when(e == 0)
    def _():
        for g0 in range(X_SLOTS - 1):
            @pl.when(g0 < n_used)
            def _():
                x_copy(g0, g0).start(priority=row_queue)

    nb = nb_ref[e]

    @pl.when(nb > 0)
    def _():
        wgu_s[:, :D_EXPERT] = wg_ref[0].astype(BF16)
        wgu_s[:, D_EXPERT:] = wu_ref[0].astype(BF16)
        wd_s[...] = wd_ref[0].astype(BF16)

    def block(b, carry):
        g = fb_ref[e] + b
        xslot = lax.rem(g, X_SLOTS)
        slot = g & 1
        x_copy(g, xslot).wait()
        ahead = g + (X_SLOTS - 1)

        @pl.when(ahead < n_used)
        def _():
            x_copy(ahead, lax.rem(ahead, X_SLOTS)).start(priority=row_queue)

        x = _unpack_bf16_pairs(xbuf[xslot])
        live = lax.broadcasted_iota(I32, (bm, 1), 0) < cnt_ref[e] - b * bm
        xb = jnp.where(live, x, 0.0).astype(BF16)
        gu = _dot(xb, wgu_s[...])
        gt = gu[:, :D_EXPERT]
        act = (gt * jax.nn.sigmoid(gt) * gu[:, D_EXPERT:]).astype(BF16)
        y = _dot(act, wd_s[...])

        @pl.when(g >= 2)
        def _():
            y_copy(g - 2, slot).wait()

        ybuf[slot] = _pack_bf16_pairs(y)
        y_copy(g, slot).start(priority=row_queue)
        return carry

    lax.fori_loop(0, nb, block, 0)

    @pl.when(e == pl.num_programs(0) - 1)
    def _():
        @pl.when(n_used >= 2)
        def _():
            y_copy(n_used - 2, n_used & 1).wait()

        y_copy(n_used - 1, (n_used - 1) & 1).wait()


def _experts(xs, first_block, n_blocks, counts, n_used, w_gate, w_up, w_down):
    n_slots, half = xs.shape
    d = 2 * half
    wsel = lambda e, *_: (e, 0, 0)
    grid_spec = pltpu.PrefetchScalarGridSpec(
        num_scalar_prefetch=4,
        grid=(N_EXPERTS,),
        in_specs=[pl.BlockSpec(memory_space=pl.ANY),
                  pl.BlockSpec((1, d, D_EXPERT), wsel),
                  pl.BlockSpec((1, d, D_EXPERT), wsel),
                  pl.BlockSpec((1, D_EXPERT, d), wsel)],
        out_specs=pl.BlockSpec(memory_space=pl.ANY),
        scratch_shapes=[pltpu.VMEM((d, 2 * D_EXPERT), BF16), pltpu.VMEM((D_EXPERT, d), BF16),
                        pltpu.VMEM((X_SLOTS, EXPERT_BLOCK, half), U32), pltpu.VMEM((2, EXPERT_BLOCK, half), U32),
                        pltpu.SemaphoreType.DMA((X_SLOTS,)), pltpu.SemaphoreType.DMA((2,))],
    )
    return pl.pallas_call(
        _experts_kernel,
        grid_spec=grid_spec,
        out_shape=jax.ShapeDtypeStruct((n_slots, half), U32),
        compiler_params=_params(("arbitrary",)),
        name="experts",
    )(first_block, n_blocks, counts, n_used, xs, w_gate, w_up, w_down)


def _sc_gather_rows(table, indices):
    n = indices.shape[0]
    d = table.shape[1]
    workers = SC_CORES * SC_SUBCORES
    per_w = n // workers
    nch = per_w // SC_WINDOW
    assert per_w * workers == n and nch * SC_WINDOW == per_w and nch % 2 == 0
    mesh = plsc.VectorSubcoreMesh(core_axis_name="core", subcore_axis_name="subcore",
                                  num_cores=SC_CORES, num_subcores=SC_SUBCORES)

    @functools.partial(
        pl.kernel, out_type=jax.ShapeDtypeStruct((n, d), table.dtype), mesh=mesh, name="sc_gather_rows",
        scratch_types=[pltpu.VMEM((per_w,), I32), pltpu.VMEM((2, SC_WINDOW, d), table.dtype),
                       pltpu.SemaphoreType.DMA((2,)), pltpu.SemaphoreType.DMA((2,))])
    def gather(table_hbm, idx_hbm, out_hbm, idx_v, rows_v, gsem, wsem):
        base = (lax.axis_index("subcore") * SC_CORES + lax.axis_index("core")) * per_w
        pltpu.sync_copy(idx_hbm.at[pl.ds(base, per_w)], idx_v)

        def fetch(c, slot):
            return pltpu.make_async_copy(table_hbm.at[idx_v.at[pl.ds(c * SC_WINDOW, SC_WINDOW)]],
                                         rows_v.at[slot], gsem.at[slot])

        def put(c, slot):
            return pltpu.make_async_copy(rows_v.at[slot], out_hbm.at[pl.ds(base + c * SC_WINDOW, SC_WINDOW)],
                                         wsem.at[slot])

        fetch(0, 0).start()

        @pl.loop(0, nch, step=2)
        def _(c0):
            for b in range(2):
                c = c0 + b
                fetch(c, b).wait()

                @pl.when(c + 1 < nch)
                def _():
                    @pl.when(c >= 1)
                    def _():
                        put(c - 1, 1 - b).wait()

                    fetch(c + 1, 1 - b).start()

                put(c, b).start()

        put(nch - 2, 0).wait()
        put(nch - 1, 1).wait()

    return gather(table, indices)


def _combine_kernel(yt_ref, wt_ref, res_ref, g_ref, b_ref, o_ref):
    acc = res_ref[...]
    for k in range(TOP_K):
        acc = acc + _unpack_bf16_pairs(yt_ref[k]) * wt_ref[:, k:k + 1]
    o_ref[...] = _layer_norm(acc, g_ref[...], b_ref[...])


def _combine(y_tok, wts_tok, res, ln_g, ln_b):
    t, d = res.shape
    tok = pl.BlockSpec((COMBINE_TILE, d), lambda i: (i, 0))
    return pl.pallas_call(
        _combine_kernel,
        grid=(t // COMBINE_TILE,),
        in_specs=[pl.BlockSpec((TOP_K, COMBINE_TILE, d // 2), lambda i: (0, i, 0)),
                  pl.BlockSpec((COMBINE_TILE, TOP_K), lambda i: (i, 0)),
                  tok, _const_spec(ln_g.shape), _const_spec(ln_b.shape)],
        out_specs=tok,
        out_shape=jax.ShapeDtypeStruct((t, d), F32),
        compiler_params=_params(("arbitrary",)),
        name="combine",
    )(y_tok, wts_tok, res, ln_g, ln_b)


def _prepare(l, w_in, conv_w, a_log, dt_bias, gdn_norm_g, w_a, sgu_ln_g, sgu_ln_b, sgu_w, sgu_b, w_b,
             w_mem_kv, w_c, w_o, ln1_g, ln1_b, router_w, router_bias, ws_gate, ws_up, ws_down, ln2_g, ln2_b):
    wi = w_in[l]
    d = D_MODEL
    bf = lambda a: a.astype(BF16)
    row = lambda a: a.reshape(1, -1).astype(F32)
    w_ab = jnp.zeros((d, 128), F32).at[:, :2 * GDN_HEADS].set(wi[:, _C_AB:_C_UV])
    prm = jnp.zeros((8, 128), F32).at[0, :GDN_HEADS].set(a_log[l]).at[1, :GDN_HEADS].set(dt_bias[l])
    sgu_bias = jnp.repeat(sgu_b[l].T, SGU_WIDTH // SGU_GROUPS, axis=1)
    return {
        "w_qkv": bf(wi[:, _C_QKV:_C_Z]), "w_z": bf(wi[:, _C_Z:_C_AB]), "w_ab": bf(w_ab),
        "w_uv": bf(wi[:, _C_UV:_C_XQ]), "w_xq": bf(wi[:, _C_XQ:_C_GATE]),
        "w_g0": bf(wi[:, _C_GATE:_C_GATE + d]), "w_g1": bf(wi[:, _C_GATE + d:_C_GATE + 2 * d]),
        "w_g2": bf(wi[:, _C_GATE + 2 * d:_C_GATE + 3 * d]),
        "conv_w": conv_w[l].astype(F32), "gdn_prm": prm, "gdn_norm_g": row(gdn_norm_g[l]),
        "w_a": bf(w_a[l]), "sgu_ln_g": row(sgu_ln_g[l]), "sgu_ln_b": row(sgu_ln_b[l]),
        "sgu_w": sgu_w[l].astype(F32), "sgu_bias": sgu_bias.astype(F32), "w_b": bf(w_b[l]),
        "w_mem_kv": bf(w_mem_kv[l]), "w_c": bf(w_c[l]), "w_o": bf(w_o[l]),
        "ln1_g": row(ln1_g[l]), "ln1_b": row(ln1_b[l]),
        "router_wt": bf(router_w[l].T), "router_bias": router_bias[l].reshape(-1, 1).astype(F32),
        "ws_gu": bf(jnp.concatenate([ws_gate[l], ws_up[l]], axis=1)), "ws_down": bf(ws_down[l]),
        "ln2_g": row(ln2_g[l]), "ln2_b": row(ln2_b[l]),
    }


def _layer(x, mem, wts, w_gate, w_up, w_down):
    bsz, s, d = x.shape
    t = bsz * s
    k, v = _mem_kv(mem, wts["w_mem_kv"])
    mbc = _mix_bc(x, wts, k, v)
    u, w, qd, kd, a, gl, z = _gdn_prep(x, wts)
    og = _gdn_scan(u, w, qd, kd, a, gl, z, wts["gdn_norm_g"])
    res, hp, idx, rw, rank, cnt = _merge(x, og, mbc, wts)
    nb = t * TOP_K // EXPERT_BLOCK + N_EXPERTS
    pos, meta = _plan(idx, rank, cnt)
    xs = _sc_scatter_rows(hp, pos, nb * EXPERT_BLOCK)
    y = _experts(xs, meta[:, 0], meta[:, 1], meta[:, 2], meta[:1, 3], w_gate, w_up, w_down)
    y_tok = _sc_gather_rows(y, pos.reshape(-1)).reshape(TOP_K, t, d // 2)
    out = _combine(y_tok, rw.T, res.reshape(t, d), wts["ln2_g"], wts["ln2_b"])
    return out.reshape(bsz, s, d)


def kernel(x, mem, w_in, conv_w, a_log, dt_bias, gdn_norm_g, w_a, sgu_ln_g, sgu_ln_b, sgu_w, sgu_b, w_b,
           w_mem_kv, w_c, w_o, ln1_g, ln1_b, router_w, router_bias, w_gate, w_up, w_down, ws_gate, ws_up,
           ws_down, ln2_g, ln2_b):
    assert x.shape[1] % TILE == 0 and x.shape[2] == D_MODEL
    for l in range(DEPTH):
        wts = _prepare(l, w_in, conv_w, a_log, dt_bias, gdn_norm_g, w_a, sgu_ln_g, sgu_ln_b, sgu_w, sgu_b,
                       w_b, w_mem_kv, w_c, w_o, ln1_g, ln1_b, router_w, router_bias, ws_gate, ws_up,
                       ws_down, ln2_g, ln2_b)
        x = _layer(x, mem, wts, w_gate[l], w_up[l], w_down[l])
    return x
```

```python
import functools

import jax
import jax.numpy as jnp
from jax import lax
from jax.experimental import pallas as pl
from jax.experimental.pallas import tpu as pltpu
from jax.experimental.pallas import tpu_sc as plsc

F32 = jnp.float32
BF16 = jnp.bfloat16
I32 = jnp.int32
U32 = jnp.uint32

D_MODEL = 1024
DEPTH = 1
GDN_HEADS = 4
HEAD_DIM = 128
GDN_WIDTH = GDN_HEADS * HEAD_DIM
CONV_WIDTH = 4
SGU_GROUPS = 4
SGU_WIDTH = 512
SGU_CHUNK = 128
N_MEM = 256
XATTN_HEADS = 4
XATTN_WIDTH = 512
N_EXPERTS = 256
TOP_K = 8
N_GROUPS = 8
GROUP_SIZE = N_EXPERTS // N_GROUPS
TOPK_GROUPS = 4
D_EXPERT = 256
D_SHARED = 256
ROUTED_SCALE = 2.5
LN_EPS = 1e-5
RMS_EPS = 1e-6
DEEPNORM_ALPHA = (2.0 * DEPTH) ** 0.25

_C_QKV = 0
_C_Z = 3 * GDN_WIDTH
_C_AB = 4 * GDN_WIDTH
_C_UV = _C_AB + 2 * GDN_HEADS
_C_XQ = _C_UV + 2 * SGU_WIDTH
_C_GATE = _C_XQ + XATTN_WIDTH

CHUNK = 128
TILE = 512
HALO = 8
EXPERT_BLOCK = 256
X_SLOTS = 4
SC_CORES = 2
SC_SUBCORES = 16
SC_WINDOW = 64
COMBINE_TILE = 256
VMEM_LIMIT = 56 * 1024 * 1024


def _dot(a, b):
    return jnp.dot(a, b, preferred_element_type=F32)


def _dot_nt(a, b):
    return lax.dot_general(a, b, (((1,), (1,)), ((), ())), preferred_element_type=F32)


def _split(a):
    hi = a.astype(BF16)
    lo = (a - hi.astype(F32)).astype(BF16)
    return hi, lo


def _dot3(a, b):
    ah, al = _split(a)
    bh, bl = _split(b)
    return _dot(ah, bh) + (_dot(ah, bl) + _dot(al, bh))


def _softplus(x):
    return jnp.maximum(x, 0.0) + jnp.log1p(jnp.exp(-jnp.abs(x)))


def _layer_norm(x, g, b):
    mu = jnp.mean(x, axis=-1, keepdims=True)
    xc = x - mu
    var = jnp.mean(xc * xc, axis=-1, keepdims=True)
    return xc * lax.rsqrt(var + LN_EPS) * g + b


def _const_spec(shape):
    nd = len(shape)
    return pl.BlockSpec(shape, lambda *_: (0,) * nd)


def _params(semantics):
    return pltpu.CompilerParams(dimension_semantics=semantics, vmem_limit_bytes=VMEM_LIMIT)


def _mem_kv_kernel(mem_ref, w_ref, k_ref, v_ref):
    kv = _dot(mem_ref[0].astype(BF16), w_ref[...])
    k_ref[0] = kv[:, :XATTN_WIDTH].astype(BF16)
    v_ref[0] = kv[:, XATTN_WIDTH:].astype(BF16)


def _mem_kv(mem, w_kv):
    bsz, n_mem, d = mem.shape
    out = jax.ShapeDtypeStruct((bsz, n_mem, XATTN_WIDTH), BF16)
    return pl.pallas_call(
        _mem_kv_kernel,
        grid=(bsz,),
        in_specs=[pl.BlockSpec((1, n_mem, d), lambda b: (b, 0, 0)), _const_spec(w_kv.shape)],
        out_specs=[pl.BlockSpec((1, n_mem, XATTN_WIDTH), lambda b: (b, 0, 0))] * 2,
        out_shape=[out, out],
        compiler_params=_params(("arbitrary",)),
        name="mem_kv",
    )(mem, w_kv)


def _mix_bc_kernel(x_ref, wuv_ref, wg1_ref, wb_ref, lng_ref, lnb_ref, sw_ref, sb_ref,
                   wxq_ref, k_ref, v_ref, wg2_ref, wc_ref, o_ref):
    xb = x_ref[0].astype(BF16)
    uv = jax.nn.gelu(_dot(xb, wuv_ref[...]), approximate=True)
    u = uv[:, :SGU_WIDTH]
    v = _layer_norm(uv[:, SGU_WIDTH:], lng_ref[...], lnb_ref[...])
    ii = lax.broadcasted_iota(I32, (SGU_CHUNK, SGU_CHUNK), 0)
    jj = lax.broadcasted_iota(I32, (SGU_CHUNK, SGU_CHUNK), 1)
    rows = []
    for n in range(TILE // SGU_CHUNK):
        cols = []
        for g in range(SGU_GROUPS):
            wt = jnp.where(ii >= jj, sw_ref[g], 0.0).astype(BF16)
            vg = v[n * SGU_CHUNK:(n + 1) * SGU_CHUNK, g * 128:(g + 1) * 128].astype(BF16)
            cols.append(_dot(wt, vg))
        rows.append(jnp.concatenate(cols, axis=1) + sb_ref[...])
    sg = jnp.concatenate(rows, axis=0)
    br_b = _dot((u * sg).astype(BF16), wb_ref[...])
    acc = jax.nn.sigmoid(_dot(xb, wg1_ref[...])) * br_b
    xq = _dot(xb, wxq_ref[...])
    heads = []
    for h in range(XATTN_HEADS):
        qh = xq[:, h * 128:(h + 1) * 128].astype(BF16)
        kh = k_ref[0, :, h * 128:(h + 1) * 128]
        vh = v_ref[0, :, h * 128:(h + 1) * 128]
        sc = _dot_nt(qh, kh) * (128 ** -0.5)
        p = jnp.exp(sc - jnp.max(sc, axis=-1, keepdims=True))
        p = p / jnp.sum(p, axis=-1, keepdims=True)
        heads.append(_dot(p.astype(BF16), vh))
    att = jnp.concatenate(heads, axis=1).astype(BF16)
    br_c = _dot(att, wc_ref[...])
    o_ref[0] = acc + jax.nn.sigmoid(_dot(xb, wg2_ref[...])) * br_c


def _mix_bc(x, wts, k, v):
    bsz, s, d = x.shape
    tok = pl.BlockSpec((1, TILE, d), lambda b, i: (b, i, 0))
    kvs = pl.BlockSpec((1, N_MEM, XATTN_WIDTH), lambda b, i: (b, 0, 0))
    names = ("w_uv", "w_g1", "w_b", "sgu_ln_g", "sgu_ln_b", "sgu_w", "sgu_bias", "w_xq")
    consts = [wts[n] for n in names]
    tail = [wts["w_g2"], wts["w_c"]]
    return pl.pallas_call(
        _mix_bc_kernel,
        grid=(bsz, s // TILE),
        in_specs=[tok] + [_const_spec(c.shape) for c in consts] + [kvs, kvs]
        + [_const_spec(c.shape) for c in tail],
        out_specs=tok,
        out_shape=jax.ShapeDtypeStruct((bsz, s, d), F32),
        compiler_params=_params(("arbitrary", "arbitrary")),
        name="mix_bc",
    )(x, *consts, k, v, *tail)


def _unit_lower_inverses(lmats, ii, jj):
    eye = (ii == jj).astype(F32)
    x = ii ^ jj
    lsplit = [_split(l) for l in lmats]
    ts = [eye - jnp.where(x < 2, l, 0.0) for l in lmats]
    zero = jnp.zeros((CHUNK, CHUNK), BF16)
    s = 2
    while s < CHUNK:
        off = (x < 2 * s) & ((ii & s) != 0) & ((jj & s) == 0)
        tsplit = [_split(t) for t in ts]
        prods = []
        for (th, tl), (lh, ll) in zip(tsplit, lsplit):
            oh = jnp.where(off, lh, zero)
            ol = jnp.where(off, ll, zero)
            prods.append(_dot(th, oh) + (_dot(th, ol) + _dot(tl, oh)))
        nxt = []
        for t, p, (th, tl) in zip(ts, prods, tsplit):
            ph, plo = _split(p)
            nxt.append(t - (_dot(ph, th) + (_dot(ph, tl) + _dot(plo, th))))
        ts = nxt
        s *= 2
    return ts


def _gdn_prep_kernel(x_ref, wqkv_ref, wz_ref, wab_ref, convw_ref, prm_ref,
                     u_ref, w_ref, qd_ref, kd_ref, a_ref, gl_ref, z_ref, hist_ref):
    @pl.when(pl.program_id(1) == 0)
    def _():
        hist_ref[0:HALO, :] = jnp.zeros((HALO, 3 * GDN_WIDTH), F32)

    xb = x_ref[0].astype(BF16)
    z_ref[0] = _dot(xb, wz_ref[...])
    qkv = _dot(xb, wqkv_ref[...])
    hist_ref[HALO:HALO + TILE, :] = qkv
    conv = qkv * convw_ref[CONV_WIDTH - 1:CONV_WIDTH, :]
    for j in range(CONV_WIDTH - 1):
        shift = CONV_WIDTH - 1 - j
        conv = conv + hist_ref[HALO - shift:HALO - shift + TILE, :] * convw_ref[j:j + 1, :]
    hist_ref[0:HALO, :] = qkv[TILE - HALO:, :]
    qkv = conv * jax.nn.sigmoid(conv)

    ab = _dot(xb, wab_ref[...])
    g = -jnp.exp(prm_ref[0:1, :]) * _softplus(ab + prm_ref[1:2, :])
    beta = jax.nn.sigmoid(ab)

    ii = lax.broadcasted_iota(I32, (CHUNK, CHUNK), 0)
    jj = lax.broadcasted_iota(I32, (CHUNK, CHUNK), 1)
    tri = (ii >= jj).astype(BF16)
    items = []
    for c in range(TILE // CHUNK):
        r0 = c * CHUNK
        gch = g[r0:r0 + CHUNK, :]
        g1 = gch.astype(BF16)
        r1 = gch - g1.astype(F32)
        g2 = r1.astype(BF16)
        g3 = (r1 - g2.astype(F32)).astype(BF16)
        gc = _dot(tri, g1) + (_dot(tri, g2) + _dot(tri, g3))
        gct = gc.T
        for h in range(GDN_HEADS):
            qh = qkv[r0:r0 + CHUNK, h * HEAD_DIM:(h + 1) * HEAD_DIM]
            kh = qkv[r0:r0 + CHUNK, GDN_WIDTH + h * HEAD_DIM:GDN_WIDTH + (h + 1) * HEAD_DIM]
            vh = qkv[r0:r0 + CHUNK, 2 * GDN_WIDTH + h * HEAD_DIM:2 * GDN_WIDTH + (h + 1) * HEAD_DIM]
            qn = qh * lax.rsqrt(jnp.sum(qh * qh, axis=-1, keepdims=True) + RMS_EPS) * (HEAD_DIM ** -0.5)
            kn = kh * lax.rsqrt(jnp.sum(kh * kh, axis=-1, keepdims=True) + RMS_EPS)
            gcol = gc[:, h:h + 1]
            grow = gct[h:h + 1, :]
            bcol = beta[r0:r0 + CHUNK, GDN_HEADS + h:GDN_HEADS + h + 1]
            glast = gc[CHUNK - 1:CHUNK, h:h + 1]
            decay = jnp.exp(jnp.where(ii >= jj, gcol - grow, -jnp.inf))
            knb = kn.astype(BF16)
            kk = _dot_nt(knb, knb)
            qk = _dot_nt(qn.astype(BF16), knb)
            egc = jnp.exp(gcol)
            items.append(dict(
                c=c, h=h, lmat=jnp.where(ii > jj, kk * bcol * decay, 0.0),
                rhs=jnp.concatenate([vh * bcol, kn * (bcol * egc)], axis=1),
                qd=(qn * egc).astype(BF16), kd=(kn * jnp.exp(glast - gcol)).astype(BF16),
                a=(qk * decay).astype(BF16), gl=jnp.broadcast_to(jnp.exp(glast), (8, HEAD_DIM))))
    tinvs = _unit_lower_inverses([it["lmat"] for it in items], ii, jj)
    sols = [_dot3(tinv, it["rhs"]) for tinv, it in zip(tinvs, items)]
    for it, sol in zip(items, sols):
        rows = slice(it["c"] * CHUNK, (it["c"] + 1) * CHUNK)
        lanes = slice(it["h"] * HEAD_DIM, (it["h"] + 1) * HEAD_DIM)
        u_ref[0, rows, lanes] = sol[:, :HEAD_DIM]
        w_ref[0, rows, lanes] = sol[:, HEAD_DIM:].astype(BF16)
        qd_ref[0, rows, lanes] = it["qd"]
        kd_ref[0, rows, lanes] = it["kd"]
        a_ref[0, rows, lanes] = it["a"]
        gl_ref[0, it["c"] * 8:(it["c"] + 1) * 8, lanes] = it["gl"]


def _gdn_prep(x, wts):
    bsz, s, d = x.shape
    tok = pl.BlockSpec((1, TILE, d), lambda b, i: (b, i, 0))
    hw = pl.BlockSpec((1, TILE, GDN_WIDTH), lambda b, i: (b, i, 0))
    gls = pl.BlockSpec((1, 8 * TILE // CHUNK, GDN_WIDTH), lambda b, i: (b, i, 0))
    consts = [wts[n] for n in ("w_qkv", "w_z", "w_ab", "conv_w", "gdn_prm")]
    f32o = jax.ShapeDtypeStruct((bsz, s, GDN_WIDTH), F32)
    b16o = jax.ShapeDtypeStruct((bsz, s, GDN_WIDTH), BF16)
    glo = jax.ShapeDtypeStruct((bsz, 8 * s // CHUNK, GDN_WIDTH), F32)
    return pl.pallas_call(
        _gdn_prep_kernel,
        grid=(bsz, s // TILE),
        in_specs=[tok] + [_const_spec(c.shape) for c in consts],
        out_specs=[hw, hw, hw, hw, hw, gls, hw],
        out_shape=[f32o, b16o, b16o, b16o, b16o, glo, f32o],
        scratch_shapes=[pltpu.VMEM((HALO + TILE, 3 * GDN_WIDTH), F32)],
        compiler_params=_params(("arbitrary", "arbitrary")),
        name="gdn_prep",
    )(x, *consts)


def _gdn_scan_kernel(u_ref, w_ref, qd_ref, kd_ref, a_ref, gl_ref, z_ref, gn_ref, o_ref, state_ref):
    @pl.when(pl.program_id(0) == 0)
    def _():
        state_ref[...] = jnp.zeros(state_ref.shape, F32)

    bsz = u_ref.shape[0]
    for c in range(TILE // CHUNK):
        rows = slice(c * CHUNK, (c + 1) * CHUNK)
        for b in range(bsz):
            for h in range(GDN_HEADS):
                lanes = slice(h * HEAD_DIM, (h + 1) * HEAD_DIM)
                st = state_ref[b, h]
                wq = jnp.concatenate([w_ref[b, rows, lanes], qd_ref[b, rows, lanes]], axis=0)
                ws = _dot(wq, st.astype(BF16))
                v_new = u_ref[b, rows, lanes] - ws[:CHUNK]
                vb = v_new.astype(BF16)
                o = ws[CHUNK:] + _dot(a_ref[b, rows, lanes], vb)
                kdt = kd_ref[b, rows, lanes].astype(F32).T.astype(BF16)
                state_ref[b, h] = st * gl_ref[b, c * 8:c * 8 + 1, lanes] + _dot(kdt, vb)
                zz = z_ref[b, rows, lanes]
                o = o * lax.rsqrt(jnp.mean(o * o, axis=-1, keepdims=True) + RMS_EPS) * gn_ref[...]
                o_ref[b, rows, lanes] = (o * (zz * jax.nn.sigmoid(zz))).astype(BF16)


def _gdn_scan(u, w, qd, kd, a, gl, z, gn):
    bsz, s, _ = u.shape
    hw = pl.BlockSpec((bsz, TILE, GDN_WIDTH), lambda i: (0, i, 0))
    gls = pl.BlockSpec((bsz, 8 * TILE // CHUNK, GDN_WIDTH), lambda i: (0, i, 0))
    return pl.pallas_call(
        _gdn_scan_kernel,
        grid=(s // TILE,),
        in_specs=[hw, hw, hw, hw, hw, gls, hw, _const_spec(gn.shape)],
        out_specs=hw,
        out_shape=jax.ShapeDtypeStruct((bsz, s, GDN_WIDTH), BF16),
        scratch_shapes=[pltpu.VMEM((bsz, GDN_HEADS, HEAD_DIM, HEAD_DIM), F32)],
        compiler_params=_params(("arbitrary",)),
        name="gdn_scan",
    )(u, w, qd, kd, a, gl, z, gn)


def _merge_kernel(x_ref, og_ref, mbc_ref, wa_ref, wg0_ref, wo_ref, ln1g_ref, ln1b_ref,
                  rwt_ref, rb_ref, wsgu_ref, wsd_ref,
                  res_ref, hp_ref, idx_ref, wt_ref, rank_ref, cnt_ref, carry_ref):
    first = (pl.program_id(0) == 0) & (pl.program_id(1) == 0)

    @pl.when(first)
    def _():
        carry_ref[...] = jnp.zeros(carry_ref.shape, F32)

    x = x_ref[0]
    xb = x.astype(BF16)
    br_a = _dot(og_ref[0], wa_ref[...])
    merged = jax.nn.sigmoid(_dot(xb, wg0_ref[...])) * br_a + mbc_ref[0]
    y = _dot(merged.astype(BF16), wo_ref[...])
    h = _layer_norm(DEEPNORM_ALPHA * x + y, ln1g_ref[...], ln1b_ref[...])
    hb = h.astype(BF16)
    hp_ref[...] = _pack_bf16_pairs(h)

    gu = _dot(hb, wsgu_ref[...])
    gt = gu[:, :D_SHARED]
    act = gt * jax.nn.sigmoid(gt) * gu[:, D_SHARED:]
    res_ref[0] = DEEPNORM_ALPHA * h + _dot(act.astype(BF16), wsd_ref[...])

    scores = jax.nn.sigmoid(_dot_nt(rwt_ref[...], hb))
    biased = scores + rb_ref[...]
    g3 = biased.reshape(N_GROUPS, GROUP_SIZE, TILE)
    m1 = jnp.max(g3, axis=1)
    m1b = m1[:, None, :]
    n_top = jnp.sum((g3 == m1b).astype(F32), axis=1)
    m2 = jnp.max(jnp.where(g3 < m1b, g3, -jnp.inf), axis=1)
    gs = m1 + jnp.where(n_top >= 2.0, m1, m2)
    gidx = lax.broadcasted_iota(I32, (N_GROUPS, TILE), 0)
    beaten = jnp.zeros((N_GROUPS, TILE), F32)
    for g in range(N_GROUPS):
        row = gs[g:g + 1, :]
        beaten = beaten + ((row > gs) | ((row == gs) & (g < gidx))).astype(F32)
    sel = (beaten < float(TOPK_GROUPS)).astype(F32)
    sel_e = jnp.broadcast_to(sel[:, None, :], (N_GROUPS, GROUP_SIZE, TILE)).reshape(N_EXPERTS, TILE)
    masked = jnp.where(sel_e > 0.5, biased, -jnp.inf)
    eidx = lax.broadcasted_iota(I32, (N_EXPERTS, TILE), 0)
    chosen = jnp.zeros((N_EXPERTS, TILE), F32)
    picks, pick_scores = [], []
    for _k in range(TOP_K):
        mx = jnp.max(masked, axis=0, keepdims=True)
        pick = jnp.min(jnp.where(masked == mx, eidx, N_EXPERTS), axis=0, keepdims=True)
        hot = eidx == pick
        picks.append(pick)
        pick_scores.append(jnp.sum(jnp.where(hot, scores, 0.0), axis=0, keepdims=True))
        chosen = jnp.where(hot, 1.0, chosen)
        masked = jnp.where(hot, -jnp.inf, masked)
    total = pick_scores[0]
    for sck in pick_scores[1:]:
        total = total + sck
    idx_ref[...] = jnp.concatenate(picks, axis=0)
    wt_ref[...] = jnp.concatenate([sck / total * ROUTED_SCALE for sck in pick_scores], axis=0)

    ti = lax.broadcasted_iota(I32, (TILE, TILE), 0)
    tj = lax.broadcasted_iota(I32, (TILE, TILE), 1)
    before = _dot(chosen.astype(BF16), (ti < tj).astype(BF16))
    carry = carry_ref[...]
    before = before + jnp.concatenate([carry] * (TILE // 128), axis=1)
    rank_ref[...] = jnp.concatenate(
        [jnp.sum(jnp.where(eidx == p, before, 0.0), axis=0, keepdims=True) for p in picks],
        axis=0).astype(I32)
    carry = carry + _dot(chosen.astype(BF16), jnp.ones((TILE, 128), BF16))
    carry_ref[...] = carry
    cnt_ref[...] = carry.astype(I32)


def _merge(x, og, mbc, wts):
    bsz, s, d = x.shape
    t = bsz * s
    nt = s // TILE
    tok = pl.BlockSpec((1, TILE, d), lambda b, i: (b, i, 0))
    ogs = pl.BlockSpec((1, TILE, GDN_WIDTH), lambda b, i: (b, i, 0))
    flat = lambda b, i: (b * nt + i, 0)
    lane = lambda b, i: (0, b * nt + i)
    consts = [wts[n] for n in ("w_a", "w_g0", "w_o", "ln1_g", "ln1_b", "router_wt", "router_bias",
                               "ws_gu", "ws_down")]
    return pl.pallas_call(
        _merge_kernel,
        grid=(bsz, nt),
        in_specs=[tok, ogs, tok] + [_const_spec(c.shape) for c in consts],
        out_specs=[tok,
                   pl.BlockSpec((TILE, d // 2), flat),
                   pl.BlockSpec((TOP_K, TILE), lane),
                   pl.BlockSpec((TOP_K, TILE), lane),
                   pl.BlockSpec((TOP_K, TILE), lane),
                   _const_spec((N_EXPERTS, 128))],
        out_shape=[jax.ShapeDtypeStruct((bsz, s, d), F32),
                   jax.ShapeDtypeStruct((t, d // 2), U32),
                   jax.ShapeDtypeStruct((TOP_K, t), I32),
                   jax.ShapeDtypeStruct((TOP_K, t), F32),
                   jax.ShapeDtypeStruct((TOP_K, t), I32),
                   jax.ShapeDtypeStruct((N_EXPERTS, 128), I32)],
        scratch_shapes=[pltpu.VMEM((N_EXPERTS, 128), F32)],
        compiler_params=_params(("arbitrary", "arbitrary")),
        name="merge",
    )(x, og, mbc, *consts)


def _plan_kernel(idx_ref, rank_ref, cnt_ref, pos_ref, meta_ref):
    e = N_EXPERTS
    shift = EXPERT_BLOCK.bit_length() - 1
    cnt = cnt_ref[...]
    nblk = lax.shift_right_logical(cnt + (EXPERT_BLOCK - 1), shift)
    ei = lax.broadcasted_iota(I32, (e, e), 0)
    ej = lax.broadcasted_iota(I32, (e, e), 1)
    incl = _dot((ei >= ej).astype(BF16), nblk.astype(F32).astype(BF16)).astype(I32)
    excl = incl - nblk
    start = (excl * EXPERT_BLOCK).astype(F32)
    lane = lax.broadcasted_iota(I32, (e, 128), 1)
    n_used = jnp.broadcast_to(incl[e - 1:e, :], (e, 128))
    meta_ref[...] = jnp.where(lane == 0, excl, jnp.where(lane == 1, nblk, jnp.where(lane == 2, cnt, n_used)))

    e128 = lax.broadcasted_iota(I32, (e, 128), 0)

    def body(i, carry):
        off = pl.multiple_of(i * 128, 128)
        ids = idx_ref[:, pl.ds(off, 128)]
        rows = [jnp.sum(jnp.where(e128 == ids[k:k + 1, :], start, 0.0), axis=0, keepdims=True)
                for k in range(TOP_K)]
        pos_ref[:, pl.ds(off, 128)] = jnp.concatenate(rows, axis=0).astype(I32) + rank_ref[:, pl.ds(off, 128)]
        return carry

    lax.fori_loop(0, idx_ref.shape[1] // 128, body, 0)


def _plan(idx, rank, cnt):
    t = idx.shape[1]
    return pl.pallas_call(
        _plan_kernel,
        grid=(1,),
        in_specs=[_const_spec(idx.shape), _const_spec(rank.shape), _const_spec(cnt.shape)],
        out_specs=[_const_spec((TOP_K, t)), _const_spec((N_EXPERTS, 128))],
        out_shape=[jax.ShapeDtypeStruct((TOP_K, t), I32), jax.ShapeDtypeStruct((N_EXPERTS, 128), I32)],
        compiler_params=_params(("arbitrary",)),
        name="plan",
    )(idx, rank, cnt)


def _sc_scatter_rows(rows, pos, n_slots):
    t, d = rows.shape
    workers = SC_CORES * SC_SUBCORES
    per_w = t // workers
    nwin = per_w // SC_WINDOW
    assert per_w * workers == t and nwin * SC_WINDOW == per_w and nwin % 2 == 0
    idx = pos.reshape(TOP_K, workers, nwin, SC_WINDOW).transpose(1, 2, 0, 3).reshape(workers, nwin * TOP_K, SC_WINDOW)
    mesh = plsc.VectorSubcoreMesh(core_axis_name="core", subcore_axis_name="subcore",
                                  num_cores=SC_CORES, num_subcores=SC_SUBCORES)

    @functools.partial(
        pl.kernel, out_type=jax.ShapeDtypeStruct((n_slots, d), rows.dtype), mesh=mesh, name="sc_scatter_rows",
        scratch_types=[pltpu.VMEM((nwin * TOP_K, SC_WINDOW), I32), pltpu.VMEM((2, SC_WINDOW, d), rows.dtype),
                       pltpu.SemaphoreType.DMA((2,)), pltpu.SemaphoreType.DMA((2,))])
    def scatter(rows_hbm, idx_hbm, out_hbm, idx_v, rows_v, lsem, ssem):
        wid = lax.axis_index("subcore") * SC_CORES + lax.axis_index("core")
        base = wid * per_w
        pltpu.sync_copy(idx_hbm.at[wid], idx_v)

        def load(w, slot):
            return pltpu.make_async_copy(rows_hbm.at[pl.ds(base + w * SC_WINDOW, SC_WINDOW)], rows_v.at[slot],
                                         lsem.at[slot])

        def send(w, k, slot):
            return pltpu.make_async_copy(rows_v.at[slot], out_hbm.at[idx_v.at[w * TOP_K + k]], ssem.at[slot])

        load(0, 0).start()

        @pl.loop(0, nwin, step=2)
        def _(w0):
            for b in range(2):
                w = w0 + b
                load(w, b).wait()

                @pl.when(w + 1 < nwin)
                def _():
                    @pl.when(w >= 1)
                    def _():
                        for k in range(TOP_K):
                            send(w - 1, k, 1 - b).wait()

                    load(w + 1, 1 - b).start()

                for k in range(TOP_K):
                    send(w, k, b).start()

        for k in range(TOP_K):
            send(nwin - 2, k, 0).wait()
        for k in range(TOP_K):
            send(nwin - 1, k, 1).wait()

    return scatter(rows, idx)


def _pack_bf16_pairs(a):
    n = a.shape[1] // 2
    bits = lax.bitcast_convert_type(a.astype(BF16).astype(F32), U32)
    return (bits[:, :n] >> 16) | (bits[:, n:] & jnp.uint32(0xFFFF0000))


def _unpack_bf16_pairs(words):
    lo = lax.bitcast_convert_type(words << 16, F32)
    hi = lax.bitcast_convert_type(words & jnp.uint32(0xFFFF0000), F32)
    return jnp.concatenate([lo, hi], axis=1)


def _experts_kernel(fb_ref, nb_ref, cnt_ref, nu_ref, xs_ref, wg_ref, wu_ref, wd_ref, y_ref,
                    wgu_s, wd_s, xbuf, ybuf, xsem, ysem):
    e = pl.program_id(0)
    n_used = nu_ref[0]
    bm = EXPERT_BLOCK
    row_queue = 1

    def x_copy(g, slot):
        return pltpu.make_async_copy(xs_ref.at[pl.ds(g * bm, bm), :], xbuf.at[slot], xsem.at[slot])

    def y_copy(g, slot):
        return pltpu.make_async_copy(ybuf.at[slot], y_ref.at[pl.ds(g * bm, bm), :], ysem.at[slot])

    @pl.when(e == 0)
    def _():
        for g0 in range(2):
            @pl.when(g0 < n_used)
            def _():
                x_copy(g0, g0).start(priority=row_queue)

    nb = nb_ref[e]

    @pl.when(nb > 0)
    def _():
        wgu_s[:, :D_EXPERT] = wg_ref[0].astype(BF16)
        wgu_s[:, D_EXPERT:] = wu_ref[0].astype(BF16)
        wd_s[...] = wd_ref[0].astype(BF16)

    def process(b0, width):
        g0 = fb_ref[e] + b0
        xs = []
        for i in range(width):
            g = g0 + i
            x_copy(g, g & (X_SLOTS - 1)).wait()
            xs.append(xbuf[g & (X_SLOTS - 1)])
        for i in range(width):
            ahead = g0 + 2 + i

            @pl.when(ahead < n_used)
            def _():
                x_copy(ahead, ahead & (X_SLOTS - 1)).start(priority=row_queue)

        ys = []
        for i in range(width):
            x = _unpack_bf16_pairs(xs[i])
            live = lax.broadcasted_iota(I32, (bm, 1), 0) < cnt_ref[e] - (b0 + i) * bm
            xb = jnp.where(live, x, 0.0).astype(BF16)
            gu = _dot(xb, wgu_s[...])
            gt = gu[:, :D_EXPERT]
            act = (gt * jax.nn.sigmoid(gt) * gu[:, D_EXPERT:]).astype(BF16)
            ys.append(_pack_bf16_pairs(_dot(act, wd_s[...])))
        for i in range(width):
            g = g0 + i
            slot = g & 1

            @pl.when(g >= 2)
            def _():
                y_copy(g - 2, slot).wait()

            ybuf[slot] = ys[i]
            y_copy(g, slot).start(priority=row_queue)

    def pair(p, carry):
        process(2 * p, 2)
        return carry

    lax.fori_loop(0, nb >> 1, pair, 0)

    @pl.when((nb & 1) == 1)
    def _():
        process(nb - 1, 1)

    @pl.when(e == pl.num_programs(0) - 1)
    def _():
        @pl.when(n_used >= 2)
        def _():
            y_copy(n_used - 2, n_used & 1).wait()

        y_copy(n_used - 1, (n_used - 1) & 1).wait()


def _experts(xs, first_block, n_blocks, counts, n_used, w_gate, w_up, w_down):
    n_slots, half = xs.shape
    d = 2 * half
    wsel = lambda e, *_: (e, 0, 0)
    grid_spec = pltpu.PrefetchScalarGridSpec(
        num_scalar_prefetch=4,
        grid=(N_EXPERTS,),
        in_specs=[pl.BlockSpec(memory_space=pl.ANY),
                  pl.BlockSpec((1, d, D_EXPERT), wsel),
                  pl.BlockSpec((1, d, D_EXPERT), wsel),
                  pl.BlockSpec((1, D_EXPERT, d), wsel)],
        out_specs=pl.BlockSpec(memory_space=pl.ANY),
        scratch_shapes=[pltpu.VMEM((d, 2 * D_EXPERT), BF16), pltpu.VMEM((D_EXPERT, d), BF16),
                        pltpu.VMEM((X_SLOTS, EXPERT_BLOCK, half), U32), pltpu.VMEM((2, EXPERT_BLOCK, half), U32),
                        pltpu.SemaphoreType.DMA((X_SLOTS,)), pltpu.SemaphoreType.DMA((2,))],
    )
    return pl.pallas_call(
        _experts_kernel,
        grid_spec=grid_spec,
        out_shape=jax.ShapeDtypeStruct((n_slots, half), U32),
        compiler_params=_params(("arbitrary",)),
        name="experts",
    )(first_block, n_blocks, counts, n_used, xs, w_gate, w_up, w_down)


def _sc_gather_rows(table, indices):
    n = indices.shape[0]
    d = table.shape[1]
    workers = SC_CORES * SC_SUBCORES
    per_w = n // workers
    nch = per_w // SC_WINDOW
    assert per_w * workers == n and nch * SC_WINDOW == per_w and nch % 2 == 0
    mesh = plsc.VectorSubcoreMesh(core_axis_name="core", subcore_axis_name="subcore",
                                  num_cores=SC_CORES, num_subcores=SC_SUBCORES)

    @functools.partial(
        pl.kernel, out_type=jax.ShapeDtypeStruct((n, d), table.dtype), mesh=mesh, name="sc_gather_rows",
        scratch_types=[pltpu.VMEM((per_w,), I32), pltpu.VMEM((2, SC_WINDOW, d), table.dtype),
                       pltpu.SemaphoreType.DMA((2,)), pltpu.SemaphoreType.DMA((2,))])
    def gather(table_hbm, idx_hbm, out_hbm, idx_v, rows_v, gsem, wsem):
        base = (lax.axis_index("subcore") * SC_CORES + lax.axis_index("core")) * per_w
        pltpu.sync_copy(idx_hbm.at[pl.ds(base, per_w)], idx_v)

        def fetch(c, slot):
            return pltpu.make_async_copy(table_hbm.at[idx_v.at[pl.ds(c * SC_WINDOW, SC_WINDOW)]],
                                         rows_v.at[slot], gsem.at[slot])

        def put(c, slot):
            return pltpu.make_async_copy(rows_v.at[slot], out_hbm.at[pl.ds(base + c * SC_WINDOW, SC_WINDOW)],
                                         wsem.at[slot])

        fetch(0, 0).start()

        @pl.loop(0, nch, step=2)
        def _(c0):
            for b in range(2):
                c = c0 + b
                fetch(c, b).wait()

                @pl.when(c + 1 < nch)
                def _():
                    @pl.when(c >= 1)
                    def _():
                        put(c - 1, 1 - b).wait()

                    fetch(c + 1, 1 - b).start()

                put(c, b).start()

        put(nch - 2, 0).wait()
        put(nch - 1, 1).wait()

    return gather(table, indices)


def _combine_kernel(yt_ref, wt_ref, res_ref, g_ref, b_ref, o_ref):
    acc = res_ref[...]
    for k in range(TOP_K):
        acc = acc + _unpack_bf16_pairs(yt_ref[k]) * wt_ref[:, k:k + 1]
    o_ref[...] = _layer_norm(acc, g_ref[...], b_ref[...])


def _combine(y_tok, wts_tok, res, ln_g, ln_b):
    t, d = res.shape
    tok = pl.BlockSpec((COMBINE_TILE, d), lambda i: (i, 0))
    return pl.pallas_call(
        _combine_kernel,
        grid=(t // COMBINE_TILE,),
        in_specs=[pl.BlockSpec((TOP_K, COMBINE_TILE, d // 2), lambda i: (0, i, 0)),
                  pl.BlockSpec((COMBINE_TILE, TOP_K), lambda i: (i, 0)),
                  tok, _const_spec(ln_g.shape), _const_spec(ln_b.shape)],
        out_specs=tok,
        out_shape=jax.ShapeDtypeStruct((t, d), F32),
        compiler_params=_params(("arbitrary",)),
        name="combine",
    )(y_tok, wts_tok, res, ln_g, ln_b)


def _prepare(l, w_in, conv_w, a_log, dt_bias, gdn_norm_g, w_a, sgu_ln_g, sgu_ln_b, sgu_w, sgu_b, w_b,
             w_mem_kv, w_c, w_o, ln1_g, ln1_b, router_w, router_bias, ws_gate, ws_up, ws_down, ln2_g, ln2_b):
    wi = w_in[l]
    d = D_MODEL
    bf = lambda a: a.astype(BF16)
    row = lambda a: a.reshape(1, -1).astype(F32)
    w_ab = jnp.zeros((d, 128), F32).at[:, :2 * GDN_HEADS].set(wi[:, _C_AB:_C_UV])
    prm = jnp.zeros((8, 128), F32).at[0, :GDN_HEADS].set(a_log[l]).at[1, :GDN_HEADS].set(dt_bias[l])
    sgu_bias = jnp.repeat(sgu_b[l].T, SGU_WIDTH // SGU_GROUPS, axis=1)
    return {
        "w_qkv": bf(wi[:, _C_QKV:_C_Z]), "w_z": bf(wi[:, _C_Z:_C_AB]), "w_ab": bf(w_ab),
        "w_uv": bf(wi[:, _C_UV:_C_XQ]), "w_xq": bf(wi[:, _C_XQ:_C_GATE]),
        "w_g0": bf(wi[:, _C_GATE:_C_GATE + d]), "w_g1": bf(wi[:, _C_GATE + d:_C_GATE + 2 * d]),
        "w_g2": bf(wi[:, _C_GATE + 2 * d:_C_GATE + 3 * d]),
        "conv_w": conv_w[l].astype(F32), "gdn_prm": prm, "gdn_norm_g": row(gdn_norm_g[l]),
        "w_a": bf(w_a[l]), "sgu_ln_g": row(sgu_ln_g[l]), "sgu_ln_b": row(sgu_ln_b[l]),
        "sgu_w": sgu_w[l].astype(F32), "sgu_bias": sgu_bias.astype(F32), "w_b": bf(w_b[l]),
        "w_mem_kv": bf(w_mem_kv[l]), "w_c": bf(w_c[l]), "w_o": bf(w_o[l]),
        "ln1_g": row(ln1_g[l]), "ln1_b": row(ln1_b[l]),
        "router_wt": bf(router_w[l].T), "router_bias": router_bias[l].reshape(-1, 1).astype(F32),
        "ws_gu": bf(jnp.concatenate([ws_gate[l], ws_up[l]], axis=1)), "ws_down": bf(ws_down[l]),
        "ln2_g": row(ln2_g[l]), "ln2_b": row(ln2_b[l]),
    }


def _layer(x, mem, wts, w_gate, w_up, w_down):
    bsz, s, d = x.shape
    t = bsz * s
    k, v = _mem_kv(mem, wts["w_mem_kv"])
    mbc = _mix_bc(x, wts, k, v)
    u, w, qd, kd, a, gl, z = _gdn_prep(x, wts)
    og = _gdn_scan(u, w, qd, kd, a, gl, z, wts["gdn_norm_g"])
    res, hp, idx, rw, rank, cnt = _merge(x, og, mbc, wts)
    nb = t * TOP_K // EXPERT_BLOCK + N_EXPERTS
    pos, meta = _plan(idx, rank, cnt)
    xs = _sc_scatter_rows(hp, pos, nb * EXPERT_BLOCK)
    y = _experts(xs, meta[:, 0], meta[:, 1], meta[:, 2], meta[:1, 3], w_gate, w_up, w_down)
    y_tok = _sc_gather_rows(y, pos.reshape(-1)).reshape(TOP_K, t, d // 2)
    out = _combine(y_tok, rw.T, res.reshape(t, d), wts["ln2_g"], wts["ln2_b"])
    return out.reshape(bsz, s, d)


def kernel(x, mem, w_in, conv_w, a_log, dt_bias, gdn_norm_g, w_a, sgu_ln_g, sgu_ln_b, sgu_w, sgu_b, w_b,
           w_mem_kv, w_c, w_o, ln1_g, ln1_b, router_w, router_bias, w_gate, w_up, w_down, ws_gate, ws_up,
           ws_down, ln2_g, ln2_b):
    assert x.shape[1] % TILE == 0 and x.shape[2] == D_MODEL
    for l in range(DEPTH):
        wts = _prepare(l, w_in, conv_w, a_log, dt_bias, gdn_norm_g, w_a, sgu_ln_g, sgu_ln_b, sgu_w, sgu_b,
                       w_b, w_mem_kv, w_c, w_o, ln1_g, ln1_b, router_w, router_bias, ws_gate, ws_up,
                       ws_down, ln2_g, ln2_b)
        x = _layer(x, mem, wts, w_gate[l], w_up[l], w_down[l])
    return x
```

```python
import functools

import jax
import jax.numpy as jnp
from jax import lax
from jax.experimental import pallas as pl
from jax.experimental.pallas import tpu as pltpu
from jax.experimental.pallas import tpu_sc as plsc

F32 = jnp.float32
BF16 = jnp.bfloat16
I32 = jnp.int32
U32 = jnp.uint32

D_MODEL = 1024
DEPTH = 1
GDN_HEADS = 4
HEAD_DIM = 128
GDN_WIDTH = GDN_HEADS * HEAD_DIM
CONV_WIDTH = 4
SGU_GROUPS = 4
SGU_WIDTH = 512
SGU_CHUNK = 128
N_MEM = 256
XATTN_HEADS = 4
XATTN_WIDTH = 512
N_EXPERTS = 256
TOP_K = 8
N_GROUPS = 8
GROUP_SIZE = N_EXPERTS // N_GROUPS
TOPK_GROUPS = 4
D_EXPERT = 256
D_SHARED = 256
ROUTED_SCALE = 2.5
LN_EPS = 1e-5
RMS_EPS = 1e-6
DEEPNORM_ALPHA = (2.0 * DEPTH) ** 0.25

_C_QKV = 0
_C_Z = 3 * GDN_WIDTH
_C_AB = 4 * GDN_WIDTH
_C_UV = _C_AB + 2 * GDN_HEADS
_C_XQ = _C_UV + 2 * SGU_WIDTH
_C_GATE = _C_XQ + XATTN_WIDTH

CHUNK = 128
TILE = 512
HALO = 8
EXPERT_BLOCK = 256
X_SLOTS = 4
W_SLOTS = 3
SC_CORES = 2
SC_SUBCORES = 16
SC_WINDOW = 64
COMBINE_TILE = 256
VMEM_LIMIT = 56 * 1024 * 1024


def _dot(a, b):
    return jnp.dot(a, b, preferred_element_type=F32)


def _dot_nt(a, b):
    return lax.dot_general(a, b, (((1,), (1,)), ((), ())), preferred_element_type=F32)


def _split(a):
    hi = a.astype(BF16)
    lo = (a - hi.astype(F32)).astype(BF16)
    return hi, lo


def _dot3(a, b):
    ah, al = _split(a)
    bh, bl = _split(b)
    return _dot(ah, bh) + (_dot(ah, bl) + _dot(al, bh))


def _softplus(x):
    return jnp.maximum(x, 0.0) + jnp.log1p(jnp.exp(-jnp.abs(x)))


def _layer_norm(x, g, b):
    mu = jnp.mean(x, axis=-1, keepdims=True)
    xc = x - mu
    var = jnp.mean(xc * xc, axis=-1, keepdims=True)
    return xc * lax.rsqrt(var + LN_EPS) * g + b


def _const_spec(shape):
    nd = len(shape)
    return pl.BlockSpec(shape, lambda *_: (0,) * nd)


def _params(semantics):
    return pltpu.CompilerParams(dimension_semantics=semantics, vmem_limit_bytes=VMEM_LIMIT)


def _mem_kv_kernel(mem_ref, w_ref, k_ref, v_ref):
    kv = _dot(mem_ref[0].astype(BF16), w_ref[...])
    k_ref[0] = kv[:, :XATTN_WIDTH].astype(BF16)
    v_ref[0] = kv[:, XATTN_WIDTH:].astype(BF16)


def _mem_kv(mem, w_kv):
    bsz, n_mem, d = mem.shape
    out = jax.ShapeDtypeStruct((bsz, n_mem, XATTN_WIDTH), BF16)
    return pl.pallas_call(
        _mem_kv_kernel,
        grid=(bsz,),
        in_specs=[pl.BlockSpec((1, n_mem, d), lambda b: (b, 0, 0)), _const_spec(w_kv.shape)],
        out_specs=[pl.BlockSpec((1, n_mem, XATTN_WIDTH), lambda b: (b, 0, 0))] * 2,
        out_shape=[out, out],
        compiler_params=_params(("arbitrary",)),
        name="mem_kv",
    )(mem, w_kv)


def _mix_bc_kernel(x_ref, wuv_ref, wg1_ref, wb_ref, lng_ref, lnb_ref, sw_ref, sb_ref,
                   wxq_ref, k_ref, v_ref, wg2_ref, wc_ref, o_ref):
    xb = x_ref[0].astype(BF16)
    uv = jax.nn.gelu(_dot(xb, wuv_ref[...]), approximate=True)
    u = uv[:, :SGU_WIDTH]
    v = _layer_norm(uv[:, SGU_WIDTH:], lng_ref[...], lnb_ref[...])
    ii = lax.broadcasted_iota(I32, (SGU_CHUNK, SGU_CHUNK), 0)
    jj = lax.broadcasted_iota(I32, (SGU_CHUNK, SGU_CHUNK), 1)
    rows = []
    for n in range(TILE // SGU_CHUNK):
        cols = []
        for g in range(SGU_GROUPS):
            wt = jnp.where(ii >= jj, sw_ref[g], 0.0).astype(BF16)
            vg = v[n * SGU_CHUNK:(n + 1) * SGU_CHUNK, g * 128:(g + 1) * 128].astype(BF16)
            cols.append(_dot(wt, vg))
        rows.append(jnp.concatenate(cols, axis=1) + sb_ref[...])
    sg = jnp.concatenate(rows, axis=0)
    br_b = _dot((u * sg).astype(BF16), wb_ref[...])
    acc = jax.nn.sigmoid(_dot(xb, wg1_ref[...])) * br_b
    xq = _dot(xb, wxq_ref[...])
    heads = []
    for h in range(XATTN_HEADS):
        qh = xq[:, h * 128:(h + 1) * 128].astype(BF16)
        kh = k_ref[0, :, h * 128:(h + 1) * 128]
        vh = v_ref[0, :, h * 128:(h + 1) * 128]
        sc = _dot_nt(qh, kh) * (128 ** -0.5)
        p = jnp.exp(sc - jnp.max(sc, axis=-1, keepdims=True))
        p = p / jnp.sum(p, axis=-1, keepdims=True)
        heads.append(_dot(p.astype(BF16), vh))
    att = jnp.concatenate(heads, axis=1).astype(BF16)
    br_c = _dot(att, wc_ref[...])
    o_ref[0] = acc + jax.nn.sigmoid(_dot(xb, wg2_ref[...])) * br_c


def _mix_bc(x, wts, k, v):
    bsz, s, d = x.shape
    tok = pl.BlockSpec((1, TILE, d), lambda b, i: (b, i, 0))
    kvs = pl.BlockSpec((1, N_MEM, XATTN_WIDTH), lambda b, i: (b, 0, 0))
    names = ("w_uv", "w_g1", "w_b", "sgu_ln_g", "sgu_ln_b", "sgu_w", "sgu_bias", "w_xq")
    consts = [wts[n] for n in names]
    tail = [wts["w_g2"], wts["w_c"]]
    return pl.pallas_call(
        _mix_bc_kernel,
        grid=(bsz, s // TILE),
        in_specs=[tok] + [_const_spec(c.shape) for c in consts] + [kvs, kvs]
        + [_const_spec(c.shape) for c in tail],
        out_specs=tok,
        out_shape=jax.ShapeDtypeStruct((bsz, s, d), F32),
        compiler_params=_params(("arbitrary", "arbitrary")),
        name="mix_bc",
    )(x, *consts, k, v, *tail)


def _unit_lower_inverses(lmats, ii, jj):
    eye = (ii == jj).astype(F32)
    x = ii ^ jj
    lsplit = [_split(l) for l in lmats]
    ts = [eye - jnp.where(x < 2, l, 0.0) for l in lmats]
    zero = jnp.zeros((CHUNK, CHUNK), BF16)
    s = 2
    while s < CHUNK:
        off = (x < 2 * s) & ((ii & s) != 0) & ((jj & s) == 0)
        tsplit = [_split(t) for t in ts]
        prods = []
        for (th, tl), (lh, ll) in zip(tsplit, lsplit):
            oh = jnp.where(off, lh, zero)
            ol = jnp.where(off, ll, zero)
            prods.append(_dot(th, oh) + (_dot(th, ol) + _dot(tl, oh)))
        nxt = []
        for t, p, (th, tl) in zip(ts, prods, tsplit):
            ph, plo = _split(p)
            nxt.append(t - (_dot(ph, th) + (_dot(ph, tl) + _dot(plo, th))))
        ts = nxt
        s *= 2
    return ts


def _gdn_prep_kernel(x_ref, wqkv_ref, wz_ref, wab_ref, convw_ref, prm_ref,
                     u_ref, w_ref, qd_ref, kd_ref, a_ref, gl_ref, z_ref, hist_ref):
    @pl.when(pl.program_id(1) == 0)
    def _():
        hist_ref[0:HALO, :] = jnp.zeros((HALO, 3 * GDN_WIDTH), F32)

    xb = x_ref[0].astype(BF16)
    z_ref[0] = _dot(xb, wz_ref[...])
    qkv = _dot(xb, wqkv_ref[...])
    hist_ref[HALO:HALO + TILE, :] = qkv
    conv = qkv * convw_ref[CONV_WIDTH - 1:CONV_WIDTH, :]
    for j in range(CONV_WIDTH - 1):
        shift = CONV_WIDTH - 1 - j
        conv = conv + hist_ref[HALO - shift:HALO - shift + TILE, :] * convw_ref[j:j + 1, :]
    hist_ref[0:HALO, :] = qkv[TILE - HALO:, :]
    qkv = conv * jax.nn.sigmoid(conv)

    ab = _dot(xb, wab_ref[...])
    g = -jnp.exp(prm_ref[0:1, :]) * _softplus(ab + prm_ref[1:2, :])
    beta = jax.nn.sigmoid(ab)

    ii = lax.broadcasted_iota(I32, (CHUNK, CHUNK), 0)
    jj = lax.broadcasted_iota(I32, (CHUNK, CHUNK), 1)
    tri = (ii >= jj).astype(BF16)
    items = []
    for c in range(TILE // CHUNK):
        r0 = c * CHUNK
        gch = g[r0:r0 + CHUNK, :]
        g1 = gch.astype(BF16)
        r1 = gch - g1.astype(F32)
        g2 = r1.astype(BF16)
        g3 = (r1 - g2.astype(F32)).astype(BF16)
        gc = _dot(tri, g1) + (_dot(tri, g2) + _dot(tri, g3))
        gct = gc.T
        for h in range(GDN_HEADS):
            qh = qkv[r0:r0 + CHUNK, h * HEAD_DIM:(h + 1) * HEAD_DIM]
            kh = qkv[r0:r0 + CHUNK, GDN_WIDTH + h * HEAD_DIM:GDN_WIDTH + (h + 1) * HEAD_DIM]
            vh = qkv[r0:r0 + CHUNK, 2 * GDN_WIDTH + h * HEAD_DIM:2 * GDN_WIDTH + (h + 1) * HEAD_DIM]
            qn = qh * lax.rsqrt(jnp.sum(qh * qh, axis=-1, keepdims=True) + RMS_EPS) * (HEAD_DIM ** -0.5)
            kn = kh * lax.rsqrt(jnp.sum(kh * kh, axis=-1, keepdims=True) + RMS_EPS)
            gcol = gc[:, h:h + 1]
            grow = gct[h:h + 1, :]
            bcol = beta[r0:r0 + CHUNK, GDN_HEADS + h:GDN_HEADS + h + 1]
            glast = gc[CHUNK - 1:CHUNK, h:h + 1]
            decay = jnp.exp(jnp.where(ii >= jj, gcol - grow, -jnp.inf))
            knb = kn.astype(BF16)
            kk = _dot_nt(knb, knb)
            qk = _dot_nt(qn.astype(BF16), knb)
            egc = jnp.exp(gcol)
            items.append(dict(
                c=c, h=h, lmat=jnp.where(ii > jj, kk * bcol * decay, 0.0),
                rhs=jnp.concatenate([vh * bcol, kn * (bcol * egc)], axis=1),
                qd=(qn * egc).astype(BF16), kd=(kn * jnp.exp(glast - gcol)).astype(BF16),
                a=(qk * decay).astype(BF16), gl=jnp.broadcast_to(jnp.exp(glast), (8, HEAD_DIM))))
    tinvs = _unit_lower_inverses([it["lmat"] for it in items], ii, jj)
    sols = [_dot3(tinv, it["rhs"]) for tinv, it in zip(tinvs, items)]
    for it, sol in zip(items, sols):
        rows = slice(it["c"] * CHUNK, (it["c"] + 1) * CHUNK)
        lanes = slice(it["h"] * HEAD_DIM, (it["h"] + 1) * HEAD_DIM)
        u_ref[0, rows, lanes] = sol[:, :HEAD_DIM]
        w_ref[0, rows, lanes] = sol[:, HEAD_DIM:].astype(BF16)
        qd_ref[0, rows, lanes] = it["qd"]
        kd_ref[0, rows, lanes] = it["kd"]
        a_ref[0, rows, lanes] = it["a"]
        gl_ref[0, it["c"] * 8:(it["c"] + 1) * 8, lanes] = it["gl"]


def _gdn_prep(x, wts):
    bsz, s, d = x.shape
    tok = pl.BlockSpec((1, TILE, d), lambda b, i: (b, i, 0))
    hw = pl.BlockSpec((1, TILE, GDN_WIDTH), lambda b, i: (b, i, 0))
    gls = pl.BlockSpec((1, 8 * TILE // CHUNK, GDN_WIDTH), lambda b, i: (b, i, 0))
    consts = [wts[n] for n in ("w_qkv", "w_z", "w_ab", "conv_w", "gdn_prm")]
    f32o = jax.ShapeDtypeStruct((bsz, s, GDN_WIDTH), F32)
    b16o = jax.ShapeDtypeStruct((bsz, s, GDN_WIDTH), BF16)
    glo = jax.ShapeDtypeStruct((bsz, 8 * s // CHUNK, GDN_WIDTH), F32)
    return pl.pallas_call(
        _gdn_prep_kernel,
        grid=(bsz, s // TILE),
        in_specs=[tok] + [_const_spec(c.shape) for c in consts],
        out_specs=[hw, hw, hw, hw, hw, gls, hw],
        out_shape=[f32o, b16o, b16o, b16o, b16o, glo, f32o],
        scratch_shapes=[pltpu.VMEM((HALO + TILE, 3 * GDN_WIDTH), F32)],
        compiler_params=_params(("arbitrary", "arbitrary")),
        name="gdn_prep",
    )(x, *consts)


def _gdn_scan_kernel(u_ref, w_ref, qd_ref, kd_ref, a_ref, gl_ref, z_ref, gn_ref, o_ref, state_ref):
    @pl.when(pl.program_id(0) == 0)
    def _():
        state_ref[...] = jnp.zeros(state_ref.shape, F32)

    bsz = u_ref.shape[0]
    for c in range(TILE // CHUNK):
        rows = slice(c * CHUNK, (c + 1) * CHUNK)
        for b in range(bsz):
            for h in range(GDN_HEADS):
                lanes = slice(h * HEAD_DIM, (h + 1) * HEAD_DIM)
                st = state_ref[b, h]
                wq = jnp.concatenate([w_ref[b, rows, lanes], qd_ref[b, rows, lanes]], axis=0)
                ws = _dot(wq, st.astype(BF16))
                v_new = u_ref[b, rows, lanes] - ws[:CHUNK]
                vb = v_new.astype(BF16)
                o = ws[CHUNK:] + _dot(a_ref[b, rows, lanes], vb)
                kdt = kd_ref[b, rows, lanes].astype(F32).T.astype(BF16)
                state_ref[b, h] = st * gl_ref[b, c * 8:c * 8 + 1, lanes] + _dot(kdt, vb)
                zz = z_ref[b, rows, lanes]
                o = o * lax.rsqrt(jnp.mean(o * o, axis=-1, keepdims=True) + RMS_EPS) * gn_ref[...]
                o_ref[b, rows, lanes] = (o * (zz * jax.nn.sigmoid(zz))).astype(BF16)


def _gdn_scan(u, w, qd, kd, a, gl, z, gn):
    bsz, s, _ = u.shape
    hw = pl.BlockSpec((bsz, TILE, GDN_WIDTH), lambda i: (0, i, 0))
    gls = pl.BlockSpec((bsz, 8 * TILE // CHUNK, GDN_WIDTH), lambda i: (0, i, 0))
    return pl.pallas_call(
        _gdn_scan_kernel,
        grid=(s // TILE,),
        in_specs=[hw, hw, hw, hw, hw, gls, hw, _const_spec(gn.shape)],
        out_specs=hw,
        out_shape=jax.ShapeDtypeStruct((bsz, s, GDN_WIDTH), BF16),
        scratch_shapes=[pltpu.VMEM((bsz, GDN_HEADS, HEAD_DIM, HEAD_DIM), F32)],
        compiler_params=_params(("arbitrary",)),
        name="gdn_scan",
    )(u, w, qd, kd, a, gl, z, gn)


def _merge_kernel(x_ref, og_ref, mbc_ref, wa_ref, wg0_ref, wo_ref, ln1g_ref, ln1b_ref,
                  rwt_ref, rb_ref, wsgu_ref, wsd_ref,
                  res_ref, hp_ref, idx_ref, wt_ref, rank_ref, cnt_ref, carry_ref):
    first = (pl.program_id(0) == 0) & (pl.program_id(1) == 0)

    @pl.when(first)
    def _():
        carry_ref[...] = jnp.zeros(carry_ref.shape, F32)

    x = x_ref[0]
    xb = x.astype(BF16)
    br_a = _dot(og_ref[0], wa_ref[...])
    merged = jax.nn.sigmoid(_dot(xb, wg0_ref[...])) * br_a + mbc_ref[0]
    y = _dot(merged.astype(BF16), wo_ref[...])
    h = _layer_norm(DEEPNORM_ALPHA * x + y, ln1g_ref[...], ln1b_ref[...])
    hb = h.astype(BF16)
    hp_ref[...] = _pack_bf16_pairs(h)

    gu = _dot(hb, wsgu_ref[...])
    gt = gu[:, :D_SHARED]
    act = gt * jax.nn.sigmoid(gt) * gu[:, D_SHARED:]
    res_ref[0] = DEEPNORM_ALPHA * h + _dot(act.astype(BF16), wsd_ref[...])

    scores = jax.nn.sigmoid(_dot_nt(rwt_ref[...], hb))
    biased = scores + rb_ref[...]
    g3 = biased.reshape(N_GROUPS, GROUP_SIZE, TILE)
    m1 = jnp.max(g3, axis=1)
    m1b = m1[:, None, :]
    n_top = jnp.sum((g3 == m1b).astype(F32), axis=1)
    m2 = jnp.max(jnp.where(g3 < m1b, g3, -jnp.inf), axis=1)
    gs = m1 + jnp.where(n_top >= 2.0, m1, m2)
    gidx = lax.broadcasted_iota(I32, (N_GROUPS, TILE), 0)
    beaten = jnp.zeros((N_GROUPS, TILE), F32)
    for g in range(N_GROUPS):
        row = gs[g:g + 1, :]
        beaten = beaten + ((row > gs) | ((row == gs) & (g < gidx))).astype(F32)
    sel = (beaten < float(TOPK_GROUPS)).astype(F32)
    sel_e = jnp.broadcast_to(sel[:, None, :], (N_GROUPS, GROUP_SIZE, TILE)).reshape(N_EXPERTS, TILE)
    masked = jnp.where(sel_e > 0.5, biased, -jnp.inf)
    eidx = lax.broadcasted_iota(I32, (N_EXPERTS, TILE), 0)
    chosen = jnp.zeros((N_EXPERTS, TILE), F32)
    picks, pick_scores = [], []
    for _k in range(TOP_K):
        mx = jnp.max(masked, axis=0, keepdims=True)
        pick = jnp.min(jnp.where(masked == mx, eidx, N_EXPERTS), axis=0, keepdims=True)
        hot = eidx == pick
        picks.append(pick)
        pick_scores.append(jnp.sum(jnp.where(hot, scores, 0.0), axis=0, keepdims=True))
        chosen = jnp.where(hot, 1.0, chosen)
        masked = jnp.where(hot, -jnp.inf, masked)
    total = pick_scores[0]
    for sck in pick_scores[1:]:
        total = total + sck
    idx_ref[...] = jnp.concatenate(picks, axis=0)
    wt_ref[...] = jnp.concatenate([sck / total * ROUTED_SCALE for sck in pick_scores], axis=0)

    ti = lax.broadcasted_iota(I32, (TILE, TILE), 0)
    tj = lax.broadcasted_iota(I32, (TILE, TILE), 1)
    before = _dot(chosen.astype(BF16), (ti < tj).astype(BF16))
    carry = carry_ref[...]
    before = before + jnp.concatenate([carry] * (TILE // 128), axis=1)
    rank_ref[...] = jnp.concatenate(
        [jnp.sum(jnp.where(eidx == p, before, 0.0), axis=0, keepdims=True) for p in picks],
        axis=0).astype(I32)
    carry = carry + _dot(chosen.astype(BF16), jnp.ones((TILE, 128), BF16))
    carry_ref[...] = carry
    cnt_ref[...] = carry.astype(I32)


def _merge(x, og, mbc, wts):
    bsz, s, d = x.shape
    t = bsz * s
    nt = s // TILE
    tok = pl.BlockSpec((1, TILE, d), lambda b, i: (b, i, 0))
    ogs = pl.BlockSpec((1, TILE, GDN_WIDTH), lambda b, i: (b, i, 0))
    flat = lambda b, i: (b * nt + i, 0)
    lane = lambda b, i: (0, b * nt + i)
    consts = [wts[n] for n in ("w_a", "w_g0", "w_o", "ln1_g", "ln1_b", "router_wt", "router_bias",
                               "ws_gu", "ws_down")]
    return pl.pallas_call(
        _merge_kernel,
        grid=(bsz, nt),
        in_specs=[tok, ogs, tok] + [_const_spec(c.shape) for c in consts],
        out_specs=[tok,
                   pl.BlockSpec((TILE, d // 2), flat),
                   pl.BlockSpec((TOP_K, TILE), lane),
                   pl.BlockSpec((TOP_K, TILE), lane),
                   pl.BlockSpec((TOP_K, TILE), lane),
                   _const_spec((N_EXPERTS, 128))],
        out_shape=[jax.ShapeDtypeStruct((bsz, s, d), F32),
                   jax.ShapeDtypeStruct((t, d // 2), U32),
                   jax.ShapeDtypeStruct((TOP_K, t), I32),
                   jax.ShapeDtypeStruct((TOP_K, t), F32),
                   jax.ShapeDtypeStruct((TOP_K, t), I32),
                   jax.ShapeDtypeStruct((N_EXPERTS, 128), I32)],
        scratch_shapes=[pltpu.VMEM((N_EXPERTS, 128), F32)],
        compiler_params=_params(("arbitrary", "arbitrary")),
        name="merge",
    )(x, og, mbc, *consts)


def _plan_kernel(idx_ref, rank_ref, cnt_ref, pos_ref, meta_ref):
    e = N_EXPERTS
    shift = EXPERT_BLOCK.bit_length() - 1
    cnt = cnt_ref[...]
    nblk = lax.shift_right_logical(cnt + (EXPERT_BLOCK - 1), shift)
    ei = lax.broadcasted_iota(I32, (e, e), 0)
    ej = lax.broadcasted_iota(I32, (e, e), 1)
    incl = _dot((ei >= ej).astype(BF16), nblk.astype(F32).astype(BF16)).astype(I32)
    excl = incl - nblk
    start = (excl * EXPERT_BLOCK).astype(F32)
    lane = lax.broadcasted_iota(I32, (e, 128), 1)
    n_used = jnp.broadcast_to(incl[e - 1:e, :], (e, 128))
    meta_ref[...] = jnp.where(lane == 0, excl, jnp.where(lane == 1, nblk, jnp.where(lane == 2, cnt, n_used)))

    e128 = lax.broadcasted_iota(I32, (e, 128), 0)

    def body(i, carry):
        off = pl.multiple_of(i * 128, 128)
        ids = idx_ref[:, pl.ds(off, 128)]
        rows = [jnp.sum(jnp.where(e128 == ids[k:k + 1, :], start, 0.0), axis=0, keepdims=True)
                for k in range(TOP_K)]
        pos_ref[:, pl.ds(off, 128)] = jnp.concatenate(rows, axis=0).astype(I32) + rank_ref[:, pl.ds(off, 128)]
        return carry

    lax.fori_loop(0, idx_ref.shape[1] // 128, body, 0)


def _plan(idx, rank, cnt):
    t = idx.shape[1]
    return pl.pallas_call(
        _plan_kernel,
        grid=(1,),
        in_specs=[_const_spec(idx.shape), _const_spec(rank.shape), _const_spec(cnt.shape)],
        out_specs=[_const_spec((TOP_K, t)), _const_spec((N_EXPERTS, 128))],
        out_shape=[jax.ShapeDtypeStruct((TOP_K, t), I32), jax.ShapeDtypeStruct((N_EXPERTS, 128), I32)],
        compiler_params=_params(("arbitrary",)),
        name="plan",
    )(idx, rank, cnt)


def _sc_scatter_rows(rows, pos, n_slots):
    t, d = rows.shape
    workers = SC_CORES * SC_SUBCORES
    per_w = t // workers
    nwin = per_w // SC_WINDOW
    assert per_w * workers == t and nwin * SC_WINDOW == per_w and nwin % 2 == 0
    idx = pos.reshape(TOP_K, workers, nwin, SC_WINDOW).transpose(1, 2, 0, 3).reshape(workers, nwin * TOP_K, SC_WINDOW)
    mesh = plsc.VectorSubcoreMesh(core_axis_name="core", subcore_axis_name="subcore",
                                  num_cores=SC_CORES, num_subcores=SC_SUBCORES)

    @functools.partial(
        pl.kernel, out_type=jax.ShapeDtypeStruct((n_slots, d), rows.dtype), mesh=mesh, name="sc_scatter_rows",
        scratch_types=[pltpu.VMEM((nwin * TOP_K, SC_WINDOW), I32), pltpu.VMEM((2, SC_WINDOW, d), rows.dtype),
                       pltpu.SemaphoreType.DMA((2,)), pltpu.SemaphoreType.DMA((2,))])
    def scatter(rows_hbm, idx_hbm, out_hbm, idx_v, rows_v, lsem, ssem):
        wid = lax.axis_index("subcore") * SC_CORES + lax.axis_index("core")
        base = wid * per_w
        pltpu.sync_copy(idx_hbm.at[wid], idx_v)

        def load(w, slot):
            return pltpu.make_async_copy(rows_hbm.at[pl.ds(base + w * SC_WINDOW, SC_WINDOW)], rows_v.at[slot],
                                         lsem.at[slot])

        def send(w, k, slot):
            return pltpu.make_async_copy(rows_v.at[slot], out_hbm.at[idx_v.at[w * TOP_K + k]], ssem.at[slot])

        load(0, 0).start()

        @pl.loop(0, nwin, step=2)
        def _(w0):
            for b in range(2):
                w = w0 + b
                load(w, b).wait()

                @pl.when(w + 1 < nwin)
                def _():
                    @pl.when(w >= 1)
                    def _():
                        for k in range(TOP_K):
                            send(w - 1, k, 1 - b).wait()

                    load(w + 1, 1 - b).start()

                for k in range(TOP_K):
                    send(w, k, b).start()

        for k in range(TOP_K):
            send(nwin - 2, k, 0).wait()
        for k in range(TOP_K):
            send(nwin - 1, k, 1).wait()

    return scatter(rows, idx)


def _pack_bf16_pairs(a):
    n = a.shape[1] // 2
    bits = lax.bitcast_convert_type(a.astype(BF16).astype(F32), U32)
    return (bits[:, :n] >> 16) | (bits[:, n:] & jnp.uint32(0xFFFF0000))


def _unpack_bf16_pairs(words):
    lo = lax.bitcast_convert_type(words << 16, F32)
    hi = lax.bitcast_convert_type(words & jnp.uint32(0xFFFF0000), F32)
    return jnp.concatenate([lo, hi], axis=1)


def _experts_kernel(fb_ref, nb_ref, cnt_ref, nu_ref, xs_ref, wg_ref, wu_ref, wd_ref, y_ref,
                    wgbuf, wubuf, wdbuf, wsem, wgu_s, wd_s, xbuf, ybuf, xsem, ysem):
    e = pl.program_id(0)
    n_used = nu_ref[0]
    bm = EXPERT_BLOCK
    row_queue = 1

    def x_copy(g, slot):
        return pltpu.make_async_copy(xs_ref.at[pl.ds(g * bm, bm), :], xbuf.at[slot], xsem.at[slot])

    def y_copy(g, slot):
        return pltpu.make_async_copy(ybuf.at[slot], y_ref.at[pl.ds(g * bm, bm), :], ysem.at[slot])

    @pl.when(e == 0)
    def _():
        for g0 in range(2):
            @pl.when(g0 < n_used)
            def _():
                x_copy(g0, g0).start(priority=row_queue)

    def w_copies(j, slot):
        return [pltpu.make_async_copy(src.at[j], dst.at[slot], wsem.at[slot])
                for src, dst in ((wg_ref, wgbuf), (wu_ref, wubuf), (wd_ref, wdbuf))]

    @pl.when(e == 0)
    def _():
        for j in range(W_SLOTS - 1):
            for cp in w_copies(j, j):
                cp.start()

    wslot = lax.rem(e, W_SLOTS)
    for cp in w_copies(e, wslot):
        cp.wait()
    ahead_e = e + (W_SLOTS - 1)

    @pl.when(ahead_e < pl.num_programs(0))
    def _():
        for cp in w_copies(ahead_e, lax.rem(ahead_e, W_SLOTS)):
            cp.start()

    nb = nb_ref[e]

    @pl.when(nb > 0)
    def _():
        wgu_s[:, :D_EXPERT] = wgbuf[wslot].astype(BF16)
        wgu_s[:, D_EXPERT:] = wubuf[wslot].astype(BF16)
        wd_s[...] = wdbuf[wslot].astype(BF16)

    def process(b0, width):
        g0 = fb_ref[e] + b0
        xs = []
        for i in range(width):
            g = g0 + i
            x_copy(g, g & (X_SLOTS - 1)).wait()
            xs.append(xbuf[g & (X_SLOTS - 1)])
        for i in range(width):
            ahead = g0 + 2 + i

            @pl.when(ahead < n_used)
            def _():
                x_copy(ahead, ahead & (X_SLOTS - 1)).start(priority=row_queue)

        ys = []
        for i in range(width):
            x = _unpack_bf16_pairs(xs[i])
            live = lax.broadcasted_iota(I32, (bm, 1), 0) < cnt_ref[e] - (b0 + i) * bm
            xb = jnp.where(live, x, 0.0).astype(BF16)
            gu = _dot(xb, wgu_s[...])
            gt = gu[:, :D_EXPERT]
            act = (gt * jax.nn.sigmoid(gt) * gu[:, D_EXPERT:]).astype(BF16)
            ys.append(_pack_bf16_pairs(_dot(act, wd_s[...])))
        for i in range(width):
            g = g0 + i
            slot = g & 1

            @pl.when(g >= 2)
            def _():
                y_copy(g - 2, slot).wait()

            ybuf[slot] = ys[i]
            y_copy(g, slot).start(priority=row_queue)

    def pair(p, carry):
        process(2 * p, 2)
        return carry

    lax.fori_loop(0, nb >> 1, pair, 0)

    @pl.when((nb & 1) == 1)
    def _():
        process(nb - 1, 1)

    @pl.when(e == pl.num_programs(0) - 1)
    def _():
        @pl.when(n_used >= 2)
        def _():
            y_copy(n_used - 2, n_used & 1).wait()

        y_copy(n_used - 1, (n_used - 1) & 1).wait()


def _experts(xs, first_block, n_blocks, counts, n_used, w_gate, w_up, w_down):
    n_slots, half = xs.shape
    d = 2 * half
    grid_spec = pltpu.PrefetchScalarGridSpec(
        num_scalar_prefetch=4,
        grid=(N_EXPERTS,),
        in_specs=[pl.BlockSpec(memory_space=pl.ANY)] * 4,
        out_specs=pl.BlockSpec(memory_space=pl.ANY),
        scratch_shapes=[pltpu.VMEM((W_SLOTS, d, D_EXPERT), F32), pltpu.VMEM((W_SLOTS, d, D_EXPERT), F32),
                        pltpu.VMEM((W_SLOTS, D_EXPERT, d), F32), pltpu.SemaphoreType.DMA((W_SLOTS,)),
                        pltpu.VMEM((d, 2 * D_EXPERT), BF16), pltpu.VMEM((D_EXPERT, d), BF16),
                        pltpu.VMEM((X_SLOTS, EXPERT_BLOCK, half), U32), pltpu.VMEM((2, EXPERT_BLOCK, half), U32),
                        pltpu.SemaphoreType.DMA((X_SLOTS,)), pltpu.SemaphoreType.DMA((2,))],
    )
    return pl.pallas_call(
        _experts_kernel,
        grid_spec=grid_spec,
        out_shape=jax.ShapeDtypeStruct((n_slots, half), U32),
        compiler_params=_params(("arbitrary",)),
        name="experts",
    )(first_block, n_blocks, counts, n_used, xs, w_gate, w_up, w_down)


def _sc_gather_rows(table, indices):
    n = indices.shape[0]
    d = table.shape[1]
    workers = SC_CORES * SC_SUBCORES
    per_w = n // workers
    nch = per_w // SC_WINDOW
    assert per_w * workers == n and nch * SC_WINDOW == per_w and nch % 2 == 0
    mesh = plsc.VectorSubcoreMesh(core_axis_name="core", subcore_axis_name="subcore",
                                  num_cores=SC_CORES, num_subcores=SC_SUBCORES)

    @functools.partial(
        pl.kernel, out_type=jax.ShapeDtypeStruct((n, d), table.dtype), mesh=mesh, name="sc_gather_rows",
        scratch_types=[pltpu.VMEM((per_w,), I32), pltpu.VMEM((2, SC_WINDOW, d), table.dtype),
                       pltpu.SemaphoreType.DMA((2,)), pltpu.SemaphoreType.DMA((2,))])
    def gather(table_hbm, idx_hbm, out_hbm, idx_v, rows_v, gsem, wsem):
        base = (lax.axis_index("subcore") * SC_CORES + lax.axis_index("core")) * per_w
        pltpu.sync_copy(idx_hbm.at[pl.ds(base, per_w)], idx_v)

        def fetch(c, slot):
            return pltpu.make_async_copy(table_hbm.at[idx_v.at[pl.ds(c * SC_WINDOW, SC_WINDOW)]],
                                         rows_v.at[slot], gsem.at[slot])

        def put(c, slot):
            return pltpu.make_async_copy(rows_v.at[slot], out_hbm.at[pl.ds(base + c * SC_WINDOW, SC_WINDOW)],
                                         wsem.at[slot])

        fetch(0, 0).start()

        @pl.loop(0, nch, step=2)
        def _(c0):
            for b in range(2):
                c = c0 + b
                fetch(c, b).wait()

                @pl.when(c + 1 < nch)
                def _():
                    @pl.when(c >= 1)
                    def _():
                        put(c - 1, 1 - b).wait()

                    fetch(c + 1, 1 - b).start()

                put(c, b).start()

        put(nch - 2, 0).wait()
        put(nch - 1, 1).wait()

    return gather(table, indices)


def _combine_kernel(yt_ref, wt_ref, res_ref, g_ref, b_ref, o_ref):
    acc = res_ref[...]
    for k in range(TOP_K):
        acc = acc + _unpack_bf16_pairs(yt_ref[k]) * wt_ref[:, k:k + 1]
    o_ref[...] = _layer_norm(acc, g_ref[...], b_ref[...])


def _combine(y_tok, wts_tok, res, ln_g, ln_b):
    t, d = res.shape
    tok = pl.BlockSpec((COMBINE_TILE, d), lambda i: (i, 0))
    return pl.pallas_call(
        _combine_kernel,
        grid=(t // COMBINE_TILE,),
        in_specs=[pl.BlockSpec((TOP_K, COMBINE_TILE, d // 2), lambda i: (0, i, 0)),
                  pl.BlockSpec((COMBINE_TILE, TOP_K), lambda i: (i, 0)),
                  tok, _const_spec(ln_g.shape), _const_spec(ln_b.shape)],
        out_specs=tok,
        out_shape=jax.ShapeDtypeStruct((t, d), F32),
        compiler_params=_params(("arbitrary",)),
        name="combine",
    )(y_tok, wts_tok, res, ln_g, ln_b)


def _prepare(l, w_in, conv_w, a_log, dt_bias, gdn_norm_g, w_a, sgu_ln_g, sgu_ln_b, sgu_w, sgu_b, w_b,
             w_mem_kv, w_c, w_o, ln1_g, ln1_b, router_w, router_bias, ws_gate, ws_up, ws_down, ln2_g, ln2_b):
    wi = w_in[l]
    d = D_MODEL
    bf = lambda a: a.astype(BF16)
    row = lambda a: a.reshape(1, -1).astype(F32)
    w_ab = jnp.zeros((d, 128), F32).at[:, :2 * GDN_HEADS].set(wi[:, _C_AB:_C_UV])
    prm = jnp.zeros((8, 128), F32).at[0, :GDN_HEADS].set(a_log[l]).at[1, :GDN_HEADS].set(dt_bias[l])
    sgu_bias = jnp.repeat(sgu_b[l].T, SGU_WIDTH // SGU_GROUPS, axis=1)
    return {
        "w_qkv": bf(wi[:, _C_QKV:_C_Z]), "w_z": bf(wi[:, _C_Z:_C_AB]), "w_ab": bf(w_ab),
        "w_uv": bf(wi[:, _C_UV:_C_XQ]), "w_xq": bf(wi[:, _C_XQ:_C_GATE]),
        "w_g0": bf(wi[:, _C_GATE:_C_GATE + d]), "w_g1": bf(wi[:, _C_GATE + d:_C_GATE + 2 * d]),
        "w_g2": bf(wi[:, _C_GATE + 2 * d:_C_GATE + 3 * d]),
        "conv_w": conv_w[l].astype(F32), "gdn_prm": prm, "gdn_norm_g": row(gdn_norm_g[l]),
        "w_a": bf(w_a[l]), "sgu_ln_g": row(sgu_ln_g[l]), "sgu_ln_b": row(sgu_ln_b[l]),
        "sgu_w": sgu_w[l].astype(F32), "sgu_bias": sgu_bias.astype(F32), "w_b": bf(w_b[l]),
        "w_mem_kv": bf(w_mem_kv[l]), "w_c": bf(w_c[l]), "w_o": bf(w_o[l]),
        "ln1_g": row(ln1_g[l]), "ln1_b": row(ln1_b[l]),
        "router_wt": bf(router_w[l].T), "router_bias": router_bias[l].reshape(-1, 1).astype(F32),
        "ws_gu": bf(jnp.concatenate([ws_gate[l], ws_up[l]], axis=1)), "ws_down": bf(ws_down[l]),
        "ln2_g": row(ln2_g[l]), "ln2_b": row(ln2_b[l]),
    }


def _layer(x, mem, wts, w_gate, w_up, w_down):
    bsz, s, d = x.shape
    t = bsz * s
    k, v = _mem_kv(mem, wts["w_mem_kv"])
    mbc = _mix_bc(x, wts, k, v)
    u, w, qd, kd, a, gl, z = _gdn_prep(x, wts)
    og = _gdn_scan(u, w, qd, kd, a, gl, z, wts["gdn_norm_g"])
    res, hp, idx, rw, rank, cnt = _merge(x, og, mbc, wts)
    nb = t * TOP_K // EXPERT_BLOCK + N_EXPERTS
    pos, meta = _plan(idx, rank, cnt)
    xs = _sc_scatter_rows(hp, pos, nb * EXPERT_BLOCK)
    y = _experts(xs, meta[:, 0], meta[:, 1], meta[:, 2], meta[:1, 3], w_gate, w_up, w_down)
    y_tok = _sc_gather_rows(y, pos.reshape(-1)).reshape(TOP_K, t, d // 2)
    out = _combine(y_tok, rw.T, res.reshape(t, d), wts["ln2_g"], wts["ln2_b"])
    return out.reshape(bsz, s, d)


def kernel(x, mem, w_in, conv_w, a_log, dt_bias, gdn_norm_g, w_a, sgu_ln_g, sgu_ln_b, sgu_w, sgu_b, w_b,
           w_mem_kv, w_c, w_o, ln1_g, ln1_b, router_w, router_bias, w_gate, w_up, w_down, ws_gate, ws_up,
           ws_down, ln2_g, ln2_b):
    assert x.shape[1] % TILE == 0 and x.shape[2] == D_MODEL
    for l in range(DEPTH):
        wts = _prepare(l, w_in, conv_w, a_log, dt_bias, gdn_norm_g, w_a, sgu_ln_g, sgu_ln_b, sgu_w, sgu_b,
                       w_b, w_mem_kv, w_c, w_o, ln1_g, ln1_b, router_w, router_bias, ws_gate, ws_up,
                       ws_down, ln2_g, ln2_b)
        x = _layer(x, mem, wts, w_gate[l], w_up[l], w_down[l])
    return x
```

```python
import functools

import jax
import jax.numpy as jnp
from jax import lax
from jax.experimental import pallas as pl
from jax.experimental.pallas import tpu as pltpu
from jax.experimental.pallas import tpu_sc as plsc

F32 = jnp.float32
BF16 = jnp.bfloat16
I32 = jnp.int32
U32 = jnp.uint32

D_MODEL = 1024
DEPTH = 1
GDN_HEADS = 4
HEAD_DIM = 128
GDN_WIDTH = GDN_HEADS * HEAD_DIM
CONV_WIDTH = 4
SGU_GROUPS = 4
SGU_WIDTH = 512
SGU_CHUNK = 128
N_MEM = 256
XATTN_HEADS = 4
XATTN_WIDTH = 512
N_EXPERTS = 256
TOP_K = 8
N_GROUPS = 8
GROUP_SIZE = N_EXPERTS // N_GROUPS
TOPK_GROUPS = 4
D_EXPERT = 256
D_SHARED = 256
ROUTED_SCALE = 2.5
LN_EPS = 1e-5
RMS_EPS = 1e-6
DEEPNORM_ALPHA = (2.0 * DEPTH) ** 0.25

_C_QKV = 0
_C_Z = 3 * GDN_WIDTH
_C_AB = 4 * GDN_WIDTH
_C_UV = _C_AB + 2 * GDN_HEADS
_C_XQ = _C_UV + 2 * SGU_WIDTH
_C_GATE = _C_XQ + XATTN_WIDTH

CHUNK = 128
TILE = 512
HALO = 8
EXPERT_BLOCK = 256
X_SLOTS = 8
X_AHEAD = X_SLOTS - 2
Y_SLOTS = 4
W_SLOTS = 3
SC_CORES = 2
SC_SUBCORES = 16
SC_WINDOW = 64
COMBINE_TILE = 256
VMEM_LIMIT = 56 * 1024 * 1024


def _dot(a, b):
    return jnp.dot(a, b, preferred_element_type=F32)


def _dot_nt(a, b):
    return lax.dot_general(a, b, (((1,), (1,)), ((), ())), preferred_element_type=F32)


def _split(a):
    hi = a.astype(BF16)
    lo = (a - hi.astype(F32)).astype(BF16)
    return hi, lo


def _dot3(a, b):
    ah, al = _split(a)
    bh, bl = _split(b)
    return _dot(ah, bh) + (_dot(ah, bl) + _dot(al, bh))


def _softplus(x):
    return jnp.maximum(x, 0.0) + jnp.log1p(jnp.exp(-jnp.abs(x)))


def _layer_norm(x, g, b):
    mu = jnp.mean(x, axis=-1, keepdims=True)
    xc = x - mu
    var = jnp.mean(xc * xc, axis=-1, keepdims=True)
    return xc * lax.rsqrt(var + LN_EPS) * g + b


def _const_spec(shape):
    nd = len(shape)
    return pl.BlockSpec(shape, lambda *_: (0,) * nd)


def _params(semantics):
    return pltpu.CompilerParams(dimension_semantics=semantics, vmem_limit_bytes=VMEM_LIMIT)


def _mem_kv_kernel(mem_ref, w_ref, k_ref, v_ref):
    kv = _dot(mem_ref[0].astype(BF16), w_ref[...])
    k_ref[0] = kv[:, :XATTN_WIDTH].astype(BF16)
    v_ref[0] = kv[:, XATTN_WIDTH:].astype(BF16)


def _mem_kv(mem, w_kv):
    bsz, n_mem, d = mem.shape
    out = jax.ShapeDtypeStruct((bsz, n_mem, XATTN_WIDTH), BF16)
    return pl.pallas_call(
        _mem_kv_kernel,
        grid=(bsz,),
        in_specs=[pl.BlockSpec((1, n_mem, d), lambda b: (b, 0, 0)), _const_spec(w_kv.shape)],
        out_specs=[pl.BlockSpec((1, n_mem, XATTN_WIDTH), lambda b: (b, 0, 0))] * 2,
        out_shape=[out, out],
        compiler_params=_params(("arbitrary",)),
        name="mem_kv",
    )(mem, w_kv)


def _mix_bc_kernel(x_ref, wuv_ref, wg1_ref, wb_ref, lng_ref, lnb_ref, sw_ref, sb_ref,
                   wxq_ref, k_ref, v_ref, wg2_ref, wc_ref, o_ref):
    xb = x_ref[0].astype(BF16)
    uv = jax.nn.gelu(_dot(xb, wuv_ref[...]), approximate=True)
    u = uv[:, :SGU_WIDTH]
    v = _layer_norm(uv[:, SGU_WIDTH:], lng_ref[...], lnb_ref[...])
    ii = lax.broadcasted_iota(I32, (SGU_CHUNK, SGU_CHUNK), 0)
    jj = lax.broadcasted_iota(I32, (SGU_CHUNK, SGU_CHUNK), 1)
    rows = []
    for n in range(TILE // SGU_CHUNK):
        cols = []
        for g in range(SGU_GROUPS):
            wt = jnp.where(ii >= jj, sw_ref[g], 0.0).astype(BF16)
            vg = v[n * SGU_CHUNK:(n + 1) * SGU_CHUNK, g * 128:(g + 1) * 128].astype(BF16)
            cols.append(_dot(wt, vg))
        rows.append(jnp.concatenate(cols, axis=1) + sb_ref[...])
    sg = jnp.concatenate(rows, axis=0)
    br_b = _dot((u * sg).astype(BF16), wb_ref[...])
    acc = jax.nn.sigmoid(_dot(xb, wg1_ref[...])) * br_b
    xq = _dot(xb, wxq_ref[...])
    heads = []
    for h in range(XATTN_HEADS):
        qh = xq[:, h * 128:(h + 1) * 128].astype(BF16)
        kh = k_ref[0, :, h * 128:(h + 1) * 128]
        vh = v_ref[0, :, h * 128:(h + 1) * 128]
        sc = _dot_nt(qh, kh) * (128 ** -0.5)
        p = jnp.exp(sc - jnp.max(sc, axis=-1, keepdims=True))
        p = p / jnp.sum(p, axis=-1, keepdims=True)
        heads.append(_dot(p.astype(BF16), vh))
    att = jnp.concatenate(heads, axis=1).astype(BF16)
    br_c = _dot(att, wc_ref[...])
    o_ref[0] = acc + jax.nn.sigmoid(_dot(xb, wg2_ref[...])) * br_c


def _mix_bc(x, wts, k, v):
    bsz, s, d = x.shape
    tok = pl.BlockSpec((1, TILE, d), lambda b, i: (b, i, 0))
    kvs = pl.BlockSpec((1, N_MEM, XATTN_WIDTH), lambda b, i: (b, 0, 0))
    names = ("w_uv", "w_g1", "w_b", "sgu_ln_g", "sgu_ln_b", "sgu_w", "sgu_bias", "w_xq")
    consts = [wts[n] for n in names]
    tail = [wts["w_g2"], wts["w_c"]]
    return pl.pallas_call(
        _mix_bc_kernel,
        grid=(bsz, s // TILE),
        in_specs=[tok] + [_const_spec(c.shape) for c in consts] + [kvs, kvs]
        + [_const_spec(c.shape) for c in tail],
        out_specs=tok,
        out_shape=jax.ShapeDtypeStruct((bsz, s, d), F32),
        compiler_params=_params(("arbitrary", "arbitrary")),
        name="mix_bc",
    )(x, *consts, k, v, *tail)


def _unit_lower_inverses(lmats, ii, jj):
    eye = (ii == jj).astype(F32)
    x = ii ^ jj
    lsplit = [_split(l) for l in lmats]
    ts = [eye - jnp.where(x < 2, l, 0.0) for l in lmats]
    zero = jnp.zeros((CHUNK, CHUNK), BF16)
    s = 2
    while s < CHUNK:
        off = (x < 2 * s) & ((ii & s) != 0) & ((jj & s) == 0)
        tsplit = [_split(t) for t in ts]
        prods = []
        for (th, tl), (lh, ll) in zip(tsplit, lsplit):
            oh = jnp.where(off, lh, zero)
            ol = jnp.where(off, ll, zero)
            prods.append(_dot(th, oh) + (_dot(th, ol) + _dot(tl, oh)))
        nxt = []
        for t, p, (th, tl) in zip(ts, prods, tsplit):
            ph, plo = _split(p)
            nxt.append(t - (_dot(ph, th) + (_dot(ph, tl) + _dot(plo, th))))
        ts = nxt
        s *= 2
    return ts


def _gdn_prep_kernel(x_ref, wqkv_ref, wz_ref, wab_ref, convw_ref, prm_ref,
                     u_ref, w_ref, qd_ref, kd_ref, a_ref, gl_ref, z_ref, hist_ref):
    @pl.when(pl.program_id(1) == 0)
    def _():
        hist_ref[0:HALO, :] = jnp.zeros((HALO, 3 * GDN_WIDTH), F32)

    xb = x_ref[0].astype(BF16)
    z_ref[0] = _dot(xb, wz_ref[...])
    qkv = _dot(xb, wqkv_ref[...])
    hist_ref[HALO:HALO + TILE, :] = qkv
    conv = qkv * convw_ref[CONV_WIDTH - 1:CONV_WIDTH, :]
    for j in range(CONV_WIDTH - 1):
        shift = CONV_WIDTH - 1 - j
        conv = conv + hist_ref[HALO - shift:HALO - shift + TILE, :] * convw_ref[j:j + 1, :]
    hist_ref[0:HALO, :] = qkv[TILE - HALO:, :]
    qkv = conv * jax.nn.sigmoid(conv)

    ab = _dot(xb, wab_ref[...])
    g = -jnp.exp(prm_ref[0:1, :]) * _softplus(ab + prm_ref[1:2, :])
    beta = jax.nn.sigmoid(ab)

    ii = lax.broadcasted_iota(I32, (CHUNK, CHUNK), 0)
    jj = lax.broadcasted_iota(I32, (CHUNK, CHUNK), 1)
    tri = (ii >= jj).astype(BF16)
    items = []
    for c in range(TILE // CHUNK):
        r0 = c * CHUNK
        gch = g[r0:r0 + CHUNK, :]
        g1 = gch.astype(BF16)
        r1 = gch - g1.astype(F32)
        g2 = r1.astype(BF16)
        g3 = (r1 - g2.astype(F32)).astype(BF16)
        gc = _dot(tri, g1) + (_dot(tri, g2) + _dot(tri, g3))
        gct = gc.T
        for h in range(GDN_HEADS):
            qh = qkv[r0:r0 + CHUNK, h * HEAD_DIM:(h + 1) * HEAD_DIM]
            kh = qkv[r0:r0 + CHUNK, GDN_WIDTH + h * HEAD_DIM:GDN_WIDTH + (h + 1) * HEAD_DIM]
            vh = qkv[r0:r0 + CHUNK, 2 * GDN_WIDTH + h * HEAD_DIM:2 * GDN_WIDTH + (h + 1) * HEAD_DIM]
            qn = qh * lax.rsqrt(jnp.sum(qh * qh, axis=-1, keepdims=True) + RMS_EPS) * (HEAD_DIM ** -0.5)
            kn = kh * lax.rsqrt(jnp.sum(kh * kh, axis=-1, keepdims=True) + RMS_EPS)
            gcol = gc[:, h:h + 1]
            grow = gct[h:h + 1, :]
            bcol = beta[r0:r0 + CHUNK, GDN_HEADS + h:GDN_HEADS + h + 1]
            glast = gc[CHUNK - 1:CHUNK, h:h + 1]
            decay = jnp.exp(jnp.where(ii >= jj, gcol - grow, -jnp.inf))
            knb = kn.astype(BF16)
            kk = _dot_nt(knb, knb)
            qk = _dot_nt(qn.astype(BF16), knb)
            egc = jnp.exp(gcol)
            items.append(dict(
                c=c, h=h, lmat=jnp.where(ii > jj, kk * bcol * decay, 0.0),
                rhs=jnp.concatenate([vh * bcol, kn * (bcol * egc)], axis=1),
                qd=(qn * egc).astype(BF16), kd=(kn * jnp.exp(glast - gcol)).astype(BF16),
                a=(qk * decay).astype(BF16), gl=jnp.broadcast_to(jnp.exp(glast), (8, HEAD_DIM))))
    tinvs = _unit_lower_inverses([it["lmat"] for it in items], ii, jj)
    sols = [_dot3(tinv, it["rhs"]) for tinv, it in zip(tinvs, items)]
    for it, sol in zip(items, sols):
        rows = slice(it["c"] * CHUNK, (it["c"] + 1) * CHUNK)
        lanes = slice(it["h"] * HEAD_DIM, (it["h"] + 1) * HEAD_DIM)
        u_ref[0, rows, lanes] = sol[:, :HEAD_DIM]
        w_ref[0, rows, lanes] = sol[:, HEAD_DIM:].astype(BF16)
        qd_ref[0, rows, lanes] = it["qd"]
        kd_ref[0, rows, lanes] = it["kd"]
        a_ref[0, rows, lanes] = it["a"]
        gl_ref[0, it["c"] * 8:(it["c"] + 1) * 8, lanes] = it["gl"]


def _gdn_prep(x, wts):
    bsz, s, d = x.shape
    tok = pl.BlockSpec((1, TILE, d), lambda b, i: (b, i, 0))
    hw = pl.BlockSpec((1, TILE, GDN_WIDTH), lambda b, i: (b, i, 0))
    gls = pl.BlockSpec((1, 8 * TILE // CHUNK, GDN_WIDTH), lambda b, i: (b, i, 0))
    consts = [wts[n] for n in ("w_qkv", "w_z", "w_ab", "conv_w", "gdn_prm")]
    f32o = jax.ShapeDtypeStruct((bsz, s, GDN_WIDTH), F32)
    b16o = jax.ShapeDtypeStruct((bsz, s, GDN_WIDTH), BF16)
    glo = jax.ShapeDtypeStruct((bsz, 8 * s // CHUNK, GDN_WIDTH), F32)
    return pl.pallas_call(
        _gdn_prep_kernel,
        grid=(bsz, s // TILE),
        in_specs=[tok] + [_const_spec(c.shape) for c in consts],
        out_specs=[hw, hw, hw, hw, hw, gls, hw],
        out_shape=[f32o, b16o, b16o, b16o, b16o, glo, f32o],
        scratch_shapes=[pltpu.VMEM((HALO + TILE, 3 * GDN_WIDTH), F32)],
        compiler_params=_params(("arbitrary", "arbitrary")),
        name="gdn_prep",
    )(x, *consts)


def _gdn_scan_kernel(u_ref, w_ref, qd_ref, kd_ref, a_ref, gl_ref, z_ref, gn_ref, o_ref, state_ref):
    @pl.when(pl.program_id(0) == 0)
    def _():
        state_ref[...] = jnp.zeros(state_ref.shape, F32)

    bsz = u_ref.shape[0]
    for c in range(TILE // CHUNK):
        rows = slice(c * CHUNK, (c + 1) * CHUNK)
        for b in range(bsz):
            for h in range(GDN_HEADS):
                lanes = slice(h * HEAD_DIM, (h + 1) * HEAD_DIM)
                st = state_ref[b, h]
                wq = jnp.concatenate([w_ref[b, rows, lanes], qd_ref[b, rows, lanes]], axis=0)
                ws = _dot(wq, st.astype(BF16))
                v_new = u_ref[b, rows, lanes] - ws[:CHUNK]
                vb = v_new.astype(BF16)
                o = ws[CHUNK:] + _dot(a_ref[b, rows, lanes], vb)
                kdt = kd_ref[b, rows, lanes].astype(F32).T.astype(BF16)
                state_ref[b, h] = st * gl_ref[b, c * 8:c * 8 + 1, lanes] + _dot(kdt, vb)
                zz = z_ref[b, rows, lanes]
                o = o * lax.rsqrt(jnp.mean(o * o, axis=-1, keepdims=True) + RMS_EPS) * gn_ref[...]
                o_ref[b, rows, lanes] = (o * (zz * jax.nn.sigmoid(zz))).astype(BF16)


def _gdn_scan(u, w, qd, kd, a, gl, z, gn):
    bsz, s, _ = u.shape
    hw = pl.BlockSpec((bsz, TILE, GDN_WIDTH), lambda i: (0, i, 0))
    gls = pl.BlockSpec((bsz, 8 * TILE // CHUNK, GDN_WIDTH), lambda i: (0, i, 0))
    return pl.pallas_call(
        _gdn_scan_kernel,
        grid=(s // TILE,),
        in_specs=[hw, hw, hw, hw, hw, gls, hw, _const_spec(gn.shape)],
        out_specs=hw,
        out_shape=jax.ShapeDtypeStruct((bsz, s, GDN_WIDTH), BF16),
        scratch_shapes=[pltpu.VMEM((bsz, GDN_HEADS, HEAD_DIM, HEAD_DIM), F32)],
        compiler_params=_params(("arbitrary",)),
        name="gdn_scan",
    )(u, w, qd, kd, a, gl, z, gn)


def _merge_kernel(x_ref, og_ref, mbc_ref, wa_ref, wg0_ref, wo_ref, ln1g_ref, ln1b_ref,
                  rwt_ref, rb_ref, wsgu_ref, wsd_ref,
                  res_ref, hp_ref, idx_ref, wt_ref, rank_ref, cnt_ref, carry_ref):
    first = (pl.program_id(0) == 0) & (pl.program_id(1) == 0)

    @pl.when(first)
    def _():
        carry_ref[...] = jnp.zeros(carry_ref.shape, F32)

    x = x_ref[0]
    xb = x.astype(BF16)
    br_a = _dot(og_ref[0], wa_ref[...])
    merged = jax.nn.sigmoid(_dot(xb, wg0_ref[...])) * br_a + mbc_ref[0]
    y = _dot(merged.astype(BF16), wo_ref[...])
    h = _layer_norm(DEEPNORM_ALPHA * x + y, ln1g_ref[...], ln1b_ref[...])
    hb = h.astype(BF16)
    hp_ref[...] = _pack_bf16_pairs(h)

    gu = _dot(hb, wsgu_ref[...])
    gt = gu[:, :D_SHARED]
    act = gt * jax.nn.sigmoid(gt) * gu[:, D_SHARED:]
    res_ref[0] = DEEPNORM_ALPHA * h + _dot(act.astype(BF16), wsd_ref[...])

    scores = jax.nn.sigmoid(_dot_nt(rwt_ref[...], hb))
    biased = scores + rb_ref[...]
    g3 = biased.reshape(N_GROUPS, GROUP_SIZE, TILE)
    m1 = jnp.max(g3, axis=1)
    m1b = m1[:, None, :]
    n_top = jnp.sum((g3 == m1b).astype(F32), axis=1)
    m2 = jnp.max(jnp.where(g3 < m1b, g3, -jnp.inf), axis=1)
    gs = m1 + jnp.where(n_top >= 2.0, m1, m2)
    gidx = lax.broadcasted_iota(I32, (N_GROUPS, TILE), 0)
    beaten = jnp.zeros((N_GROUPS, TILE), F32)
    for g in range(N_GROUPS):
        row = gs[g:g + 1, :]
        beaten = beaten + ((row > gs) | ((row == gs) & (g < gidx))).astype(F32)
    sel = (beaten < float(TOPK_GROUPS)).astype(F32)
    sel_e = jnp.broadcast_to(sel[:, None, :], (N_GROUPS, GROUP_SIZE, TILE)).reshape(N_EXPERTS, TILE)
    masked = jnp.where(sel_e > 0.5, biased, -jnp.inf)
    eidx = lax.broadcasted_iota(I32, (N_EXPERTS, TILE), 0)
    chosen = jnp.zeros((N_EXPERTS, TILE), F32)
    picks, pick_scores = [], []
    for _k in range(TOP_K):
        mx = jnp.max(masked, axis=0, keepdims=True)
        pick = jnp.min(jnp.where(masked == mx, eidx, N_EXPERTS), axis=0, keepdims=True)
        hot = eidx == pick
        picks.append(pick)
        pick_scores.append(jnp.sum(jnp.where(hot, scores, 0.0), axis=0, keepdims=True))
        chosen = jnp.where(hot, 1.0, chosen)
        masked = jnp.where(hot, -jnp.inf, masked)
    total = pick_scores[0]
    for sck in pick_scores[1:]:
        total = total + sck
    idx_ref[...] = jnp.concatenate(picks, axis=0)
    wt_ref[...] = jnp.concatenate([sck / total * ROUTED_SCALE for sck in pick_scores], axis=0)

    ti = lax.broadcasted_iota(I32, (TILE, TILE), 0)
    tj = lax.broadcasted_iota(I32, (TILE, TILE), 1)
    before = _dot(chosen.astype(BF16), (ti < tj).astype(BF16))
    carry = carry_ref[...]
    before = before + jnp.concatenate([carry] * (TILE // 128), axis=1)
    rank_ref[...] = jnp.concatenate(
        [jnp.sum(jnp.where(eidx == p, before, 0.0), axis=0, keepdims=True) for p in picks],
        axis=0).astype(I32)
    carry = carry + _dot(chosen.astype(BF16), jnp.ones((TILE, 128), BF16))
    carry_ref[...] = carry
    cnt_ref[...] = carry.astype(I32)


def _merge(x, og, mbc, wts):
    bsz, s, d = x.shape
    t = bsz * s
    nt = s // TILE
    tok = pl.BlockSpec((1, TILE, d), lambda b, i: (b, i, 0))
    ogs = pl.BlockSpec((1, TILE, GDN_WIDTH), lambda b, i: (b, i, 0))
    flat = lambda b, i: (b * nt + i, 0)
    lane = lambda b, i: (0, b * nt + i)
    consts = [wts[n] for n in ("w_a", "w_g0", "w_o", "ln1_g", "ln1_b", "router_wt", "router_bias",
                               "ws_gu", "ws_down")]
    return pl.pallas_call(
        _merge_kernel,
        grid=(bsz, nt),
        in_specs=[tok, ogs, tok] + [_const_spec(c.shape) for c in consts],
        out_specs=[tok,
                   pl.BlockSpec((TILE, d // 2), flat),
                   pl.BlockSpec((TOP_K, TILE), lane),
                   pl.BlockSpec((TOP_K, TILE), lane),
                   pl.BlockSpec((TOP_K, TILE), lane),
                   _const_spec((N_EXPERTS, 128))],
        out_shape=[jax.ShapeDtypeStruct((bsz, s, d), F32),
                   jax.ShapeDtypeStruct((t, d // 2), U32),
                   jax.ShapeDtypeStruct((TOP_K, t), I32),
                   jax.ShapeDtypeStruct((TOP_K, t), F32),
                   jax.ShapeDtypeStruct((TOP_K, t), I32),
                   jax.ShapeDtypeStruct((N_EXPERTS, 128), I32)],
        scratch_shapes=[pltpu.VMEM((N_EXPERTS, 128), F32)],
        compiler_params=_params(("arbitrary", "arbitrary")),
        name="merge",
    )(x, og, mbc, *consts)


def _plan_kernel(idx_ref, rank_ref, cnt_ref, pos_ref, meta_ref):
    e = N_EXPERTS
    shift = EXPERT_BLOCK.bit_length() - 1
    cnt = cnt_ref[...]
    nblk = lax.shift_right_logical(cnt + (EXPERT_BLOCK - 1), shift)
    ei = lax.broadcasted_iota(I32, (e, e), 0)
    ej = lax.broadcasted_iota(I32, (e, e), 1)
    incl = _dot((ei >= ej).astype(BF16), nblk.astype(F32).astype(BF16)).astype(I32)
    excl = incl - nblk
    start = (excl * EXPERT_BLOCK).astype(F32)
    lane = lax.broadcasted_iota(I32, (e, 128), 1)
    n_used = jnp.broadcast_to(incl[e - 1:e, :], (e, 128))
    meta_ref[...] = jnp.where(lane == 0, excl, jnp.where(lane == 1, nblk, jnp.where(lane == 2, cnt, n_used)))

    e128 = lax.broadcasted_iota(I32, (e, 128), 0)

    def body(i, carry):
        off = pl.multiple_of(i * 128, 128)
        ids = idx_ref[:, pl.ds(off, 128)]
        rows = [jnp.sum(jnp.where(e128 == ids[k:k + 1, :], start, 0.0), axis=0, keepdims=True)
                for k in range(TOP_K)]
        pos_ref[:, pl.ds(off, 128)] = jnp.concatenate(rows, axis=0).astype(I32) + rank_ref[:, pl.ds(off, 128)]
        return carry

    lax.fori_loop(0, idx_ref.shape[1] // 128, body, 0)


def _plan(idx, rank, cnt):
    t = idx.shape[1]
    return pl.pallas_call(
        _plan_kernel,
        grid=(1,),
        in_specs=[_const_spec(idx.shape), _const_spec(rank.shape), _const_spec(cnt.shape)],
        out_specs=[_const_spec((TOP_K, t)), _const_spec((N_EXPERTS, 128))],
        out_shape=[jax.ShapeDtypeStruct((TOP_K, t), I32), jax.ShapeDtypeStruct((N_EXPERTS, 128), I32)],
        compiler_params=_params(("arbitrary",)),
        name="plan",
    )(idx, rank, cnt)


def _sc_scatter_rows(rows, pos, n_slots):
    t, d = rows.shape
    workers = SC_CORES * SC_SUBCORES
    per_w = t // workers
    nwin = per_w // SC_WINDOW
    assert per_w * workers == t and nwin * SC_WINDOW == per_w and nwin % 2 == 0
    idx = pos.reshape(TOP_K, workers, nwin, SC_WINDOW).transpose(1, 2, 0, 3).reshape(workers, nwin * TOP_K, SC_WINDOW)
    mesh = plsc.VectorSubcoreMesh(core_axis_name="core", subcore_axis_name="subcore",
                                  num_cores=SC_CORES, num_subcores=SC_SUBCORES)

    @functools.partial(
        pl.kernel, out_type=jax.ShapeDtypeStruct((n_slots, d), rows.dtype), mesh=mesh, name="sc_scatter_rows",
        scratch_types=[pltpu.VMEM((nwin * TOP_K, SC_WINDOW), I32), pltpu.VMEM((2, SC_WINDOW, d), rows.dtype),
                       pltpu.SemaphoreType.DMA((2,)), pltpu.SemaphoreType.DMA((2,))])
    def scatter(rows_hbm, idx_hbm, out_hbm, idx_v, rows_v, lsem, ssem):
        wid = lax.axis_index("subcore") * SC_CORES + lax.axis_index("core")
        base = wid * per_w
        pltpu.sync_copy(idx_hbm.at[wid], idx_v)

        def load(w, slot):
            return pltpu.make_async_copy(rows_hbm.at[pl.ds(base + w * SC_WINDOW, SC_WINDOW)], rows_v.at[slot],
                                         lsem.at[slot])

        def send(w, k, slot):
            return pltpu.make_async_copy(rows_v.at[slot], out_hbm.at[idx_v.at[w * TOP_K + k]], ssem.at[slot])

        load(0, 0).start()

        @pl.loop(0, nwin, step=2)
        def _(w0):
            for b in range(2):
                w = w0 + b
                load(w, b).wait()

                @pl.when(w + 1 < nwin)
                def _():
                    @pl.when(w >= 1)
                    def _():
                        for k in range(TOP_K):
                            send(w - 1, k, 1 - b).wait()

                    load(w + 1, 1 - b).start()

                for k in range(TOP_K):
                    send(w, k, b).start()

        for k in range(TOP_K):
            send(nwin - 2, k, 0).wait()
        for k in range(TOP_K):
            send(nwin - 1, k, 1).wait()

    return scatter(rows, idx)


def _pack_bf16_pairs(a):
    n = a.shape[1] // 2
    bits = lax.bitcast_convert_type(a.astype(BF16).astype(F32), U32)
    return (bits[:, :n] >> 16) | (bits[:, n:] & jnp.uint32(0xFFFF0000))


def _unpack_bf16_pairs(words):
    lo = lax.bitcast_convert_type(words << 16, F32)
    hi = lax.bitcast_convert_type(words & jnp.uint32(0xFFFF0000), F32)
    return jnp.concatenate([lo, hi], axis=1)


def _experts_kernel(fb_ref, nb_ref, cnt_ref, nu_ref, xs_ref, wg_ref, wu_ref, wd_ref, y_ref,
                    wgbuf, wubuf, wdbuf, wsem, wgu_s, wd_s, xbuf, ybuf, xsem, ysem):
    e = pl.program_id(0)
    n_used = nu_ref[0]
    bm = EXPERT_BLOCK
    row_queue = 1

    def x_copy(g, slot):
        return pltpu.make_async_copy(xs_ref.at[pl.ds(g * bm, bm), :], xbuf.at[slot], xsem.at[slot])

    def y_copy(g, slot):
        return pltpu.make_async_copy(ybuf.at[slot], y_ref.at[pl.ds(g * bm, bm), :], ysem.at[slot])

    @pl.when(e == 0)
    def _():
        for g0 in range(X_AHEAD):
            @pl.when(g0 < n_used)
            def _():
                x_copy(g0, g0).start(priority=row_queue)

    def w_copies(j, slot):
        return [pltpu.make_async_copy(src.at[j], dst.at[slot], wsem.at[slot])
                for src, dst in ((wg_ref, wgbuf), (wu_ref, wubuf), (wd_ref, wdbuf))]

    @pl.when(e == 0)
    def _():
        for j in range(W_SLOTS - 1):
            for cp in w_copies(j, j):
                cp.start()

    wslot = lax.rem(e, W_SLOTS)
    for cp in w_copies(e, wslot):
        cp.wait()
    ahead_e = e + (W_SLOTS - 1)

    @pl.when(ahead_e < pl.num_programs(0))
    def _():
        for cp in w_copies(ahead_e, lax.rem(ahead_e, W_SLOTS)):
            cp.start()

    nb = nb_ref[e]

    @pl.when(nb > 0)
    def _():
        wgu_s[:, :D_EXPERT] = wgbuf[wslot].astype(BF16)
        wgu_s[:, D_EXPERT:] = wubuf[wslot].astype(BF16)
        wd_s[...] = wdbuf[wslot].astype(BF16)

    def process(b0, width):
        g0 = fb_ref[e] + b0
        xs = []
        for i in range(width):
            g = g0 + i
            x_copy(g, g & (X_SLOTS - 1)).wait()
            xs.append(xbuf[g & (X_SLOTS - 1)])
        for i in range(width):
            ahead = g0 + X_AHEAD + i

            @pl.when(ahead < n_used)
            def _():
                x_copy(ahead, ahead & (X_SLOTS - 1)).start(priority=row_queue)

        ys = []
        for i in range(width):
            x = _unpack_bf16_pairs(xs[i])
            live = lax.broadcasted_iota(I32, (bm, 1), 0) < cnt_ref[e] - (b0 + i) * bm
            xb = jnp.where(live, x, 0.0).astype(BF16)
            gu = _dot(xb, wgu_s[...])
            gt = gu[:, :D_EXPERT]
            act = (gt * jax.nn.sigmoid(gt) * gu[:, D_EXPERT:]).astype(BF16)
            ys.append(_pack_bf16_pairs(_dot(act, wd_s[...])))
        for i in range(width):
            g = g0 + i
            slot = g & (Y_SLOTS - 1)

            @pl.when(g >= Y_SLOTS)
            def _():
                y_copy(g - Y_SLOTS, slot).wait()

            ybuf[slot] = ys[i]
            y_copy(g, slot).start(priority=row_queue)

    def pair(p, carry):
        process(2 * p, 2)
        return carry

    lax.fori_loop(0, nb >> 1, pair, 0)

    @pl.when((nb & 1) == 1)
    def _():
        process(nb - 1, 1)

    @pl.when(e == pl.num_programs(0) - 1)
    def _():
        for back in range(Y_SLOTS, 0, -1):
            @pl.when(n_used >= back)
            def _():
                y_copy(n_used - back, (n_used - back) & (Y_SLOTS - 1)).wait()


def _experts(xs, first_block, n_blocks, counts, n_used, w_gate, w_up, w_down):
    n_slots, half = xs.shape
    d = 2 * half
    grid_spec = pltpu.PrefetchScalarGridSpec(
        num_scalar_prefetch=4,
        grid=(N_EXPERTS,),
        in_specs=[pl.BlockSpec(memory_space=pl.ANY)] * 4,
        out_specs=pl.BlockSpec(memory_space=pl.ANY),
        scratch_shapes=[pltpu.VMEM((W_SLOTS, d, D_EXPERT), F32), pltpu.VMEM((W_SLOTS, d, D_EXPERT), F32),
                        pltpu.VMEM((W_SLOTS, D_EXPERT, d), F32), pltpu.SemaphoreType.DMA((W_SLOTS,)),
                        pltpu.VMEM((d, 2 * D_EXPERT), BF16), pltpu.VMEM((D_EXPERT, d), BF16),
                        pltpu.VMEM((X_SLOTS, EXPERT_BLOCK, half), U32),
                        pltpu.VMEM((Y_SLOTS, EXPERT_BLOCK, half), U32),
                        pltpu.SemaphoreType.DMA((X_SLOTS,)), pltpu.SemaphoreType.DMA((Y_SLOTS,))],
    )
    return pl.pallas_call(
        _experts_kernel,
        grid_spec=grid_spec,
        out_shape=jax.ShapeDtypeStruct((n_slots, half), U32),
        compiler_params=_params(("arbitrary",)),
        name="experts",
    )(first_block, n_blocks, counts, n_used, xs, w_gate, w_up, w_down)


def _sc_gather_rows(table, indices):
    n = indices.shape[0]
    d = table.shape[1]
    workers = SC_CORES * SC_SUBCORES
    per_w = n // workers
    nch = per_w // SC_WINDOW
    assert per_w * workers == n and nch * SC_WINDOW == per_w and nch % 2 == 0
    mesh = plsc.VectorSubcoreMesh(core_axis_name="core", subcore_axis_name="subcore",
                                  num_cores=SC_CORES, num_subcores=SC_SUBCORES)

    @functools.partial(
        pl.kernel, out_type=jax.ShapeDtypeStruct((n, d), table.dtype), mesh=mesh, name="sc_gather_rows",
        scratch_types=[pltpu.VMEM((per_w,), I32), pltpu.VMEM((2, SC_WINDOW, d), table.dtype),
                       pltpu.SemaphoreType.DMA((2,)), pltpu.SemaphoreType.DMA((2,))])
    def gather(table_hbm, idx_hbm, out_hbm, idx_v, rows_v, gsem, wsem):
        base = (lax.axis_index("subcore") * SC_CORES + lax.axis_index("core")) * per_w
        pltpu.sync_copy(idx_hbm.at[pl.ds(base, per_w)], idx_v)

        def fetch(c, slot):
            return pltpu.make_async_copy(table_hbm.at[idx_v.at[pl.ds(c * SC_WINDOW, SC_WINDOW)]],
                                         rows_v.at[slot], gsem.at[slot])

        def put(c, slot):
            return pltpu.make_async_copy(rows_v.at[slot], out_hbm.at[pl.ds(base + c * SC_WINDOW, SC_WINDOW)],
                                         wsem.at[slot])

        fetch(0, 0).start()

        @pl.loop(0, nch, step=2)
        def _(c0):
            for b in range(2):
                c = c0 + b
                fetch(c, b).wait()

                @pl.when(c + 1 < nch)
                def _():
                    @pl.when(c >= 1)
                    def _():
                        put(c - 1, 1 - b).wait()

                    fetch(c + 1, 1 - b).start()

                put(c, b).start()

        put(nch - 2, 0).wait()
        put(nch - 1, 1).wait()

    return gather(table, indices)


def _combine_kernel(yt_ref, wt_ref, res_ref, g_ref, b_ref, o_ref):
    acc = res_ref[...]
    for k in range(TOP_K):
        acc = acc + _unpack_bf16_pairs(yt_ref[k]) * wt_ref[:, k:k + 1]
    o_ref[...] = _layer_norm(acc, g_ref[...], b_ref[...])


def _combine(y_tok, wts_tok, res, ln_g, ln_b):
    t, d = res.shape
    tok = pl.BlockSpec((COMBINE_TILE, d), lambda i: (i, 0))
    return pl.pallas_call(
        _combine_kernel,
        grid=(t // COMBINE_TILE,),
        in_specs=[pl.BlockSpec((TOP_K, COMBINE_TILE, d // 2), lambda i: (0, i, 0)),
                  pl.BlockSpec((COMBINE_TILE, TOP_K), lambda i: (i, 0)),
                  tok, _const_spec(ln_g.shape), _const_spec(ln_b.shape)],
        out_specs=tok,
        out_shape=jax.ShapeDtypeStruct((t, d), F32),
        compiler_params=_params(("arbitrary",)),
        name="combine",
    )(y_tok, wts_tok, res, ln_g, ln_b)


def _prepare(l, w_in, conv_w, a_log, dt_bias, gdn_norm_g, w_a, sgu_ln_g, sgu_ln_b, sgu_w, sgu_b, w_b,
             w_mem_kv, w_c, w_o, ln1_g, ln1_b, router_w, router_bias, ws_gate, ws_up, ws_down, ln2_g, ln2_b):
    wi = w_in[l]
    d = D_MODEL
    bf = lambda a: a.astype(BF16)
    row = lambda a: a.reshape(1, -1).astype(F32)
    w_ab = jnp.zeros((d, 128), F32).at[:, :2 * GDN_HEADS].set(wi[:, _C_AB:_C_UV])
    prm = jnp.zeros((8, 128), F32).at[0, :GDN_HEADS].set(a_log[l]).at[1, :GDN_HEADS].set(dt_bias[l])
    sgu_bias = jnp.repeat(sgu_b[l].T, SGU_WIDTH // SGU_GROUPS, axis=1)
    return {
        "w_qkv": bf(wi[:, _C_QKV:_C_Z]), "w_z": bf(wi[:, _C_Z:_C_AB]), "w_ab": bf(w_ab),
        "w_uv": bf(wi[:, _C_UV:_C_XQ]), "w_xq": bf(wi[:, _C_XQ:_C_GATE]),
        "w_g0": bf(wi[:, _C_GATE:_C_GATE + d]), "w_g1": bf(wi[:, _C_GATE + d:_C_GATE + 2 * d]),
        "w_g2": bf(wi[:, _C_GATE + 2 * d:_C_GATE + 3 * d]),
        "conv_w": conv_w[l].astype(F32), "gdn_prm": prm, "gdn_norm_g": row(gdn_norm_g[l]),
        "w_a": bf(w_a[l]), "sgu_ln_g": row(sgu_ln_g[l]), "sgu_ln_b": row(sgu_ln_b[l]),
        "sgu_w": sgu_w[l].astype(F32), "sgu_bias": sgu_bias.astype(F32), "w_b": bf(w_b[l]),
        "w_mem_kv": bf(w_mem_kv[l]), "w_c": bf(w_c[l]), "w_o": bf(w_o[l]),
        "ln1_g": row(ln1_g[l]), "ln1_b": row(ln1_b[l]),
        "router_wt": bf(router_w[l].T), "router_bias": router_bias[l].reshape(-1, 1).astype(F32),
        "ws_gu": bf(jnp.concatenate([ws_gate[l], ws_up[l]], axis=1)), "ws_down": bf(ws_down[l]),
        "ln2_g": row(ln2_g[l]), "ln2_b": row(ln2_b[l]),
    }


def _layer(x, mem, wts, w_gate, w_up, w_down):
    bsz, s, d = x.shape
    t = bsz * s
    k, v = _mem_kv(mem, wts["w_mem_kv"])
    mbc = _mix_bc(x, wts, k, v)
    u, w, qd, kd, a, gl, z = _gdn_prep(x, wts)
    og = _gdn_scan(u, w, qd, kd, a, gl, z, wts["gdn_norm_g"])
    res, hp, idx, rw, rank, cnt = _merge(x, og, mbc, wts)
    nb = t * TOP_K // EXPERT_BLOCK + N_EXPERTS
    pos, meta = _plan(idx, rank, cnt)
    xs = _sc_scatter_rows(hp, pos, nb * EXPERT_BLOCK)
    y = _experts(xs, meta[:, 0], meta[:, 1], meta[:, 2], meta[:1, 3], w_gate, w_up, w_down)
    y_tok = _sc_gather_rows(y, pos.reshape(-1)).reshape(TOP_K, t, d // 2)
    out = _combine(y_tok, rw.T, res.reshape(t, d), wts["ln2_g"], wts["ln2_b"])
    return out.reshape(bsz, s, d)


def kernel(x, mem, w_in, conv_w, a_log, dt_bias, gdn_norm_g, w_a, sgu_ln_g, sgu_ln_b, sgu_w, sgu_b, w_b,
           w_mem_kv, w_c, w_o, ln1_g, ln1_b, router_w, router_bias, w_gate, w_up, w_down, ws_gate, ws_up,
           ws_down, ln2_g, ln2_b):
    assert x.shape[1] % TILE == 0 and x.shape[2] == D_MODEL
    for l in range(DEPTH):
        wts = _prepare(l, w_in, conv_w, a_log, dt_bias, gdn_norm_g, w_a, sgu_ln_g, sgu_ln_b, sgu_w, sgu_b,
                       w_b, w_mem_kv, w_c, w_o, ln1_g, ln1_b, router_w, router_bias, ws_gate, ws_up,
                       ws_down, ln2_g, ln2_b)
        x = _layer(x, mem, wts, w_gate[l], w_up[l], w_down[l])
    return x
```

```python
import functools

import jax
import jax.numpy as jnp
from jax import lax
from jax.experimental import pallas as pl
from jax.experimental.pallas import tpu as pltpu
from jax.experimental.pallas import tpu_sc as plsc

F32 = jnp.float32
BF16 = jnp.bfloat16
I32 = jnp.int32
U32 = jnp.uint32

D_MODEL = 1024
DEPTH = 1
GDN_HEADS = 4
HEAD_DIM = 128
GDN_WIDTH = GDN_HEADS * HEAD_DIM
CONV_WIDTH = 4
SGU_GROUPS = 4
SGU_WIDTH = 512
SGU_CHUNK = 128
N_MEM = 256
XATTN_HEADS = 4
XATTN_WIDTH = 512
N_EXPERTS = 256
TOP_K = 8
N_GROUPS = 8
GROUP_SIZE = N_EXPERTS // N_GROUPS
TOPK_GROUPS = 4
D_EXPERT = 256
D_SHARED = 256
ROUTED_SCALE = 2.5
LN_EPS = 1e-5
RMS_EPS = 1e-6
DEEPNORM_ALPHA = (2.0 * DEPTH) ** 0.25

_C_QKV = 0
_C_Z = 3 * GDN_WIDTH
_C_AB = 4 * GDN_WIDTH
_C_UV = _C_AB + 2 * GDN_HEADS
_C_XQ = _C_UV + 2 * SGU_WIDTH
_C_GATE = _C_XQ + XATTN_WIDTH

CHUNK = 128
TILE = 512
HALO = 8
EXPERT_BLOCK = 256
X_SLOTS = 16
X_AHEAD = X_SLOTS - 2
Y_SLOTS = 8
W_SLOTS = 3
SC_CORES = 2
SC_SUBCORES = 16
SC_WINDOW = 64
COMBINE_TILE = 256
COMBINE_PARTS = 4
VMEM_LIMIT = 56 * 1024 * 1024


def _dot(a, b):
    return jnp.dot(a, b, preferred_element_type=F32)


def _dot_nt(a, b):
    return lax.dot_general(a, b, (((1,), (1,)), ((), ())), preferred_element_type=F32)


def _split(a):
    hi = a.astype(BF16)
    lo = (a - hi.astype(F32)).astype(BF16)
    return hi, lo


def _dot3(a, b):
    ah, al = _split(a)
    bh, bl = _split(b)
    return _dot(ah, bh) + (_dot(ah, bl) + _dot(al, bh))


def _softplus(x):
    return jnp.maximum(x, 0.0) + jnp.log1p(jnp.exp(-jnp.abs(x)))


def _layer_norm(x, g, b):
    mu = jnp.mean(x, axis=-1, keepdims=True)
    xc = x - mu
    var = jnp.mean(xc * xc, axis=-1, keepdims=True)
    return xc * lax.rsqrt(var + LN_EPS) * g + b


def _const_spec(shape):
    nd = len(shape)
    return pl.BlockSpec(shape, lambda *_: (0,) * nd)


def _params(semantics):
    return pltpu.CompilerParams(dimension_semantics=semantics, vmem_limit_bytes=VMEM_LIMIT)


def _mem_kv_kernel(mem_ref, w_ref, k_ref, v_ref):
    kv = _dot(mem_ref[0].astype(BF16), w_ref[...])
    k_ref[0] = kv[:, :XATTN_WIDTH].astype(BF16)
    v_ref[0] = kv[:, XATTN_WIDTH:].astype(BF16)


def _mem_kv(mem, w_kv):
    bsz, n_mem, d = mem.shape
    out = jax.ShapeDtypeStruct((bsz, n_mem, XATTN_WIDTH), BF16)
    return pl.pallas_call(
        _mem_kv_kernel,
        grid=(bsz,),
        in_specs=[pl.BlockSpec((1, n_mem, d), lambda b: (b, 0, 0)), _const_spec(w_kv.shape)],
        out_specs=[pl.BlockSpec((1, n_mem, XATTN_WIDTH), lambda b: (b, 0, 0))] * 2,
        out_shape=[out, out],
        compiler_params=_params(("arbitrary",)),
        name="mem_kv",
    )(mem, w_kv)


def _mix_bc_kernel(x_ref, wuv_ref, wg1_ref, wb_ref, lng_ref, lnb_ref, sw_ref, sb_ref,
                   wxq_ref, k_ref, v_ref, wg2_ref, wc_ref, o_ref):
    xb = x_ref[0].astype(BF16)
    uv = jax.nn.gelu(_dot(xb, wuv_ref[...]), approximate=True)
    u = uv[:, :SGU_WIDTH]
    v = _layer_norm(uv[:, SGU_WIDTH:], lng_ref[...], lnb_ref[...])
    ii = lax.broadcasted_iota(I32, (SGU_CHUNK, SGU_CHUNK), 0)
    jj = lax.broadcasted_iota(I32, (SGU_CHUNK, SGU_CHUNK), 1)
    rows = []
    for n in range(TILE // SGU_CHUNK):
        cols = []
        for g in range(SGU_GROUPS):
            wt = jnp.where(ii >= jj, sw_ref[g], 0.0).astype(BF16)
            vg = v[n * SGU_CHUNK:(n + 1) * SGU_CHUNK, g * 128:(g + 1) * 128].astype(BF16)
            cols.append(_dot(wt, vg))
        rows.append(jnp.concatenate(cols, axis=1) + sb_ref[...])
    sg = jnp.concatenate(rows, axis=0)
    br_b = _dot((u * sg).astype(BF16), wb_ref[...])
    acc = jax.nn.sigmoid(_dot(xb, wg1_ref[...])) * br_b
    xq = _dot(xb, wxq_ref[...])
    heads = []
    for h in range(XATTN_HEADS):
        qh = xq[:, h * 128:(h + 1) * 128].astype(BF16)
        kh = k_ref[0, :, h * 128:(h + 1) * 128]
        vh = v_ref[0, :, h * 128:(h + 1) * 128]
        sc = _dot_nt(qh, kh) * (128 ** -0.5)
        p = jnp.exp(sc - jnp.max(sc, axis=-1, keepdims=True))
        p = p / jnp.sum(p, axis=-1, keepdims=True)
        heads.append(_dot(p.astype(BF16), vh))
    att = jnp.concatenate(heads, axis=1).astype(BF16)
    br_c = _dot(att, wc_ref[...])
    o_ref[0] = acc + jax.nn.sigmoid(_dot(xb, wg2_ref[...])) * br_c


def _mix_bc(x, wts, k, v):
    bsz, s, d = x.shape
    tok = pl.BlockSpec((1, TILE, d), lambda b, i: (b, i, 0))
    kvs = pl.BlockSpec((1, N_MEM, XATTN_WIDTH), lambda b, i: (b, 0, 0))
    names = ("w_uv", "w_g1", "w_b", "sgu_ln_g", "sgu_ln_b", "sgu_w", "sgu_bias", "w_xq")
    consts = [wts[n] for n in names]
    tail = [wts["w_g2"], wts["w_c"]]
    return pl.pallas_call(
        _mix_bc_kernel,
        grid=(bsz, s // TILE),
        in_specs=[tok] + [_const_spec(c.shape) for c in consts] + [kvs, kvs]
        + [_const_spec(c.shape) for c in tail],
        out_specs=tok,
        out_shape=jax.ShapeDtypeStruct((bsz, s, d), F32),
        compiler_params=_params(("arbitrary", "arbitrary")),
        name="mix_bc",
    )(x, *consts, k, v, *tail)


def _unit_lower_inverses(lmats, ii, jj):
    eye = (ii == jj).astype(F32)
    x = ii ^ jj
    lsplit = [_split(l) for l in lmats]
    ts = [eye - jnp.where(x < 2, l, 0.0) for l in lmats]
    zero = jnp.zeros((CHUNK, CHUNK), BF16)
    s = 2
    while s < CHUNK:
        off = (x < 2 * s) & ((ii & s) != 0) & ((jj & s) == 0)
        tsplit = [_split(t) for t in ts]
        prods = []
        for (th, tl), (lh, ll) in zip(tsplit, lsplit):
            oh = jnp.where(off, lh, zero)
            ol = jnp.where(off, ll, zero)
            prods.append(_dot(th, oh) + (_dot(th, ol) + _dot(tl, oh)))
        nxt = []
        for t, p, (th, tl) in zip(ts, prods, tsplit):
            ph, plo = _split(p)
            nxt.append(t - (_dot(ph, th) + (_dot(ph, tl) + _dot(plo, th))))
        ts = nxt
        s *= 2
    return ts


def _gdn_prep_kernel(x_ref, wqkv_ref, wz_ref, wab_ref, convw_ref, prm_ref,
                     u_ref, w_ref, qd_ref, kd_ref, a_ref, gl_ref, z_ref, hist_ref):
    @pl.when(pl.program_id(1) == 0)
    def _():
        hist_ref[0:HALO, :] = jnp.zeros((HALO, 3 * GDN_WIDTH), F32)

    xb = x_ref[0].astype(BF16)
    z_ref[0] = _dot(xb, wz_ref[...])
    qkv = _dot(xb, wqkv_ref[...])
    hist_ref[HALO:HALO + TILE, :] = qkv
    conv = qkv * convw_ref[CONV_WIDTH - 1:CONV_WIDTH, :]
    for j in range(CONV_WIDTH - 1):
        shift = CONV_WIDTH - 1 - j
        conv = conv + hist_ref[HALO - shift:HALO - shift + TILE, :] * convw_ref[j:j + 1, :]
    hist_ref[0:HALO, :] = qkv[TILE - HALO:, :]
    qkv = conv * jax.nn.sigmoid(conv)

    ab = _dot(xb, wab_ref[...])
    g = -jnp.exp(prm_ref[0:1, :]) * _softplus(ab + prm_ref[1:2, :])
    beta = jax.nn.sigmoid(ab)

    ii = lax.broadcasted_iota(I32, (CHUNK, CHUNK), 0)
    jj = lax.broadcasted_iota(I32, (CHUNK, CHUNK), 1)
    tri = (ii >= jj).astype(BF16)
    items = []
    for c in range(TILE // CHUNK):
        r0 = c * CHUNK
        gch = g[r0:r0 + CHUNK, :]
        g1 = gch.astype(BF16)
        r1 = gch - g1.astype(F32)
        g2 = r1.astype(BF16)
        g3 = (r1 - g2.astype(F32)).astype(BF16)
        gc = _dot(tri, g1) + (_dot(tri, g2) + _dot(tri, g3))
        gct = gc.T
        for h in range(GDN_HEADS):
            qh = qkv[r0:r0 + CHUNK, h * HEAD_DIM:(h + 1) * HEAD_DIM]
            kh = qkv[r0:r0 + CHUNK, GDN_WIDTH + h * HEAD_DIM:GDN_WIDTH + (h + 1) * HEAD_DIM]
            vh = qkv[r0:r0 + CHUNK, 2 * GDN_WIDTH + h * HEAD_DIM:2 * GDN_WIDTH + (h + 1) * HEAD_DIM]
            qn = qh * lax.rsqrt(jnp.sum(qh * qh, axis=-1, keepdims=True) + RMS_EPS) * (HEAD_DIM ** -0.5)
            kn = kh * lax.rsqrt(jnp.sum(kh * kh, axis=-1, keepdims=True) + RMS_EPS)
            gcol = gc[:, h:h + 1]
            grow = gct[h:h + 1, :]
            bcol = beta[r0:r0 + CHUNK, GDN_HEADS + h:GDN_HEADS + h + 1]
            glast = gc[CHUNK - 1:CHUNK, h:h + 1]
            decay = jnp.exp(jnp.where(ii >= jj, gcol - grow, -jnp.inf))
            knb = kn.astype(BF16)
            kk = _dot_nt(knb, knb)
            qk = _dot_nt(qn.astype(BF16), knb)
            egc = jnp.exp(gcol)
            items.append(dict(
                c=c, h=h, lmat=jnp.where(ii > jj, kk * bcol * decay, 0.0),
                rhs=jnp.concatenate([vh * bcol, kn * (bcol * egc)], axis=1),
                qd=(qn * egc).astype(BF16), kd=(kn * jnp.exp(glast - gcol)).astype(BF16),
                a=(qk * decay).astype(BF16), gl=jnp.broadcast_to(jnp.exp(glast), (8, HEAD_DIM))))
    tinvs = _unit_lower_inverses([it["lmat"] for it in items], ii, jj)
    sols = [_dot3(tinv, it["rhs"]) for tinv, it in zip(tinvs, items)]
    for it, sol in zip(items, sols):
        rows = slice(it["c"] * CHUNK, (it["c"] + 1) * CHUNK)
        lanes = slice(it["h"] * HEAD_DIM, (it["h"] + 1) * HEAD_DIM)
        u_ref[0, rows, lanes] = sol[:, :HEAD_DIM]
        w_ref[0, rows, lanes] = sol[:, HEAD_DIM:].astype(BF16)
        qd_ref[0, rows, lanes] = it["qd"]
        kd_ref[0, rows, lanes] = it["kd"]
        a_ref[0, rows, lanes] = it["a"]
        gl_ref[0, it["c"] * 8:(it["c"] + 1) * 8, lanes] = it["gl"]


def _gdn_prep(x, wts):
    bsz, s, d = x.shape
    tok = pl.BlockSpec((1, TILE, d), lambda b, i: (b, i, 0))
    hw = pl.BlockSpec((1, TILE, GDN_WIDTH), lambda b, i: (b, i, 0))
    gls = pl.BlockSpec((1, 8 * TILE // CHUNK, GDN_WIDTH), lambda b, i: (b, i, 0))
    consts = [wts[n] for n in ("w_qkv", "w_z", "w_ab", "conv_w", "gdn_prm")]
    f32o = jax.ShapeDtypeStruct((bsz, s, GDN_WIDTH), F32)
    b16o = jax.ShapeDtypeStruct((bsz, s, GDN_WIDTH), BF16)
    glo = jax.ShapeDtypeStruct((bsz, 8 * s // CHUNK, GDN_WIDTH), F32)
    return pl.pallas_call(
        _gdn_prep_kernel,
        grid=(bsz, s // TILE),
        in_specs=[tok] + [_const_spec(c.shape) for c in consts],
        out_specs=[hw, hw, hw, hw, hw, gls, hw],
        out_shape=[f32o, b16o, b16o, b16o, b16o, glo, f32o],
        scratch_shapes=[pltpu.VMEM((HALO + TILE, 3 * GDN_WIDTH), F32)],
        compiler_params=_params(("arbitrary", "arbitrary")),
        name="gdn_prep",
    )(x, *consts)


def _gdn_scan_kernel(u_ref, w_ref, qd_ref, kd_ref, a_ref, gl_ref, z_ref, gn_ref, o_ref, state_ref):
    @pl.when(pl.program_id(0) == 0)
    def _():
        state_ref[...] = jnp.zeros(state_ref.shape, F32)

    bsz = u_ref.shape[0]
    for c in range(TILE // CHUNK):
        rows = slice(c * CHUNK, (c + 1) * CHUNK)
        for b in range(bsz):
            for h in range(GDN_HEADS):
                lanes = slice(h * HEAD_DIM, (h + 1) * HEAD_DIM)
                st = state_ref[b, h]
                wq = jnp.concatenate([w_ref[b, rows, lanes], qd_ref[b, rows, lanes]], axis=0)
                ws = _dot(wq, st.astype(BF16))
                v_new = u_ref[b, rows, lanes] - ws[:CHUNK]
                vb = v_new.astype(BF16)
                o = ws[CHUNK:] + _dot(a_ref[b, rows, lanes], vb)
                kdt = kd_ref[b, rows, lanes].astype(F32).T.astype(BF16)
                state_ref[b, h] = st * gl_ref[b, c * 8:c * 8 + 1, lanes] + _dot(kdt, vb)
                zz = z_ref[b, rows, lanes]
                o = o * lax.rsqrt(jnp.mean(o * o, axis=-1, keepdims=True) + RMS_EPS) * gn_ref[...]
                o_ref[b, rows, lanes] = (o * (zz * jax.nn.sigmoid(zz))).astype(BF16)


def _gdn_scan(u, w, qd, kd, a, gl, z, gn):
    bsz, s, _ = u.shape
    hw = pl.BlockSpec((bsz, TILE, GDN_WIDTH), lambda i: (0, i, 0))
    gls = pl.BlockSpec((bsz, 8 * TILE // CHUNK, GDN_WIDTH), lambda i: (0, i, 0))
    return pl.pallas_call(
        _gdn_scan_kernel,
        grid=(s // TILE,),
        in_specs=[hw, hw, hw, hw, hw, gls, hw, _const_spec(gn.shape)],
        out_specs=hw,
        out_shape=jax.ShapeDtypeStruct((bsz, s, GDN_WIDTH), BF16),
        scratch_shapes=[pltpu.VMEM((bsz, GDN_HEADS, HEAD_DIM, HEAD_DIM), F32)],
        compiler_params=_params(("arbitrary",)),
        name="gdn_scan",
    )(u, w, qd, kd, a, gl, z, gn)


def _merge_kernel(x_ref, og_ref, mbc_ref, wa_ref, wg0_ref, wo_ref, ln1g_ref, ln1b_ref,
                  rwt_ref, rb_ref, wsgu_ref, wsd_ref,
                  res_ref, hp_ref, idx_ref, wt_ref, rank_ref, cnt_ref, carry_ref):
    first = (pl.program_id(0) == 0) & (pl.program_id(1) == 0)

    @pl.when(first)
    def _():
        carry_ref[...] = jnp.zeros(carry_ref.shape, F32)

    x = x_ref[0]
    xb = x.astype(BF16)
    br_a = _dot(og_ref[0], wa_ref[...])
    merged = jax.nn.sigmoid(_dot(xb, wg0_ref[...])) * br_a + mbc_ref[0]
    y = _dot(merged.astype(BF16), wo_ref[...])
    h = _layer_norm(DEEPNORM_ALPHA * x + y, ln1g_ref[...], ln1b_ref[...])
    hb = h.astype(BF16)
    hp_ref[...] = _pack_bf16_pairs(h)

    gu = _dot(hb, wsgu_ref[...])
    gt = gu[:, :D_SHARED]
    act = gt * jax.nn.sigmoid(gt) * gu[:, D_SHARED:]
    res_ref[0] = DEEPNORM_ALPHA * h + _dot(act.astype(BF16), wsd_ref[...])

    scores = jax.nn.sigmoid(_dot_nt(rwt_ref[...], hb))
    biased = scores + rb_ref[...]
    g3 = biased.reshape(N_GROUPS, GROUP_SIZE, TILE)
    m1 = jnp.max(g3, axis=1)
    m1b = m1[:, None, :]
    n_top = jnp.sum((g3 == m1b).astype(F32), axis=1)
    m2 = jnp.max(jnp.where(g3 < m1b, g3, -jnp.inf), axis=1)
    gs = m1 + jnp.where(n_top >= 2.0, m1, m2)
    gidx = lax.broadcasted_iota(I32, (N_GROUPS, TILE), 0)
    beaten = jnp.zeros((N_GROUPS, TILE), F32)
    for g in range(N_GROUPS):
        row = gs[g:g + 1, :]
        beaten = beaten + ((row > gs) | ((row == gs) & (g < gidx))).astype(F32)
    sel = (beaten < float(TOPK_GROUPS)).astype(F32)
    sel_e = jnp.broadcast_to(sel[:, None, :], (N_GROUPS, GROUP_SIZE, TILE)).reshape(N_EXPERTS, TILE)
    masked = jnp.where(sel_e > 0.5, biased, -jnp.inf)
    eidx = lax.broadcasted_iota(I32, (N_EXPERTS, TILE), 0)
    chosen = jnp.zeros((N_EXPERTS, TILE), F32)
    picks, pick_scores = [], []
    for _k in range(TOP_K):
        mx = jnp.max(masked, axis=0, keepdims=True)
        pick = jnp.min(jnp.where(masked == mx, eidx, N_EXPERTS), axis=0, keepdims=True)
        hot = eidx == pick
        picks.append(pick)
        pick_scores.append(jnp.sum(jnp.where(hot, scores, 0.0), axis=0, keepdims=True))
        chosen = jnp.where(hot, 1.0, chosen)
        masked = jnp.where(hot, -jnp.inf, masked)
    total = pick_scores[0]
    for sck in pick_scores[1:]:
        total = total + sck
    idx_ref[...] = jnp.concatenate(picks, axis=0)
    wt_ref[...] = jnp.concatenate([sck / total * ROUTED_SCALE for sck in pick_scores], axis=0)

    ti = lax.broadcasted_iota(I32, (TILE, TILE), 0)
    tj = lax.broadcasted_iota(I32, (TILE, TILE), 1)
    before = _dot(chosen.astype(BF16), (ti < tj).astype(BF16))
    carry = carry_ref[...]
    before = before + jnp.concatenate([carry] * (TILE // 128), axis=1)
    rank_ref[...] = jnp.concatenate(
        [jnp.sum(jnp.where(eidx == p, before, 0.0), axis=0, keepdims=True) for p in picks],
        axis=0).astype(I32)
    carry = carry + _dot(chosen.astype(BF16), jnp.ones((TILE, 128), BF16))
    carry_ref[...] = carry
    cnt_ref[...] = carry.astype(I32)


def _merge(x, og, mbc, wts):
    bsz, s, d = x.shape
    t = bsz * s
    nt = s // TILE
    tok = pl.BlockSpec((1, TILE, d), lambda b, i: (b, i, 0))
    ogs = pl.BlockSpec((1, TILE, GDN_WIDTH), lambda b, i: (b, i, 0))
    flat = lambda b, i: (b * nt + i, 0)
    lane = lambda b, i: (0, b * nt + i)
    consts = [wts[n] for n in ("w_a", "w_g0", "w_o", "ln1_g", "ln1_b", "router_wt", "router_bias",
                               "ws_gu", "ws_down")]
    return pl.pallas_call(
        _merge_kernel,
        grid=(bsz, nt),
        in_specs=[tok, ogs, tok] + [_const_spec(c.shape) for c in consts],
        out_specs=[tok,
                   pl.BlockSpec((TILE, d // 2), flat),
                   pl.BlockSpec((TOP_K, TILE), lane),
                   pl.BlockSpec((TOP_K, TILE), lane),
                   pl.BlockSpec((TOP_K, TILE), lane),
                   _const_spec((N_EXPERTS, 128))],
        out_shape=[jax.ShapeDtypeStruct((bsz, s, d), F32),
                   jax.ShapeDtypeStruct((t, d // 2), U32),
                   jax.ShapeDtypeStruct((TOP_K, t), I32),
                   jax.ShapeDtypeStruct((TOP_K, t), F32),
                   jax.ShapeDtypeStruct((TOP_K, t), I32),
                   jax.ShapeDtypeStruct((N_EXPERTS, 128), I32)],
        scratch_shapes=[pltpu.VMEM((N_EXPERTS, 128), F32)],
        compiler_params=_params(("arbitrary", "arbitrary")),
        name="merge",
    )(x, og, mbc, *consts)


def _plan_kernel(idx_ref, rank_ref, cnt_ref, pos_ref, meta_ref):
    e = N_EXPERTS
    shift = EXPERT_BLOCK.bit_length() - 1
    cnt = cnt_ref[...]
    nblk = lax.shift_right_logical(cnt + (EXPERT_BLOCK - 1), shift)
    ei = lax.broadcasted_iota(I32, (e, e), 0)
    ej = lax.broadcasted_iota(I32, (e, e), 1)
    incl = _dot((ei >= ej).astype(BF16), nblk.astype(F32).astype(BF16)).astype(I32)
    excl = incl - nblk
    start = (excl * EXPERT_BLOCK).astype(F32)
    lane = lax.broadcasted_iota(I32, (e, 128), 1)
    n_used = jnp.broadcast_to(incl[e - 1:e, :], (e, 128))
    meta_ref[...] = jnp.where(lane == 0, excl, jnp.where(lane == 1, nblk, jnp.where(lane == 2, cnt, n_used)))

    e128 = lax.broadcasted_iota(I32, (e, 128), 0)

    def body(i, carry):
        off = pl.multiple_of(i * 128, 128)
        ids = idx_ref[:, pl.ds(off, 128)]
        rows = [jnp.sum(jnp.where(e128 == ids[k:k + 1, :], start, 0.0), axis=0, keepdims=True)
                for k in range(TOP_K)]
        pos_ref[:, pl.ds(off, 128)] = jnp.concatenate(rows, axis=0).astype(I32) + rank_ref[:, pl.ds(off, 128)]
        return carry

    lax.fori_loop(0, idx_ref.shape[1] // 128, body, 0)


def _plan(idx, rank, cnt):
    t = idx.shape[1]
    return pl.pallas_call(
        _plan_kernel,
        grid=(1,),
        in_specs=[_const_spec(idx.shape), _const_spec(rank.shape), _const_spec(cnt.shape)],
        out_specs=[_const_spec((TOP_K, t)), _const_spec((N_EXPERTS, 128))],
        out_shape=[jax.ShapeDtypeStruct((TOP_K, t), I32), jax.ShapeDtypeStruct((N_EXPERTS, 128), I32)],
        compiler_params=_params(("arbitrary",)),
        name="plan",
    )(idx, rank, cnt)


def _sc_scatter_rows(rows, pos, n_slots):
    t, d = rows.shape
    workers = SC_CORES * SC_SUBCORES
    per_w = t // workers
    nwin = per_w // SC_WINDOW
    assert per_w * workers == t and nwin * SC_WINDOW == per_w and nwin % 2 == 0
    idx = pos.reshape(TOP_K, workers, nwin, SC_WINDOW).transpose(1, 2, 0, 3).reshape(workers, nwin * TOP_K, SC_WINDOW)
    mesh = plsc.VectorSubcoreMesh(core_axis_name="core", subcore_axis_name="subcore",
                                  num_cores=SC_CORES, num_subcores=SC_SUBCORES)

    @functools.partial(
        pl.kernel, out_type=jax.ShapeDtypeStruct((n_slots, d), rows.dtype), mesh=mesh, name="sc_scatter_rows",
        scratch_types=[pltpu.VMEM((nwin * TOP_K, SC_WINDOW), I32), pltpu.VMEM((2, SC_WINDOW, d), rows.dtype),
                       pltpu.SemaphoreType.DMA((2,)), pltpu.SemaphoreType.DMA((2,))])
    def scatter(rows_hbm, idx_hbm, out_hbm, idx_v, rows_v, lsem, ssem):
        wid = lax.axis_index("subcore") * SC_CORES + lax.axis_index("core")
        base = wid * per_w
        pltpu.sync_copy(idx_hbm.at[wid], idx_v)

        def load(w, slot):
            return pltpu.make_async_copy(rows_hbm.at[pl.ds(base + w * SC_WINDOW, SC_WINDOW)], rows_v.at[slot],
                                         lsem.at[slot])

        def send(w, k, slot):
            return pltpu.make_async_copy(rows_v.at[slot], out_hbm.at[idx_v.at[w * TOP_K + k]], ssem.at[slot])

        load(0, 0).start()

        @pl.loop(0, nwin, step=2)
        def _(w0):
            for b in range(2):
                w = w0 + b
                load(w, b).wait()

                @pl.when(w + 1 < nwin)
                def _():
                    @pl.when(w >= 1)
                    def _():
                        for k in range(TOP_K):
                            send(w - 1, k, 1 - b).wait()

                    load(w + 1, 1 - b).start()

                for k in range(TOP_K):
                    send(w, k, b).start()

        for k in range(TOP_K):
            send(nwin - 2, k, 0).wait()
        for k in range(TOP_K):
            send(nwin - 1, k, 1).wait()

    return scatter(rows, idx)


def _pack_bf16_pairs(a):
    n = a.shape[1] // 2
    bits = lax.bitcast_convert_type(a.astype(BF16).astype(F32), U32)
    return (bits[:, :n] >> 16) | (bits[:, n:] & jnp.uint32(0xFFFF0000))


def _unpack_bf16_pairs(words):
    lo = lax.bitcast_convert_type(words << 16, F32)
    hi = lax.bitcast_convert_type(words & jnp.uint32(0xFFFF0000), F32)
    return jnp.concatenate([lo, hi], axis=1)


def _experts_kernel(fb_ref, nb_ref, cnt_ref, nu_ref, xs_ref, wg_ref, wu_ref, wd_ref, y_ref,
                    wgbuf, wubuf, wdbuf, wsem, wgu_s, wd_s, xbuf, ybuf, xsem, ysem):
    e = pl.program_id(0)
    n_used = nu_ref[0]
    bm = EXPERT_BLOCK
    row_queue = 1

    def x_copy(g, slot):
        return pltpu.make_async_copy(xs_ref.at[pl.ds(g * bm, bm), :], xbuf.at[slot], xsem.at[slot])

    def y_copy(g, slot):
        return pltpu.make_async_copy(ybuf.at[slot], y_ref.at[pl.ds(g * bm, bm), :], ysem.at[slot])

    @pl.when(e == 0)
    def _():
        for g0 in range(X_AHEAD):
            @pl.when(g0 < n_used)
            def _():
                x_copy(g0, g0).start(priority=row_queue)

    def w_copies(j, slot):
        return [pltpu.make_async_copy(src.at[j], dst.at[slot], wsem.at[slot])
                for src, dst in ((wg_ref, wgbuf), (wu_ref, wubuf), (wd_ref, wdbuf))]

    @pl.when(e == 0)
    def _():
        for j in range(W_SLOTS - 1):
            for cp in w_copies(j, j):
                cp.start()

    wslot = lax.rem(e, W_SLOTS)
    for cp in w_copies(e, wslot):
        cp.wait()
    ahead_e = e + (W_SLOTS - 1)

    @pl.when(ahead_e < pl.num_programs(0))
    def _():
        for cp in w_copies(ahead_e, lax.rem(ahead_e, W_SLOTS)):
            cp.start()

    nb = nb_ref[e]

    @pl.when(nb > 0)
    def _():
        wgu_s[:, :D_EXPERT] = wgbuf[wslot].astype(BF16)
        wgu_s[:, D_EXPERT:] = wubuf[wslot].astype(BF16)
        wd_s[...] = wdbuf[wslot].astype(BF16)

    def process(b0, width):
        g0 = fb_ref[e] + b0
        xs = []
        for i in range(width):
            g = g0 + i
            x_copy(g, g & (X_SLOTS - 1)).wait()
            xs.append(xbuf[g & (X_SLOTS - 1)])
        for i in range(width):
            ahead = g0 + X_AHEAD + i

            @pl.when(ahead < n_used)
            def _():
                x_copy(ahead, ahead & (X_SLOTS - 1)).start(priority=row_queue)

        ys = []
        for i in range(width):
            x = _unpack_bf16_pairs(xs[i])
            live = lax.broadcasted_iota(I32, (bm, 1), 0) < cnt_ref[e] - (b0 + i) * bm
            xb = jnp.where(live, x, 0.0).astype(BF16)
            gu = _dot(xb, wgu_s[...])
            gt = gu[:, :D_EXPERT]
            act = (gt * jax.nn.sigmoid(gt) * gu[:, D_EXPERT:]).astype(BF16)
            ys.append(_pack_bf16_pairs(_dot(act, wd_s[...])))
        for i in range(width):
            g = g0 + i
            slot = g & (Y_SLOTS - 1)

            @pl.when(g >= Y_SLOTS)
            def _():
                y_copy(g - Y_SLOTS, slot).wait()

            ybuf[slot] = ys[i]
            y_copy(g, slot).start(priority=row_queue)

    def pair(p, carry):
        process(2 * p, 2)
        return carry

    lax.fori_loop(0, nb >> 1, pair, 0)

    @pl.when((nb & 1) == 1)
    def _():
        process(nb - 1, 1)

    @pl.when(e == pl.num_programs(0) - 1)
    def _():
        for back in range(Y_SLOTS, 0, -1):
            @pl.when(n_used >= back)
            def _():
                y_copy(n_used - back, (n_used - back) & (Y_SLOTS - 1)).wait()


def _experts(xs, first_block, n_blocks, counts, n_used, w_gate, w_up, w_down):
    n_slots, half = xs.shape
    d = 2 * half
    grid_spec = pltpu.PrefetchScalarGridSpec(
        num_scalar_prefetch=4,
        grid=(N_EXPERTS,),
        in_specs=[pl.BlockSpec(memory_space=pl.ANY)] * 4,
        out_specs=pl.BlockSpec(memory_space=pl.ANY),
        scratch_shapes=[pltpu.VMEM((W_SLOTS, d, D_EXPERT), F32), pltpu.VMEM((W_SLOTS, d, D_EXPERT), F32),
                        pltpu.VMEM((W_SLOTS, D_EXPERT, d), F32), pltpu.SemaphoreType.DMA((W_SLOTS,)),
                        pltpu.VMEM((d, 2 * D_EXPERT), BF16), pltpu.VMEM((D_EXPERT, d), BF16),
                        pltpu.VMEM((X_SLOTS, EXPERT_BLOCK, half), U32),
                        pltpu.VMEM((Y_SLOTS, EXPERT_BLOCK, half), U32),
                        pltpu.SemaphoreType.DMA((X_SLOTS,)), pltpu.SemaphoreType.DMA((Y_SLOTS,))],
    )
    return pl.pallas_call(
        _experts_kernel,
        grid_spec=grid_spec,
        out_shape=jax.ShapeDtypeStruct((n_slots, half), U32),
        compiler_params=_params(("arbitrary",)),
        name="experts",
    )(first_block, n_blocks, counts, n_used, xs, w_gate, w_up, w_down)


def _sc_gather_rows(table, indices):
    n = indices.shape[0]
    d = table.shape[1]
    workers = SC_CORES * SC_SUBCORES
    per_w = n // workers
    nch = per_w // SC_WINDOW
    assert per_w * workers == n and nch * SC_WINDOW == per_w and nch % 2 == 0
    mesh = plsc.VectorSubcoreMesh(core_axis_name="core", subcore_axis_name="subcore",
                                  num_cores=SC_CORES, num_subcores=SC_SUBCORES)

    @functools.partial(
        pl.kernel, out_type=jax.ShapeDtypeStruct((n, d), table.dtype), mesh=mesh, name="sc_gather_rows",
        scratch_types=[pltpu.VMEM((per_w,), I32), pltpu.VMEM((2, SC_WINDOW, d), table.dtype),
                       pltpu.SemaphoreType.DMA((2,)), pltpu.SemaphoreType.DMA((2,))])
    def gather(table_hbm, idx_hbm, out_hbm, idx_v, rows_v, gsem, wsem):
        base = (lax.axis_index("subcore") * SC_CORES + lax.axis_index("core")) * per_w
        pltpu.sync_copy(idx_hbm.at[pl.ds(base, per_w)], idx_v)

        def fetch(c, slot):
            return pltpu.make_async_copy(table_hbm.at[idx_v.at[pl.ds(c * SC_WINDOW, SC_WINDOW)]],
                                         rows_v.at[slot], gsem.at[slot])

        def put(c, slot):
            return pltpu.make_async_copy(rows_v.at[slot], out_hbm.at[pl.ds(base + c * SC_WINDOW, SC_WINDOW)],
                                         wsem.at[slot])

        fetch(0, 0).start()

        @pl.loop(0, nch, step=2)
        def _(c0):
            for b in range(2):
                c = c0 + b
                fetch(c, b).wait()

                @pl.when(c + 1 < nch)
                def _():
                    @pl.when(c >= 1)
                    def _():
                        put(c - 1, 1 - b).wait()

                    fetch(c + 1, 1 - b).start()

                put(c, b).start()

        put(nch - 2, 0).wait()
        put(nch - 1, 1).wait()

    return gather(table, indices)


def _combine_kernel(yt_ref, wt_ref, res_ref, g_ref, b_ref, *rest):
    o_ref = rest[-1]
    acc = res_ref[...]
    for k in range(TOP_K):
        acc = acc + _unpack_bf16_pairs(yt_ref[k]) * wt_ref[:, k:k + 1]
    o_ref[...] = _layer_norm(acc, g_ref[...], b_ref[...])


def _combine(y_tok, wts_tok, res, ln_g, ln_b, part, out_prev):
    t, d = res.shape
    tiles = t // COMBINE_PARTS // COMBINE_TILE
    tok = pl.BlockSpec((COMBINE_TILE, d), lambda i: (part * tiles + i, 0))
    in_specs = [pl.BlockSpec((TOP_K, COMBINE_TILE, d // 2), lambda i: (0, i, 0)),
                pl.BlockSpec((COMBINE_TILE, TOP_K), lambda i: (part * tiles + i, 0)),
                tok, _const_spec(ln_g.shape), _const_spec(ln_b.shape)]
    args = [y_tok, wts_tok, res, ln_g, ln_b]
    aliases = {}
    if out_prev is not None:
        in_specs.append(pl.BlockSpec(memory_space=pl.ANY))
        args.append(out_prev)
        aliases = {len(args) - 1: 0}
    return pl.pallas_call(
        _combine_kernel,
        grid=(tiles,),
        in_specs=in_specs,
        out_specs=tok,
        out_shape=jax.ShapeDtypeStruct((t, d), F32),
        input_output_aliases=aliases,
        compiler_params=_params(("arbitrary",)),
        name="combine",
    )(*args)


def _prepare(l, w_in, conv_w, a_log, dt_bias, gdn_norm_g, w_a, sgu_ln_g, sgu_ln_b, sgu_w, sgu_b, w_b,
             w_mem_kv, w_c, w_o, ln1_g, ln1_b, router_w, router_bias, ws_gate, ws_up, ws_down, ln2_g, ln2_b):
    wi = w_in[l]
    d = D_MODEL
    bf = lambda a: a.astype(BF16)
    row = lambda a: a.reshape(1, -1).astype(F32)
    w_ab = jnp.zeros((d, 128), F32).at[:, :2 * GDN_HEADS].set(wi[:, _C_AB:_C_UV])
    prm = jnp.zeros((8, 128), F32).at[0, :GDN_HEADS].set(a_log[l]).at[1, :GDN_HEADS].set(dt_bias[l])
    sgu_bias = jnp.repeat(sgu_b[l].T, SGU_WIDTH // SGU_GROUPS, axis=1)
    return {
        "w_qkv": bf(wi[:, _C_QKV:_C_Z]), "w_z": bf(wi[:, _C_Z:_C_AB]), "w_ab": bf(w_ab),
        "w_uv": bf(wi[:, _C_UV:_C_XQ]), "w_xq": bf(wi[:, _C_XQ:_C_GATE]),
        "w_g0": bf(wi[:, _C_GATE:_C_GATE + d]), "w_g1": bf(wi[:, _C_GATE + d:_C_GATE + 2 * d]),
        "w_g2": bf(wi[:, _C_GATE + 2 * d:_C_GATE + 3 * d]),
        "conv_w": conv_w[l].astype(F32), "gdn_prm": prm, "gdn_norm_g": row(gdn_norm_g[l]),
        "w_a": bf(w_a[l]), "sgu_ln_g": row(sgu_ln_g[l]), "sgu_ln_b": row(sgu_ln_b[l]),
        "sgu_w": sgu_w[l].astype(F32), "sgu_bias": sgu_bias.astype(F32), "w_b": bf(w_b[l]),
        "w_mem_kv": bf(w_mem_kv[l]), "w_c": bf(w_c[l]), "w_o": bf(w_o[l]),
        "ln1_g": row(ln1_g[l]), "ln1_b": row(ln1_b[l]),
        "router_wt": bf(router_w[l].T), "router_bias": router_bias[l].reshape(-1, 1).astype(F32),
        "ws_gu": bf(jnp.concatenate([ws_gate[l], ws_up[l]], axis=1)), "ws_down": bf(ws_down[l]),
        "ln2_g": row(ln2_g[l]), "ln2_b": row(ln2_b[l]),
    }


def _layer(x, mem, wts, w_gate, w_up, w_down):
    bsz, s, d = x.shape
    t = bsz * s
    k, v = _mem_kv(mem, wts["w_mem_kv"])
    mbc = _mix_bc(x, wts, k, v)
    u, w, qd, kd, a, gl, z = _gdn_prep(x, wts)
    og = _gdn_scan(u, w, qd, kd, a, gl, z, wts["gdn_norm_g"])
    res, hp, idx, rw, rank, cnt = _merge(x, og, mbc, wts)
    nb = t * TOP_K // EXPERT_BLOCK + N_EXPERTS
    pos, meta = _plan(idx, rank, cnt)
    xs = _sc_scatter_rows(hp, pos, nb * EXPERT_BLOCK)
    y = _experts(xs, meta[:, 0], meta[:, 1], meta[:, 2], meta[:1, 3], w_gate, w_up, w_down)
    tp = t // COMBINE_PARTS
    out = None
    for part in range(COMBINE_PARTS):
        y_tok = _sc_gather_rows(y, pos[:, part * tp:(part + 1) * tp].reshape(-1)).reshape(TOP_K, tp, d // 2)
        out = _combine(y_tok, rw.T, res.reshape(t, d), wts["ln2_g"], wts["ln2_b"], part, out)
    return out.reshape(bsz, s, d)


def kernel(x, mem, w_in, conv_w, a_log, dt_bias, gdn_norm_g, w_a, sgu_ln_g, sgu_ln_b, sgu_w, sgu_b, w_b,
           w_mem_kv, w_c, w_o, ln1_g, ln1_b, router_w, router_bias, w_gate, w_up, w_down, ws_gate, ws_up,
           ws_down, ln2_g, ln2_b):
    assert x.shape[1] % TILE == 0 and x.shape[2] == D_MODEL
    for l in range(DEPTH):
        wts = _prepare(l, w_in, conv_w, a_log, dt_bias, gdn_norm_g, w_a, sgu_ln_g, sgu_ln_b, sgu_w, sgu_b,
                       w_b, w_mem_kv, w_c, w_o, ln1_g, ln1_b, router_w, router_bias, ws_gate, ws_up,
                       ws_down, ln2_g, ln2_b)
        x = _layer(x, mem, wts, w_gate[l], w_up[l], w_down[l])
    return x
```

```python
import functools

import jax
import jax.numpy as jnp
from jax import lax
from jax.experimental import pallas as pl
from jax.experimental.pallas import tpu as pltpu
from jax.experimental.pallas import tpu_sc as plsc

F32 = jnp.float32
BF16 = jnp.bfloat16
I32 = jnp.int32
U32 = jnp.uint32

D_MODEL = 1024
DEPTH = 1
GDN_HEADS = 4
HEAD_DIM = 128
GDN_WIDTH = GDN_HEADS * HEAD_DIM
CONV_WIDTH = 4
SGU_GROUPS = 4
SGU_WIDTH = 512
SGU_CHUNK = 128
N_MEM = 256
XATTN_HEADS = 4
XATTN_WIDTH = 512
N_EXPERTS = 256
TOP_K = 8
N_GROUPS = 8
GROUP_SIZE = N_EXPERTS // N_GROUPS
TOPK_GROUPS = 4
D_EXPERT = 256
D_SHARED = 256
ROUTED_SCALE = 2.5
LN_EPS = 1e-5
RMS_EPS = 1e-6
DEEPNORM_ALPHA = (2.0 * DEPTH) ** 0.25

_C_QKV = 0
_C_Z = 3 * GDN_WIDTH
_C_AB = 4 * GDN_WIDTH
_C_UV = _C_AB + 2 * GDN_HEADS
_C_XQ = _C_UV + 2 * SGU_WIDTH
_C_GATE = _C_XQ + XATTN_WIDTH

CHUNK = 128
GDN_GROUP = 2
TILE = 512
HALO = 8
EXPERT_BLOCK = 256
X_SLOTS = 8
X_AHEAD = X_SLOTS - 2
Y_SLOTS = 4
W_SLOTS = 3
SC_CORES = 2
SC_SUBCORES = 16
SC_WINDOW = 64
COMBINE_TILE = 256
VMEM_LIMIT = 56 * 1024 * 1024


def _dot(a, b):
    return jnp.dot(a, b, preferred_element_type=F32)


def _dot_nt(a, b):
    return lax.dot_general(a, b, (((1,), (1,)), ((), ())), preferred_element_type=F32)


def _split(a):
    hi = a.astype(BF16)
    lo = (a - hi.astype(F32)).astype(BF16)
    return hi, lo


def _dot3_parts(ah, al, bh, bl):
    return _dot(jnp.concatenate([ah, ah, al], axis=1), jnp.concatenate([bh, bl, bh], axis=0))


def _dot3(a, b):
    return _dot3_parts(*_split(a), *_split(b))


def _softplus(x):
    return jnp.maximum(x, 0.0) + jnp.log1p(jnp.exp(-jnp.abs(x)))


def _layer_norm(x, g, b):
    mu = jnp.mean(x, axis=-1, keepdims=True)
    xc = x - mu
    var = jnp.mean(xc * xc, axis=-1, keepdims=True)
    return xc * lax.rsqrt(var + LN_EPS) * g + b


def _const_spec(shape):
    nd = len(shape)
    return pl.BlockSpec(shape, lambda *_: (0,) * nd)


def _params(semantics):
    return pltpu.CompilerParams(dimension_semantics=semantics, vmem_limit_bytes=VMEM_LIMIT)


def _mem_kv_kernel(mem_ref, w_ref, k_ref, v_ref):
    kv = _dot(mem_ref[0].astype(BF16), w_ref[...])
    k_ref[0] = kv[:, :XATTN_WIDTH].astype(BF16)
    v_ref[0] = kv[:, XATTN_WIDTH:].astype(BF16)


def _mem_kv(mem, w_kv):
    bsz, n_mem, d = mem.shape
    out = jax.ShapeDtypeStruct((bsz, n_mem, XATTN_WIDTH), BF16)
    return pl.pallas_call(
        _mem_kv_kernel,
        grid=(bsz,),
        in_specs=[pl.BlockSpec((1, n_mem, d), lambda b: (b, 0, 0)), _const_spec(w_kv.shape)],
        out_specs=[pl.BlockSpec((1, n_mem, XATTN_WIDTH), lambda b: (b, 0, 0))] * 2,
        out_shape=[out, out],
        compiler_params=_params(("arbitrary",)),
        name="mem_kv",
    )(mem, w_kv)


def _mix_bc_kernel(x_ref, wuv_ref, wg1_ref, wb_ref, lng_ref, lnb_ref, sw_ref, sb_ref,
                   wxq_ref, k_ref, v_ref, wg2_ref, wc_ref, o_ref):
    xb = x_ref[0].astype(BF16)
    uv = jax.nn.gelu(_dot(xb, wuv_ref[...]), approximate=True)
    u = uv[:, :SGU_WIDTH]
    v = _layer_norm(uv[:, SGU_WIDTH:], lng_ref[...], lnb_ref[...])
    ii = lax.broadcasted_iota(I32, (SGU_CHUNK, SGU_CHUNK), 0)
    jj = lax.broadcasted_iota(I32, (SGU_CHUNK, SGU_CHUNK), 1)
    rows = []
    for n in range(TILE // SGU_CHUNK):
        cols = []
        for g in range(SGU_GROUPS):
            wt = jnp.where(ii >= jj, sw_ref[g], 0.0).astype(BF16)
            vg = v[n * SGU_CHUNK:(n + 1) * SGU_CHUNK, g * 128:(g + 1) * 128].astype(BF16)
            cols.append(_dot(wt, vg))
        rows.append(jnp.concatenate(cols, axis=1) + sb_ref[...])
    sg = jnp.concatenate(rows, axis=0)
    br_b = _dot((u * sg).astype(BF16), wb_ref[...])
    acc = jax.nn.sigmoid(_dot(xb, wg1_ref[...])) * br_b
    xq = _dot(xb, wxq_ref[...])
    heads = []
    for h in range(XATTN_HEADS):
        qh = xq[:, h * 128:(h + 1) * 128].astype(BF16)
        kh = k_ref[0, :, h * 128:(h + 1) * 128]
        vh = v_ref[0, :, h * 128:(h + 1) * 128]
        sc = _dot_nt(qh, kh) * (128 ** -0.5)
        p = jnp.exp(sc - jnp.max(sc, axis=-1, keepdims=True))
        p = p / jnp.sum(p, axis=-1, keepdims=True)
        heads.append(_dot(p.astype(BF16), vh))
    att = jnp.concatenate(heads, axis=1).astype(BF16)
    br_c = _dot(att, wc_ref[...])
    o_ref[0] = acc + jax.nn.sigmoid(_dot(xb, wg2_ref[...])) * br_c


def _mix_bc(x, wts, k, v):
    bsz, s, d = x.shape
    tok = pl.BlockSpec((1, TILE, d), lambda b, i: (b, i, 0))
    kvs = pl.BlockSpec((1, N_MEM, XATTN_WIDTH), lambda b, i: (b, 0, 0))
    names = ("w_uv", "w_g1", "w_b", "sgu_ln_g", "sgu_ln_b", "sgu_w", "sgu_bias", "w_xq")
    consts = [wts[n] for n in names]
    tail = [wts["w_g2"], wts["w_c"]]
    return pl.pallas_call(
        _mix_bc_kernel,
        grid=(bsz, s // TILE),
        in_specs=[tok] + [_const_spec(c.shape) for c in consts] + [kvs, kvs]
        + [_const_spec(c.shape) for c in tail],
        out_specs=tok,
        out_shape=jax.ShapeDtypeStruct((bsz, s, d), F32),
        compiler_params=_params(("arbitrary", "arbitrary")),
        name="mix_bc",
    )(x, *consts, k, v, *tail)


def _unit_lower_inverses(lmats, ii, jj):
    eye = (ii == jj).astype(F32)
    x = ii ^ jj
    lsplit = [_split(l) for l in lmats]
    ts = [eye - jnp.where(x < 2, l, 0.0) for l in lmats]
    zero = jnp.zeros((CHUNK, CHUNK), BF16)
    s = 2
    while s < CHUNK:
        off = (x < 2 * s) & ((ii & s) != 0) & ((jj & s) == 0)
        tsplit = [_split(t) for t in ts]
        prods = []
        for (th, tl), (lh, ll) in zip(tsplit, lsplit):
            oh = jnp.where(off, lh, zero)
            ol = jnp.where(off, ll, zero)
            prods.append(_dot3_parts(th, tl, oh, ol))
        nxt = []
        for t, p, (th, tl) in zip(ts, prods, tsplit):
            ph, plo = _split(p)
            nxt.append(t - _dot3_parts(ph, plo, th, tl))
        ts = nxt
        s *= 2
    return ts


def _gdn_kernel(x_ref, wqkv_ref, wz_ref, wab_ref, convw_ref, prm_ref, gn_ref, o_ref, hist_ref, state_ref):
    @pl.when(pl.program_id(1) == 0)
    def _():
        hist_ref[0:HALO, :] = jnp.zeros((HALO, 3 * GDN_WIDTH), F32)
        state_ref[...] = jnp.zeros(state_ref.shape, F32)

    xb = x_ref[0].astype(BF16)
    z = _dot(xb, wz_ref[...])
    qkv = _dot(xb, wqkv_ref[...])
    hist_ref[HALO:HALO + TILE, :] = qkv
    conv = qkv * convw_ref[CONV_WIDTH - 1:CONV_WIDTH, :]
    for j in range(CONV_WIDTH - 1):
        shift = CONV_WIDTH - 1 - j
        conv = conv + hist_ref[HALO - shift:HALO - shift + TILE, :] * convw_ref[j:j + 1, :]
    hist_ref[0:HALO, :] = qkv[TILE - HALO:, :]
    qkv = conv * jax.nn.sigmoid(conv)

    ab = _dot(xb, wab_ref[...])
    g = -jnp.exp(prm_ref[0:1, :]) * _softplus(ab + prm_ref[1:2, :])
    beta = jax.nn.sigmoid(ab)

    ii = lax.broadcasted_iota(I32, (CHUNK, CHUNK), 0)
    jj = lax.broadcasted_iota(I32, (CHUNK, CHUNK), 1)
    tri = (ii >= jj).astype(BF16)
    def chunk_items(c):
        items = []
        r0 = c * CHUNK
        gch = g[r0:r0 + CHUNK, :]
        g1 = gch.astype(BF16)
        r1 = gch - g1.astype(F32)
        g2 = r1.astype(BF16)
        g3 = (r1 - g2.astype(F32)).astype(BF16)
        gc = _dot(jnp.concatenate([tri, tri, tri], axis=1),
                  jnp.concatenate([g1, g2, g3], axis=0))
        gct = gc.T
        for h in range(GDN_HEADS):
            qh = qkv[r0:r0 + CHUNK, h * HEAD_DIM:(h + 1) * HEAD_DIM]
            kh = qkv[r0:r0 + CHUNK, GDN_WIDTH + h * HEAD_DIM:GDN_WIDTH + (h + 1) * HEAD_DIM]
            vh = qkv[r0:r0 + CHUNK, 2 * GDN_WIDTH + h * HEAD_DIM:2 * GDN_WIDTH + (h + 1) * HEAD_DIM]
            qn = qh * lax.rsqrt(jnp.sum(qh * qh, axis=-1, keepdims=True) + RMS_EPS) * (HEAD_DIM ** -0.5)
            kn = kh * lax.rsqrt(jnp.sum(kh * kh, axis=-1, keepdims=True) + RMS_EPS)
            gcol = gc[:, h:h + 1]
            grow = gct[h:h + 1, :]
            bcol = beta[r0:r0 + CHUNK, GDN_HEADS + h:GDN_HEADS + h + 1]
            glast = gc[CHUNK - 1:CHUNK, h:h + 1]
            decay = jnp.exp(jnp.where(ii >= jj, gcol - grow, -jnp.inf))
            knb = kn.astype(BF16)
            kk = _dot_nt(knb, knb)
            qk = _dot_nt(qn.astype(BF16), knb)
            egc = jnp.exp(gcol)
            items.append(dict(
                c=c, h=h, lmat=jnp.where(ii > jj, kk * bcol * decay, 0.0),
                rhs=jnp.concatenate([vh * bcol, kn * (bcol * egc)], axis=1),
                qd=(qn * egc).astype(BF16), kdt=(kn * jnp.exp(glast - gcol)).T.astype(BF16),
                a=(qk * decay).astype(BF16), gl=jnp.exp(glast)))
        return items

    states = [state_ref[h] for h in range(GDN_HEADS)]
    for c0 in range(0, TILE // CHUNK, GDN_GROUP):
        items = [it for c in range(c0, c0 + GDN_GROUP) for it in chunk_items(c)]
        tinvs = _unit_lower_inverses([it["lmat"] for it in items], ii, jj)
        sols = [_dot3(tinv, it["rhs"]) for tinv, it in zip(tinvs, items)]
        for it, sol in zip(items, sols):
            h = it["h"]
            rows = slice(it["c"] * CHUNK, (it["c"] + 1) * CHUNK)
            lanes = slice(h * HEAD_DIM, (h + 1) * HEAD_DIM)
            st = states[h]
            wq = jnp.concatenate([sol[:, HEAD_DIM:].astype(BF16), it["qd"]], axis=0)
            ws = _dot(wq, st.astype(BF16))
            vb = (sol[:, :HEAD_DIM] - ws[:CHUNK]).astype(BF16)
            o = ws[CHUNK:] + _dot(it["a"], vb)
            states[h] = st * it["gl"] + _dot(it["kdt"], vb)
            zz = z[rows, lanes]
            o = o * lax.rsqrt(jnp.mean(o * o, axis=-1, keepdims=True) + RMS_EPS) * gn_ref[...]
            o_ref[0, rows, lanes] = (o * (zz * jax.nn.sigmoid(zz))).astype(BF16)
    for h in range(GDN_HEADS):
        state_ref[h] = states[h]


def _gdn(x, wts):
    bsz, s, d = x.shape
    tok = pl.BlockSpec((1, TILE, d), lambda b, i: (b, i, 0))
    hw = pl.BlockSpec((1, TILE, GDN_WIDTH), lambda b, i: (b, i, 0))
    consts = [wts[n] for n in ("w_qkv", "w_z", "w_ab", "conv_w", "gdn_prm", "gdn_norm_g")]
    return pl.pallas_call(
        _gdn_kernel,
        grid=(bsz, s // TILE),
        in_specs=[tok] + [_const_spec(c.shape) for c in consts],
        out_specs=hw,
        out_shape=jax.ShapeDtypeStruct((bsz, s, GDN_WIDTH), BF16),
        scratch_shapes=[pltpu.VMEM((HALO + TILE, 3 * GDN_WIDTH), F32),
                        pltpu.VMEM((GDN_HEADS, HEAD_DIM, HEAD_DIM), F32)],
        compiler_params=_params(("arbitrary", "arbitrary")),
        name="gdn",
    )(x, *consts)


def _merge_kernel(x_ref, og_ref, mbc_ref, wa_ref, wg0_ref, wo_ref, ln1g_ref, ln1b_ref,
                  rwt_ref, rb_ref, wsgu_ref, wsd_ref,
                  res_ref, hp_ref, idx_ref, wt_ref, rank_ref, cnt_ref, carry_ref):
    first = (pl.program_id(0) == 0) & (pl.program_id(1) == 0)

    @pl.when(first)
    def _():
        carry_ref[...] = jnp.zeros(carry_ref.shape, F32)

    x = x_ref[0]
    xb = x.astype(BF16)
    br_a = _dot(og_ref[0], wa_ref[...])
    merged = jax.nn.sigmoid(_dot(xb, wg0_ref[...])) * br_a + mbc_ref[0]
    y = _dot(merged.astype(BF16), wo_ref[...])
    h = _layer_norm(DEEPNORM_ALPHA * x + y, ln1g_ref[...], ln1b_ref[...])
    hb = h.astype(BF16)
    hp_ref[...] = _pack_bf16_pairs(h)

    gu = _dot(hb, wsgu_ref[...])
    gt = gu[:, :D_SHARED]
    act = gt * jax.nn.sigmoid(gt) * gu[:, D_SHARED:]
    res_ref[0] = DEEPNORM_ALPHA * h + _dot(act.astype(BF16), wsd_ref[...])

    scores = jax.nn.sigmoid(_dot_nt(rwt_ref[...], hb))
    biased = scores + rb_ref[...]
    g3 = biased.reshape(N_GROUPS, GROUP_SIZE, TILE)
    m1 = jnp.max(g3, axis=1)
    m1b = m1[:, None, :]
    n_top = jnp.sum((g3 == m1b).astype(F32), axis=1)
    m2 = jnp.max(jnp.where(g3 < m1b, g3, -jnp.inf), axis=1)
    gs = m1 + jnp.where(n_top >= 2.0, m1, m2)
    gidx = lax.broadcasted_iota(I32, (N_GROUPS, TILE), 0)
    beaten = jnp.zeros((N_GROUPS, TILE), F32)
    for g in range(N_GROUPS):
        row = gs[g:g + 1, :]
        beaten = beaten + ((row > gs) | ((row == gs) & (g < gidx))).astype(F32)
    sel = (beaten < float(TOPK_GROUPS)).astype(F32)
    sel_e = jnp.broadcast_to(sel[:, None, :], (N_GROUPS, GROUP_SIZE, TILE)).reshape(N_EXPERTS, TILE)
    masked = jnp.where(sel_e > 0.5, biased, -jnp.inf)
    eidx = lax.broadcasted_iota(I32, (N_EXPERTS, TILE), 0)
    chosen = jnp.zeros((N_EXPERTS, TILE), F32)
    picks, pick_scores = [], []
    for _k in range(TOP_K):
        mx = jnp.max(masked, axis=0, keepdims=True)
        pick = jnp.min(jnp.where(masked == mx, eidx, N_EXPERTS), axis=0, keepdims=True)
        hot = eidx == pick
        picks.append(pick)
        pick_scores.append(jnp.sum(jnp.where(hot, scores, 0.0), axis=0, keepdims=True))
        chosen = jnp.where(hot, 1.0, chosen)
        masked = jnp.where(hot, -jnp.inf, masked)
    total = pick_scores[0]
    for sck in pick_scores[1:]:
        total = total + sck
    idx_ref[...] = jnp.concatenate(picks, axis=0)
    wt_ref[...] = jnp.concatenate([sck / total * ROUTED_SCALE for sck in pick_scores], axis=0)

    ti = lax.broadcasted_iota(I32, (TILE, TILE), 0)
    tj = lax.broadcasted_iota(I32, (TILE, TILE), 1)
    before = _dot(chosen.astype(BF16), (ti < tj).astype(BF16))
    carry = carry_ref[...]
    before = before + jnp.concatenate([carry] * (TILE // 128), axis=1)
    rank_ref[...] = jnp.concatenate(
        [jnp.sum(jnp.where(eidx == p, before, 0.0), axis=0, keepdims=True) for p in picks],
        axis=0).astype(I32)
    carry = carry + _dot(chosen.astype(BF16), jnp.ones((TILE, 128), BF16))
    carry_ref[...] = carry
    cnt_ref[...] = carry.astype(I32)


def _merge(x, og, mbc, wts):
    bsz, s, d = x.shape
    t = bsz * s
    nt = s // TILE
    tok = pl.BlockSpec((1, TILE, d), lambda b, i: (b, i, 0))
    ogs = pl.BlockSpec((1, TILE, GDN_WIDTH), lambda b, i: (b, i, 0))
    flat = lambda b, i: (b * nt + i, 0)
    lane = lambda b, i: (0, b * nt + i)
    consts = [wts[n] for n in ("w_a", "w_g0", "w_o", "ln1_g", "ln1_b", "router_wt", "router_bias",
                               "ws_gu", "ws_down")]
    return pl.pallas_call(
        _merge_kernel,
        grid=(bsz, nt),
        in_specs=[tok, ogs, tok] + [_const_spec(c.shape) for c in consts],
        out_specs=[tok,
                   pl.BlockSpec((TILE, d // 2), flat),
                   pl.BlockSpec((TOP_K, TILE), lane),
                   pl.BlockSpec((TOP_K, TILE), lane),
                   pl.BlockSpec((TOP_K, TILE), lane),
                   _const_spec((N_EXPERTS, 128))],
        out_shape=[jax.ShapeDtypeStruct((bsz, s, d), F32),
                   jax.ShapeDtypeStruct((t, d // 2), U32),
                   jax.ShapeDtypeStruct((TOP_K, t), I32),
                   jax.ShapeDtypeStruct((TOP_K, t), F32),
                   jax.ShapeDtypeStruct((TOP_K, t), I32),
                   jax.ShapeDtypeStruct((N_EXPERTS, 128), I32)],
        scratch_shapes=[pltpu.VMEM((N_EXPERTS, 128), F32)],
        compiler_params=_params(("arbitrary", "arbitrary")),
        name="merge",
    )(x, og, mbc, *consts)


def _plan_kernel(idx_ref, rank_ref, cnt_ref, pos_ref, win_ref, meta_ref):
    e = N_EXPERTS
    shift = EXPERT_BLOCK.bit_length() - 1
    cnt = cnt_ref[...]
    nblk = lax.shift_right_logical(cnt + (EXPERT_BLOCK - 1), shift)
    ei = lax.broadcasted_iota(I32, (e, e), 0)
    ej = lax.broadcasted_iota(I32, (e, e), 1)
    incl = _dot((ei >= ej).astype(BF16), nblk.astype(F32).astype(BF16)).astype(I32)
    excl = incl - nblk
    start = (excl * EXPERT_BLOCK).astype(F32)
    lane = lax.broadcasted_iota(I32, (e, 128), 1)
    n_used = jnp.broadcast_to(incl[e - 1:e, :], (e, 128))
    meta_ref[...] = jnp.where(lane == 0, excl, jnp.where(lane == 1, nblk, jnp.where(lane == 2, cnt, n_used)))

    e128 = lax.broadcasted_iota(I32, (e, 128), 0)

    def body(i, carry):
        off = pl.multiple_of(i * 128, 128)
        ids = idx_ref[:, pl.ds(off, 128)]
        rows = [jnp.sum(jnp.where(e128 == ids[k:k + 1, :], start, 0.0), axis=0, keepdims=True)
                for k in range(TOP_K)]
        p = jnp.concatenate(rows, axis=0).astype(I32) + rank_ref[:, pl.ds(off, 128)]
        pos_ref[:, pl.ds(off, 128)] = p
        for half in range(128 // SC_WINDOW):
            row0 = pl.multiple_of((i * (128 // SC_WINDOW) + half) * TOP_K, TOP_K)
            win_ref[pl.ds(row0, TOP_K), :] = p[:, half * SC_WINDOW:(half + 1) * SC_WINDOW]
        return carry

    lax.fori_loop(0, idx_ref.shape[1] // 128, body, 0)


def _plan(idx, rank, cnt):
    t = idx.shape[1]
    return pl.pallas_call(
        _plan_kernel,
        grid=(1,),
        in_specs=[_const_spec(idx.shape), _const_spec(rank.shape), _const_spec(cnt.shape)],
        out_specs=[_const_spec((TOP_K, t)), _const_spec((t // SC_WINDOW * TOP_K, SC_WINDOW)),
                   _const_spec((N_EXPERTS, 128))],
        out_shape=[jax.ShapeDtypeStruct((TOP_K, t), I32),
                   jax.ShapeDtypeStruct((t // SC_WINDOW * TOP_K, SC_WINDOW), I32),
                   jax.ShapeDtypeStruct((N_EXPERTS, 128), I32)],
        compiler_params=_params(("arbitrary",)),
        name="plan",
    )(idx, rank, cnt)


def _sc_scatter_rows(rows, win_slots, n_slots):
    t, d = rows.shape
    workers = SC_CORES * SC_SUBCORES
    per_w = t // workers
    nwin = per_w // SC_WINDOW
    assert per_w * workers == t and nwin * SC_WINDOW == per_w and nwin % 2 == 0
    idx = win_slots.reshape(workers, nwin * TOP_K, SC_WINDOW)
    mesh = plsc.VectorSubcoreMesh(core_axis_name="core", subcore_axis_name="subcore",
                                  num_cores=SC_CORES, num_subcores=SC_SUBCORES)

    @functools.partial(
        pl.kernel, out_type=jax.ShapeDtypeStruct((n_slots, d), rows.dtype), mesh=mesh, name="sc_scatter_rows",
        scratch_types=[pltpu.VMEM((nwin * TOP_K, SC_WINDOW), I32), pltpu.VMEM((2, SC_WINDOW, d), rows.dtype),
                       pltpu.SemaphoreType.DMA((2,)), pltpu.SemaphoreType.DMA((2,))])
    def scatter(rows_hbm, idx_hbm, out_hbm, idx_v, rows_v, lsem, ssem):
        wid = lax.axis_index("subcore") * SC_CORES + lax.axis_index("core")
        base = wid * per_w
        pltpu.sync_copy(idx_hbm.at[wid], idx_v)

        def load(w, slot):
            return pltpu.make_async_copy(rows_hbm.at[pl.ds(base + w * SC_WINDOW, SC_WINDOW)], rows_v.at[slot],
                                         lsem.at[slot])

        def send(w, k, slot):
            return pltpu.make_async_copy(rows_v.at[slot], out_hbm.at[idx_v.at[w * TOP_K + k]], ssem.at[slot])

        load(0, 0).start()

        @pl.loop(0, nwin, step=2)
        def _(w0):
            for b in range(2):
                w = w0 + b
                load(w, b).wait()

                @pl.when(w + 1 < nwin)
                def _():
                    @pl.when(w >= 1)
                    def _():
                        for k in range(TOP_K):
                            send(w - 1, k, 1 - b).wait()

                    load(w + 1, 1 - b).start()

                for k in range(TOP_K):
                    send(w, k, b).start()

        for k in range(TOP_K):
            send(nwin - 2, k, 0).wait()
        for k in range(TOP_K):
            send(nwin - 1, k, 1).wait()

    return scatter(rows, idx)


def _pack_bf16_pairs(a):
    n = a.shape[1] // 2
    bits = lax.bitcast_convert_type(a.astype(BF16).astype(F32), U32)
    return (bits[:, :n] >> 16) | (bits[:, n:] & jnp.uint32(0xFFFF0000))


def _unpack_bf16_pairs(words):
    lo = lax.bitcast_convert_type(words << 16, F32)
    hi = lax.bitcast_convert_type(words & jnp.uint32(0xFFFF0000), F32)
    return jnp.concatenate([lo, hi], axis=1)


def _experts_kernel(fb_ref, nb_ref, cnt_ref, nu_ref, xs_ref, wg_ref, wu_ref, wd_ref, y_ref,
                    wgbuf, wubuf, wdbuf, wsem, wgu_s, wd_s, xbuf, ybuf, xsem, ysem):
    e = pl.program_id(0)
    n_used = nu_ref[0]
    bm = EXPERT_BLOCK
    row_queue = 1

    def x_copy(g, slot):
        return pltpu.make_async_copy(xs_ref.at[pl.ds(g * bm, bm), :], xbuf.at[slot], xsem.at[slot])

    def y_copy(g, slot):
        return pltpu.make_async_copy(ybuf.at[slot], y_ref.at[pl.ds(g * bm, bm), :], ysem.at[slot])

    @pl.when(e == 0)
    def _():
        for g0 in range(X_AHEAD):
            @pl.when(g0 < n_used)
            def _():
                x_copy(g0, g0).start(priority=row_queue)

    def w_copies(j, slot):
        return [pltpu.make_async_copy(src.at[j], dst.at[slot], wsem.at[slot])
                for src, dst in ((wg_ref, wgbuf), (wu_ref, wubuf), (wd_ref, wdbuf))]

    @pl.when(e == 0)
    def _():
        for j in range(W_SLOTS - 1):
            for cp in w_copies(j, j):
                cp.start()

    wslot = lax.rem(e, W_SLOTS)
    for cp in w_copies(e, wslot):
        cp.wait()
    ahead_e = e + (W_SLOTS - 1)

    @pl.when(ahead_e < pl.num_programs(0))
    def _():
        for cp in w_copies(ahead_e, lax.rem(ahead_e, W_SLOTS)):
            cp.start()

    nb = nb_ref[e]

    @pl.when(nb > 0)
    def _():
        wgu_s[:, :D_EXPERT] = wgbuf[wslot].astype(BF16)
        wgu_s[:, D_EXPERT:] = wubuf[wslot].astype(BF16)
        wd_s[...] = wdbuf[wslot].astype(BF16)

    def process(b0, width):
        g0 = fb_ref[e] + b0
        xs = []
        for i in range(width):
            g = g0 + i
            x_copy(g, g & (X_SLOTS - 1)).wait()
            xs.append(xbuf[g & (X_SLOTS - 1)])
        for i in range(width):
            ahead = g0 + X_AHEAD + i

            @pl.when(ahead < n_used)
            def _():
                x_copy(ahead, ahead & (X_SLOTS - 1)).start(priority=row_queue)

        ys = []
        for i in range(width):
            x = _unpack_bf16_pairs(xs[i])
            live = lax.broadcasted_iota(I32, (bm, 1), 0) < cnt_ref[e] - (b0 + i) * bm
            xb = jnp.where(live, x, 0.0).astype(BF16)
            gu = _dot(xb, wgu_s[...])
            gt = gu[:, :D_EXPERT]
            act = (gt * jax.nn.sigmoid(gt) * gu[:, D_EXPERT:]).astype(BF16)
            ys.append(_pack_bf16_pairs(_dot(act, wd_s[...])))
        for i in range(width):
            g = g0 + i
            slot = g & (Y_SLOTS - 1)

            @pl.when(g >= Y_SLOTS)
            def _():
                y_copy(g - Y_SLOTS, slot).wait()

            ybuf[slot] = ys[i]
            y_copy(g, slot).start(priority=row_queue)

    def pair(p, carry):
        process(2 * p, 2)
        return carry

    lax.fori_loop(0, nb >> 1, pair, 0)

    @pl.when((nb & 1) == 1)
    def _():
        process(nb - 1, 1)

    @pl.when(e == pl.num_programs(0) - 1)
    def _():
        for back in range(Y_SLOTS, 0, -1):
            @pl.when(n_used >= back)
            def _():
                y_copy(n_used - back, (n_used - back) & (Y_SLOTS - 1)).wait()


def _experts(xs, first_block, n_blocks, counts, n_used, w_gate, w_up, w_down):
    n_slots, half = xs.shape
    d = 2 * half
    grid_spec = pltpu.PrefetchScalarGridSpec(
        num_scalar_prefetch=4,
        grid=(N_EXPERTS,),
        in_specs=[pl.BlockSpec(memory_space=pl.ANY)] * 4,
        out_specs=pl.BlockSpec(memory_space=pl.ANY),
        scratch_shapes=[pltpu.VMEM((W_SLOTS, d, D_EXPERT), F32), pltpu.VMEM((W_SLOTS, d, D_EXPERT), F32),
                        pltpu.VMEM((W_SLOTS, D_EXPERT, d), F32), pltpu.SemaphoreType.DMA((W_SLOTS,)),
                        pltpu.VMEM((d, 2 * D_EXPERT), BF16), pltpu.VMEM((D_EXPERT, d), BF16),
                        pltpu.VMEM((X_SLOTS, EXPERT_BLOCK, half), U32),
                        pltpu.VMEM((Y_SLOTS, EXPERT_BLOCK, half), U32),
                        pltpu.SemaphoreType.DMA((X_SLOTS,)), pltpu.SemaphoreType.DMA((Y_SLOTS,))],
    )
    return pl.pallas_call(
        _experts_kernel,
        grid_spec=grid_spec,
        out_shape=jax.ShapeDtypeStruct((n_slots, half), U32),
        compiler_params=_params(("arbitrary",)),
        name="experts",
    )(first_block, n_blocks, counts, n_used, xs, w_gate, w_up, w_down)


def _sc_gather_rows(table, indices):
    n = indices.shape[0]
    d = table.shape[1]
    workers = SC_CORES * SC_SUBCORES
    per_w = n // workers
    nch = per_w // SC_WINDOW
    assert per_w * workers == n and nch * SC_WINDOW == per_w and nch % 2 == 0
    mesh = plsc.VectorSubcoreMesh(core_axis_name="core", subcore_axis_name="subcore",
                                  num_cores=SC_CORES, num_subcores=SC_SUBCORES)

    @functools.partial(
        pl.kernel, out_type=jax.ShapeDtypeStruct((n, d), table.dtype), mesh=mesh, name="sc_gather_rows",
        scratch_types=[pltpu.VMEM((per_w,), I32), pltpu.VMEM((2, SC_WINDOW, d), table.dtype),
                       pltpu.SemaphoreType.DMA((2,)), pltpu.SemaphoreType.DMA((2,))])
    def gather(table_hbm, idx_hbm, out_hbm, idx_v, rows_v, gsem, wsem):
        base = (lax.axis_index("subcore") * SC_CORES + lax.axis_index("core")) * per_w
        pltpu.sync_copy(idx_hbm.at[pl.ds(base, per_w)], idx_v)

        def fetch(c, slot):
            return pltpu.make_async_copy(table_hbm.at[idx_v.at[pl.ds(c * SC_WINDOW, SC_WINDOW)]],
                                         rows_v.at[slot], gsem.at[slot])

        def put(c, slot):
            return pltpu.make_async_copy(rows_v.at[slot], out_hbm.at[pl.ds(base + c * SC_WINDOW, SC_WINDOW)],
                                         wsem.at[slot])

        fetch(0, 0).start()

        @pl.loop(0, nch, step=2)
        def _(c0):
            for b in range(2):
                c = c0 + b
                fetch(c, b).wait()

                @pl.when(c + 1 < nch)
                def _():
                    @pl.when(c >= 1)
                    def _():
                        put(c - 1, 1 - b).wait()

                    fetch(c + 1, 1 - b).start()

                put(c, b).start()

        put(nch - 2, 0).wait()
        put(nch - 1, 1).wait()

    return gather(table, indices)


def _combine_kernel(yt_ref, wt_ref, res_ref, g_ref, b_ref, o_ref):
    acc = res_ref[...]
    for k in range(TOP_K):
        acc = acc + _unpack_bf16_pairs(yt_ref[k]) * wt_ref[:, k:k + 1]
    o_ref[...] = _layer_norm(acc, g_ref[...], b_ref[...])


def _combine(y_tok, wts_tok, res, ln_g, ln_b):
    t, d = res.shape
    tok = pl.BlockSpec((COMBINE_TILE, d), lambda i: (i, 0))
    return pl.pallas_call(
        _combine_kernel,
        grid=(t // COMBINE_TILE,),
        in_specs=[pl.BlockSpec((TOP_K, COMBINE_TILE, d // 2), lambda i: (0, i, 0)),
                  pl.BlockSpec((COMBINE_TILE, TOP_K), lambda i: (i, 0)),
                  tok, _const_spec(ln_g.shape), _const_spec(ln_b.shape)],
        out_specs=tok,
        out_shape=jax.ShapeDtypeStruct((t, d), F32),
        compiler_params=_params(("arbitrary",)),
        name="combine",
    )(y_tok, wts_tok, res, ln_g, ln_b)


def _prepare(l, w_in, conv_w, a_log, dt_bias, gdn_norm_g, w_a, sgu_ln_g, sgu_ln_b, sgu_w, sgu_b, w_b,
             w_mem_kv, w_c, w_o, ln1_g, ln1_b, router_w, router_bias, ws_gate, ws_up, ws_down, ln2_g, ln2_b):
    wi = w_in[l]
    d = D_MODEL
    bf = lambda a: a.astype(BF16)
    row = lambda a: a.reshape(1, -1).astype(F32)
    w_ab = jnp.zeros((d, 128), F32).at[:, :2 * GDN_HEADS].set(wi[:, _C_AB:_C_UV])
    prm = jnp.zeros((8, 128), F32).at[0, :GDN_HEADS].set(a_log[l]).at[1, :GDN_HEADS].set(dt_bias[l])
    sgu_bias = jnp.repeat(sgu_b[l].T, SGU_WIDTH // SGU_GROUPS, axis=1)
    return {
        "w_qkv": bf(wi[:, _C_QKV:_C_Z]), "w_z": bf(wi[:, _C_Z:_C_AB]), "w_ab": bf(w_ab),
        "w_uv": bf(wi[:, _C_UV:_C_XQ]), "w_xq": bf(wi[:, _C_XQ:_C_GATE]),
        "w_g0": bf(wi[:, _C_GATE:_C_GATE + d]), "w_g1": bf(wi[:, _C_GATE + d:_C_GATE + 2 * d]),
        "w_g2": bf(wi[:, _C_GATE + 2 * d:_C_GATE + 3 * d]),
        "conv_w": conv_w[l].astype(F32), "gdn_prm": prm, "gdn_norm_g": row(gdn_norm_g[l]),
        "w_a": bf(w_a[l]), "sgu_ln_g": row(sgu_ln_g[l]), "sgu_ln_b": row(sgu_ln_b[l]),
        "sgu_w": sgu_w[l].astype(F32), "sgu_bias": sgu_bias.astype(F32), "w_b": bf(w_b[l]),
        "w_mem_kv": bf(w_mem_kv[l]), "w_c": bf(w_c[l]), "w_o": bf(w_o[l]),
        "ln1_g": row(ln1_g[l]), "ln1_b": row(ln1_b[l]),
        "router_wt": bf(router_w[l].T), "router_bias": router_bias[l].reshape(-1, 1).astype(F32),
        "ws_gu": bf(jnp.concatenate([ws_gate[l], ws_up[l]], axis=1)), "ws_down": bf(ws_down[l]),
        "ln2_g": row(ln2_g[l]), "ln2_b": row(ln2_b[l]),
    }


def _layer(x, mem, wts, w_gate, w_up, w_down):
    bsz, s, d = x.shape
    t = bsz * s
    k, v = _mem_kv(mem, wts["w_mem_kv"])
    mbc = _mix_bc(x, wts, k, v)
    og = _gdn(x, wts)
    res, hp, idx, rw, rank, cnt = _merge(x, og, mbc, wts)
    nb = t * TOP_K // EXPERT_BLOCK + N_EXPERTS
    pos, win_slots, meta = _plan(idx, rank, cnt)
    xs = _sc_scatter_rows(hp, win_slots, nb * EXPERT_BLOCK)
    y = _experts(xs, meta[:, 0], meta[:, 1], meta[:, 2], meta[:1, 3], w_gate, w_up, w_down)
    y_tok = _sc_gather_rows(y, pos.reshape(-1)).reshape(TOP_K, t, d // 2)
    out = _combine(y_tok, rw.T, res.reshape(t, d), wts["ln2_g"], wts["ln2_b"])
    return out.reshape(bsz, s, d)


def kernel(x, mem, w_in, conv_w, a_log, dt_bias, gdn_norm_g, w_a, sgu_ln_g, sgu_ln_b, sgu_w, sgu_b, w_b,
           w_mem_kv, w_c, w_o, ln1_g, ln1_b, router_w, router_bias, w_gate, w_up, w_down, ws_gate, ws_up,
           ws_down, ln2_g, ln2_b):
    assert x.shape[1] % TILE == 0 and x.shape[2] == D_MODEL
    for l in range(DEPTH):
        wts = _prepare(l, w_in, conv_w, a_log, dt_bias, gdn_norm_g, w_a, sgu_ln_g, sgu_ln_b, sgu_w, sgu_b,
                       w_b, w_mem_kv, w_c, w_o, ln1_g, ln1_b, router_w, router_bias, ws_gate, ws_up,
                       ws_down, ln2_g, ln2_b)
        x = _layer(x, mem, wts, w_gate[l], w_up[l], w_down[l])
    return x
```

```python
import functools

import jax
import jax.numpy as jnp
from jax import lax
from jax.experimental import pallas as pl
from jax.experimental.pallas import tpu as pltpu
from jax.experimental.pallas import tpu_sc as plsc

F32 = jnp.float32
BF16 = jnp.bfloat16
I32 = jnp.int32
U32 = jnp.uint32

D_MODEL = 1024
DEPTH = 1
GDN_HEADS = 4
HEAD_DIM = 128
GDN_WIDTH = GDN_HEADS * HEAD_DIM
CONV_WIDTH = 4
SGU_GROUPS = 4
SGU_WIDTH = 512
SGU_CHUNK = 128
N_MEM = 256
XATTN_HEADS = 4
XATTN_WIDTH = 512
N_EXPERTS = 256
TOP_K = 8
N_GROUPS = 8
GROUP_SIZE = N_EXPERTS // N_GROUPS
TOPK_GROUPS = 4
D_EXPERT = 256
D_SHARED = 256
ROUTED_SCALE = 2.5
LN_EPS = 1e-5
RMS_EPS = 1e-6
DEEPNORM_ALPHA = (2.0 * DEPTH) ** 0.25

_C_QKV = 0
_C_Z = 3 * GDN_WIDTH
_C_AB = 4 * GDN_WIDTH
_C_UV = _C_AB + 2 * GDN_HEADS
_C_XQ = _C_UV + 2 * SGU_WIDTH
_C_GATE = _C_XQ + XATTN_WIDTH

CHUNK = 128
GDN_GROUP = 2
TILE = 512
HALO = 8
EXPERT_BLOCK = 256
X_SLOTS = 8
X_AHEAD = X_SLOTS - 2
Y_SLOTS = 4
W_SLOTS = 3
SC_CORES = 2
SC_SUBCORES = 16
SC_WINDOW = 64
SC_LANES = 16
SC_PACK_CHUNK_WORDS = 32768
COMBINE_TILE = 256
VMEM_LIMIT = 56 * 1024 * 1024


def _dot(a, b):
    return jnp.dot(a, b, preferred_element_type=F32)


def _dot_nt(a, b):
    return lax.dot_general(a, b, (((1,), (1,)), ((), ())), preferred_element_type=F32)


def _split(a):
    hi = a.astype(BF16)
    lo = (a - hi.astype(F32)).astype(BF16)
    return hi, lo


def _dot3_parts(ah, al, bh, bl):
    return _dot(jnp.concatenate([ah, ah, al], axis=1), jnp.concatenate([bh, bl, bh], axis=0))


def _dot3(a, b):
    return _dot3_parts(*_split(a), *_split(b))


def _softplus(x):
    return jnp.maximum(x, 0.0) + jnp.log1p(jnp.exp(-jnp.abs(x)))


def _layer_norm(x, g, b):
    mu = jnp.mean(x, axis=-1, keepdims=True)
    xc = x - mu
    var = jnp.mean(xc * xc, axis=-1, keepdims=True)
    return xc * lax.rsqrt(var + LN_EPS) * g + b


def _const_spec(shape):
    nd = len(shape)
    return pl.BlockSpec(shape, lambda *_: (0,) * nd)


def _params(semantics):
    return pltpu.CompilerParams(dimension_semantics=semantics, vmem_limit_bytes=VMEM_LIMIT)


def _mem_kv_kernel(mem_ref, w_ref, k_ref, v_ref):
    kv = _dot(mem_ref[0].astype(BF16), w_ref[...])
    k_ref[0] = kv[:, :XATTN_WIDTH].astype(BF16)
    v_ref[0] = kv[:, XATTN_WIDTH:].astype(BF16)


def _mem_kv(mem, w_kv):
    bsz, n_mem, d = mem.shape
    out = jax.ShapeDtypeStruct((bsz, n_mem, XATTN_WIDTH), BF16)
    return pl.pallas_call(
        _mem_kv_kernel,
        grid=(bsz,),
        in_specs=[pl.BlockSpec((1, n_mem, d), lambda b: (b, 0, 0)), _const_spec(w_kv.shape)],
        out_specs=[pl.BlockSpec((1, n_mem, XATTN_WIDTH), lambda b: (b, 0, 0))] * 2,
        out_shape=[out, out],
        compiler_params=_params(("arbitrary",)),
        name="mem_kv",
    )(mem, w_kv)


def _mix_bc_kernel(x_ref, wuv_ref, wg1_ref, wb_ref, lng_ref, lnb_ref, sw_ref, sb_ref,
                   wxq_ref, k_ref, v_ref, wg2_ref, wc_ref, o_ref):
    xb = x_ref[0].astype(BF16)
    uv = jax.nn.gelu(_dot(xb, wuv_ref[...]), approximate=True)
    u = uv[:, :SGU_WIDTH]
    v = _layer_norm(uv[:, SGU_WIDTH:], lng_ref[...], lnb_ref[...])
    ii = lax.broadcasted_iota(I32, (SGU_CHUNK, SGU_CHUNK), 0)
    jj = lax.broadcasted_iota(I32, (SGU_CHUNK, SGU_CHUNK), 1)
    rows = []
    for n in range(TILE // SGU_CHUNK):
        cols = []
        for g in range(SGU_GROUPS):
            wt = jnp.where(ii >= jj, sw_ref[g], 0.0).astype(BF16)
            vg = v[n * SGU_CHUNK:(n + 1) * SGU_CHUNK, g * 128:(g + 1) * 128].astype(BF16)
            cols.append(_dot(wt, vg))
        rows.append(jnp.concatenate(cols, axis=1) + sb_ref[...])
    sg = jnp.concatenate(rows, axis=0)
    br_b = _dot((u * sg).astype(BF16), wb_ref[...])
    acc = jax.nn.sigmoid(_dot(xb, wg1_ref[...])) * br_b
    xq = _dot(xb, wxq_ref[...])
    heads = []
    for h in range(XATTN_HEADS):
        qh = xq[:, h * 128:(h + 1) * 128].astype(BF16)
        kh = k_ref[0, :, h * 128:(h + 1) * 128]
        vh = v_ref[0, :, h * 128:(h + 1) * 128]
        sc = _dot_nt(qh, kh) * (128 ** -0.5)
        p = jnp.exp(sc - jnp.max(sc, axis=-1, keepdims=True))
        p = p / jnp.sum(p, axis=-1, keepdims=True)
        heads.append(_dot(p.astype(BF16), vh))
    att = jnp.concatenate(heads, axis=1).astype(BF16)
    br_c = _dot(att, wc_ref[...])
    o_ref[0] = acc + jax.nn.sigmoid(_dot(xb, wg2_ref[...])) * br_c


def _mix_bc(x, wts, k, v):
    bsz, s, d = x.shape
    tok = pl.BlockSpec((1, TILE, d), lambda b, i: (b, i, 0))
    kvs = pl.BlockSpec((1, N_MEM, XATTN_WIDTH), lambda b, i: (b, 0, 0))
    names = ("w_uv", "w_g1", "w_b", "sgu_ln_g", "sgu_ln_b", "sgu_w", "sgu_bias", "w_xq")
    consts = [wts[n] for n in names]
    tail = [wts["w_g2"], wts["w_c"]]
    return pl.pallas_call(
        _mix_bc_kernel,
        grid=(bsz, s // TILE),
        in_specs=[tok] + [_const_spec(c.shape) for c in consts] + [kvs, kvs]
        + [_const_spec(c.shape) for c in tail],
        out_specs=tok,
        out_shape=jax.ShapeDtypeStruct((bsz, s, d), F32),
        compiler_params=_params(("arbitrary", "arbitrary")),
        name="mix_bc",
    )(x, *consts, k, v, *tail)


def _unit_lower_inverses(lmats, ii, jj):
    eye = (ii == jj).astype(F32)
    x = ii ^ jj
    lsplit = [_split(l) for l in lmats]
    ts = [eye - jnp.where(x < 2, l, 0.0) for l in lmats]
    zero = jnp.zeros((CHUNK, CHUNK), BF16)
    s = 2
    while s < CHUNK:
        off = (x < 2 * s) & ((ii & s) != 0) & ((jj & s) == 0)
        tsplit = [_split(t) for t in ts]
        prods = []
        for (th, tl), (lh, ll) in zip(tsplit, lsplit):
            oh = jnp.where(off, lh, zero)
            ol = jnp.where(off, ll, zero)
            prods.append(_dot3_parts(th, tl, oh, ol))
        nxt = []
        for t, p, (th, tl) in zip(ts, prods, tsplit):
            ph, plo = _split(p)
            nxt.append(t - _dot3_parts(ph, plo, th, tl))
        ts = nxt
        s *= 2
    return ts


def _gdn_kernel(x_ref, wqkv_ref, wz_ref, wab_ref, convw_ref, prm_ref, gn_ref, o_ref, hist_ref, state_ref):
    @pl.when(pl.program_id(1) == 0)
    def _():
        hist_ref[0:HALO, :] = jnp.zeros((HALO, 3 * GDN_WIDTH), F32)
        state_ref[...] = jnp.zeros(state_ref.shape, F32)

    xb = x_ref[0].astype(BF16)
    z = _dot(xb, wz_ref[...])
    qkv = _dot(xb, wqkv_ref[...])
    hist_ref[HALO:HALO + TILE, :] = qkv
    conv = qkv * convw_ref[CONV_WIDTH - 1:CONV_WIDTH, :]
    for j in range(CONV_WIDTH - 1):
        shift = CONV_WIDTH - 1 - j
        conv = conv + hist_ref[HALO - shift:HALO - shift + TILE, :] * convw_ref[j:j + 1, :]
    hist_ref[0:HALO, :] = qkv[TILE - HALO:, :]
    qkv = conv * jax.nn.sigmoid(conv)

    ab = _dot(xb, wab_ref[...])
    g = -jnp.exp(prm_ref[0:1, :]) * _softplus(ab + prm_ref[1:2, :])
    beta = jax.nn.sigmoid(ab)

    ii = lax.broadcasted_iota(I32, (CHUNK, CHUNK), 0)
    jj = lax.broadcasted_iota(I32, (CHUNK, CHUNK), 1)
    tri = (ii >= jj).astype(BF16)
    def chunk_items(c):
        items = []
        r0 = c * CHUNK
        gch = g[r0:r0 + CHUNK, :]
        g1 = gch.astype(BF16)
        r1 = gch - g1.astype(F32)
        g2 = r1.astype(BF16)
        g3 = (r1 - g2.astype(F32)).astype(BF16)
        gc = _dot(jnp.concatenate([tri, tri, tri], axis=1),
                  jnp.concatenate([g1, g2, g3], axis=0))
        gct = gc.T
        for h in range(GDN_HEADS):
            qh = qkv[r0:r0 + CHUNK, h * HEAD_DIM:(h + 1) * HEAD_DIM]
            kh = qkv[r0:r0 + CHUNK, GDN_WIDTH + h * HEAD_DIM:GDN_WIDTH + (h + 1) * HEAD_DIM]
            vh = qkv[r0:r0 + CHUNK, 2 * GDN_WIDTH + h * HEAD_DIM:2 * GDN_WIDTH + (h + 1) * HEAD_DIM]
            qn = qh * lax.rsqrt(jnp.sum(qh * qh, axis=-1, keepdims=True) + RMS_EPS) * (HEAD_DIM ** -0.5)
            kn = kh * lax.rsqrt(jnp.sum(kh * kh, axis=-1, keepdims=True) + RMS_EPS)
            gcol = gc[:, h:h + 1]
            grow = gct[h:h + 1, :]
            bcol = beta[r0:r0 + CHUNK, GDN_HEADS + h:GDN_HEADS + h + 1]
            glast = gc[CHUNK - 1:CHUNK, h:h + 1]
            decay = jnp.exp(jnp.where(ii >= jj, gcol - grow, -jnp.inf))
            knb = kn.astype(BF16)
            kk = _dot_nt(knb, knb)
            qk = _dot_nt(qn.astype(BF16), knb)
            egc = jnp.exp(gcol)
            items.append(dict(
                c=c, h=h, lmat=jnp.where(ii > jj, kk * bcol * decay, 0.0),
                rhs=jnp.concatenate([vh * bcol, kn * (bcol * egc)], axis=1),
                qd=(qn * egc).astype(BF16), kdt=(kn * jnp.exp(glast - gcol)).T.astype(BF16),
                a=(qk * decay).astype(BF16), gl=jnp.exp(glast)))
        return items

    states = [state_ref[h] for h in range(GDN_HEADS)]
    for c0 in range(0, TILE // CHUNK, GDN_GROUP):
        items = [it for c in range(c0, c0 + GDN_GROUP) for it in chunk_items(c)]
        tinvs = _unit_lower_inverses([it["lmat"] for it in items], ii, jj)
        sols = [_dot3(tinv, it["rhs"]) for tinv, it in zip(tinvs, items)]
        for it, sol in zip(items, sols):
            h = it["h"]
            rows = slice(it["c"] * CHUNK, (it["c"] + 1) * CHUNK)
            lanes = slice(h * HEAD_DIM, (h + 1) * HEAD_DIM)
            st = states[h]
            wq = jnp.concatenate([sol[:, HEAD_DIM:].astype(BF16), it["qd"]], axis=0)
            ws = _dot(wq, st.astype(BF16))
            vb = (sol[:, :HEAD_DIM] - ws[:CHUNK]).astype(BF16)
            o = ws[CHUNK:] + _dot(it["a"], vb)
            states[h] = st * it["gl"] + _dot(it["kdt"], vb)
            zz = z[rows, lanes]
            o = o * lax.rsqrt(jnp.mean(o * o, axis=-1, keepdims=True) + RMS_EPS) * gn_ref[...]
            o_ref[0, rows, lanes] = (o * (zz * jax.nn.sigmoid(zz))).astype(BF16)
    for h in range(GDN_HEADS):
        state_ref[h] = states[h]


def _gdn(x, wts):
    bsz, s, d = x.shape
    tok = pl.BlockSpec((1, TILE, d), lambda b, i: (b, i, 0))
    hw = pl.BlockSpec((1, TILE, GDN_WIDTH), lambda b, i: (b, i, 0))
    consts = [wts[n] for n in ("w_qkv", "w_z", "w_ab", "conv_w", "gdn_prm", "gdn_norm_g")]
    return pl.pallas_call(
        _gdn_kernel,
        grid=(bsz, s // TILE),
        in_specs=[tok] + [_const_spec(c.shape) for c in consts],
        out_specs=hw,
        out_shape=jax.ShapeDtypeStruct((bsz, s, GDN_WIDTH), BF16),
        scratch_shapes=[pltpu.VMEM((HALO + TILE, 3 * GDN_WIDTH), F32),
                        pltpu.VMEM((GDN_HEADS, HEAD_DIM, HEAD_DIM), F32)],
        compiler_params=_params(("arbitrary", "arbitrary")),
        name="gdn",
    )(x, *consts)


def _merge_kernel(x_ref, og_ref, mbc_ref, wa_ref, wg0_ref, wo_ref, ln1g_ref, ln1b_ref,
                  rwt_ref, rb_ref, wsgu_ref, wsd_ref,
                  res_ref, hp_ref, idx_ref, wt_ref, rank_ref, cnt_ref, carry_ref):
    first = (pl.program_id(0) == 0) & (pl.program_id(1) == 0)

    @pl.when(first)
    def _():
        carry_ref[...] = jnp.zeros(carry_ref.shape, F32)

    x = x_ref[0]
    xb = x.astype(BF16)
    br_a = _dot(og_ref[0], wa_ref[...])
    merged = jax.nn.sigmoid(_dot(xb, wg0_ref[...])) * br_a + mbc_ref[0]
    y = _dot(merged.astype(BF16), wo_ref[...])
    h = _layer_norm(DEEPNORM_ALPHA * x + y, ln1g_ref[...], ln1b_ref[...])
    hb = h.astype(BF16)
    hp_ref[...] = _pack_bf16_pairs(h)

    gu = _dot(hb, wsgu_ref[...])
    gt = gu[:, :D_SHARED]
    act = gt * jax.nn.sigmoid(gt) * gu[:, D_SHARED:]
    res_ref[0] = DEEPNORM_ALPHA * h + _dot(act.astype(BF16), wsd_ref[...])

    scores = jax.nn.sigmoid(_dot_nt(rwt_ref[...], hb))
    biased = scores + rb_ref[...]
    g3 = biased.reshape(N_GROUPS, GROUP_SIZE, TILE)
    m1 = jnp.max(g3, axis=1)
    m1b = m1[:, None, :]
    n_top = jnp.sum((g3 == m1b).astype(F32), axis=1)
    m2 = jnp.max(jnp.where(g3 < m1b, g3, -jnp.inf), axis=1)
    gs = m1 + jnp.where(n_top >= 2.0, m1, m2)
    gidx = lax.broadcasted_iota(I32, (N_GROUPS, TILE), 0)
    beaten = jnp.zeros((N_GROUPS, TILE), F32)
    for g in range(N_GROUPS):
        row = gs[g:g + 1, :]
        beaten = beaten + ((row > gs) | ((row == gs) & (g < gidx))).astype(F32)
    sel = (beaten < float(TOPK_GROUPS)).astype(F32)
    sel_e = jnp.broadcast_to(sel[:, None, :], (N_GROUPS, GROUP_SIZE, TILE)).reshape(N_EXPERTS, TILE)
    masked = jnp.where(sel_e > 0.5, biased, -jnp.inf)
    eidx = lax.broadcasted_iota(I32, (N_EXPERTS, TILE), 0)
    chosen = jnp.zeros((N_EXPERTS, TILE), F32)
    picks, pick_scores = [], []
    for _k in range(TOP_K):
        mx = jnp.max(masked, axis=0, keepdims=True)
        pick = jnp.min(jnp.where(masked == mx, eidx, N_EXPERTS), axis=0, keepdims=True)
        hot = eidx == pick
        picks.append(pick)
        pick_scores.append(jnp.sum(jnp.where(hot, scores, 0.0), axis=0, keepdims=True))
        chosen = jnp.where(hot, 1.0, chosen)
        masked = jnp.where(hot, -jnp.inf, masked)
    total = pick_scores[0]
    for sck in pick_scores[1:]:
        total = total + sck
    idx_ref[...] = jnp.concatenate(picks, axis=0)
    wt_ref[...] = jnp.concatenate([sck / total * ROUTED_SCALE for sck in pick_scores], axis=0)

    ti = lax.broadcasted_iota(I32, (TILE, TILE), 0)
    tj = lax.broadcasted_iota(I32, (TILE, TILE), 1)
    before = _dot(chosen.astype(BF16), (ti < tj).astype(BF16))
    carry = carry_ref[...]
    before = before + jnp.concatenate([carry] * (TILE // 128), axis=1)
    rank_ref[...] = jnp.concatenate(
        [jnp.sum(jnp.where(eidx == p, before, 0.0), axis=0, keepdims=True) for p in picks],
        axis=0).astype(I32)
    carry = carry + _dot(chosen.astype(BF16), jnp.ones((TILE, 128), BF16))
    carry_ref[...] = carry
    cnt_ref[...] = carry.astype(I32)


def _merge(x, og, mbc, wts):
    bsz, s, d = x.shape
    t = bsz * s
    nt = s // TILE
    tok = pl.BlockSpec((1, TILE, d), lambda b, i: (b, i, 0))
    ogs = pl.BlockSpec((1, TILE, GDN_WIDTH), lambda b, i: (b, i, 0))
    flat = lambda b, i: (b * nt + i, 0)
    lane = lambda b, i: (0, b * nt + i)
    consts = [wts[n] for n in ("w_a", "w_g0", "w_o", "ln1_g", "ln1_b", "router_wt", "router_bias",
                               "ws_gu", "ws_down")]
    return pl.pallas_call(
        _merge_kernel,
        grid=(bsz, nt),
        in_specs=[tok, ogs, tok] + [_const_spec(c.shape) for c in consts],
        out_specs=[tok,
                   pl.BlockSpec((TILE, d // 2), flat),
                   pl.BlockSpec((TOP_K, TILE), lane),
                   pl.BlockSpec((TOP_K, TILE), lane),
                   pl.BlockSpec((TOP_K, TILE), lane),
                   _const_spec((N_EXPERTS, 128))],
        out_shape=[jax.ShapeDtypeStruct((bsz, s, d), F32),
                   jax.ShapeDtypeStruct((t, d // 2), U32),
                   jax.ShapeDtypeStruct((TOP_K, t), I32),
                   jax.ShapeDtypeStruct((TOP_K, t), F32),
                   jax.ShapeDtypeStruct((TOP_K, t), I32),
                   jax.ShapeDtypeStruct((N_EXPERTS, 128), I32)],
        scratch_shapes=[pltpu.VMEM((N_EXPERTS, 128), F32)],
        compiler_params=_params(("arbitrary", "arbitrary")),
        name="merge",
    )(x, og, mbc, *consts)


def _plan_kernel(idx_ref, rank_ref, cnt_ref, pos_ref, win_ref, meta_ref):
    e = N_EXPERTS
    shift = EXPERT_BLOCK.bit_length() - 1
    cnt = cnt_ref[...]
    nblk = lax.shift_right_logical(cnt + (EXPERT_BLOCK - 1), shift)
    ei = lax.broadcasted_iota(I32, (e, e), 0)
    ej = lax.broadcasted_iota(I32, (e, e), 1)
    incl = _dot((ei >= ej).astype(BF16), nblk.astype(F32).astype(BF16)).astype(I32)
    excl = incl - nblk
    start = (excl * EXPERT_BLOCK).astype(F32)
    lane = lax.broadcasted_iota(I32, (e, 128), 1)
    n_used = jnp.broadcast_to(incl[e - 1:e, :], (e, 128))
    meta_ref[...] = jnp.where(lane == 0, excl, jnp.where(lane == 1, nblk, jnp.where(lane == 2, cnt, n_used)))

    e128 = lax.broadcasted_iota(I32, (e, 128), 0)

    def body(i, carry):
        off = pl.multiple_of(i * 128, 128)
        ids = idx_ref[:, pl.ds(off, 128)]
        rows = [jnp.sum(jnp.where(e128 == ids[k:k + 1, :], start, 0.0), axis=0, keepdims=True)
                for k in range(TOP_K)]
        p = jnp.concatenate(rows, axis=0).astype(I32) + rank_ref[:, pl.ds(off, 128)]
        pos_ref[:, pl.ds(off, 128)] = p
        for half in range(128 // SC_WINDOW):
            row0 = pl.multiple_of((i * (128 // SC_WINDOW) + half) * TOP_K, TOP_K)
            win_ref[pl.ds(row0, TOP_K), :] = p[:, half * SC_WINDOW:(half + 1) * SC_WINDOW]
        return carry

    lax.fori_loop(0, idx_ref.shape[1] // 128, body, 0)


def _plan(idx, rank, cnt):
    t = idx.shape[1]
    return pl.pallas_call(
        _plan_kernel,
        grid=(1,),
        in_specs=[_const_spec(idx.shape), _const_spec(rank.shape), _const_spec(cnt.shape)],
        out_specs=[_const_spec((TOP_K, t)), _const_spec((t // SC_WINDOW * TOP_K, SC_WINDOW)),
                   _const_spec((N_EXPERTS, 128))],
        out_shape=[jax.ShapeDtypeStruct((TOP_K, t), I32),
                   jax.ShapeDtypeStruct((t // SC_WINDOW * TOP_K, SC_WINDOW), I32),
                   jax.ShapeDtypeStruct((N_EXPERTS, 128), I32)],
        compiler_params=_params(("arbitrary",)),
        name="plan",
    )(idx, rank, cnt)


def _sc_scatter_rows(rows, win_slots, n_slots):
    t, d = rows.shape
    workers = SC_CORES * SC_SUBCORES
    per_w = t // workers
    nwin = per_w // SC_WINDOW
    assert per_w * workers == t and nwin * SC_WINDOW == per_w and nwin % 2 == 0
    idx = win_slots.reshape(workers, nwin * TOP_K, SC_WINDOW)
    mesh = plsc.VectorSubcoreMesh(core_axis_name="core", subcore_axis_name="subcore",
                                  num_cores=SC_CORES, num_subcores=SC_SUBCORES)

    @functools.partial(
        pl.kernel, out_type=jax.ShapeDtypeStruct((n_slots, d), rows.dtype), mesh=mesh, name="sc_scatter_rows",
        scratch_types=[pltpu.VMEM((nwin * TOP_K, SC_WINDOW), I32), pltpu.VMEM((2, SC_WINDOW, d), rows.dtype),
                       pltpu.SemaphoreType.DMA((2,)), pltpu.SemaphoreType.DMA((2,))])
    def scatter(rows_hbm, idx_hbm, out_hbm, idx_v, rows_v, lsem, ssem):
        wid = lax.axis_index("subcore") * SC_CORES + lax.axis_index("core")
        base = wid * per_w
        pltpu.sync_copy(idx_hbm.at[wid], idx_v)

        def load(w, slot):
            return pltpu.make_async_copy(rows_hbm.at[pl.ds(base + w * SC_WINDOW, SC_WINDOW)], rows_v.at[slot],
                                         lsem.at[slot])

        def send(w, k, slot):
            return pltpu.make_async_copy(rows_v.at[slot], out_hbm.at[idx_v.at[w * TOP_K + k]], ssem.at[slot])

        load(0, 0).start()

        @pl.loop(0, nwin, step=2)
        def _(w0):
            for b in range(2):
                w = w0 + b
                load(w, b).wait()

                @pl.when(w + 1 < nwin)
                def _():
                    @pl.when(w >= 1)
                    def _():
                        for k in range(TOP_K):
                            send(w - 1, k, 1 - b).wait()

                    load(w + 1, 1 - b).start()

                for k in range(TOP_K):
                    send(w, k, b).start()

        for k in range(TOP_K):
            send(nwin - 2, k, 0).wait()
        for k in range(TOP_K):
            send(nwin - 1, k, 1).wait()

    return scatter(rows, idx)


def _pack_bf16_pairs(a):
    n = a.shape[1] // 2
    bits = lax.bitcast_convert_type(a.astype(BF16).astype(F32), U32)
    return (bits[:, :n] >> 16) | (bits[:, n:] & jnp.uint32(0xFFFF0000))


def _unpack_bf16_pairs(words):
    high = jnp.asarray(-65536, words.dtype) if words.dtype == I32 else jnp.uint32(0xFFFF0000)
    lo = lax.bitcast_convert_type(words << 16, F32)
    hi = lax.bitcast_convert_type(words & high, F32)
    return jnp.concatenate([lo, hi], axis=1)


def _sc_pack_weights(w):
    e, r, c = w.shape
    half = c // 2
    lanes = SC_LANES
    workers = SC_CORES * SC_SUBCORES
    rows_w = e // workers * r
    chunk = SC_PACK_CHUNK_WORDS // c
    nch = rows_w // chunk
    assert e % workers == 0 and nch * chunk == rows_w and nch % 2 == 0 and half % lanes == 0
    mesh = plsc.VectorSubcoreMesh(core_axis_name="core", subcore_axis_name="subcore",
                                  num_cores=SC_CORES, num_subcores=SC_SUBCORES)

    @functools.partial(
        pl.kernel, out_type=jax.ShapeDtypeStruct((e * r, half), I32), mesh=mesh, name="sc_pack_weights",
        compiler_params=pltpu.CompilerParams(needs_layout_passes=False),
        scratch_types=[pltpu.VMEM((2, chunk, c), F32), pltpu.VMEM((2, chunk, half), I32),
                       pltpu.SemaphoreType.DMA((2,)), pltpu.SemaphoreType.DMA((2,))])
    def pack(w_hbm, out_hbm, inb, outb, isem, osem):
        base = (lax.axis_index("subcore") * SC_CORES + lax.axis_index("core")) * rows_w

        def load(k, slot):
            return pltpu.make_async_copy(w_hbm.at[pl.ds(base + k * chunk, chunk)], inb.at[slot], isem.at[slot])

        def put(k, slot):
            return pltpu.make_async_copy(outb.at[slot], out_hbm.at[pl.ds(base + k * chunk, chunk)], osem.at[slot])

        def rounded(x):
            bits = lax.bitcast_convert_type(x, I32)
            return bits + 0x7FFF + (lax.shift_right_logical(bits, 16) & 1)

        load(0, 0).start()

        @pl.loop(0, nch, step=2)
        def _(k0):
            for b in range(2):
                k = k0 + b
                load(k, b).wait()

                @pl.when(k + 1 < nch)
                def _():
                    load(k + 1, 1 - b).start()

                @pl.when(k >= 2)
                def _():
                    put(k - 2, b).wait()

                @pl.loop(0, chunk)
                def _(row):
                    for j in range(half // lanes):
                        lo = rounded(inb[b, row, pl.ds(j * lanes, lanes)])
                        hi = rounded(inb[b, row, pl.ds(half + j * lanes, lanes)])
                        outb[b, row, pl.ds(j * lanes, lanes)] = lax.shift_right_logical(lo, 16) | (hi & -65536)

                put(k, b).start()

        put(nch - 2, 0).wait()
        put(nch - 1, 1).wait()

    return pack(w.reshape(e * r, c)).reshape(e, r, half)


def _experts_kernel(fb_ref, nb_ref, cnt_ref, nu_ref, xs_ref, wg_ref, wu_ref, wd_ref, y_ref,
                    wgbuf, wubuf, wdbuf, wsem, wgu_s, wd_s, xbuf, ybuf, xsem, ysem):
    e = pl.program_id(0)
    n_used = nu_ref[0]
    bm = EXPERT_BLOCK
    row_queue = 1

    def x_copy(g, slot):
        return pltpu.make_async_copy(xs_ref.at[pl.ds(g * bm, bm), :], xbuf.at[slot], xsem.at[slot])

    def y_copy(g, slot):
        return pltpu.make_async_copy(ybuf.at[slot], y_ref.at[pl.ds(g * bm, bm), :], ysem.at[slot])

    @pl.when(e == 0)
    def _():
        for g0 in range(X_AHEAD):
            @pl.when(g0 < n_used)
            def _():
                x_copy(g0, g0).start(priority=row_queue)

    def w_copies(j, slot):
        return [pltpu.make_async_copy(src.at[j], dst.at[slot], wsem.at[slot])
                for src, dst in ((wg_ref, wgbuf), (wu_ref, wubuf), (wd_ref, wdbuf))]

    @pl.when(e == 0)
    def _():
        for j in range(W_SLOTS - 1):
            for cp in w_copies(j, j):
                cp.start()

    wslot = lax.rem(e, W_SLOTS)
    for cp in w_copies(e, wslot):
        cp.wait()
    ahead_e = e + (W_SLOTS - 1)

    @pl.when(ahead_e < pl.num_programs(0))
    def _():
        for cp in w_copies(ahead_e, lax.rem(ahead_e, W_SLOTS)):
            cp.start()

    nb = nb_ref[e]

    @pl.when(nb > 0)
    def _():
        wgu_s[:, :D_EXPERT] = _unpack_bf16_pairs(wgbuf[wslot]).astype(BF16)
        wgu_s[:, D_EXPERT:] = _unpack_bf16_pairs(wubuf[wslot]).astype(BF16)
        wd_s[...] = _unpack_bf16_pairs(wdbuf[wslot]).astype(BF16)

    def process(b0, width):
        g0 = fb_ref[e] + b0
        xs = []
        for i in range(width):
            g = g0 + i
            x_copy(g, g & (X_SLOTS - 1)).wait()
            xs.append(xbuf[g & (X_SLOTS - 1)])
        for i in range(width):
            ahead = g0 + X_AHEAD + i

            @pl.when(ahead < n_used)
            def _():
                x_copy(ahead, ahead & (X_SLOTS - 1)).start(priority=row_queue)

        ys = []
        for i in range(width):
            x = _unpack_bf16_pairs(xs[i])
            live = lax.broadcasted_iota(I32, (bm, 1), 0) < cnt_ref[e] - (b0 + i) * bm
            xb = jnp.where(live, x, 0.0).astype(BF16)
            gu = _dot(xb, wgu_s[...])
            gt = gu[:, :D_EXPERT]
            act = (gt * jax.nn.sigmoid(gt) * gu[:, D_EXPERT:]).astype(BF16)
            ys.append(_pack_bf16_pairs(_dot(act, wd_s[...])))
        for i in range(width):
            g = g0 + i
            slot = g & (Y_SLOTS - 1)

            @pl.when(g >= Y_SLOTS)
            def _():
                y_copy(g - Y_SLOTS, slot).wait()

            ybuf[slot] = ys[i]
            y_copy(g, slot).start(priority=row_queue)

    def pair(p, carry):
        process(2 * p, 2)
        return carry

    lax.fori_loop(0, nb >> 1, pair, 0)

    @pl.when((nb & 1) == 1)
    def _():
        process(nb - 1, 1)

    @pl.when(e == pl.num_programs(0) - 1)
    def _():
        for back in range(Y_SLOTS, 0, -1):
            @pl.when(n_used >= back)
            def _():
                y_copy(n_used - back, (n_used - back) & (Y_SLOTS - 1)).wait()


def _experts(xs, first_block, n_blocks, counts, n_used, w_gate, w_up, w_down):
    n_slots, half = xs.shape
    d = 2 * half
    grid_spec = pltpu.PrefetchScalarGridSpec(
        num_scalar_prefetch=4,
        grid=(N_EXPERTS,),
        in_specs=[pl.BlockSpec(memory_space=pl.ANY)] * 4,
        out_specs=pl.BlockSpec(memory_space=pl.ANY),
        scratch_shapes=[pltpu.VMEM((W_SLOTS, d, D_EXPERT // 2), I32), pltpu.VMEM((W_SLOTS, d, D_EXPERT // 2), I32),
                        pltpu.VMEM((W_SLOTS, D_EXPERT, d // 2), I32), pltpu.SemaphoreType.DMA((W_SLOTS,)),
                        pltpu.VMEM((d, 2 * D_EXPERT), BF16), pltpu.VMEM((D_EXPERT, d), BF16),
                        pltpu.VMEM((X_SLOTS, EXPERT_BLOCK, half), U32),
                        pltpu.VMEM((Y_SLOTS, EXPERT_BLOCK, half), U32),
                        pltpu.SemaphoreType.DMA((X_SLOTS,)), pltpu.SemaphoreType.DMA((Y_SLOTS,))],
    )
    return pl.pallas_call(
        _experts_kernel,
        grid_spec=grid_spec,
        out_shape=jax.ShapeDtypeStruct((n_slots, half), U32),
        compiler_params=_params(("arbitrary",)),
        name="experts",
    )(first_block, n_blocks, counts, n_used, xs, w_gate, w_up, w_down)


def _sc_gather_rows(table, indices):
    n = indices.shape[0]
    d = table.shape[1]
    workers = SC_CORES * SC_SUBCORES
    per_w = n // workers
    nch = per_w // SC_WINDOW
    assert per_w * workers == n and nch * SC_WINDOW == per_w and nch % 2 == 0
    mesh = plsc.VectorSubcoreMesh(core_axis_name="core", subcore_axis_name="subcore",
                                  num_cores=SC_CORES, num_subcores=SC_SUBCORES)

    @functools.partial(
        pl.kernel, out_type=jax.ShapeDtypeStruct((n, d), table.dtype), mesh=mesh, name="sc_gather_rows",
        scratch_types=[pltpu.VMEM((per_w,), I32), pltpu.VMEM((2, SC_WINDOW, d), table.dtype),
                       pltpu.SemaphoreType.DMA((2,)), pltpu.SemaphoreType.DMA((2,))])
    def gather(table_hbm, idx_hbm, out_hbm, idx_v, rows_v, gsem, wsem):
        base = (lax.axis_index("subcore") * SC_CORES + lax.axis_index("core")) * per_w
        pltpu.sync_copy(idx_hbm.at[pl.ds(base, per_w)], idx_v)

        def fetch(c, slot):
            return pltpu.make_async_copy(table_hbm.at[idx_v.at[pl.ds(c * SC_WINDOW, SC_WINDOW)]],
                                         rows_v.at[slot], gsem.at[slot])

        def put(c, slot):
            return pltpu.make_async_copy(rows_v.at[slot], out_hbm.at[pl.ds(base + c * SC_WINDOW, SC_WINDOW)],
                                         wsem.at[slot])

        fetch(0, 0).start()

        @pl.loop(0, nch, step=2)
        def _(c0):
            for b in range(2):
                c = c0 + b
                fetch(c, b).wait()

                @pl.when(c + 1 < nch)
                def _():
                    @pl.when(c >= 1)
                    def _():
                        put(c - 1, 1 - b).wait()

                    fetch(c + 1, 1 - b).start()

                put(c, b).start()

        put(nch - 2, 0).wait()
        put(nch - 1, 1).wait()

    return gather(table, indices)


def _combine_kernel(yt_ref, wt_ref, res_ref, g_ref, b_ref, o_ref):
    acc = res_ref[...]
    for k in range(TOP_K):
        acc = acc + _unpack_bf16_pairs(yt_ref[k]) * wt_ref[:, k:k + 1]
    o_ref[...] = _layer_norm(acc, g_ref[...], b_ref[...])


def _combine(y_tok, wts_tok, res, ln_g, ln_b):
    t, d = res.shape
    tok = pl.BlockSpec((COMBINE_TILE, d), lambda i: (i, 0))
    return pl.pallas_call(
        _combine_kernel,
        grid=(t // COMBINE_TILE,),
        in_specs=[pl.BlockSpec((TOP_K, COMBINE_TILE, d // 2), lambda i: (0, i, 0)),
                  pl.BlockSpec((COMBINE_TILE, TOP_K), lambda i: (i, 0)),
                  tok, _const_spec(ln_g.shape), _const_spec(ln_b.shape)],
        out_specs=tok,
        out_shape=jax.ShapeDtypeStruct((t, d), F32),
        compiler_params=_params(("arbitrary",)),
        name="combine",
    )(y_tok, wts_tok, res, ln_g, ln_b)


def _prepare(l, w_in, conv_w, a_log, dt_bias, gdn_norm_g, w_a, sgu_ln_g, sgu_ln_b, sgu_w, sgu_b, w_b,
             w_mem_kv, w_c, w_o, ln1_g, ln1_b, router_w, router_bias, ws_gate, ws_up, ws_down, ln2_g, ln2_b):
    wi = w_in[l]
    d = D_MODEL
    bf = lambda a: a.astype(BF16)
    row = lambda a: a.reshape(1, -1).astype(F32)
    w_ab = jnp.zeros((d, 128), F32).at[:, :2 * GDN_HEADS].set(wi[:, _C_AB:_C_UV])
    prm = jnp.zeros((8, 128), F32).at[0, :GDN_HEADS].set(a_log[l]).at[1, :GDN_HEADS].set(dt_bias[l])
    sgu_bias = jnp.repeat(sgu_b[l].T, SGU_WIDTH // SGU_GROUPS, axis=1)
    return {
        "w_qkv": bf(wi[:, _C_QKV:_C_Z]), "w_z": bf(wi[:, _C_Z:_C_AB]), "w_ab": bf(w_ab),
        "w_uv": bf(wi[:, _C_UV:_C_XQ]), "w_xq": bf(wi[:, _C_XQ:_C_GATE]),
        "w_g0": bf(wi[:, _C_GATE:_C_GATE + d]), "w_g1": bf(wi[:, _C_GATE + d:_C_GATE + 2 * d]),
        "w_g2": bf(wi[:, _C_GATE + 2 * d:_C_GATE + 3 * d]),
        "conv_w": conv_w[l].astype(F32), "gdn_prm": prm, "gdn_norm_g": row(gdn_norm_g[l]),
        "w_a": bf(w_a[l]), "sgu_ln_g": row(sgu_ln_g[l]), "sgu_ln_b": row(sgu_ln_b[l]),
        "sgu_w": sgu_w[l].astype(F32), "sgu_bias": sgu_bias.astype(F32), "w_b": bf(w_b[l]),
        "w_mem_kv": bf(w_mem_kv[l]), "w_c": bf(w_c[l]), "w_o": bf(w_o[l]),
        "ln1_g": row(ln1_g[l]), "ln1_b": row(ln1_b[l]),
        "router_wt": bf(router_w[l].T), "router_bias": router_bias[l].reshape(-1, 1).astype(F32),
        "ws_gu": bf(jnp.concatenate([ws_gate[l], ws_up[l]], axis=1)), "ws_down": bf(ws_down[l]),
        "ln2_g": row(ln2_g[l]), "ln2_b": row(ln2_b[l]),
    }


def _layer(x, mem, wts, w_gate, w_up, w_down):
    bsz, s, d = x.shape
    t = bsz * s
    w_gate, w_up, w_down = (_sc_pack_weights(w) for w in (w_gate, w_up, w_down))
    k, v = _mem_kv(mem, wts["w_mem_kv"])
    mbc = _mix_bc(x, wts, k, v)
    og = _gdn(x, wts)
    res, hp, idx, rw, rank, cnt = _merge(x, og, mbc, wts)
    nb = t * TOP_K // EXPERT_BLOCK + N_EXPERTS
    pos, win_slots, meta = _plan(idx, rank, cnt)
    xs = _sc_scatter_rows(hp, win_slots, nb * EXPERT_BLOCK)
    y = _experts(xs, meta[:, 0], meta[:, 1], meta[:, 2], meta[:1, 3], w_gate, w_up, w_down)
    y_tok = _sc_gather_rows(y, pos.reshape(-1)).reshape(TOP_K, t, d // 2)
    out = _combine(y_tok, rw.T, res.reshape(t, d), wts["ln2_g"], wts["ln2_b"])
    return out.reshape(bsz, s, d)


def kernel(x, mem, w_in, conv_w, a_log, dt_bias, gdn_norm_g, w_a, sgu_ln_g, sgu_ln_b, sgu_w, sgu_b, w_b,
           w_mem_kv, w_c, w_o, ln1_g, ln1_b, router_w, router_bias, w_gate, w_up, w_down, ws_gate, ws_up,
           ws_down, ln2_g, ln2_b):
    assert x.shape[1] % TILE == 0 and x.shape[2] == D_MODEL
    for l in range(DEPTH):
        wts = _prepare(l, w_in, conv_w, a_log, dt_bias, gdn_norm_g, w_a, sgu_ln_g, sgu_ln_b, sgu_w, sgu_b,
                       w_b, w_mem_kv, w_c, w_o, ln1_g, ln1_b, router_w, router_bias, ws_gate, ws_up,
                       ws_down, ln2_g, ln2_b)
        x = _layer(x, mem, wts, w_gate[l], w_up[l], w_down[l])
    return x
```

```python
import functools

import jax
import jax.numpy as jnp
from jax import lax
from jax.experimental import pallas as pl
from jax.experimental.pallas import tpu as pltpu
from jax.experimental.pallas import tpu_sc as plsc

F32 = jnp.float32
BF16 = jnp.bfloat16
I32 = jnp.int32
U32 = jnp.uint32

D_MODEL = 1024
DEPTH = 1
GDN_HEADS = 4
HEAD_DIM = 128
GDN_WIDTH = GDN_HEADS * HEAD_DIM
CONV_WIDTH = 4
SGU_GROUPS = 4
SGU_WIDTH = 512
SGU_CHUNK = 128
N_MEM = 256
XATTN_HEADS = 4
XATTN_WIDTH = 512
N_EXPERTS = 256
TOP_K = 8
N_GROUPS = 8
GROUP_SIZE = N_EXPERTS // N_GROUPS
TOPK_GROUPS = 4
D_EXPERT = 256
D_SHARED = 256
ROUTED_SCALE = 2.5
LN_EPS = 1e-5
RMS_EPS = 1e-6
DEEPNORM_ALPHA = (2.0 * DEPTH) ** 0.25

_C_QKV = 0
_C_Z = 3 * GDN_WIDTH
_C_AB = 4 * GDN_WIDTH
_C_UV = _C_AB + 2 * GDN_HEADS
_C_XQ = _C_UV + 2 * SGU_WIDTH
_C_GATE = _C_XQ + XATTN_WIDTH

CHUNK = 128
GDN_GROUP = 2
TILE = 512
HALO = 8
EXPERT_BLOCK = 288
X_SLOTS = 8
X_AHEAD = X_SLOTS - 2
Y_SLOTS = 4
W_SLOTS = 3
SC_CORES = 2
SC_SUBCORES = 16
SC_WINDOW = 64
COMBINE_TILE = 256
VMEM_LIMIT = 56 * 1024 * 1024


def _dot(a, b):
    return jnp.dot(a, b, preferred_element_type=F32)


def _dot_nt(a, b):
    return lax.dot_general(a, b, (((1,), (1,)), ((), ())), preferred_element_type=F32)


def _split(a):
    hi = a.astype(BF16)
    lo = (a - hi.astype(F32)).astype(BF16)
    return hi, lo


def _dot3_parts(ah, al, bh, bl):
    return _dot(jnp.concatenate([ah, ah, al], axis=1), jnp.concatenate([bh, bl, bh], axis=0))


def _dot3(a, b):
    return _dot3_parts(*_split(a), *_split(b))


def _softplus(x):
    return jnp.maximum(x, 0.0) + jnp.log1p(jnp.exp(-jnp.abs(x)))


def _layer_norm(x, g, b):
    mu = jnp.mean(x, axis=-1, keepdims=True)
    xc = x - mu
    var = jnp.mean(xc * xc, axis=-1, keepdims=True)
    return xc * lax.rsqrt(var + LN_EPS) * g + b


def _const_spec(shape):
    nd = len(shape)
    return pl.BlockSpec(shape, lambda *_: (0,) * nd)


def _params(semantics):
    return pltpu.CompilerParams(dimension_semantics=semantics, vmem_limit_bytes=VMEM_LIMIT)


def _mem_kv_kernel(mem_ref, w_ref, k_ref, v_ref):
    kv = _dot(mem_ref[0].astype(BF16), w_ref[...])
    k_ref[0] = kv[:, :XATTN_WIDTH].astype(BF16)
    v_ref[0] = kv[:, XATTN_WIDTH:].astype(BF16)


def _mem_kv(mem, w_kv):
    bsz, n_mem, d = mem.shape
    out = jax.ShapeDtypeStruct((bsz, n_mem, XATTN_WIDTH), BF16)
    return pl.pallas_call(
        _mem_kv_kernel,
        grid=(bsz,),
        in_specs=[pl.BlockSpec((1, n_mem, d), lambda b: (b, 0, 0)), _const_spec(w_kv.shape)],
        out_specs=[pl.BlockSpec((1, n_mem, XATTN_WIDTH), lambda b: (b, 0, 0))] * 2,
        out_shape=[out, out],
        compiler_params=_params(("arbitrary",)),
        name="mem_kv",
    )(mem, w_kv)


def _mix_bc_kernel(x_ref, wuv_ref, wg1_ref, wb_ref, lng_ref, lnb_ref, sw_ref, sb_ref,
                   wxq_ref, k_ref, v_ref, wg2_ref, wc_ref, o_ref):
    xb = x_ref[0].astype(BF16)
    uv = jax.nn.gelu(_dot(xb, wuv_ref[...]), approximate=True)
    u = uv[:, :SGU_WIDTH]
    v = _layer_norm(uv[:, SGU_WIDTH:], lng_ref[...], lnb_ref[...])
    ii = lax.broadcasted_iota(I32, (SGU_CHUNK, SGU_CHUNK), 0)
    jj = lax.broadcasted_iota(I32, (SGU_CHUNK, SGU_CHUNK), 1)
    rows = []
    for n in range(TILE // SGU_CHUNK):
        cols = []
        for g in range(SGU_GROUPS):
            wt = jnp.where(ii >= jj, sw_ref[g], 0.0).astype(BF16)
            vg = v[n * SGU_CHUNK:(n + 1) * SGU_CHUNK, g * 128:(g + 1) * 128].astype(BF16)
            cols.append(_dot(wt, vg))
        rows.append(jnp.concatenate(cols, axis=1) + sb_ref[...])
    sg = jnp.concatenate(rows, axis=0)
    br_b = _dot((u * sg).astype(BF16), wb_ref[...])
    acc = jax.nn.sigmoid(_dot(xb, wg1_ref[...])) * br_b
    xq = _dot(xb, wxq_ref[...])
    heads = []
    for h in range(XATTN_HEADS):
        qh = xq[:, h * 128:(h + 1) * 128].astype(BF16)
        kh = k_ref[0, :, h * 128:(h + 1) * 128]
        vh = v_ref[0, :, h * 128:(h + 1) * 128]
        sc = _dot_nt(qh, kh) * (128 ** -0.5)
        p = jnp.exp(sc - jnp.max(sc, axis=-1, keepdims=True))
        p = p / jnp.sum(p, axis=-1, keepdims=True)
        heads.append(_dot(p.astype(BF16), vh))
    att = jnp.concatenate(heads, axis=1).astype(BF16)
    br_c = _dot(att, wc_ref[...])
    o_ref[0] = acc + jax.nn.sigmoid(_dot(xb, wg2_ref[...])) * br_c


def _mix_bc(x, wts, k, v):
    bsz, s, d = x.shape
    tok = pl.BlockSpec((1, TILE, d), lambda b, i: (b, i, 0))
    kvs = pl.BlockSpec((1, N_MEM, XATTN_WIDTH), lambda b, i: (b, 0, 0))
    names = ("w_uv", "w_g1", "w_b", "sgu_ln_g", "sgu_ln_b", "sgu_w", "sgu_bias", "w_xq")
    consts = [wts[n] for n in names]
    tail = [wts["w_g2"], wts["w_c"]]
    return pl.pallas_call(
        _mix_bc_kernel,
        grid=(bsz, s // TILE),
        in_specs=[tok] + [_const_spec(c.shape) for c in consts] + [kvs, kvs]
        + [_const_spec(c.shape) for c in tail],
        out_specs=tok,
        out_shape=jax.ShapeDtypeStruct((bsz, s, d), F32),
        compiler_params=_params(("arbitrary", "arbitrary")),
        name="mix_bc",
    )(x, *consts, k, v, *tail)


def _unit_lower_inverses(lmats, ii, jj):
    eye = (ii == jj).astype(F32)
    x = ii ^ jj
    lsplit = [_split(l) for l in lmats]
    ts = [eye - jnp.where(x < 2, l, 0.0) for l in lmats]
    zero = jnp.zeros((CHUNK, CHUNK), BF16)
    s = 2
    while s < CHUNK:
        off = (x < 2 * s) & ((ii & s) != 0) & ((jj & s) == 0)
        tsplit = [_split(t) for t in ts]
        prods = []
        for (th, tl), (lh, ll) in zip(tsplit, lsplit):
            oh = jnp.where(off, lh, zero)
            ol = jnp.where(off, ll, zero)
            prods.append(_dot3_parts(th, tl, oh, ol))
        nxt = []
        for t, p, (th, tl) in zip(ts, prods, tsplit):
            ph, plo = _split(p)
            nxt.append(t - _dot3_parts(ph, plo, th, tl))
        ts = nxt
        s *= 2
    return ts


def _gdn_kernel(x_ref, wqkv_ref, wz_ref, wab_ref, convw_ref, prm_ref, gn_ref, o_ref, hist_ref, state_ref):
    @pl.when(pl.program_id(1) == 0)
    def _():
        hist_ref[0:HALO, :] = jnp.zeros((HALO, 3 * GDN_WIDTH), F32)
        state_ref[...] = jnp.zeros(state_ref.shape, F32)

    xb = x_ref[0].astype(BF16)
    z = _dot(xb, wz_ref[...])
    qkv = _dot(xb, wqkv_ref[...])
    hist_ref[HALO:HALO + TILE, :] = qkv
    conv = qkv * convw_ref[CONV_WIDTH - 1:CONV_WIDTH, :]
    for j in range(CONV_WIDTH - 1):
        shift = CONV_WIDTH - 1 - j
        conv = conv + hist_ref[HALO - shift:HALO - shift + TILE, :] * convw_ref[j:j + 1, :]
    hist_ref[0:HALO, :] = qkv[TILE - HALO:, :]
    qkv = conv * jax.nn.sigmoid(conv)

    ab = _dot(xb, wab_ref[...])
    g = -jnp.exp(prm_ref[0:1, :]) * _softplus(ab + prm_ref[1:2, :])
    beta = jax.nn.sigmoid(ab)

    ii = lax.broadcasted_iota(I32, (CHUNK, CHUNK), 0)
    jj = lax.broadcasted_iota(I32, (CHUNK, CHUNK), 1)
    tri = (ii >= jj).astype(BF16)
    def chunk_items(c):
        items = []
        r0 = c * CHUNK
        gch = g[r0:r0 + CHUNK, :]
        g1 = gch.astype(BF16)
        r1 = gch - g1.astype(F32)
        g2 = r1.astype(BF16)
        g3 = (r1 - g2.astype(F32)).astype(BF16)
        gc = _dot(jnp.concatenate([tri, tri, tri], axis=1),
                  jnp.concatenate([g1, g2, g3], axis=0))
        gct = gc.T
        for h in range(GDN_HEADS):
            qh = qkv[r0:r0 + CHUNK, h * HEAD_DIM:(h + 1) * HEAD_DIM]
            kh = qkv[r0:r0 + CHUNK, GDN_WIDTH + h * HEAD_DIM:GDN_WIDTH + (h + 1) * HEAD_DIM]
            vh = qkv[r0:r0 + CHUNK, 2 * GDN_WIDTH + h * HEAD_DIM:2 * GDN_WIDTH + (h + 1) * HEAD_DIM]
            qn = qh * lax.rsqrt(jnp.sum(qh * qh, axis=-1, keepdims=True) + RMS_EPS) * (HEAD_DIM ** -0.5)
            kn = kh * lax.rsqrt(jnp.sum(kh * kh, axis=-1, keepdims=True) + RMS_EPS)
            gcol = gc[:, h:h + 1]
            grow = gct[h:h + 1, :]
            bcol = beta[r0:r0 + CHUNK, GDN_HEADS + h:GDN_HEADS + h + 1]
            glast = gc[CHUNK - 1:CHUNK, h:h + 1]
            decay = jnp.exp(jnp.where(ii >= jj, gcol - grow, -jnp.inf))
            knb = kn.astype(BF16)
            kk = _dot_nt(knb, knb)
            qk = _dot_nt(qn.astype(BF16), knb)
            egc = jnp.exp(gcol)
            items.append(dict(
                c=c, h=h, lmat=jnp.where(ii > jj, kk * bcol * decay, 0.0),
                rhs=jnp.concatenate([vh * bcol, kn * (bcol * egc)], axis=1),
                qd=(qn * egc).astype(BF16), kdt=(kn * jnp.exp(glast - gcol)).T.astype(BF16),
                a=(qk * decay).astype(BF16), gl=jnp.exp(glast)))
        return items

    states = [state_ref[h] for h in range(GDN_HEADS)]
    for c0 in range(0, TILE // CHUNK, GDN_GROUP):
        items = [it for c in range(c0, c0 + GDN_GROUP) for it in chunk_items(c)]
        tinvs = _unit_lower_inverses([it["lmat"] for it in items], ii, jj)
        sols = [_dot3(tinv, it["rhs"]) for tinv, it in zip(tinvs, items)]
        for it, sol in zip(items, sols):
            h = it["h"]
            rows = slice(it["c"] * CHUNK, (it["c"] + 1) * CHUNK)
            lanes = slice(h * HEAD_DIM, (h + 1) * HEAD_DIM)
            st = states[h]
            wq = jnp.concatenate([sol[:, HEAD_DIM:].astype(BF16), it["qd"]], axis=0)
            ws = _dot(wq, st.astype(BF16))
            vb = (sol[:, :HEAD_DIM] - ws[:CHUNK]).astype(BF16)
            o = ws[CHUNK:] + _dot(it["a"], vb)
            states[h] = st * it["gl"] + _dot(it["kdt"], vb)
            zz = z[rows, lanes]
            o = o * lax.rsqrt(jnp.mean(o * o, axis=-1, keepdims=True) + RMS_EPS) * gn_ref[...]
            o_ref[0, rows, lanes] = (o * (zz * jax.nn.sigmoid(zz))).astype(BF16)
    for h in range(GDN_HEADS):
        state_ref[h] = states[h]


def _gdn(x, wts):
    bsz, s, d = x.shape
    tok = pl.BlockSpec((1, TILE, d), lambda b, i: (b, i, 0))
    hw = pl.BlockSpec((1, TILE, GDN_WIDTH), lambda b, i: (b, i, 0))
    consts = [wts[n] for n in ("w_qkv", "w_z", "w_ab", "conv_w", "gdn_prm", "gdn_norm_g")]
    return pl.pallas_call(
        _gdn_kernel,
        grid=(bsz, s // TILE),
        in_specs=[tok] + [_const_spec(c.shape) for c in consts],
        out_specs=hw,
        out_shape=jax.ShapeDtypeStruct((bsz, s, GDN_WIDTH), BF16),
        scratch_shapes=[pltpu.VMEM((HALO + TILE, 3 * GDN_WIDTH), F32),
                        pltpu.VMEM((GDN_HEADS, HEAD_DIM, HEAD_DIM), F32)],
        compiler_params=_params(("arbitrary", "arbitrary")),
        name="gdn",
    )(x, *consts)


def _merge_kernel(x_ref, og_ref, mbc_ref, wa_ref, wg0_ref, wo_ref, ln1g_ref, ln1b_ref,
                  rwt_ref, rb_ref, wsgu_ref, wsd_ref,
                  res_ref, hp_ref, idx_ref, wt_ref, rank_ref, cnt_ref, carry_ref):
    first = (pl.program_id(0) == 0) & (pl.program_id(1) == 0)

    @pl.when(first)
    def _():
        carry_ref[...] = jnp.zeros(carry_ref.shape, F32)

    x = x_ref[0]
    xb = x.astype(BF16)
    br_a = _dot(og_ref[0], wa_ref[...])
    merged = jax.nn.sigmoid(_dot(xb, wg0_ref[...])) * br_a + mbc_ref[0]
    y = _dot(merged.astype(BF16), wo_ref[...])
    h = _layer_norm(DEEPNORM_ALPHA * x + y, ln1g_ref[...], ln1b_ref[...])
    hb = h.astype(BF16)
    hp_ref[...] = _pack_bf16_pairs(h)

    gu = _dot(hb, wsgu_ref[...])
    gt = gu[:, :D_SHARED]
    act = gt * jax.nn.sigmoid(gt) * gu[:, D_SHARED:]
    res_ref[0] = DEEPNORM_ALPHA * h + _dot(act.astype(BF16), wsd_ref[...])

    scores = jax.nn.sigmoid(_dot_nt(rwt_ref[...], hb))
    biased = scores + rb_ref[...]
    g3 = biased.reshape(N_GROUPS, GROUP_SIZE, TILE)
    m1 = jnp.max(g3, axis=1)
    m1b = m1[:, None, :]
    n_top = jnp.sum((g3 == m1b).astype(F32), axis=1)
    m2 = jnp.max(jnp.where(g3 < m1b, g3, -jnp.inf), axis=1)
    gs = m1 + jnp.where(n_top >= 2.0, m1, m2)
    gidx = lax.broadcasted_iota(I32, (N_GROUPS, TILE), 0)
    beaten = jnp.zeros((N_GROUPS, TILE), F32)
    for g in range(N_GROUPS):
        row = gs[g:g + 1, :]
        beaten = beaten + ((row > gs) | ((row == gs) & (g < gidx))).astype(F32)
    sel = (beaten < float(TOPK_GROUPS)).astype(F32)
    sel_e = jnp.broadcast_to(sel[:, None, :], (N_GROUPS, GROUP_SIZE, TILE)).reshape(N_EXPERTS, TILE)
    masked = jnp.where(sel_e > 0.5, biased, -jnp.inf)
    eidx = lax.broadcasted_iota(I32, (N_EXPERTS, TILE), 0)
    chosen = jnp.zeros((N_EXPERTS, TILE), F32)
    picks, pick_scores = [], []
    for _k in range(TOP_K):
        mx = jnp.max(masked, axis=0, keepdims=True)
        pick = jnp.min(jnp.where(masked == mx, eidx, N_EXPERTS), axis=0, keepdims=True)
        hot = eidx == pick
        picks.append(pick)
        pick_scores.append(jnp.sum(jnp.where(hot, scores, 0.0), axis=0, keepdims=True))
        chosen = jnp.where(hot, 1.0, chosen)
        masked = jnp.where(hot, -jnp.inf, masked)
    total = pick_scores[0]
    for sck in pick_scores[1:]:
        total = total + sck
    idx_ref[...] = jnp.concatenate(picks, axis=0)
    wt_ref[...] = jnp.concatenate([sck / total * ROUTED_SCALE for sck in pick_scores], axis=0)

    ti = lax.broadcasted_iota(I32, (TILE, TILE), 0)
    tj = lax.broadcasted_iota(I32, (TILE, TILE), 1)
    before = _dot(chosen.astype(BF16), (ti < tj).astype(BF16))
    carry = carry_ref[...]
    before = before + jnp.concatenate([carry] * (TILE // 128), axis=1)
    rank_ref[...] = jnp.concatenate(
        [jnp.sum(jnp.where(eidx == p, before, 0.0), axis=0, keepdims=True) for p in picks],
        axis=0).astype(I32)
    carry = carry + _dot(chosen.astype(BF16), jnp.ones((TILE, 128), BF16))
    carry_ref[...] = carry
    cnt_ref[...] = carry.astype(I32)


def _merge(x, og, mbc, wts):
    bsz, s, d = x.shape
    t = bsz * s
    nt = s // TILE
    tok = pl.BlockSpec((1, TILE, d), lambda b, i: (b, i, 0))
    ogs = pl.BlockSpec((1, TILE, GDN_WIDTH), lambda b, i: (b, i, 0))
    flat = lambda b, i: (b * nt + i, 0)
    lane = lambda b, i: (0, b * nt + i)
    consts = [wts[n] for n in ("w_a", "w_g0", "w_o", "ln1_g", "ln1_b", "router_wt", "router_bias",
                               "ws_gu", "ws_down")]
    return pl.pallas_call(
        _merge_kernel,
        grid=(bsz, nt),
        in_specs=[tok, ogs, tok] + [_const_spec(c.shape) for c in consts],
        out_specs=[tok,
                   pl.BlockSpec((TILE, d // 2), flat),
                   pl.BlockSpec((TOP_K, TILE), lane),
                   pl.BlockSpec((TOP_K, TILE), lane),
                   pl.BlockSpec((TOP_K, TILE), lane),
                   _const_spec((N_EXPERTS, 128))],
        out_shape=[jax.ShapeDtypeStruct((bsz, s, d), F32),
                   jax.ShapeDtypeStruct((t, d // 2), U32),
                   jax.ShapeDtypeStruct((TOP_K, t), I32),
                   jax.ShapeDtypeStruct((TOP_K, t), F32),
                   jax.ShapeDtypeStruct((TOP_K, t), I32),
                   jax.ShapeDtypeStruct((N_EXPERTS, 128), I32)],
        scratch_shapes=[pltpu.VMEM((N_EXPERTS, 128), F32)],
        compiler_params=_params(("arbitrary", "arbitrary")),
        name="merge",
    )(x, og, mbc, *consts)


def _plan_kernel(idx_ref, rank_ref, cnt_ref, pos_ref, win_ref, meta_ref):
    e = N_EXPERTS
    cnt = cnt_ref[...]
    padded = cnt + (EXPERT_BLOCK - 1)
    nblk = jnp.floor(padded.astype(F32) * (1.0 / EXPERT_BLOCK)).astype(I32)
    nblk = jnp.where(nblk * EXPERT_BLOCK > padded, nblk - 1, nblk)
    nblk = jnp.where((nblk + 1) * EXPERT_BLOCK <= padded, nblk + 1, nblk)
    ei = lax.broadcasted_iota(I32, (e, e), 0)
    ej = lax.broadcasted_iota(I32, (e, e), 1)
    incl = _dot((ei >= ej).astype(BF16), nblk.astype(F32).astype(BF16)).astype(I32)
    excl = incl - nblk
    start = (excl * EXPERT_BLOCK).astype(F32)
    lane = lax.broadcasted_iota(I32, (e, 128), 1)
    n_used = jnp.broadcast_to(incl[e - 1:e, :], (e, 128))
    meta_ref[...] = jnp.where(lane == 0, excl, jnp.where(lane == 1, nblk, jnp.where(lane == 2, cnt, n_used)))

    e128 = lax.broadcasted_iota(I32, (e, 128), 0)

    def body(i, carry):
        off = pl.multiple_of(i * 128, 128)
        ids = idx_ref[:, pl.ds(off, 128)]
        rows = [jnp.sum(jnp.where(e128 == ids[k:k + 1, :], start, 0.0), axis=0, keepdims=True)
                for k in range(TOP_K)]
        p = jnp.concatenate(rows, axis=0).astype(I32) + rank_ref[:, pl.ds(off, 128)]
        pos_ref[:, pl.ds(off, 128)] = p
        for half in range(128 // SC_WINDOW):
            row0 = pl.multiple_of((i * (128 // SC_WINDOW) + half) * TOP_K, TOP_K)
            win_ref[pl.ds(row0, TOP_K), :] = p[:, half * SC_WINDOW:(half + 1) * SC_WINDOW]
        return carry

    lax.fori_loop(0, idx_ref.shape[1] // 128, body, 0)


def _plan(idx, rank, cnt):
    t = idx.shape[1]
    return pl.pallas_call(
        _plan_kernel,
        grid=(1,),
        in_specs=[_const_spec(idx.shape), _const_spec(rank.shape), _const_spec(cnt.shape)],
        out_specs=[_const_spec((TOP_K, t)), _const_spec((t // SC_WINDOW * TOP_K, SC_WINDOW)),
                   _const_spec((N_EXPERTS, 128))],
        out_shape=[jax.ShapeDtypeStruct((TOP_K, t), I32),
                   jax.ShapeDtypeStruct((t // SC_WINDOW * TOP_K, SC_WINDOW), I32),
                   jax.ShapeDtypeStruct((N_EXPERTS, 128), I32)],
        compiler_params=_params(("arbitrary",)),
        name="plan",
    )(idx, rank, cnt)


def _sc_scatter_rows(rows, win_slots, n_slots):
    t, d = rows.shape
    workers = SC_CORES * SC_SUBCORES
    per_w = t // workers
    nwin = per_w // SC_WINDOW
    assert per_w * workers == t and nwin * SC_WINDOW == per_w and nwin % 2 == 0
    idx = win_slots.reshape(workers, nwin * TOP_K, SC_WINDOW)
    mesh = plsc.VectorSubcoreMesh(core_axis_name="core", subcore_axis_name="subcore",
                                  num_cores=SC_CORES, num_subcores=SC_SUBCORES)

    @functools.partial(
        pl.kernel, out_type=jax.ShapeDtypeStruct((n_slots, d), rows.dtype), mesh=mesh, name="sc_scatter_rows",
        scratch_types=[pltpu.VMEM((nwin * TOP_K, SC_WINDOW), I32), pltpu.VMEM((2, SC_WINDOW, d), rows.dtype),
                       pltpu.SemaphoreType.DMA((2,)), pltpu.SemaphoreType.DMA((2,))])
    def scatter(rows_hbm, idx_hbm, out_hbm, idx_v, rows_v, lsem, ssem):
        wid = lax.axis_index("subcore") * SC_CORES + lax.axis_index("core")
        base = wid * per_w
        pltpu.sync_copy(idx_hbm.at[wid], idx_v)

        def load(w, slot):
            return pltpu.make_async_copy(rows_hbm.at[pl.ds(base + w * SC_WINDOW, SC_WINDOW)], rows_v.at[slot],
                                         lsem.at[slot])

        def send(w, k, slot):
            return pltpu.make_async_copy(rows_v.at[slot], out_hbm.at[idx_v.at[w * TOP_K + k]], ssem.at[slot])

        load(0, 0).start()

        @pl.loop(0, nwin, step=2)
        def _(w0):
            for b in range(2):
                w = w0 + b
                load(w, b).wait()

                @pl.when(w + 1 < nwin)
                def _():
                    @pl.when(w >= 1)
                    def _():
                        for k in range(TOP_K):
                            send(w - 1, k, 1 - b).wait()

                    load(w + 1, 1 - b).start()

                for k in range(TOP_K):
                    send(w, k, b).start()

        for k in range(TOP_K):
            send(nwin - 2, k, 0).wait()
        for k in range(TOP_K):
            send(nwin - 1, k, 1).wait()

    return scatter(rows, idx)


def _pack_bf16_pairs(a):
    n = a.shape[1] // 2
    bits = lax.bitcast_convert_type(a.astype(BF16).astype(F32), U32)
    return (bits[:, :n] >> 16) | (bits[:, n:] & jnp.uint32(0xFFFF0000))


def _unpack_bf16_pairs(words):
    lo = lax.bitcast_convert_type(words << 16, F32)
    hi = lax.bitcast_convert_type(words & jnp.uint32(0xFFFF0000), F32)
    return jnp.concatenate([lo, hi], axis=1)


def _experts_kernel(fb_ref, nb_ref, cnt_ref, nu_ref, xs_ref, wg_ref, wu_ref, wd_ref, y_ref,
                    wgbuf, wubuf, wdbuf, wsem, wgu_s, wd_s, xbuf, ybuf, xsem, ysem):
    e = pl.program_id(0)
    n_used = nu_ref[0]
    bm = EXPERT_BLOCK
    row_queue = 1

    def x_copy(g, slot):
        return pltpu.make_async_copy(xs_ref.at[pl.ds(g * bm, bm), :], xbuf.at[slot], xsem.at[slot])

    def y_copy(g, slot):
        return pltpu.make_async_copy(ybuf.at[slot], y_ref.at[pl.ds(g * bm, bm), :], ysem.at[slot])

    @pl.when(e == 0)
    def _():
        for g0 in range(X_AHEAD):
            @pl.when(g0 < n_used)
            def _():
                x_copy(g0, g0).start(priority=row_queue)

    def w_copies(j, slot):
        return [pltpu.make_async_copy(src.at[j], dst.at[slot], wsem.at[slot])
                for src, dst in ((wg_ref, wgbuf), (wu_ref, wubuf), (wd_ref, wdbuf))]

    @pl.when(e == 0)
    def _():
        for j in range(W_SLOTS - 1):
            for cp in w_copies(j, j):
                cp.start()

    wslot = lax.rem(e, W_SLOTS)
    for cp in w_copies(e, wslot):
        cp.wait()
    ahead_e = e + (W_SLOTS - 1)

    @pl.when(ahead_e < pl.num_programs(0))
    def _():
        for cp in w_copies(ahead_e, lax.rem(ahead_e, W_SLOTS)):
            cp.start()

    nb = nb_ref[e]

    @pl.when(nb > 0)
    def _():
        wgu_s[:, :D_EXPERT] = wgbuf[wslot].astype(BF16)
        wgu_s[:, D_EXPERT:] = wubuf[wslot].astype(BF16)
        wd_s[...] = wdbuf[wslot].astype(BF16)

    def process(b0, width):
        g0 = fb_ref[e] + b0
        xs = []
        for i in range(width):
            g = g0 + i
            x_copy(g, g & (X_SLOTS - 1)).wait()
            xs.append(xbuf[g & (X_SLOTS - 1)])
        for i in range(width):
            ahead = g0 + X_AHEAD + i

            @pl.when(ahead < n_used)
            def _():
                x_copy(ahead, ahead & (X_SLOTS - 1)).start(priority=row_queue)

        ys = []
        for i in range(width):
            x = _unpack_bf16_pairs(xs[i])
            live = lax.broadcasted_iota(I32, (bm, 1), 0) < cnt_ref[e] - (b0 + i) * bm
            xb = jnp.where(live, x, 0.0).astype(BF16)
            gu = _dot(xb, wgu_s[...])
            gt = gu[:, :D_EXPERT]
            act = (gt * jax.nn.sigmoid(gt) * gu[:, D_EXPERT:]).astype(BF16)
            ys.append(_pack_bf16_pairs(_dot(act, wd_s[...])))
        for i in range(width):
            g = g0 + i
            slot = g & (Y_SLOTS - 1)

            @pl.when(g >= Y_SLOTS)
            def _():
                y_copy(g - Y_SLOTS, slot).wait()

            ybuf[slot] = ys[i]
            y_copy(g, slot).start(priority=row_queue)

    def pair(p, carry):
        process(2 * p, 2)
        return carry

    lax.fori_loop(0, nb >> 1, pair, 0)

    @pl.when((nb & 1) == 1)
    def _():
        process(nb - 1, 1)

    @pl.when(e == pl.num_programs(0) - 1)
    def _():
        for back in range(Y_SLOTS, 0, -1):
            @pl.when(n_used >= back)
            def _():
                y_copy(n_used - back, (n_used - back) & (Y_SLOTS - 1)).wait()


def _experts(xs, first_block, n_blocks, counts, n_used, w_gate, w_up, w_down):
    n_slots, half = xs.shape
    d = 2 * half
    grid_spec = pltpu.PrefetchScalarGridSpec(
        num_scalar_prefetch=4,
        grid=(N_EXPERTS,),
        in_specs=[pl.BlockSpec(memory_space=pl.ANY)] * 4,
        out_specs=pl.BlockSpec(memory_space=pl.ANY),
        scratch_shapes=[pltpu.VMEM((W_SLOTS, d, D_EXPERT), F32), pltpu.VMEM((W_SLOTS, d, D_EXPERT), F32),
                        pltpu.VMEM((W_SLOTS, D_EXPERT, d), F32), pltpu.SemaphoreType.DMA((W_SLOTS,)),
                        pltpu.VMEM((d, 2 * D_EXPERT), BF16), pltpu.VMEM((D_EXPERT, d), BF16),
                        pltpu.VMEM((X_SLOTS, EXPERT_BLOCK, half), U32),
                        pltpu.VMEM((Y_SLOTS, EXPERT_BLOCK, half), U32),
                        pltpu.SemaphoreType.DMA((X_SLOTS,)), pltpu.SemaphoreType.DMA((Y_SLOTS,))],
    )
    return pl.pallas_call(
        _experts_kernel,
        grid_spec=grid_spec,
        out_shape=jax.ShapeDtypeStruct((n_slots, half), U32),
        compiler_params=_params(("arbitrary",)),
        name="experts",
    )(first_block, n_blocks, counts, n_used, xs, w_gate, w_up, w_down)


def _sc_gather_rows(table, indices):
    n = indices.shape[0]
    d = table.shape[1]
    workers = SC_CORES * SC_SUBCORES
    per_w = n // workers
    nch = per_w // SC_WINDOW
    assert per_w * workers == n and nch * SC_WINDOW == per_w and nch % 2 == 0
    mesh = plsc.VectorSubcoreMesh(core_axis_name="core", subcore_axis_name="subcore",
                                  num_cores=SC_CORES, num_subcores=SC_SUBCORES)

    @functools.partial(
        pl.kernel, out_type=jax.ShapeDtypeStruct((n, d), table.dtype), mesh=mesh, name="sc_gather_rows",
        scratch_types=[pltpu.VMEM((per_w,), I32), pltpu.VMEM((2, SC_WINDOW, d), table.dtype),
                       pltpu.SemaphoreType.DMA((2,)), pltpu.SemaphoreType.DMA((2,))])
    def gather(table_hbm, idx_hbm, out_hbm, idx_v, rows_v, gsem, wsem):
        base = (lax.axis_index("subcore") * SC_CORES + lax.axis_index("core")) * per_w
        pltpu.sync_copy(idx_hbm.at[pl.ds(base, per_w)], idx_v)

        def fetch(c, slot):
            return pltpu.make_async_copy(table_hbm.at[idx_v.at[pl.ds(c * SC_WINDOW, SC_WINDOW)]],
                                         rows_v.at[slot], gsem.at[slot])

        def put(c, slot):
            return pltpu.make_async_copy(rows_v.at[slot], out_hbm.at[pl.ds(base + c * SC_WINDOW, SC_WINDOW)],
                                         wsem.at[slot])

        fetch(0, 0).start()

        @pl.loop(0, nch, step=2)
        def _(c0):
            for b in range(2):
                c = c0 + b
                fetch(c, b).wait()

                @pl.when(c + 1 < nch)
                def _():
                    @pl.when(c >= 1)
                    def _():
                        put(c - 1, 1 - b).wait()

                    fetch(c + 1, 1 - b).start()

                put(c, b).start()

        put(nch - 2, 0).wait()
        put(nch - 1, 1).wait()

    return gather(table, indices)


def _combine_kernel(yt_ref, wt_ref, res_ref, g_ref, b_ref, o_ref):
    acc = res_ref[...]
    for k in range(TOP_K):
        acc = acc + _unpack_bf16_pairs(yt_ref[k]) * wt_ref[:, k:k + 1]
    o_ref[...] = _layer_norm(acc, g_ref[...], b_ref[...])


def _combine(y_tok, wts_tok, res, ln_g, ln_b):
    t, d = res.shape
    tok = pl.BlockSpec((COMBINE_TILE, d), lambda i: (i, 0))
    return pl.pallas_call(
        _combine_kernel,
        grid=(t // COMBINE_TILE,),
        in_specs=[pl.BlockSpec((TOP_K, COMBINE_TILE, d // 2), lambda i: (0, i, 0)),
                  pl.BlockSpec((COMBINE_TILE, TOP_K), lambda i: (i, 0)),
                  tok, _const_spec(ln_g.shape), _const_spec(ln_b.shape)],
        out_specs=tok,
        out_shape=jax.ShapeDtypeStruct((t, d), F32),
        compiler_params=_params(("arbitrary",)),
        name="combine",
    )(y_tok, wts_tok, res, ln_g, ln_b)


def _prepare(l, w_in, conv_w, a_log, dt_bias, gdn_norm_g, w_a, sgu_ln_g, sgu_ln_b, sgu_w, sgu_b, w_b,
             w_mem_kv, w_c, w_o, ln1_g, ln1_b, router_w, router_bias, ws_gate, ws_up, ws_down, ln2_g, ln2_b):
    wi = w_in[l]
    d = D_MODEL
    bf = lambda a: a.astype(BF16)
    row = lambda a: a.reshape(1, -1).astype(F32)
    w_ab = jnp.zeros((d, 128), F32).at[:, :2 * GDN_HEADS].set(wi[:, _C_AB:_C_UV])
    prm = jnp.zeros((8, 128), F32).at[0, :GDN_HEADS].set(a_log[l]).at[1, :GDN_HEADS].set(dt_bias[l])
    sgu_bias = jnp.repeat(sgu_b[l].T, SGU_WIDTH // SGU_GROUPS, axis=1)
    return {
        "w_qkv": bf(wi[:, _C_QKV:_C_Z]), "w_z": bf(wi[:, _C_Z:_C_AB]), "w_ab": bf(w_ab),
        "w_uv": bf(wi[:, _C_UV:_C_XQ]), "w_xq": bf(wi[:, _C_XQ:_C_GATE]),
        "w_g0": bf(wi[:, _C_GATE:_C_GATE + d]), "w_g1": bf(wi[:, _C_GATE + d:_C_GATE + 2 * d]),
        "w_g2": bf(wi[:, _C_GATE + 2 * d:_C_GATE + 3 * d]),
        "conv_w": conv_w[l].astype(F32), "gdn_prm": prm, "gdn_norm_g": row(gdn_norm_g[l]),
        "w_a": bf(w_a[l]), "sgu_ln_g": row(sgu_ln_g[l]), "sgu_ln_b": row(sgu_ln_b[l]),
        "sgu_w": sgu_w[l].astype(F32), "sgu_bias": sgu_bias.astype(F32), "w_b": bf(w_b[l]),
        "w_mem_kv": bf(w_mem_kv[l]), "w_c": bf(w_c[l]), "w_o": bf(w_o[l]),
        "ln1_g": row(ln1_g[l]), "ln1_b": row(ln1_b[l]),
        "router_wt": bf(router_w[l].T), "router_bias": router_bias[l].reshape(-1, 1).astype(F32),
        "ws_gu": bf(jnp.concatenate([ws_gate[l], ws_up[l]], axis=1)), "ws_down": bf(ws_down[l]),
        "ln2_g": row(ln2_g[l]), "ln2_b": row(ln2_b[l]),
    }


def _layer(x, mem, wts, w_gate, w_up, w_down):
    bsz, s, d = x.shape
    t = bsz * s
    k, v = _mem_kv(mem, wts["w_mem_kv"])
    mbc = _mix_bc(x, wts, k, v)
    og = _gdn(x, wts)
    res, hp, idx, rw, rank, cnt = _merge(x, og, mbc, wts)
    nb = t * TOP_K // EXPERT_BLOCK + N_EXPERTS
    pos, win_slots, meta = _plan(idx, rank, cnt)
    xs = _sc_scatter_rows(hp, win_slots, nb * EXPERT_BLOCK)
    y = _experts(xs, meta[:, 0], meta[:, 1], meta[:, 2], meta[:1, 3], w_gate, w_up, w_down)
    y_tok = _sc_gather_rows(y, pos.reshape(-1)).reshape(TOP_K, t, d // 2)
    out = _combine(y_tok, rw.T, res.reshape(t, d), wts["ln2_g"], wts["ln2_b"])
    return out.reshape(bsz, s, d)


def kernel(x, mem, w_in, conv_w, a_log, dt_bias, gdn_norm_g, w_a, sgu_ln_g, sgu_ln_b, sgu_w, sgu_b, w_b,
           w_mem_kv, w_c, w_o, ln1_g, ln1_b, router_w, router_bias, w_gate, w_up, w_down, ws_gate, ws_up,
           ws_down, ln2_g, ln2_b):
    assert x.shape[1] % TILE == 0 and x.shape[2] == D_MODEL
    for l in range(DEPTH):
        wts = _prepare(l, w_in, conv_w, a_log, dt_bias, gdn_norm_g, w_a, sgu_ln_g, sgu_ln_b, sgu_w, sgu_b,
                       w_b, w_mem_kv, w_c, w_o, ln1_g, ln1_b, router_w, router_bias, ws_gate, ws_up,
                       ws_down, ln2_g, ln2_b)
        x = _layer(x, mem, wts, w_gate[l], w_up[l], w_down[l])
    return x
```

```python
import functools

import jax
import jax.numpy as jnp
from jax import lax
from jax.experimental import pallas as pl
from jax.experimental.pallas import tpu as pltpu
from jax.experimental.pallas import tpu_sc as plsc

F32 = jnp.float32
BF16 = jnp.bfloat16
I32 = jnp.int32
U32 = jnp.uint32

D_MODEL = 1024
DEPTH = 1
GDN_HEADS = 4
HEAD_DIM = 128
GDN_WIDTH = GDN_HEADS * HEAD_DIM
CONV_WIDTH = 4
SGU_GROUPS = 4
SGU_WIDTH = 512
SGU_CHUNK = 128
N_MEM = 256
XATTN_HEADS = 4
XATTN_WIDTH = 512
N_EXPERTS = 256
TOP_K = 8
N_GROUPS = 8
GROUP_SIZE = N_EXPERTS // N_GROUPS
TOPK_GROUPS = 4
D_EXPERT = 256
D_SHARED = 256
ROUTED_SCALE = 2.5
LN_EPS = 1e-5
RMS_EPS = 1e-6
DEEPNORM_ALPHA = (2.0 * DEPTH) ** 0.25

_C_QKV = 0
_C_Z = 3 * GDN_WIDTH
_C_AB = 4 * GDN_WIDTH
_C_UV = _C_AB + 2 * GDN_HEADS
_C_XQ = _C_UV + 2 * SGU_WIDTH
_C_GATE = _C_XQ + XATTN_WIDTH

CHUNK = 128
GDN_GROUP = 2
TILE = 512
HALO = 8
EXPERT_BLOCK = 288
X_SLOTS = 8
X_AHEAD = X_SLOTS - 2
Y_SLOTS = 4
W_SLOTS = 3
SC_CORES = 2
SC_SUBCORES = 16
SC_WINDOW = 64
COMBINE_TILE = 256
VMEM_LIMIT = 56 * 1024 * 1024


def _dot(a, b):
    return jnp.dot(a, b, preferred_element_type=F32)


def _dot_nt(a, b):
    return lax.dot_general(a, b, (((1,), (1,)), ((), ())), preferred_element_type=F32)


def _split(a):
    hi = a.astype(BF16)
    lo = (a - hi.astype(F32)).astype(BF16)
    return hi, lo


def _dot3_parts(ah, al, bh, bl):
    return _dot(jnp.concatenate([ah, ah, al], axis=1), jnp.concatenate([bh, bl, bh], axis=0))


def _dot3(a, b):
    return _dot3_parts(*_split(a), *_split(b))


def _softplus(x):
    return jnp.maximum(x, 0.0) + jnp.log1p(jnp.exp(-jnp.abs(x)))


def _layer_norm(x, g, b):
    mu = jnp.mean(x, axis=-1, keepdims=True)
    xc = x - mu
    var = jnp.mean(xc * xc, axis=-1, keepdims=True)
    return xc * lax.rsqrt(var + LN_EPS) * g + b


def _const_spec(shape):
    nd = len(shape)
    return pl.BlockSpec(shape, lambda *_: (0,) * nd)


def _params(semantics):
    return pltpu.CompilerParams(dimension_semantics=semantics, vmem_limit_bytes=VMEM_LIMIT)


def _mem_kv_kernel(mem_ref, w_ref, k_ref, v_ref):
    kv = _dot(mem_ref[0].astype(BF16), w_ref[...])
    k_ref[0] = kv[:, :XATTN_WIDTH].astype(BF16)
    v_ref[0] = kv[:, XATTN_WIDTH:].astype(BF16)


def _mem_kv(mem, w_kv):
    bsz, n_mem, d = mem.shape
    out = jax.ShapeDtypeStruct((bsz, n_mem, XATTN_WIDTH), BF16)
    return pl.pallas_call(
        _mem_kv_kernel,
        grid=(bsz,),
        in_specs=[pl.BlockSpec((1, n_mem, d), lambda b: (b, 0, 0)), _const_spec(w_kv.shape)],
        out_specs=[pl.BlockSpec((1, n_mem, XATTN_WIDTH), lambda b: (b, 0, 0))] * 2,
        out_shape=[out, out],
        compiler_params=_params(("arbitrary",)),
        name="mem_kv",
    )(mem, w_kv)


def _mix_bc_kernel(x_ref, wuv_ref, wg1_ref, wb_ref, lng_ref, lnb_ref, sw_ref, sb_ref,
                   wxq_ref, k_ref, v_ref, wg2_ref, wc_ref, o_ref):
    xb = x_ref[0].astype(BF16)
    uv = jax.nn.gelu(_dot(xb, wuv_ref[...]), approximate=True)
    u = uv[:, :SGU_WIDTH]
    v = _layer_norm(uv[:, SGU_WIDTH:], lng_ref[...], lnb_ref[...])
    ii = lax.broadcasted_iota(I32, (SGU_CHUNK, SGU_CHUNK), 0)
    jj = lax.broadcasted_iota(I32, (SGU_CHUNK, SGU_CHUNK), 1)
    rows = []
    for n in range(TILE // SGU_CHUNK):
        cols = []
        for g in range(SGU_GROUPS):
            wt = jnp.where(ii >= jj, sw_ref[g], 0.0).astype(BF16)
            vg = v[n * SGU_CHUNK:(n + 1) * SGU_CHUNK, g * 128:(g + 1) * 128].astype(BF16)
            cols.append(_dot(wt, vg))
        rows.append(jnp.concatenate(cols, axis=1) + sb_ref[...])
    sg = jnp.concatenate(rows, axis=0)
    br_b = _dot((u * sg).astype(BF16), wb_ref[...])
    acc = jax.nn.sigmoid(_dot(xb, wg1_ref[...])) * br_b
    xq = _dot(xb, wxq_ref[...])
    heads = []
    for h in range(XATTN_HEADS):
        qh = xq[:, h * 128:(h + 1) * 128].astype(BF16)
        kh = k_ref[0, :, h * 128:(h + 1) * 128]
        vh = v_ref[0, :, h * 128:(h + 1) * 128]
        sc = _dot_nt(qh, kh) * (128 ** -0.5)
        p = jnp.exp(sc - jnp.max(sc, axis=-1, keepdims=True))
        p = p / jnp.sum(p, axis=-1, keepdims=True)
        heads.append(_dot(p.astype(BF16), vh))
    att = jnp.concatenate(heads, axis=1).astype(BF16)
    br_c = _dot(att, wc_ref[...])
    o_ref[0] = acc + jax.nn.sigmoid(_dot(xb, wg2_ref[...])) * br_c


def _mix_bc(x, wts, k, v):
    bsz, s, d = x.shape
    tok = pl.BlockSpec((1, TILE, d), lambda b, i: (b, i, 0))
    kvs = pl.BlockSpec((1, N_MEM, XATTN_WIDTH), lambda b, i: (b, 0, 0))
    names = ("w_uv", "w_g1", "w_b", "sgu_ln_g", "sgu_ln_b", "sgu_w", "sgu_bias", "w_xq")
    consts = [wts[n] for n in names]
    tail = [wts["w_g2"], wts["w_c"]]
    return pl.pallas_call(
        _mix_bc_kernel,
        grid=(bsz, s // TILE),
        in_specs=[tok] + [_const_spec(c.shape) for c in consts] + [kvs, kvs]
        + [_const_spec(c.shape) for c in tail],
        out_specs=tok,
        out_shape=jax.ShapeDtypeStruct((bsz, s, d), F32),
        compiler_params=_params(("arbitrary", "arbitrary")),
        name="mix_bc",
    )(x, *consts, k, v, *tail)


def _unit_lower_inverses(lmats, ii, jj):
    eye = (ii == jj).astype(F32)
    x = ii ^ jj
    lsplit = [_split(l) for l in lmats]
    ts = [eye - jnp.where(x < 2, l, 0.0) for l in lmats]
    zero = jnp.zeros((CHUNK, CHUNK), BF16)
    s = 2
    while s < CHUNK:
        off = (x < 2 * s) & ((ii & s) != 0) & ((jj & s) == 0)
        tsplit = [_split(t) for t in ts]
        prods = []
        for (th, tl), (lh, ll) in zip(tsplit, lsplit):
            oh = jnp.where(off, lh, zero)
            ol = jnp.where(off, ll, zero)
            prods.append(_dot3_parts(th, tl, oh, ol))
        nxt = []
        for t, p, (th, tl) in zip(ts, prods, tsplit):
            ph, plo = _split(p)
            nxt.append(t - _dot3_parts(ph, plo, th, tl))
        ts = nxt
        s *= 2
    return ts


def _gdn_kernel(x_ref, wqkv_ref, wz_ref, wab_ref, convw_ref, prm_ref, gn_ref, o_ref, hist_ref, state_ref):
    @pl.when(pl.program_id(1) == 0)
    def _():
        hist_ref[0:HALO, :] = jnp.zeros((HALO, 3 * GDN_WIDTH), F32)
        state_ref[...] = jnp.zeros(state_ref.shape, F32)

    xb = x_ref[0].astype(BF16)
    z = _dot(xb, wz_ref[...])
    qkv = _dot(xb, wqkv_ref[...])
    hist_ref[HALO:HALO + TILE, :] = qkv
    conv = qkv * convw_ref[CONV_WIDTH - 1:CONV_WIDTH, :]
    for j in range(CONV_WIDTH - 1):
        shift = CONV_WIDTH - 1 - j
        conv = conv + hist_ref[HALO - shift:HALO - shift + TILE, :] * convw_ref[j:j + 1, :]
    hist_ref[0:HALO, :] = qkv[TILE - HALO:, :]
    qkv = conv * jax.nn.sigmoid(conv)

    ab = _dot(xb, wab_ref[...])
    g = -jnp.exp(prm_ref[0:1, :]) * _softplus(ab + prm_ref[1:2, :])
    beta = jax.nn.sigmoid(ab)

    ii = lax.broadcasted_iota(I32, (CHUNK, CHUNK), 0)
    jj = lax.broadcasted_iota(I32, (CHUNK, CHUNK), 1)
    tri = (ii >= jj).astype(BF16)
    def chunk_items(c):
        items = []
        r0 = c * CHUNK
        gch = g[r0:r0 + CHUNK, :]
        g1 = gch.astype(BF16)
        r1 = gch - g1.astype(F32)
        g2 = r1.astype(BF16)
        g3 = (r1 - g2.astype(F32)).astype(BF16)
        gc = _dot(jnp.concatenate([tri, tri, tri], axis=1),
                  jnp.concatenate([g1, g2, g3], axis=0))
        gct = gc.T
        for h in range(GDN_HEADS):
            qh = qkv[r0:r0 + CHUNK, h * HEAD_DIM:(h + 1) * HEAD_DIM]
            kh = qkv[r0:r0 + CHUNK, GDN_WIDTH + h * HEAD_DIM:GDN_WIDTH + (h + 1) * HEAD_DIM]
            vh = qkv[r0:r0 + CHUNK, 2 * GDN_WIDTH + h * HEAD_DIM:2 * GDN_WIDTH + (h + 1) * HEAD_DIM]
            qn = qh * lax.rsqrt(jnp.sum(qh * qh, axis=-1, keepdims=True) + RMS_EPS) * (HEAD_DIM ** -0.5)
            kn = kh * lax.rsqrt(jnp.sum(kh * kh, axis=-1, keepdims=True) + RMS_EPS)
            gcol = gc[:, h:h + 1]
            grow = gct[h:h + 1, :]
            bcol = beta[r0:r0 + CHUNK, GDN_HEADS + h:GDN_HEADS + h + 1]
            glast = gc[CHUNK - 1:CHUNK, h:h + 1]
            decay = jnp.exp(jnp.where(ii >= jj, gcol - grow, -jnp.inf))
            knb = kn.astype(BF16)
            kk = _dot_nt(knb, knb)
            qk = _dot_nt(qn.astype(BF16), knb)
            egc = jnp.exp(gcol)
            items.append(dict(
                c=c, h=h, lmat=jnp.where(ii > jj, kk * bcol * decay, 0.0),
                rhs=jnp.concatenate([vh * bcol, kn * (bcol * egc)], axis=1),
                qd=(qn * egc).astype(BF16), kdt=(kn * jnp.exp(glast - gcol)).T.astype(BF16),
                a=(qk * decay).astype(BF16), gl=jnp.exp(glast)))
        return items

    states = [state_ref[h] for h in range(GDN_HEADS)]
    for c0 in range(0, TILE // CHUNK, GDN_GROUP):
        items = [it for c in range(c0, c0 + GDN_GROUP) for it in chunk_items(c)]
        tinvs = _unit_lower_inverses([it["lmat"] for it in items], ii, jj)
        sols = [_dot3(tinv, it["rhs"]) for tinv, it in zip(tinvs, items)]
        for it, sol in zip(items, sols):
            h = it["h"]
            rows = slice(it["c"] * CHUNK, (it["c"] + 1) * CHUNK)
            lanes = slice(h * HEAD_DIM, (h + 1) * HEAD_DIM)
            st = states[h]
            wq = jnp.concatenate([sol[:, HEAD_DIM:].astype(BF16), it["qd"]], axis=0)
            ws = _dot(wq, st.astype(BF16))
            vb = (sol[:, :HEAD_DIM] - ws[:CHUNK]).astype(BF16)
            o = ws[CHUNK:] + _dot(it["a"], vb)
            states[h] = st * it["gl"] + _dot(it["kdt"], vb)
            zz = z[rows, lanes]
            o = o * lax.rsqrt(jnp.mean(o * o, axis=-1, keepdims=True) + RMS_EPS) * gn_ref[...]
            o_ref[0, rows, lanes] = (o * (zz * jax.nn.sigmoid(zz))).astype(BF16)
    for h in range(GDN_HEADS):
        state_ref[h] = states[h]


def _gdn(x, wts):
    bsz, s, d = x.shape
    tok = pl.BlockSpec((1, TILE, d), lambda b, i: (b, i, 0))
    hw = pl.BlockSpec((1, TILE, GDN_WIDTH), lambda b, i: (b, i, 0))
    consts = [wts[n] for n in ("w_qkv", "w_z", "w_ab", "conv_w", "gdn_prm", "gdn_norm_g")]
    return pl.pallas_call(
        _gdn_kernel,
        grid=(bsz, s // TILE),
        in_specs=[tok] + [_const_spec(c.shape) for c in consts],
        out_specs=hw,
        out_shape=jax.ShapeDtypeStruct((bsz, s, GDN_WIDTH), BF16),
        scratch_shapes=[pltpu.VMEM((HALO + TILE, 3 * GDN_WIDTH), F32),
                        pltpu.VMEM((GDN_HEADS, HEAD_DIM, HEAD_DIM), F32)],
        compiler_params=_params(("arbitrary", "arbitrary")),
        name="gdn",
    )(x, *consts)


def _merge_kernel(x_ref, og_ref, mbc_ref, wa_ref, wg0_ref, wo_ref, ln1g_ref, ln1b_ref,
                  rwt_ref, rb_ref, wsgu_ref, wsd_ref,
                  res_ref, hp_ref, idx_ref, wt_ref, rank_ref, cnt_ref, carry_ref):
    first = (pl.program_id(0) == 0) & (pl.program_id(1) == 0)

    @pl.when(first)
    def _():
        carry_ref[...] = jnp.zeros(carry_ref.shape, F32)

    x = x_ref[0]
    xb = x.astype(BF16)
    br_a = _dot(og_ref[0], wa_ref[...])
    merged = jax.nn.sigmoid(_dot(xb, wg0_ref[...])) * br_a + mbc_ref[0]
    y = _dot(merged.astype(BF16), wo_ref[...])
    h = _layer_norm(DEEPNORM_ALPHA * x + y, ln1g_ref[...], ln1b_ref[...])
    hb = h.astype(BF16)
    hp_ref[...] = _pack_bf16_pairs(h)

    gu = _dot(hb, wsgu_ref[...])
    gt = gu[:, :D_SHARED]
    act = gt * jax.nn.sigmoid(gt) * gu[:, D_SHARED:]
    res_ref[0] = DEEPNORM_ALPHA * h + _dot(act.astype(BF16), wsd_ref[...])

    scores = jax.nn.sigmoid(_dot_nt(rwt_ref[...], hb))
    biased = scores + rb_ref[...]
    g3 = biased.reshape(N_GROUPS, GROUP_SIZE, TILE)
    m1 = jnp.max(g3, axis=1)
    m1b = m1[:, None, :]
    n_top = jnp.sum((g3 == m1b).astype(F32), axis=1)
    m2 = jnp.max(jnp.where(g3 < m1b, g3, -jnp.inf), axis=1)
    gs = m1 + jnp.where(n_top >= 2.0, m1, m2)
    gidx = lax.broadcasted_iota(I32, (N_GROUPS, TILE), 0)
    beaten = jnp.zeros((N_GROUPS, TILE), F32)
    for g in range(N_GROUPS):
        row = gs[g:g + 1, :]
        beaten = beaten + ((row > gs) | ((row == gs) & (g < gidx))).astype(F32)
    sel = (beaten < float(TOPK_GROUPS)).astype(F32)
    sel_e = jnp.broadcast_to(sel[:, None, :], (N_GROUPS, GROUP_SIZE, TILE)).reshape(N_EXPERTS, TILE)
    masked = jnp.where(sel_e > 0.5, biased, -jnp.inf)
    eidx = lax.broadcasted_iota(I32, (N_EXPERTS, TILE), 0)
    chosen = jnp.zeros((N_EXPERTS, TILE), F32)
    picks, pick_scores = [], []
    for _k in range(TOP_K):
        mx = jnp.max(masked, axis=0, keepdims=True)
        pick = jnp.min(jnp.where(masked == mx, eidx, N_EXPERTS), axis=0, keepdims=True)
        hot = eidx == pick
        picks.append(pick)
        pick_scores.append(jnp.sum(jnp.where(hot, scores, 0.0), axis=0, keepdims=True))
        chosen = jnp.where(hot, 1.0, chosen)
        masked = jnp.where(hot, -jnp.inf, masked)
    total = pick_scores[0]
    for sck in pick_scores[1:]:
        total = total + sck
    idx_ref[...] = jnp.concatenate(picks, axis=0)
    wt_ref[...] = jnp.concatenate([sck / total * ROUTED_SCALE for sck in pick_scores], axis=0)

    ti = lax.broadcasted_iota(I32, (TILE, TILE), 0)
    tj = lax.broadcasted_iota(I32, (TILE, TILE), 1)
    before = _dot(chosen.astype(BF16), (ti < tj).astype(BF16))
    carry = carry_ref[...]
    before = before + jnp.concatenate([carry] * (TILE // 128), axis=1)
    rank_ref[...] = jnp.concatenate(
        [jnp.sum(jnp.where(eidx == p, before, 0.0), axis=0, keepdims=True) for p in picks],
        axis=0).astype(I32)
    carry = carry + _dot(chosen.astype(BF16), jnp.ones((TILE, 128), BF16))
    carry_ref[...] = carry
    cnt_ref[...] = carry.astype(I32)


def _merge(x, og, mbc, wts):
    bsz, s, d = x.shape
    t = bsz * s
    nt = s // TILE
    tok = pl.BlockSpec((1, TILE, d), lambda b, i: (b, i, 0))
    ogs = pl.BlockSpec((1, TILE, GDN_WIDTH), lambda b, i: (b, i, 0))
    flat = lambda b, i: (b * nt + i, 0)
    lane = lambda b, i: (0, b * nt + i)
    consts = [wts[n] for n in ("w_a", "w_g0", "w_o", "ln1_g", "ln1_b", "router_wt", "router_bias",
                               "ws_gu", "ws_down")]
    return pl.pallas_call(
        _merge_kernel,
        grid=(bsz, nt),
        in_specs=[tok, ogs, tok] + [_const_spec(c.shape) for c in consts],
        out_specs=[tok,
                   pl.BlockSpec((TILE, d // 2), flat),
                   pl.BlockSpec((TOP_K, TILE), lane),
                   pl.BlockSpec((TOP_K, TILE), lane),
                   pl.BlockSpec((TOP_K, TILE), lane),
                   _const_spec((N_EXPERTS, 128))],
        out_shape=[jax.ShapeDtypeStruct((bsz, s, d), F32),
                   jax.ShapeDtypeStruct((t, d // 2), U32),
                   jax.ShapeDtypeStruct((TOP_K, t), I32),
                   jax.ShapeDtypeStruct((TOP_K, t), F32),
                   jax.ShapeDtypeStruct((TOP_K, t), I32),
                   jax.ShapeDtypeStruct((N_EXPERTS, 128), I32)],
        scratch_shapes=[pltpu.VMEM((N_EXPERTS, 128), F32)],
        compiler_params=_params(("arbitrary", "arbitrary")),
        name="merge",
    )(x, og, mbc, *consts)


def _plan_kernel(idx_ref, rank_ref, cnt_ref, pos_ref, win_ref, meta_ref):
    e = N_EXPERTS
    cnt = cnt_ref[...]
    padded = cnt + (EXPERT_BLOCK - 1)
    nblk = jnp.floor(padded.astype(F32) * (1.0 / EXPERT_BLOCK)).astype(I32)
    nblk = jnp.where(nblk * EXPERT_BLOCK > padded, nblk - 1, nblk)
    nblk = jnp.where((nblk + 1) * EXPERT_BLOCK <= padded, nblk + 1, nblk)
    ei = lax.broadcasted_iota(I32, (e, e), 0)
    ej = lax.broadcasted_iota(I32, (e, e), 1)
    incl = _dot((ei >= ej).astype(BF16), nblk.astype(F32).astype(BF16)).astype(I32)
    excl = incl - nblk
    start = (excl * EXPERT_BLOCK).astype(F32)
    lane = lax.broadcasted_iota(I32, (e, 128), 1)
    n_used = jnp.broadcast_to(incl[e - 1:e, :], (e, 128))
    meta_ref[...] = jnp.where(lane == 0, excl, jnp.where(lane == 1, nblk, jnp.where(lane == 2, cnt, n_used)))

    e128 = lax.broadcasted_iota(I32, (e, 128), 0)

    def body(i, carry):
        off = pl.multiple_of(i * 128, 128)
        ids = idx_ref[:, pl.ds(off, 128)]
        rows = [jnp.sum(jnp.where(e128 == ids[k:k + 1, :], start, 0.0), axis=0, keepdims=True)
                for k in range(TOP_K)]
        p = jnp.concatenate(rows, axis=0).astype(I32) + rank_ref[:, pl.ds(off, 128)]
        pos_ref[:, pl.ds(off, 128)] = p
        for half in range(128 // SC_WINDOW):
            row0 = pl.multiple_of((i * (128 // SC_WINDOW) + half) * TOP_K, TOP_K)
            win_ref[pl.ds(row0, TOP_K), :] = p[:, half * SC_WINDOW:(half + 1) * SC_WINDOW]
        return carry

    lax.fori_loop(0, idx_ref.shape[1] // 128, body, 0)


def _plan(idx, rank, cnt):
    t = idx.shape[1]
    return pl.pallas_call(
        _plan_kernel,
        grid=(1,),
        in_specs=[_const_spec(idx.shape), _const_spec(rank.shape), _const_spec(cnt.shape)],
        out_specs=[_const_spec((TOP_K, t)), _const_spec((t // SC_WINDOW * TOP_K, SC_WINDOW)),
                   _const_spec((N_EXPERTS, 128))],
        out_shape=[jax.ShapeDtypeStruct((TOP_K, t), I32),
                   jax.ShapeDtypeStruct((t // SC_WINDOW * TOP_K, SC_WINDOW), I32),
                   jax.ShapeDtypeStruct((N_EXPERTS, 128), I32)],
        compiler_params=_params(("arbitrary",)),
        name="plan",
    )(idx, rank, cnt)


def _sc_scatter_rows(rows, win_slots, n_slots):
    t, d = rows.shape
    workers = SC_CORES * SC_SUBCORES
    per_w = t // workers
    nwin = per_w // SC_WINDOW
    assert per_w * workers == t and nwin * SC_WINDOW == per_w and nwin % 2 == 0
    idx = win_slots.reshape(workers, nwin * TOP_K, SC_WINDOW)
    mesh = plsc.VectorSubcoreMesh(core_axis_name="core", subcore_axis_name="subcore",
                                  num_cores=SC_CORES, num_subcores=SC_SUBCORES)

    @functools.partial(
        pl.kernel, out_type=jax.ShapeDtypeStruct((n_slots, d), rows.dtype), mesh=mesh, name="sc_scatter_rows",
        scratch_types=[pltpu.VMEM((nwin * TOP_K, SC_WINDOW), I32), pltpu.VMEM((2, SC_WINDOW, d), rows.dtype),
                       pltpu.SemaphoreType.DMA((2,)), pltpu.SemaphoreType.DMA((2,))])
    def scatter(rows_hbm, idx_hbm, out_hbm, idx_v, rows_v, lsem, ssem):
        wid = lax.axis_index("subcore") * SC_CORES + lax.axis_index("core")
        base = wid * per_w
        pltpu.sync_copy(idx_hbm.at[wid], idx_v)

        def load(w, slot):
            return pltpu.make_async_copy(rows_hbm.at[pl.ds(base + w * SC_WINDOW, SC_WINDOW)], rows_v.at[slot],
                                         lsem.at[slot])

        def send(w, k, slot):
            return pltpu.make_async_copy(rows_v.at[slot], out_hbm.at[idx_v.at[w * TOP_K + k]], ssem.at[slot])

        load(0, 0).start()

        @pl.loop(0, nwin, step=2)
        def _(w0):
            for b in range(2):
                w = w0 + b
                load(w, b).wait()

                @pl.when(w + 1 < nwin)
                def _():
                    @pl.when(w >= 1)
                    def _():
                        for k in range(TOP_K):
                            send(w - 1, k, 1 - b).wait()

                    load(w + 1, 1 - b).start()

                for k in range(TOP_K):
                    send(w, k, b).start()

        for k in range(TOP_K):
            send(nwin - 2, k, 0).wait()
        for k in range(TOP_K):
            send(nwin - 1, k, 1).wait()

    return scatter(rows, idx)


def _pack_bf16_pairs(a):
    n = a.shape[1] // 2
    bits = lax.bitcast_convert_type(a.astype(BF16).astype(F32), U32)
    return (bits[:, :n] >> 16) | (bits[:, n:] & jnp.uint32(0xFFFF0000))


def _unpack_bf16_pairs(words):
    lo = lax.bitcast_convert_type(words << 16, F32)
    hi = lax.bitcast_convert_type(words & jnp.uint32(0xFFFF0000), F32)
    return jnp.concatenate([lo, hi], axis=1)


def _experts_kernel(*refs):
    def body(e, carry):
        _expert_step(e, *refs)
        return carry

    lax.fori_loop(0, N_EXPERTS, body, 0)


def _expert_step(e, fb_ref, nb_ref, cnt_ref, nu_ref, xs_ref, wg_ref, wu_ref, wd_ref, y_ref,
                 wgbuf, wubuf, wdbuf, wsem, wgu_s, wd_s, xbuf, ybuf, xsem, ysem):
    n_used = nu_ref[0]
    bm = EXPERT_BLOCK
    row_queue = 1

    def x_copy(g, slot):
        return pltpu.make_async_copy(xs_ref.at[pl.ds(g * bm, bm), :], xbuf.at[slot], xsem.at[slot])

    def y_copy(g, slot):
        return pltpu.make_async_copy(ybuf.at[slot], y_ref.at[pl.ds(g * bm, bm), :], ysem.at[slot])

    @pl.when(e == 0)
    def _():
        for g0 in range(X_AHEAD):
            @pl.when(g0 < n_used)
            def _():
                x_copy(g0, g0).start(priority=row_queue)

    def w_copies(j, slot):
        return [pltpu.make_async_copy(src.at[j], dst.at[slot], wsem.at[slot])
                for src, dst in ((wg_ref, wgbuf), (wu_ref, wubuf), (wd_ref, wdbuf))]

    @pl.when(e == 0)
    def _():
        for j in range(W_SLOTS - 1):
            for cp in w_copies(j, j):
                cp.start()

    wslot = lax.rem(e, W_SLOTS)
    for cp in w_copies(e, wslot):
        cp.wait()
    ahead_e = e + (W_SLOTS - 1)

    @pl.when(ahead_e < N_EXPERTS)
    def _():
        for cp in w_copies(ahead_e, lax.rem(ahead_e, W_SLOTS)):
            cp.start()

    nb = nb_ref[e]

    @pl.when(nb > 0)
    def _():
        wgu_s[:, :D_EXPERT] = wgbuf[wslot].astype(BF16)
        wgu_s[:, D_EXPERT:] = wubuf[wslot].astype(BF16)
        wd_s[...] = wdbuf[wslot].astype(BF16)

    def process(b0, width):
        g0 = fb_ref[e] + b0
        xs = []
        for i in range(width):
            g = g0 + i
            x_copy(g, g & (X_SLOTS - 1)).wait()
            xs.append(xbuf[g & (X_SLOTS - 1)])
        for i in range(width):
            ahead = g0 + X_AHEAD + i

            @pl.when(ahead < n_used)
            def _():
                x_copy(ahead, ahead & (X_SLOTS - 1)).start(priority=row_queue)

        ys = []
        for i in range(width):
            x = _unpack_bf16_pairs(xs[i])
            live = lax.broadcasted_iota(I32, (bm, 1), 0) < cnt_ref[e] - (b0 + i) * bm
            xb = jnp.where(live, x, 0.0).astype(BF16)
            gu = _dot(xb, wgu_s[...])
            gt = gu[:, :D_EXPERT]
            act = (gt * jax.nn.sigmoid(gt) * gu[:, D_EXPERT:]).astype(BF16)
            ys.append(_pack_bf16_pairs(_dot(act, wd_s[...])))
        for i in range(width):
            g = g0 + i
            slot = g & (Y_SLOTS - 1)

            @pl.when(g >= Y_SLOTS)
            def _():
                y_copy(g - Y_SLOTS, slot).wait()

            ybuf[slot] = ys[i]
            y_copy(g, slot).start(priority=row_queue)

    def pair(p, carry):
        process(2 * p, 2)
        return carry

    lax.fori_loop(0, nb >> 1, pair, 0)

    @pl.when((nb & 1) == 1)
    def _():
        process(nb - 1, 1)

    @pl.when(e == N_EXPERTS - 1)
    def _():
        for back in range(Y_SLOTS, 0, -1):
            @pl.when(n_used >= back)
            def _():
                y_copy(n_used - back, (n_used - back) & (Y_SLOTS - 1)).wait()


def _experts(xs, first_block, n_blocks, counts, n_used, w_gate, w_up, w_down):
    n_slots, half = xs.shape
    d = 2 * half
    grid_spec = pltpu.PrefetchScalarGridSpec(
        num_scalar_prefetch=4,
        grid=(1,),
        in_specs=[pl.BlockSpec(memory_space=pl.ANY)] * 4,
        out_specs=pl.BlockSpec(memory_space=pl.ANY),
        scratch_shapes=[pltpu.VMEM((W_SLOTS, d, D_EXPERT), F32), pltpu.VMEM((W_SLOTS, d, D_EXPERT), F32),
                        pltpu.VMEM((W_SLOTS, D_EXPERT, d), F32), pltpu.SemaphoreType.DMA((W_SLOTS,)),
                        pltpu.VMEM((d, 2 * D_EXPERT), BF16), pltpu.VMEM((D_EXPERT, d), BF16),
                        pltpu.VMEM((X_SLOTS, EXPERT_BLOCK, half), U32),
                        pltpu.VMEM((Y_SLOTS, EXPERT_BLOCK, half), U32),
                        pltpu.SemaphoreType.DMA((X_SLOTS,)), pltpu.SemaphoreType.DMA((Y_SLOTS,))],
    )
    return pl.pallas_call(
        _experts_kernel,
        grid_spec=grid_spec,
        out_shape=jax.ShapeDtypeStruct((n_slots, half), U32),
        compiler_params=_params(("arbitrary",)),
        name="experts",
    )(first_block, n_blocks, counts, n_used, xs, w_gate, w_up, w_down)


def _sc_gather_rows(table, indices):
    n = indices.shape[0]
    d = table.shape[1]
    workers = SC_CORES * SC_SUBCORES
    per_w = n // workers
    nch = per_w // SC_WINDOW
    assert per_w * workers == n and nch * SC_WINDOW == per_w and nch % 2 == 0
    mesh = plsc.VectorSubcoreMesh(core_axis_name="core", subcore_axis_name="subcore",
                                  num_cores=SC_CORES, num_subcores=SC_SUBCORES)

    @functools.partial(
        pl.kernel, out_type=jax.ShapeDtypeStruct((n, d), table.dtype), mesh=mesh, name="sc_gather_rows",
        scratch_types=[pltpu.VMEM((per_w,), I32), pltpu.VMEM((2, SC_WINDOW, d), table.dtype),
                       pltpu.SemaphoreType.DMA((2,)), pltpu.SemaphoreType.DMA((2,))])
    def gather(table_hbm, idx_hbm, out_hbm, idx_v, rows_v, gsem, wsem):
        base = (lax.axis_index("subcore") * SC_CORES + lax.axis_index("core")) * per_w
        pltpu.sync_copy(idx_hbm.at[pl.ds(base, per_w)], idx_v)

        def fetch(c, slot):
            return pltpu.make_async_copy(table_hbm.at[idx_v.at[pl.ds(c * SC_WINDOW, SC_WINDOW)]],
                                         rows_v.at[slot], gsem.at[slot])

        def put(c, slot):
            return pltpu.make_async_copy(rows_v.at[slot], out_hbm.at[pl.ds(base + c * SC_WINDOW, SC_WINDOW)],
                                         wsem.at[slot])

        fetch(0, 0).start()

        @pl.loop(0, nch, step=2)
        def _(c0):
            for b in range(2):
                c = c0 + b
                fetch(c, b).wait()

                @pl.when(c + 1 < nch)
                def _():
                    @pl.when(c >= 1)
                    def _():
                        put(c - 1, 1 - b).wait()

                    fetch(c + 1, 1 - b).start()

                put(c, b).start()

        put(nch - 2, 0).wait()
        put(nch - 1, 1).wait()

    return gather(table, indices)


def _combine_kernel(yt_ref, wt_ref, res_ref, g_ref, b_ref, o_ref):
    acc = res_ref[...]
    for k in range(TOP_K):
        acc = acc + _unpack_bf16_pairs(yt_ref[k]) * wt_ref[:, k:k + 1]
    o_ref[...] = _layer_norm(acc, g_ref[...], b_ref[...])


def _combine(y_tok, wts_tok, res, ln_g, ln_b):
    t, d = res.shape
    tok = pl.BlockSpec((COMBINE_TILE, d), lambda i: (i, 0))
    return pl.pallas_call(
        _combine_kernel,
        grid=(t // COMBINE_TILE,),
        in_specs=[pl.BlockSpec((TOP_K, COMBINE_TILE, d // 2), lambda i: (0, i, 0)),
                  pl.BlockSpec((COMBINE_TILE, TOP_K), lambda i: (i, 0)),
                  tok, _const_spec(ln_g.shape), _const_spec(ln_b.shape)],
        out_specs=tok,
        out_shape=jax.ShapeDtypeStruct((t, d), F32),
        compiler_params=_params(("arbitrary",)),
        name="combine",
    )(y_tok, wts_tok, res, ln_g, ln_b)


def _prepare(l, w_in, conv_w, a_log, dt_bias, gdn_norm_g, w_a, sgu_ln_g, sgu_ln_b, sgu_w, sgu_b, w_b,
             w_mem_kv, w_c, w_o, ln1_g, ln1_b, router_w, router_bias, ws_gate, ws_up, ws_down, ln2_g, ln2_b):
    wi = w_in[l]
    d = D_MODEL
    bf = lambda a: a.astype(BF16)
    row = lambda a: a.reshape(1, -1).astype(F32)
    w_ab = jnp.zeros((d, 128), F32).at[:, :2 * GDN_HEADS].set(wi[:, _C_AB:_C_UV])
    prm = jnp.zeros((8, 128), F32).at[0, :GDN_HEADS].set(a_log[l]).at[1, :GDN_HEADS].set(dt_bias[l])
    sgu_bias = jnp.repeat(sgu_b[l].T, SGU_WIDTH // SGU_GROUPS, axis=1)
    return {
        "w_qkv": bf(wi[:, _C_QKV:_C_Z]), "w_z": bf(wi[:, _C_Z:_C_AB]), "w_ab": bf(w_ab),
        "w_uv": bf(wi[:, _C_UV:_C_XQ]), "w_xq": bf(wi[:, _C_XQ:_C_GATE]),
        "w_g0": bf(wi[:, _C_GATE:_C_GATE + d]), "w_g1": bf(wi[:, _C_GATE + d:_C_GATE + 2 * d]),
        "w_g2": bf(wi[:, _C_GATE + 2 * d:_C_GATE + 3 * d]),
        "conv_w": conv_w[l].astype(F32), "gdn_prm": prm, "gdn_norm_g": row(gdn_norm_g[l]),
        "w_a": bf(w_a[l]), "sgu_ln_g": row(sgu_ln_g[l]), "sgu_ln_b": row(sgu_ln_b[l]),
        "sgu_w": sgu_w[l].astype(F32), "sgu_bias": sgu_bias.astype(F32), "w_b": bf(w_b[l]),
        "w_mem_kv": bf(w_mem_kv[l]), "w_c": bf(w_c[l]), "w_o": bf(w_o[l]),
        "ln1_g": row(ln1_g[l]), "ln1_b": row(ln1_b[l]),
        "router_wt": bf(router_w[l].T), "router_bias": router_bias[l].reshape(-1, 1).astype(F32),
        "ws_gu": bf(jnp.concatenate([ws_gate[l], ws_up[l]], axis=1)), "ws_down": bf(ws_down[l]),
        "ln2_g": row(ln2_g[l]), "ln2_b": row(ln2_b[l]),
    }


def _layer(x, mem, wts, w_gate, w_up, w_down):
    bsz, s, d = x.shape
    t = bsz * s
    k, v = _mem_kv(mem, wts["w_mem_kv"])
    mbc = _mix_bc(x, wts, k, v)
    og = _gdn(x, wts)
    res, hp, idx, rw, rank, cnt = _merge(x, og, mbc, wts)
    nb = t * TOP_K // EXPERT_BLOCK + N_EXPERTS
    pos, win_slots, meta = _plan(idx, rank, cnt)
    xs = _sc_scatter_rows(hp, win_slots, nb * EXPERT_BLOCK)
    y = _experts(xs, meta[:, 0], meta[:, 1], meta[:, 2], meta[:1, 3], w_gate, w_up, w_down)
    y_tok = _sc_gather_rows(y, pos.reshape(-1)).reshape(TOP_K, t, d // 2)
    out = _combine(y_tok, rw.T, res.reshape(t, d), wts["ln2_g"], wts["ln2_b"])
    return out.reshape(bsz, s, d)


def kernel(x, mem, w_in, conv_w, a_log, dt_bias, gdn_norm_g, w_a, sgu_ln_g, sgu_ln_b, sgu_w, sgu_b, w_b,
           w_mem_kv, w_c, w_o, ln1_g, ln1_b, router_w, router_bias, w_gate, w_up, w_down, ws_gate, ws_up,
           ws_down, ln2_g, ln2_b):
    assert x.shape[1] % TILE == 0 and x.shape[2] == D_MODEL
    for l in range(DEPTH):
        wts = _prepare(l, w_in, conv_w, a_log, dt_bias, gdn_norm_g, w_a, sgu_ln_g, sgu_ln_b, sgu_w, sgu_b,
                       w_b, w_mem_kv, w_c, w_o, ln1_g, ln1_b, router_w, router_bias, ws_gate, ws_up,
                       ws_down, ln2_g, ln2_b)
        x = _layer(x, mem, wts, w_gate[l], w_up[l], w_down[l])
    return x
```

```python
import functools

import jax
import jax.numpy as jnp
from jax import lax
from jax.experimental import pallas as pl
from jax.experimental.pallas import tpu as pltpu
from jax.experimental.pallas import tpu_sc as plsc

F32 = jnp.float32
BF16 = jnp.bfloat16
I32 = jnp.int32
U32 = jnp.uint32

D_MODEL = 1024
DEPTH = 1
GDN_HEADS = 4
HEAD_DIM = 128
GDN_WIDTH = GDN_HEADS * HEAD_DIM
CONV_WIDTH = 4
SGU_GROUPS = 4
SGU_WIDTH = 512
SGU_CHUNK = 128
SGU_GROUP_DIM = SGU_WIDTH // SGU_GROUPS
N_MEM = 256
XATTN_HEADS = 4
XATTN_WIDTH = 512
XATTN_HEAD_DIM = XATTN_WIDTH // XATTN_HEADS
N_EXPERTS = 256
TOP_K = 8
N_GROUPS = 8
GROUP_SIZE = N_EXPERTS // N_GROUPS
TOPK_GROUPS = 4
D_EXPERT = 256
D_SHARED = 256
ROUTED_SCALE = 2.5
LN_EPS = 1e-5
RMS_EPS = 1e-6
DEEPNORM_ALPHA = (2.0 * DEPTH) ** 0.25

_C_QKV = 0
_C_Z = 3 * GDN_WIDTH
_C_AB = 4 * GDN_WIDTH
_C_UV = _C_AB + 2 * GDN_HEADS
_C_XQ = _C_UV + 2 * SGU_WIDTH
_C_GATE = _C_XQ + XATTN_WIDTH

LANES = 128
SUBLANES = 8
CHUNK = 128
GDN_GROUP = 2
TILE = 512
HALO = SUBLANES
EXPERT_BLOCK = 288
X_SLOTS = 8
X_AHEAD = X_SLOTS - 2
Y_SLOTS = 4
W_SLOTS = 3
SC_CORES = 2
SC_SUBCORES = 16
SC_WINDOW = 64
COMBINE_TILE = 256
VMEM_LIMIT = 56 * 1024 * 1024


def _dot(a, b):
    return jnp.dot(a, b, preferred_element_type=F32)


def _dot_nt(a, b):
    return lax.dot_general(a, b, (((1,), (1,)), ((), ())), preferred_element_type=F32)


def _split(a):
    hi = a.astype(BF16)
    lo = (a - hi.astype(F32)).astype(BF16)
    return hi, lo


def _dot3_parts(ah, al, bh, bl):
    return _dot(jnp.concatenate([ah, ah, al], axis=1), jnp.concatenate([bh, bl, bh], axis=0))


def _dot3(a, b):
    return _dot3_parts(*_split(a), *_split(b))


def _softplus(x):
    return jnp.maximum(x, 0.0) + jnp.log1p(jnp.exp(-jnp.abs(x)))


def _layer_norm(x, g, b):
    mu = jnp.mean(x, axis=-1, keepdims=True)
    xc = x - mu
    var = jnp.mean(xc * xc, axis=-1, keepdims=True)
    return xc * lax.rsqrt(var + LN_EPS) * g + b


def _const_spec(shape):
    nd = len(shape)
    return pl.BlockSpec(shape, lambda *_: (0,) * nd)


def _params(semantics):
    return pltpu.CompilerParams(dimension_semantics=semantics, vmem_limit_bytes=VMEM_LIMIT)


def _mem_kv_kernel(mem_ref, w_ref, k_ref, v_ref):
    kv = _dot(mem_ref[0].astype(BF16), w_ref[...])
    k_ref[0] = kv[:, :XATTN_WIDTH].astype(BF16)
    v_ref[0] = kv[:, XATTN_WIDTH:].astype(BF16)


def _mem_kv(mem, w_kv):
    bsz, n_mem, d = mem.shape
    out = jax.ShapeDtypeStruct((bsz, n_mem, XATTN_WIDTH), BF16)
    return pl.pallas_call(
        _mem_kv_kernel,
        grid=(bsz,),
        in_specs=[pl.BlockSpec((1, n_mem, d), lambda b: (b, 0, 0)), _const_spec(w_kv.shape)],
        out_specs=[pl.BlockSpec((1, n_mem, XATTN_WIDTH), lambda b: (b, 0, 0))] * 2,
        out_shape=[out, out],
        compiler_params=_params(("arbitrary",)),
        name="mem_kv",
    )(mem, w_kv)


def _mix_bc_kernel(x_ref, wuv_ref, wg1_ref, wb_ref, lng_ref, lnb_ref, sw_ref, sb_ref,
                   wxq_ref, k_ref, v_ref, wg2_ref, wc_ref, o_ref):
    xb = x_ref[0].astype(BF16)
    uv = _dot(xb, wuv_ref[...])
    xq = _dot(xb, wxq_ref[...])
    gate_b = _dot(xb, wg1_ref[...])
    gate_c = _dot(xb, wg2_ref[...])
    scores = []
    for h in range(XATTN_HEADS):
        qh = xq[:, h * XATTN_HEAD_DIM:(h + 1) * XATTN_HEAD_DIM].astype(BF16)
        kh = k_ref[0, :, h * XATTN_HEAD_DIM:(h + 1) * XATTN_HEAD_DIM]
        scores.append(_dot_nt(qh, kh) * (XATTN_HEAD_DIM ** -0.5))
    uv = jax.nn.gelu(uv, approximate=True)
    u = uv[:, :SGU_WIDTH]
    v = _layer_norm(uv[:, SGU_WIDTH:], lng_ref[...], lnb_ref[...])
    ii = lax.broadcasted_iota(I32, (SGU_CHUNK, SGU_CHUNK), 0)
    jj = lax.broadcasted_iota(I32, (SGU_CHUNK, SGU_CHUNK), 1)
    rows = []
    for n in range(TILE // SGU_CHUNK):
        cols = []
        for g in range(SGU_GROUPS):
            wt = jnp.where(ii >= jj, sw_ref[g], 0.0).astype(BF16)
            vg = v[n * SGU_CHUNK:(n + 1) * SGU_CHUNK, g * SGU_GROUP_DIM:(g + 1) * SGU_GROUP_DIM].astype(BF16)
            cols.append(_dot(wt, vg))
        rows.append(jnp.concatenate(cols, axis=1) + sb_ref[...])
    sg = jnp.concatenate(rows, axis=0)
    br_b = _dot((u * sg).astype(BF16), wb_ref[...])
    acc = jax.nn.sigmoid(gate_b) * br_b
    heads = []
    for h, sc in enumerate(scores):
        vh = v_ref[0, :, h * XATTN_HEAD_DIM:(h + 1) * XATTN_HEAD_DIM]
        p = jnp.exp(sc - jnp.max(sc, axis=-1, keepdims=True))
        p = p / jnp.sum(p, axis=-1, keepdims=True)
        heads.append(_dot(p.astype(BF16), vh))
    att = jnp.concatenate(heads, axis=1).astype(BF16)
    br_c = _dot(att, wc_ref[...])
    o_ref[0] = acc + jax.nn.sigmoid(gate_c) * br_c


def _mix_bc(x, wts, k, v):
    bsz, s, d = x.shape
    tok = pl.BlockSpec((1, TILE, d), lambda b, i: (b, i, 0))
    kvs = pl.BlockSpec((1, N_MEM, XATTN_WIDTH), lambda b, i: (b, 0, 0))
    names = ("w_uv", "w_g1", "w_b", "sgu_ln_g", "sgu_ln_b", "sgu_w", "sgu_bias", "w_xq")
    consts = [wts[n] for n in names]
    tail = [wts["w_g2"], wts["w_c"]]
    return pl.pallas_call(
        _mix_bc_kernel,
        grid=(bsz, s // TILE),
        in_specs=[tok] + [_const_spec(c.shape) for c in consts] + [kvs, kvs]
        + [_const_spec(c.shape) for c in tail],
        out_specs=tok,
        out_shape=jax.ShapeDtypeStruct((bsz, s, d), F32),
        compiler_params=_params(("arbitrary", "arbitrary")),
        name="mix_bc",
    )(x, *consts, k, v, *tail)


def _unit_lower_inverses(lmats, ii, jj):
    eye = (ii == jj).astype(F32)
    x = ii ^ jj
    lsplit = [_split(l) for l in lmats]
    ts = [eye - jnp.where(x < 2, l, 0.0) for l in lmats]
    zero = jnp.zeros((CHUNK, CHUNK), BF16)
    s = 2
    while s < CHUNK:
        off = (x < 2 * s) & ((ii & s) != 0) & ((jj & s) == 0)
        tsplit = [_split(t) for t in ts]
        prods = []
        for (th, tl), (lh, ll) in zip(tsplit, lsplit):
            oh = jnp.where(off, lh, zero)
            ol = jnp.where(off, ll, zero)
            prods.append(_dot3_parts(th, tl, oh, ol))
        nxt = []
        for t, p, (th, tl) in zip(ts, prods, tsplit):
            ph, plo = _split(p)
            nxt.append(t - _dot3_parts(ph, plo, th, tl))
        ts = nxt
        s *= 2
    return ts


def _gdn_kernel(x_ref, wqkv_ref, wz_ref, wab_ref, convw_ref, prm_ref, gn_ref, o_ref, hist_ref, state_ref):
    @pl.when(pl.program_id(1) == 0)
    def _():
        hist_ref[0:HALO, :] = jnp.zeros((HALO, 3 * GDN_WIDTH), F32)
        state_ref[...] = jnp.zeros(state_ref.shape, F32)

    xb = x_ref[0].astype(BF16)
    z = _dot(xb, wz_ref[...])
    qkv = _dot(xb, wqkv_ref[...])
    hist_ref[HALO:HALO + TILE, :] = qkv
    conv = qkv * convw_ref[CONV_WIDTH - 1:CONV_WIDTH, :]
    for j in range(CONV_WIDTH - 1):
        shift = CONV_WIDTH - 1 - j
        conv = conv + hist_ref[HALO - shift:HALO - shift + TILE, :] * convw_ref[j:j + 1, :]
    hist_ref[0:HALO, :] = qkv[TILE - HALO:, :]
    qkv = conv * jax.nn.sigmoid(conv)

    ab = _dot(xb, wab_ref[...])
    g = -jnp.exp(prm_ref[0:1, :]) * _softplus(ab + prm_ref[1:2, :])
    beta = jax.nn.sigmoid(ab)

    ii = lax.broadcasted_iota(I32, (CHUNK, CHUNK), 0)
    jj = lax.broadcasted_iota(I32, (CHUNK, CHUNK), 1)
    tri = (ii >= jj).astype(BF16)
    def chunk_items(c):
        items = []
        r0 = c * CHUNK
        gch = g[r0:r0 + CHUNK, :]
        g1 = gch.astype(BF16)
        r1 = gch - g1.astype(F32)
        g2 = r1.astype(BF16)
        g3 = (r1 - g2.astype(F32)).astype(BF16)
        gc = _dot(jnp.concatenate([tri, tri, tri], axis=1),
                  jnp.concatenate([g1, g2, g3], axis=0))
        gct = gc.T
        for h in range(GDN_HEADS):
            qh = qkv[r0:r0 + CHUNK, h * HEAD_DIM:(h + 1) * HEAD_DIM]
            kh = qkv[r0:r0 + CHUNK, GDN_WIDTH + h * HEAD_DIM:GDN_WIDTH + (h + 1) * HEAD_DIM]
            vh = qkv[r0:r0 + CHUNK, 2 * GDN_WIDTH + h * HEAD_DIM:2 * GDN_WIDTH + (h + 1) * HEAD_DIM]
            qn = qh * lax.rsqrt(jnp.sum(qh * qh, axis=-1, keepdims=True) + RMS_EPS) * (HEAD_DIM ** -0.5)
            kn = kh * lax.rsqrt(jnp.sum(kh * kh, axis=-1, keepdims=True) + RMS_EPS)
            gcol = gc[:, h:h + 1]
            grow = gct[h:h + 1, :]
            bcol = beta[r0:r0 + CHUNK, GDN_HEADS + h:GDN_HEADS + h + 1]
            glast = gc[CHUNK - 1:CHUNK, h:h + 1]
            decay = jnp.exp(jnp.where(ii >= jj, gcol - grow, -jnp.inf))
            knb = kn.astype(BF16)
            kk = _dot_nt(knb, knb)
            qk = _dot_nt(qn.astype(BF16), knb)
            egc = jnp.exp(gcol)
            items.append(dict(
                c=c, h=h, lmat=jnp.where(ii > jj, kk * bcol * decay, 0.0),
                rhs=jnp.concatenate([vh * bcol, kn * (bcol * egc)], axis=1),
                qd=(qn * egc).astype(BF16), kdt=(kn * jnp.exp(glast - gcol)).T.astype(BF16),
                a=(qk * decay).astype(BF16), gl=jnp.exp(glast)))
        return items

    states = [state_ref[h] for h in range(GDN_HEADS)]
    for c0 in range(0, TILE // CHUNK, GDN_GROUP):
        items = [it for c in range(c0, c0 + GDN_GROUP) for it in chunk_items(c)]
        tinvs = _unit_lower_inverses([it["lmat"] for it in items], ii, jj)
        sols = [_dot3(tinv, it["rhs"]) for tinv, it in zip(tinvs, items)]
        for it, sol in zip(items, sols):
            h = it["h"]
            rows = slice(it["c"] * CHUNK, (it["c"] + 1) * CHUNK)
            lanes = slice(h * HEAD_DIM, (h + 1) * HEAD_DIM)
            st = states[h]
            wq = jnp.concatenate([sol[:, HEAD_DIM:].astype(BF16), it["qd"]], axis=0)
            ws = _dot(wq, st.astype(BF16))
            vb = (sol[:, :HEAD_DIM] - ws[:CHUNK]).astype(BF16)
            o = ws[CHUNK:] + _dot(it["a"], vb)
            states[h] = st * it["gl"] + _dot(it["kdt"], vb)
            zz = z[rows, lanes]
            o = o * lax.rsqrt(jnp.mean(o * o, axis=-1, keepdims=True) + RMS_EPS) * gn_ref[...]
            o_ref[0, rows, lanes] = (o * (zz * jax.nn.sigmoid(zz))).astype(BF16)
    for h in range(GDN_HEADS):
        state_ref[h] = states[h]


def _gdn(x, wts):
    bsz, s, d = x.shape
    tok = pl.BlockSpec((1, TILE, d), lambda b, i: (b, i, 0))
    hw = pl.BlockSpec((1, TILE, GDN_WIDTH), lambda b, i: (b, i, 0))
    consts = [wts[n] for n in ("w_qkv", "w_z", "w_ab", "conv_w", "gdn_prm", "gdn_norm_g")]
    return pl.pallas_call(
        _gdn_kernel,
        grid=(bsz, s // TILE),
        in_specs=[tok] + [_const_spec(c.shape) for c in consts],
        out_specs=hw,
        out_shape=jax.ShapeDtypeStruct((bsz, s, GDN_WIDTH), BF16),
        scratch_shapes=[pltpu.VMEM((HALO + TILE, 3 * GDN_WIDTH), F32),
                        pltpu.VMEM((GDN_HEADS, HEAD_DIM, HEAD_DIM), F32)],
        compiler_params=_params(("arbitrary", "arbitrary")),
        name="gdn",
    )(x, *consts)


def _merge_kernel(x_ref, og_ref, mbc_ref, wa_ref, wg0_ref, wo_ref, ln1g_ref, ln1b_ref,
                  rwt_ref, rb_ref, wsgu_ref, wsd_ref,
                  res_ref, hp_ref, idx_ref, wt_ref, rank_ref, cnt_ref, carry_ref):
    first = (pl.program_id(0) == 0) & (pl.program_id(1) == 0)

    @pl.when(first)
    def _():
        carry_ref[...] = jnp.zeros(carry_ref.shape, F32)

    x = x_ref[0]
    xb = x.astype(BF16)
    br_a = _dot(og_ref[0], wa_ref[...])
    merged = jax.nn.sigmoid(_dot(xb, wg0_ref[...])) * br_a + mbc_ref[0]
    y = _dot(merged.astype(BF16), wo_ref[...])
    h = _layer_norm(DEEPNORM_ALPHA * x + y, ln1g_ref[...], ln1b_ref[...])
    hb = h.astype(BF16)
    hp_ref[...] = _pack_bf16_pairs(h)

    gu = _dot(hb, wsgu_ref[...])
    gt = gu[:, :D_SHARED]
    act = gt * jax.nn.sigmoid(gt) * gu[:, D_SHARED:]
    res_ref[0] = DEEPNORM_ALPHA * h + _dot(act.astype(BF16), wsd_ref[...])

    scores = jax.nn.sigmoid(_dot_nt(rwt_ref[...], hb))
    biased = scores + rb_ref[...]
    g3 = biased.reshape(N_GROUPS, GROUP_SIZE, TILE)
    m1 = jnp.max(g3, axis=1)
    m1b = m1[:, None, :]
    n_top = jnp.sum((g3 == m1b).astype(F32), axis=1)
    m2 = jnp.max(jnp.where(g3 < m1b, g3, -jnp.inf), axis=1)
    gs = m1 + jnp.where(n_top >= 2.0, m1, m2)
    gidx = lax.broadcasted_iota(I32, (N_GROUPS, TILE), 0)
    beaten = jnp.zeros((N_GROUPS, TILE), F32)
    for g in range(N_GROUPS):
        row = gs[g:g + 1, :]
        beaten = beaten + ((row > gs) | ((row == gs) & (g < gidx))).astype(F32)
    sel = (beaten < float(TOPK_GROUPS)).astype(F32)
    sel_e = jnp.broadcast_to(sel[:, None, :], (N_GROUPS, GROUP_SIZE, TILE)).reshape(N_EXPERTS, TILE)
    masked = jnp.where(sel_e > 0.5, biased, -jnp.inf)
    eidx = lax.broadcasted_iota(I32, (N_EXPERTS, TILE), 0)
    chosen = jnp.zeros((N_EXPERTS, TILE), F32)
    picks, pick_scores = [], []
    for _k in range(TOP_K):
        mx = jnp.max(masked, axis=0, keepdims=True)
        pick = jnp.min(jnp.where(masked == mx, eidx, N_EXPERTS), axis=0, keepdims=True)
        hot = eidx == pick
        picks.append(pick)
        pick_scores.append(jnp.sum(jnp.where(hot, scores, 0.0), axis=0, keepdims=True))
        chosen = jnp.where(hot, 1.0, chosen)
        masked = jnp.where(hot, -jnp.inf, masked)
    total = pick_scores[0]
    for sck in pick_scores[1:]:
        total = total + sck
    idx_ref[...] = jnp.concatenate(picks, axis=0)
    wt_ref[...] = jnp.concatenate([sck / total * ROUTED_SCALE for sck in pick_scores], axis=0)

    ti = lax.broadcasted_iota(I32, (TILE, TILE), 0)
    tj = lax.broadcasted_iota(I32, (TILE, TILE), 1)
    before = _dot(chosen.astype(BF16), (ti < tj).astype(BF16))
    carry = carry_ref[...]
    before = before + jnp.concatenate([carry] * (TILE // LANES), axis=1)
    rank_ref[...] = jnp.concatenate(
        [jnp.sum(jnp.where(eidx == p, before, 0.0), axis=0, keepdims=True) for p in picks],
        axis=0).astype(I32)
    carry = carry + _dot(chosen.astype(BF16), jnp.ones((TILE, LANES), BF16))
    carry_ref[...] = carry
    cnt_ref[...] = carry.astype(I32)


def _merge(x, og, mbc, wts):
    bsz, s, d = x.shape
    t = bsz * s
    nt = s // TILE
    tok = pl.BlockSpec((1, TILE, d), lambda b, i: (b, i, 0))
    ogs = pl.BlockSpec((1, TILE, GDN_WIDTH), lambda b, i: (b, i, 0))
    flat = lambda b, i: (b * nt + i, 0)
    lane = lambda b, i: (0, b * nt + i)
    consts = [wts[n] for n in ("w_a", "w_g0", "w_o", "ln1_g", "ln1_b", "router_wt", "router_bias",
                               "ws_gu", "ws_down")]
    return pl.pallas_call(
        _merge_kernel,
        grid=(bsz, nt),
        in_specs=[tok, ogs, tok] + [_const_spec(c.shape) for c in consts],
        out_specs=[tok,
                   pl.BlockSpec((TILE, d // 2), flat),
                   pl.BlockSpec((TOP_K, TILE), lane),
                   pl.BlockSpec((TOP_K, TILE), lane),
                   pl.BlockSpec((TOP_K, TILE), lane),
                   _const_spec((N_EXPERTS, LANES))],
        out_shape=[jax.ShapeDtypeStruct((bsz, s, d), F32),
                   jax.ShapeDtypeStruct((t, d // 2), U32),
                   jax.ShapeDtypeStruct((TOP_K, t), I32),
                   jax.ShapeDtypeStruct((TOP_K, t), F32),
                   jax.ShapeDtypeStruct((TOP_K, t), I32),
                   jax.ShapeDtypeStruct((N_EXPERTS, LANES), I32)],
        scratch_shapes=[pltpu.VMEM((N_EXPERTS, LANES), F32)],
        compiler_params=_params(("arbitrary", "arbitrary")),
        name="merge",
    )(x, og, mbc, *consts)


def _plan_kernel(idx_ref, rank_ref, cnt_ref, pos_ref, win_ref, meta_ref):
    e = N_EXPERTS
    cnt = cnt_ref[...]
    padded = cnt + (EXPERT_BLOCK - 1)
    nblk = jnp.floor(padded.astype(F32) * (1.0 / EXPERT_BLOCK)).astype(I32)
    nblk = jnp.where(nblk * EXPERT_BLOCK > padded, nblk - 1, nblk)
    nblk = jnp.where((nblk + 1) * EXPERT_BLOCK <= padded, nblk + 1, nblk)
    ei = lax.broadcasted_iota(I32, (e, e), 0)
    ej = lax.broadcasted_iota(I32, (e, e), 1)
    incl = _dot((ei >= ej).astype(BF16), nblk.astype(F32).astype(BF16)).astype(I32)
    excl = incl - nblk
    start = (excl * EXPERT_BLOCK).astype(F32)
    lane = lax.broadcasted_iota(I32, (e, LANES), 1)
    n_used = jnp.broadcast_to(incl[e - 1:e, :], (e, LANES))
    meta_ref[...] = jnp.where(lane == 0, excl, jnp.where(lane == 1, nblk, jnp.where(lane == 2, cnt, n_used)))

    e128 = lax.broadcasted_iota(I32, (e, LANES), 0)

    def body(i, carry):
        off = pl.multiple_of(i * LANES, LANES)
        ids = idx_ref[:, pl.ds(off, LANES)]
        rows = [jnp.sum(jnp.where(e128 == ids[k:k + 1, :], start, 0.0), axis=0, keepdims=True)
                for k in range(TOP_K)]
        p = jnp.concatenate(rows, axis=0).astype(I32) + rank_ref[:, pl.ds(off, LANES)]
        pos_ref[:, pl.ds(off, LANES)] = p
        for half in range(LANES // SC_WINDOW):
            row0 = pl.multiple_of((i * (LANES // SC_WINDOW) + half) * TOP_K, TOP_K)
            win_ref[pl.ds(row0, TOP_K), :] = p[:, half * SC_WINDOW:(half + 1) * SC_WINDOW]
        return carry

    lax.fori_loop(0, idx_ref.shape[1] // LANES, body, 0)


def _plan(idx, rank, cnt):
    t = idx.shape[1]
    return pl.pallas_call(
        _plan_kernel,
        grid=(1,),
        in_specs=[_const_spec(idx.shape), _const_spec(rank.shape), _const_spec(cnt.shape)],
        out_specs=[_const_spec((TOP_K, t)), _const_spec((t // SC_WINDOW * TOP_K, SC_WINDOW)),
                   _const_spec((N_EXPERTS, LANES))],
        out_shape=[jax.ShapeDtypeStruct((TOP_K, t), I32),
                   jax.ShapeDtypeStruct((t // SC_WINDOW * TOP_K, SC_WINDOW), I32),
                   jax.ShapeDtypeStruct((N_EXPERTS, LANES), I32)],
        compiler_params=_params(("arbitrary",)),
        name="plan",
    )(idx, rank, cnt)


def _sc_scatter_rows(rows, win_slots, n_slots):
    t, d = rows.shape
    workers = SC_CORES * SC_SUBCORES
    per_w = t // workers
    nwin = per_w // SC_WINDOW
    assert per_w * workers == t and nwin * SC_WINDOW == per_w and nwin % 2 == 0
    idx = win_slots.reshape(workers, nwin * TOP_K, SC_WINDOW)
    mesh = plsc.VectorSubcoreMesh(core_axis_name="core", subcore_axis_name="subcore",
                                  num_cores=SC_CORES, num_subcores=SC_SUBCORES)

    @functools.partial(
        pl.kernel, out_type=jax.ShapeDtypeStruct((n_slots, d), rows.dtype), mesh=mesh, name="sc_scatter_rows",
        scratch_types=[pltpu.VMEM((nwin * TOP_K, SC_WINDOW), I32), pltpu.VMEM((2, SC_WINDOW, d), rows.dtype),
                       pltpu.SemaphoreType.DMA((2,)), pltpu.SemaphoreType.DMA((2,))])
    def scatter(rows_hbm, idx_hbm, out_hbm, idx_v, rows_v, lsem, ssem):
        wid = lax.axis_index("subcore") * SC_CORES + lax.axis_index("core")
        base = wid * per_w
        pltpu.sync_copy(idx_hbm.at[wid], idx_v)

        def load(w, slot):
            return pltpu.make_async_copy(rows_hbm.at[pl.ds(base + w * SC_WINDOW, SC_WINDOW)], rows_v.at[slot],
                                         lsem.at[slot])

        def send(w, k, slot):
            return pltpu.make_async_copy(rows_v.at[slot], out_hbm.at[idx_v.at[w * TOP_K + k]], ssem.at[slot])

        load(0, 0).start()

        @pl.loop(0, nwin, step=2)
        def _(w0):
            for b in range(2):
                w = w0 + b
                load(w, b).wait()

                @pl.when(w + 1 < nwin)
                def _():
                    @pl.when(w >= 1)
                    def _():
                        for k in range(TOP_K):
                            send(w - 1, k, 1 - b).wait()

                    load(w + 1, 1 - b).start()

                for k in range(TOP_K):
                    send(w, k, b).start()

        for k in range(TOP_K):
            send(nwin - 2, k, 0).wait()
        for k in range(TOP_K):
            send(nwin - 1, k, 1).wait()

    return scatter(rows, idx)


def _pack_bf16_pairs(a):
    n = a.shape[1] // 2
    bits = lax.bitcast_convert_type(a.astype(BF16).astype(F32), U32)
    return (bits[:, :n] >> 16) | (bits[:, n:] & jnp.uint32(0xFFFF0000))


def _unpack_bf16_pairs(words):
    lo = lax.bitcast_convert_type(words << 16, F32)
    hi = lax.bitcast_convert_type(words & jnp.uint32(0xFFFF0000), F32)
    return jnp.concatenate([lo, hi], axis=1)


def _experts_kernel(fb_ref, nb_ref, cnt_ref, nu_ref, xs_ref, wg_ref, wu_ref, wd_ref, y_ref,
                    wgbuf, wubuf, wdbuf, wsem, wgu_s, wd_s, xbuf, ybuf, xsem, ysem):
    e = pl.program_id(0)
    n_used = nu_ref[0]
    bm = EXPERT_BLOCK
    row_queue = 1

    def x_copy(g, slot):
        return pltpu.make_async_copy(xs_ref.at[pl.ds(g * bm, bm), :], xbuf.at[slot], xsem.at[slot])

    def y_copy(g, slot):
        return pltpu.make_async_copy(ybuf.at[slot], y_ref.at[pl.ds(g * bm, bm), :], ysem.at[slot])

    @pl.when(e == 0)
    def _():
        for g0 in range(X_AHEAD):
            @pl.when(g0 < n_used)
            def _():
                x_copy(g0, g0).start(priority=row_queue)

    def w_copies(j, slot):
        return [pltpu.make_async_copy(src.at[j], dst.at[slot], wsem.at[slot])
                for src, dst in ((wg_ref, wgbuf), (wu_ref, wubuf), (wd_ref, wdbuf))]

    @pl.when(e == 0)
    def _():
        for j in range(W_SLOTS - 1):
            for cp in w_copies(j, j):
                cp.start()

    wslot = lax.rem(e, W_SLOTS)
    for cp in w_copies(e, wslot):
        cp.wait()
    ahead_e = e + (W_SLOTS - 1)

    @pl.when(ahead_e < pl.num_programs(0))
    def _():
        for cp in w_copies(ahead_e, lax.rem(ahead_e, W_SLOTS)):
            cp.start()

    nb = nb_ref[e]

    @pl.when(nb > 0)
    def _():
        wgu_s[:, :D_EXPERT] = wgbuf[wslot].astype(BF16)
        wgu_s[:, D_EXPERT:] = wubuf[wslot].astype(BF16)
        wd_s[...] = wdbuf[wslot].astype(BF16)

    def process(b0, width):
        g0 = fb_ref[e] + b0
        xs = []
        for i in range(width):
            g = g0 + i
            x_copy(g, g & (X_SLOTS - 1)).wait()
            xs.append(xbuf[g & (X_SLOTS - 1)])
        for i in range(width):
            ahead = g0 + X_AHEAD + i

            @pl.when(ahead < n_used)
            def _():
                x_copy(ahead, ahead & (X_SLOTS - 1)).start(priority=row_queue)

        ys = []
        for i in range(width):
            x = _unpack_bf16_pairs(xs[i])
            live = lax.broadcasted_iota(I32, (bm, 1), 0) < cnt_ref[e] - (b0 + i) * bm
            xb = jnp.where(live, x, 0.0).astype(BF16)
            gu = _dot(xb, wgu_s[...])
            gt = gu[:, :D_EXPERT]
            act = (gt * jax.nn.sigmoid(gt) * gu[:, D_EXPERT:]).astype(BF16)
            ys.append(_pack_bf16_pairs(_dot(act, wd_s[...])))
        for i in range(width):
            g = g0 + i
            slot = g & (Y_SLOTS - 1)

            @pl.when(g >= Y_SLOTS)
            def _():
                y_copy(g - Y_SLOTS, slot).wait()

            ybuf[slot] = ys[i]
            y_copy(g, slot).start(priority=row_queue)

    def pair(p, carry):
        process(2 * p, 2)
        return carry

    lax.fori_loop(0, nb >> 1, pair, 0)

    @pl.when((nb & 1) == 1)
    def _():
        process(nb - 1, 1)

    @pl.when(e == pl.num_programs(0) - 1)
    def _():
        for back in range(Y_SLOTS, 0, -1):
            @pl.when(n_used >= back)
            def _():
                y_copy(n_used - back, (n_used - back) & (Y_SLOTS - 1)).wait()


def _experts(xs, first_block, n_blocks, counts, n_used, w_gate, w_up, w_down):
    n_slots, half = xs.shape
    d = 2 * half
    grid_spec = pltpu.PrefetchScalarGridSpec(
        num_scalar_prefetch=4,
        grid=(N_EXPERTS,),
        in_specs=[pl.BlockSpec(memory_space=pl.ANY)] * 4,
        out_specs=pl.BlockSpec(memory_space=pl.ANY),
        scratch_shapes=[pltpu.VMEM((W_SLOTS, d, D_EXPERT), F32), pltpu.VMEM((W_SLOTS, d, D_EXPERT), F32),
                        pltpu.VMEM((W_SLOTS, D_EXPERT, d), F32), pltpu.SemaphoreType.DMA((W_SLOTS,)),
                        pltpu.VMEM((d, 2 * D_EXPERT), BF16), pltpu.VMEM((D_EXPERT, d), BF16),
                        pltpu.VMEM((X_SLOTS, EXPERT_BLOCK, half), U32),
                        pltpu.VMEM((Y_SLOTS, EXPERT_BLOCK, half), U32),
                        pltpu.SemaphoreType.DMA((X_SLOTS,)), pltpu.SemaphoreType.DMA((Y_SLOTS,))],
    )
    return pl.pallas_call(
        _experts_kernel,
        grid_spec=grid_spec,
        out_shape=jax.ShapeDtypeStruct((n_slots, half), U32),
        compiler_params=_params(("arbitrary",)),
        name="experts",
    )(first_block, n_blocks, counts, n_used, xs, w_gate, w_up, w_down)


def _sc_gather_rows(table, indices):
    n = indices.shape[0]
    d = table.shape[1]
    workers = SC_CORES * SC_SUBCORES
    per_w = n // workers
    nch = per_w // SC_WINDOW
    assert per_w * workers == n and nch * SC_WINDOW == per_w and nch % 2 == 0
    mesh = plsc.VectorSubcoreMesh(core_axis_name="core", subcore_axis_name="subcore",
                                  num_cores=SC_CORES, num_subcores=SC_SUBCORES)

    @functools.partial(
        pl.kernel, out_type=jax.ShapeDtypeStruct((n, d), table.dtype), mesh=mesh, name="sc_gather_rows",
        scratch_types=[pltpu.VMEM((per_w,), I32), pltpu.VMEM((2, SC_WINDOW, d), table.dtype),
                       pltpu.SemaphoreType.DMA((2,)), pltpu.SemaphoreType.DMA((2,))])
    def gather(table_hbm, idx_hbm, out_hbm, idx_v, rows_v, gsem, wsem):
        base = (lax.axis_index("subcore") * SC_CORES + lax.axis_index("core")) * per_w
        pltpu.sync_copy(idx_hbm.at[pl.ds(base, per_w)], idx_v)

        def fetch(c, slot):
            return pltpu.make_async_copy(table_hbm.at[idx_v.at[pl.ds(c * SC_WINDOW, SC_WINDOW)]],
                                         rows_v.at[slot], gsem.at[slot])

        def put(c, slot):
            return pltpu.make_async_copy(rows_v.at[slot], out_hbm.at[pl.ds(base + c * SC_WINDOW, SC_WINDOW)],
                                         wsem.at[slot])

        fetch(0, 0).start()

        @pl.loop(0, nch, step=2)
        def _(c0):
            for b in range(2):
                c = c0 + b
                fetch(c, b).wait()

                @pl.when(c + 1 < nch)
                def _():
                    @pl.when(c >= 1)
                    def _():
                        put(c - 1, 1 - b).wait()

                    fetch(c + 1, 1 - b).start()

                put(c, b).start()

        put(nch - 2, 0).wait()
        put(nch - 1, 1).wait()

    return gather(table, indices)


def _combine_kernel(yt_ref, wt_ref, res_ref, g_ref, b_ref, o_ref):
    acc = res_ref[...]
    for k in range(TOP_K):
        acc = acc + _unpack_bf16_pairs(yt_ref[k]) * wt_ref[:, k:k + 1]
    o_ref[...] = _layer_norm(acc, g_ref[...], b_ref[...])


def _combine(y_tok, wts_tok, res, ln_g, ln_b):
    t, d = res.shape
    tok = pl.BlockSpec((COMBINE_TILE, d), lambda i: (i, 0))
    return pl.pallas_call(
        _combine_kernel,
        grid=(t // COMBINE_TILE,),
        in_specs=[pl.BlockSpec((TOP_K, COMBINE_TILE, d // 2), lambda i: (0, i, 0)),
                  pl.BlockSpec((COMBINE_TILE, TOP_K), lambda i: (i, 0)),
                  tok, _const_spec(ln_g.shape), _const_spec(ln_b.shape)],
        out_specs=tok,
        out_shape=jax.ShapeDtypeStruct((t, d), F32),
        compiler_params=_params(("arbitrary",)),
        name="combine",
    )(y_tok, wts_tok, res, ln_g, ln_b)


def _prepare(l, w_in, conv_w, a_log, dt_bias, gdn_norm_g, w_a, sgu_ln_g, sgu_ln_b, sgu_w, sgu_b, w_b,
             w_mem_kv, w_c, w_o, ln1_g, ln1_b, router_w, router_bias, ws_gate, ws_up, ws_down, ln2_g, ln2_b):
    wi = w_in[l]
    d = D_MODEL
    bf = lambda a: a.astype(BF16)
    row = lambda a: a.reshape(1, -1).astype(F32)
    w_ab = jnp.zeros((d, LANES), F32).at[:, :2 * GDN_HEADS].set(wi[:, _C_AB:_C_UV])
    prm = jnp.zeros((SUBLANES, LANES), F32).at[0, :GDN_HEADS].set(a_log[l]).at[1, :GDN_HEADS].set(dt_bias[l])
    sgu_bias = jnp.repeat(sgu_b[l].T, SGU_WIDTH // SGU_GROUPS, axis=1)
    return {
        "w_qkv": bf(wi[:, _C_QKV:_C_Z]), "w_z": bf(wi[:, _C_Z:_C_AB]), "w_ab": bf(w_ab),
        "w_uv": bf(wi[:, _C_UV:_C_XQ]), "w_xq": bf(wi[:, _C_XQ:_C_GATE]),
        "w_g0": bf(wi[:, _C_GATE:_C_GATE + d]), "w_g1": bf(wi[:, _C_GATE + d:_C_GATE + 2 * d]),
        "w_g2": bf(wi[:, _C_GATE + 2 * d:_C_GATE + 3 * d]),
        "conv_w": conv_w[l].astype(F32), "gdn_prm": prm, "gdn_norm_g": row(gdn_norm_g[l]),
        "w_a": bf(w_a[l]), "sgu_ln_g": row(sgu_ln_g[l]), "sgu_ln_b": row(sgu_ln_b[l]),
        "sgu_w": sgu_w[l].astype(F32), "sgu_bias": sgu_bias.astype(F32), "w_b": bf(w_b[l]),
        "w_mem_kv": bf(w_mem_kv[l]), "w_c": bf(w_c[l]), "w_o": bf(w_o[l]),
        "ln1_g": row(ln1_g[l]), "ln1_b": row(ln1_b[l]),
        "router_wt": bf(router_w[l].T), "router_bias": router_bias[l].reshape(-1, 1).astype(F32),
        "ws_gu": bf(jnp.concatenate([ws_gate[l], ws_up[l]], axis=1)), "ws_down": bf(ws_down[l]),
        "ln2_g": row(ln2_g[l]), "ln2_b": row(ln2_b[l]),
    }


def _layer(x, mem, wts, w_gate, w_up, w_down):
    bsz, s, d = x.shape
    t = bsz * s
    k, v = _mem_kv(mem, wts["w_mem_kv"])
    mbc = _mix_bc(x, wts, k, v)
    og = _gdn(x, wts)
    res, hp, idx, rw, rank, cnt = _merge(x, og, mbc, wts)
    nb = t * TOP_K // EXPERT_BLOCK + N_EXPERTS
    pos, win_slots, meta = _plan(idx, rank, cnt)
    xs = _sc_scatter_rows(hp, win_slots, nb * EXPERT_BLOCK)
    y = _experts(xs, meta[:, 0], meta[:, 1], meta[:, 2], meta[:1, 3], w_gate, w_up, w_down)
    y_tok = _sc_gather_rows(y, pos.reshape(-1)).reshape(TOP_K, t, d // 2)
    out = _combine(y_tok, rw.T, res.reshape(t, d), wts["ln2_g"], wts["ln2_b"])
    return out.reshape(bsz, s, d)


def kernel(x, mem, w_in, conv_w, a_log, dt_bias, gdn_norm_g, w_a, sgu_ln_g, sgu_ln_b, sgu_w, sgu_b, w_b,
           w_mem_kv, w_c, w_o, ln1_g, ln1_b, router_w, router_bias, w_gate, w_up, w_down, ws_gate, ws_up,
           ws_down, ln2_g, ln2_b):
    assert x.shape[1] % TILE == 0 and x.shape[2] == D_MODEL
    for l in range(DEPTH):
        wts = _prepare(l, w_in, conv_w, a_log, dt_bias, gdn_norm_g, w_a, sgu_ln_g, sgu_ln_b, sgu_w, sgu_b,
                       w_b, w_mem_kv, w_c, w_o, ln1_g, ln1_b, router_w, router_bias, ws_gate, ws_up,
                       ws_down, ln2_g, ln2_b)
        x = _layer(x, mem, wts, w_gate[l], w_up[l], w_down[l])
    return x
```

```python
import functools

import jax
import jax.numpy as jnp
from jax import lax
from jax.experimental import pallas as pl
from jax.experimental.pallas import tpu as pltpu
from jax.experimental.pallas import tpu_sc as plsc

F32 = jnp.float32
BF16 = jnp.bfloat16
I32 = jnp.int32
U32 = jnp.uint32

D_MODEL = 1024
DEPTH = 1
GDN_HEADS = 4
HEAD_DIM = 128
GDN_WIDTH = GDN_HEADS * HEAD_DIM
CONV_WIDTH = 4
SGU_GROUPS = 4
SGU_WIDTH = 512
SGU_CHUNK = 128
SGU_GROUP_DIM = SGU_WIDTH // SGU_GROUPS
N_MEM = 256
XATTN_HEADS = 4
XATTN_WIDTH = 512
XATTN_HEAD_DIM = XATTN_WIDTH // XATTN_HEADS
N_EXPERTS = 256
TOP_K = 8
N_GROUPS = 8
GROUP_SIZE = N_EXPERTS // N_GROUPS
TOPK_GROUPS = 4
D_EXPERT = 256
D_SHARED = 256
ROUTED_SCALE = 2.5
LN_EPS = 1e-5
RMS_EPS = 1e-6
DEEPNORM_ALPHA = (2.0 * DEPTH) ** 0.25

_C_QKV = 0
_C_Z = 3 * GDN_WIDTH
_C_AB = 4 * GDN_WIDTH
_C_UV = _C_AB + 2 * GDN_HEADS
_C_XQ = _C_UV + 2 * SGU_WIDTH
_C_GATE = _C_XQ + XATTN_WIDTH

LANES = 128
SUBLANES = 8
CHUNK = 128
GDN_GROUP = 2
TILE = 512
HALO = SUBLANES
EXPERT_BLOCK = 288
X_SLOTS = 8
X_AHEAD = X_SLOTS - 2
Y_SLOTS = 4
W_SLOTS = 3
SC_CORES = 2
SC_SUBCORES = 16
SC_WINDOW = 64
COMBINE_TILE = 256
VMEM_LIMIT = 56 * 1024 * 1024


def _dot(a, b):
    return jnp.dot(a, b, preferred_element_type=F32)


def _dot_nt(a, b):
    return lax.dot_general(a, b, (((1,), (1,)), ((), ())), preferred_element_type=F32)


def _split(a):
    hi = a.astype(BF16)
    lo = (a - hi.astype(F32)).astype(BF16)
    return hi, lo


def _dot3_parts(ah, al, bh, bl):
    return _dot(jnp.concatenate([ah, ah, al], axis=1), jnp.concatenate([bh, bl, bh], axis=0))


def _dot3(a, b):
    return _dot3_parts(*_split(a), *_split(b))


def _softplus(x):
    return jnp.maximum(x, 0.0) + jnp.log1p(jnp.exp(-jnp.abs(x)))


def _layer_norm(x, g, b):
    mu = jnp.mean(x, axis=-1, keepdims=True)
    xc = x - mu
    var = jnp.mean(xc * xc, axis=-1, keepdims=True)
    return xc * lax.rsqrt(var + LN_EPS) * g + b


def _const_spec(shape):
    nd = len(shape)
    return pl.BlockSpec(shape, lambda *_: (0,) * nd)


def _params(semantics):
    return pltpu.CompilerParams(dimension_semantics=semantics, vmem_limit_bytes=VMEM_LIMIT)


def _mem_kv_kernel(mem_ref, w_ref, k_ref, v_ref):
    kv = _dot(mem_ref[0].astype(BF16), w_ref[...])
    k_ref[0] = kv[:, :XATTN_WIDTH].astype(BF16)
    v_ref[0] = kv[:, XATTN_WIDTH:].astype(BF16)


def _mem_kv(mem, w_kv):
    bsz, n_mem, d = mem.shape
    out = jax.ShapeDtypeStruct((bsz, n_mem, XATTN_WIDTH), BF16)
    return pl.pallas_call(
        _mem_kv_kernel,
        grid=(bsz,),
        in_specs=[pl.BlockSpec((1, n_mem, d), lambda b: (b, 0, 0)), _const_spec(w_kv.shape)],
        out_specs=[pl.BlockSpec((1, n_mem, XATTN_WIDTH), lambda b: (b, 0, 0))] * 2,
        out_shape=[out, out],
        compiler_params=_params(("arbitrary",)),
        name="mem_kv",
    )(mem, w_kv)


def _mix_bc_kernel(x_ref, wuv_ref, wg1_ref, wb_ref, lng_ref, lnb_ref, sw_ref, sb_ref,
                   wxq_ref, k_ref, v_ref, wg2_ref, wc_ref, o_ref):
    xb = x_ref[0].astype(BF16)
    uv = _dot(xb, wuv_ref[...])
    xq = _dot(xb, wxq_ref[...])
    gate_b = _dot(xb, wg1_ref[...])
    gate_c = _dot(xb, wg2_ref[...])
    scores = []
    for h in range(XATTN_HEADS):
        qh = xq[:, h * XATTN_HEAD_DIM:(h + 1) * XATTN_HEAD_DIM].astype(BF16)
        kh = k_ref[0, :, h * XATTN_HEAD_DIM:(h + 1) * XATTN_HEAD_DIM]
        scores.append(_dot_nt(qh, kh) * (XATTN_HEAD_DIM ** -0.5))
    uv = jax.nn.gelu(uv, approximate=True)
    u = uv[:, :SGU_WIDTH]
    v = _layer_norm(uv[:, SGU_WIDTH:], lng_ref[...], lnb_ref[...])
    ii = lax.broadcasted_iota(I32, (SGU_CHUNK, SGU_CHUNK), 0)
    jj = lax.broadcasted_iota(I32, (SGU_CHUNK, SGU_CHUNK), 1)
    rows = []
    for n in range(TILE // SGU_CHUNK):
        cols = []
        for g in range(SGU_GROUPS):
            wt = jnp.where(ii >= jj, sw_ref[g], 0.0).astype(BF16)
            vg = v[n * SGU_CHUNK:(n + 1) * SGU_CHUNK, g * SGU_GROUP_DIM:(g + 1) * SGU_GROUP_DIM].astype(BF16)
            cols.append(_dot(wt, vg))
        rows.append(jnp.concatenate(cols, axis=1) + sb_ref[...])
    sg = jnp.concatenate(rows, axis=0)
    br_b = _dot((u * sg).astype(BF16), wb_ref[...])
    acc = jax.nn.sigmoid(gate_b) * br_b
    heads = []
    for h, sc in enumerate(scores):
        vh = v_ref[0, :, h * XATTN_HEAD_DIM:(h + 1) * XATTN_HEAD_DIM]
        p = jnp.exp(sc - jnp.max(sc, axis=-1, keepdims=True))
        p = p / jnp.sum(p, axis=-1, keepdims=True)
        heads.append(_dot(p.astype(BF16), vh))
    att = jnp.concatenate(heads, axis=1).astype(BF16)
    br_c = _dot(att, wc_ref[...])
    o_ref[0] = acc + jax.nn.sigmoid(gate_c) * br_c


def _mix_bc(x, wts, k, v):
    bsz, s, d = x.shape
    tok = pl.BlockSpec((1, TILE, d), lambda b, i: (b, i, 0))
    kvs = pl.BlockSpec((1, N_MEM, XATTN_WIDTH), lambda b, i: (b, 0, 0))
    names = ("w_uv", "w_g1", "w_b", "sgu_ln_g", "sgu_ln_b", "sgu_w", "sgu_bias", "w_xq")
    consts = [wts[n] for n in names]
    tail = [wts["w_g2"], wts["w_c"]]
    return pl.pallas_call(
        _mix_bc_kernel,
        grid=(bsz, s // TILE),
        in_specs=[tok] + [_const_spec(c.shape) for c in consts] + [kvs, kvs]
        + [_const_spec(c.shape) for c in tail],
        out_specs=tok,
        out_shape=jax.ShapeDtypeStruct((bsz, s, d), F32),
        compiler_params=_params(("arbitrary", "arbitrary")),
        name="mix_bc",
    )(x, *consts, k, v, *tail)


def _unit_lower_inverses(lmats, ii, jj):
    eye = (ii == jj).astype(F32)
    x = ii ^ jj
    lsplit = [_split(l) for l in lmats]
    ts = [eye - jnp.where(x < 2, l, 0.0) for l in lmats]
    zero = jnp.zeros((CHUNK, CHUNK), BF16)
    s = 2
    while s < CHUNK:
        off = (x < 2 * s) & ((ii & s) != 0) & ((jj & s) == 0)
        tsplit = [_split(t) for t in ts]
        prods = []
        for (th, tl), (lh, ll) in zip(tsplit, lsplit):
            oh = jnp.where(off, lh, zero)
            ol = jnp.where(off, ll, zero)
            prods.append(_dot3_parts(th, tl, oh, ol))
        nxt = []
        for t, p, (th, tl) in zip(ts, prods, tsplit):
            ph, plo = _split(p)
            nxt.append(t - _dot3_parts(ph, plo, th, tl))
        ts = nxt
        s *= 2
    return ts


def _gdn_kernel(x_ref, wqkv_ref, wz_ref, wab_ref, convw_ref, prm_ref, gn_ref, o_ref, hist_ref, state_ref):
    @pl.when(pl.program_id(1) == 0)
    def _():
        hist_ref[0:HALO, :] = jnp.zeros((HALO, 3 * GDN_WIDTH), F32)
        state_ref[...] = jnp.zeros(state_ref.shape, F32)

    xb = x_ref[0].astype(BF16)
    ab = _dot(xb, wab_ref[...])
    parts = []
    for p in range(3):
        cols = slice(p * GDN_WIDTH, (p + 1) * GDN_WIDTH)
        pre = _dot(xb, wqkv_ref[:, cols])
        hist_ref[HALO:HALO + TILE, cols] = pre
        conv = pre * convw_ref[CONV_WIDTH - 1:CONV_WIDTH, cols]
        for j in range(CONV_WIDTH - 1):
            shift = CONV_WIDTH - 1 - j
            conv = conv + hist_ref[HALO - shift:HALO - shift + TILE, cols] * convw_ref[j:j + 1, cols]
        hist_ref[0:HALO, cols] = pre[TILE - HALO:, :]
        parts.append(conv * jax.nn.sigmoid(conv))
    qkv = jnp.concatenate(parts, axis=1)
    z = _dot(xb, wz_ref[...])
    g = -jnp.exp(prm_ref[0:1, :]) * _softplus(ab + prm_ref[1:2, :])
    beta = jax.nn.sigmoid(ab)

    ii = lax.broadcasted_iota(I32, (CHUNK, CHUNK), 0)
    jj = lax.broadcasted_iota(I32, (CHUNK, CHUNK), 1)
    tri = (ii >= jj).astype(BF16)
    def chunk_items(c):
        items = []
        r0 = c * CHUNK
        gch = g[r0:r0 + CHUNK, :]
        g1 = gch.astype(BF16)
        r1 = gch - g1.astype(F32)
        g2 = r1.astype(BF16)
        g3 = (r1 - g2.astype(F32)).astype(BF16)
        gc = _dot(jnp.concatenate([tri, tri, tri], axis=1),
                  jnp.concatenate([g1, g2, g3], axis=0))
        gct = gc.T
        for h in range(GDN_HEADS):
            qh = qkv[r0:r0 + CHUNK, h * HEAD_DIM:(h + 1) * HEAD_DIM]
            kh = qkv[r0:r0 + CHUNK, GDN_WIDTH + h * HEAD_DIM:GDN_WIDTH + (h + 1) * HEAD_DIM]
            vh = qkv[r0:r0 + CHUNK, 2 * GDN_WIDTH + h * HEAD_DIM:2 * GDN_WIDTH + (h + 1) * HEAD_DIM]
            qn = qh * lax.rsqrt(jnp.sum(qh * qh, axis=-1, keepdims=True) + RMS_EPS) * (HEAD_DIM ** -0.5)
            kn = kh * lax.rsqrt(jnp.sum(kh * kh, axis=-1, keepdims=True) + RMS_EPS)
            gcol = gc[:, h:h + 1]
            grow = gct[h:h + 1, :]
            bcol = beta[r0:r0 + CHUNK, GDN_HEADS + h:GDN_HEADS + h + 1]
            glast = gc[CHUNK - 1:CHUNK, h:h + 1]
            decay = jnp.exp(jnp.where(ii >= jj, gcol - grow, -jnp.inf))
            knb = kn.astype(BF16)
            kk = _dot_nt(knb, knb)
            qk = _dot_nt(qn.astype(BF16), knb)
            egc = jnp.exp(gcol)
            items.append(dict(
                c=c, h=h, lmat=jnp.where(ii > jj, kk * bcol * decay, 0.0),
                rhs=jnp.concatenate([vh * bcol, kn * (bcol * egc)], axis=1),
                qd=(qn * egc).astype(BF16), kdt=(kn * jnp.exp(glast - gcol)).T.astype(BF16),
                a=(qk * decay).astype(BF16), gl=jnp.exp(glast)))
        return items

    states = [state_ref[h] for h in range(GDN_HEADS)]
    for c0 in range(0, TILE // CHUNK, GDN_GROUP):
        items = [it for c in range(c0, c0 + GDN_GROUP) for it in chunk_items(c)]
        tinvs = _unit_lower_inverses([it["lmat"] for it in items], ii, jj)
        sols = [_dot3(tinv, it["rhs"]) for tinv, it in zip(tinvs, items)]
        for it, sol in zip(items, sols):
            h = it["h"]
            rows = slice(it["c"] * CHUNK, (it["c"] + 1) * CHUNK)
            lanes = slice(h * HEAD_DIM, (h + 1) * HEAD_DIM)
            st = states[h]
            wq = jnp.concatenate([sol[:, HEAD_DIM:].astype(BF16), it["qd"]], axis=0)
            ws = _dot(wq, st.astype(BF16))
            vb = (sol[:, :HEAD_DIM] - ws[:CHUNK]).astype(BF16)
            o = ws[CHUNK:] + _dot(it["a"], vb)
            states[h] = st * it["gl"] + _dot(it["kdt"], vb)
            zz = z[rows, lanes]
            o = o * lax.rsqrt(jnp.mean(o * o, axis=-1, keepdims=True) + RMS_EPS) * gn_ref[...]
            o_ref[0, rows, lanes] = (o * (zz * jax.nn.sigmoid(zz))).astype(BF16)
    for h in range(GDN_HEADS):
        state_ref[h] = states[h]


def _gdn(x, wts):
    bsz, s, d = x.shape
    tok = pl.BlockSpec((1, TILE, d), lambda b, i: (b, i, 0))
    hw = pl.BlockSpec((1, TILE, GDN_WIDTH), lambda b, i: (b, i, 0))
    consts = [wts[n] for n in ("w_qkv", "w_z", "w_ab", "conv_w", "gdn_prm", "gdn_norm_g")]
    return pl.pallas_call(
        _gdn_kernel,
        grid=(bsz, s // TILE),
        in_specs=[tok] + [_const_spec(c.shape) for c in consts],
        out_specs=hw,
        out_shape=jax.ShapeDtypeStruct((bsz, s, GDN_WIDTH), BF16),
        scratch_shapes=[pltpu.VMEM((HALO + TILE, 3 * GDN_WIDTH), F32),
                        pltpu.VMEM((GDN_HEADS, HEAD_DIM, HEAD_DIM), F32)],
        compiler_params=_params(("arbitrary", "arbitrary")),
        name="gdn",
    )(x, *consts)


def _merge_kernel(x_ref, og_ref, mbc_ref, wa_ref, wg0_ref, wo_ref, ln1g_ref, ln1b_ref,
                  rwt_ref, rb_ref, wsgu_ref, wsd_ref,
                  res_ref, hp_ref, idx_ref, wt_ref, rank_ref, cnt_ref, carry_ref):
    first = (pl.program_id(0) == 0) & (pl.program_id(1) == 0)

    @pl.when(first)
    def _():
        carry_ref[...] = jnp.zeros(carry_ref.shape, F32)

    x = x_ref[0]
    xb = x.astype(BF16)
    br_a = _dot(og_ref[0], wa_ref[...])
    merged = jax.nn.sigmoid(_dot(xb, wg0_ref[...])) * br_a + mbc_ref[0]
    y = _dot(merged.astype(BF16), wo_ref[...])
    h = _layer_norm(DEEPNORM_ALPHA * x + y, ln1g_ref[...], ln1b_ref[...])
    hb = h.astype(BF16)
    hp_ref[...] = _pack_bf16_pairs(h)

    gu = _dot(hb, wsgu_ref[...])
    gt = gu[:, :D_SHARED]
    act = gt * jax.nn.sigmoid(gt) * gu[:, D_SHARED:]
    res_ref[0] = DEEPNORM_ALPHA * h + _dot(act.astype(BF16), wsd_ref[...])

    scores = jax.nn.sigmoid(_dot_nt(rwt_ref[...], hb))
    biased = scores + rb_ref[...]
    g3 = biased.reshape(N_GROUPS, GROUP_SIZE, TILE)
    m1 = jnp.max(g3, axis=1)
    m1b = m1[:, None, :]
    n_top = jnp.sum((g3 == m1b).astype(F32), axis=1)
    m2 = jnp.max(jnp.where(g3 < m1b, g3, -jnp.inf), axis=1)
    gs = m1 + jnp.where(n_top >= 2.0, m1, m2)
    gidx = lax.broadcasted_iota(I32, (N_GROUPS, TILE), 0)
    beaten = jnp.zeros((N_GROUPS, TILE), F32)
    for g in range(N_GROUPS):
        row = gs[g:g + 1, :]
        beaten = beaten + ((row > gs) | ((row == gs) & (g < gidx))).astype(F32)
    sel = (beaten < float(TOPK_GROUPS)).astype(F32)
    sel_e = jnp.broadcast_to(sel[:, None, :], (N_GROUPS, GROUP_SIZE, TILE)).reshape(N_EXPERTS, TILE)
    masked = jnp.where(sel_e > 0.5, biased, -jnp.inf)
    eidx = lax.broadcasted_iota(I32, (N_EXPERTS, TILE), 0)
    chosen = jnp.zeros((N_EXPERTS, TILE), F32)
    picks, pick_scores = [], []
    for _k in range(TOP_K):
        mx = jnp.max(masked, axis=0, keepdims=True)
        pick = jnp.min(jnp.where(masked == mx, eidx, N_EXPERTS), axis=0, keepdims=True)
        hot = eidx == pick
        picks.append(pick)
        pick_scores.append(jnp.sum(jnp.where(hot, scores, 0.0), axis=0, keepdims=True))
        chosen = jnp.where(hot, 1.0, chosen)
        masked = jnp.where(hot, -jnp.inf, masked)
    total = pick_scores[0]
    for sck in pick_scores[1:]:
        total = total + sck
    idx_ref[...] = jnp.concatenate(picks, axis=0)
    wt_ref[...] = jnp.concatenate([sck / total * ROUTED_SCALE for sck in pick_scores], axis=0)

    ti = lax.broadcasted_iota(I32, (TILE, TILE), 0)
    tj = lax.broadcasted_iota(I32, (TILE, TILE), 1)
    before = _dot(chosen.astype(BF16), (ti < tj).astype(BF16))
    carry = carry_ref[...]
    before = before + jnp.concatenate([carry] * (TILE // LANES), axis=1)
    rank_ref[...] = jnp.concatenate(
        [jnp.sum(jnp.where(eidx == p, before, 0.0), axis=0, keepdims=True) for p in picks],
        axis=0).astype(I32)
    carry = carry + _dot(chosen.astype(BF16), jnp.ones((TILE, LANES), BF16))
    carry_ref[...] = carry
    cnt_ref[...] = carry.astype(I32)


def _merge(x, og, mbc, wts):
    bsz, s, d = x.shape
    t = bsz * s
    nt = s // TILE
    tok = pl.BlockSpec((1, TILE, d), lambda b, i: (b, i, 0))
    ogs = pl.BlockSpec((1, TILE, GDN_WIDTH), lambda b, i: (b, i, 0))
    flat = lambda b, i: (b * nt + i, 0)
    lane = lambda b, i: (0, b * nt + i)
    consts = [wts[n] for n in ("w_a", "w_g0", "w_o", "ln1_g", "ln1_b", "router_wt", "router_bias",
                               "ws_gu", "ws_down")]
    return pl.pallas_call(
        _merge_kernel,
        grid=(bsz, nt),
        in_specs=[tok, ogs, tok] + [_const_spec(c.shape) for c in consts],
        out_specs=[tok,
                   pl.BlockSpec((TILE, d // 2), flat),
                   pl.BlockSpec((TOP_K, TILE), lane),
                   pl.BlockSpec((TOP_K, TILE), lane),
                   pl.BlockSpec((TOP_K, TILE), lane),
                   _const_spec((N_EXPERTS, LANES))],
        out_shape=[jax.ShapeDtypeStruct((bsz, s, d), F32),
                   jax.ShapeDtypeStruct((t, d // 2), U32),
                   jax.ShapeDtypeStruct((TOP_K, t), I32),
                   jax.ShapeDtypeStruct((TOP_K, t), F32),
                   jax.ShapeDtypeStruct((TOP_K, t), I32),
                   jax.ShapeDtypeStruct((N_EXPERTS, LANES), I32)],
        scratch_shapes=[pltpu.VMEM((N_EXPERTS, LANES), F32)],
        compiler_params=_params(("arbitrary", "arbitrary")),
        name="merge",
    )(x, og, mbc, *consts)


def _plan_kernel(idx_ref, rank_ref, cnt_ref, pos_ref, win_ref, meta_ref):
    e = N_EXPERTS
    cnt = cnt_ref[...]
    padded = cnt + (EXPERT_BLOCK - 1)
    nblk = jnp.floor(padded.astype(F32) * (1.0 / EXPERT_BLOCK)).astype(I32)
    nblk = jnp.where(nblk * EXPERT_BLOCK > padded, nblk - 1, nblk)
    nblk = jnp.where((nblk + 1) * EXPERT_BLOCK <= padded, nblk + 1, nblk)
    ei = lax.broadcasted_iota(I32, (e, e), 0)
    ej = lax.broadcasted_iota(I32, (e, e), 1)
    incl = _dot((ei >= ej).astype(BF16), nblk.astype(F32).astype(BF16)).astype(I32)
    excl = incl - nblk
    start = (excl * EXPERT_BLOCK).astype(F32)
    lane = lax.broadcasted_iota(I32, (e, LANES), 1)
    n_used = jnp.broadcast_to(incl[e - 1:e, :], (e, LANES))
    meta_ref[...] = jnp.where(lane == 0, excl, jnp.where(lane == 1, nblk, jnp.where(lane == 2, cnt, n_used)))

    e128 = lax.broadcasted_iota(I32, (e, LANES), 0)

    def body(i, carry):
        off = pl.multiple_of(i * LANES, LANES)
        ids = idx_ref[:, pl.ds(off, LANES)]
        rows = [jnp.sum(jnp.where(e128 == ids[k:k + 1, :], start, 0.0), axis=0, keepdims=True)
                for k in range(TOP_K)]
        p = jnp.concatenate(rows, axis=0).astype(I32) + rank_ref[:, pl.ds(off, LANES)]
        pos_ref[:, pl.ds(off, LANES)] = p
        for half in range(LANES // SC_WINDOW):
            row0 = pl.multiple_of((i * (LANES // SC_WINDOW) + half) * TOP_K, TOP_K)
            win_ref[pl.ds(row0, TOP_K), :] = p[:, half * SC_WINDOW:(half + 1) * SC_WINDOW]
        return carry

    lax.fori_loop(0, idx_ref.shape[1] // LANES, body, 0)


def _plan(idx, rank, cnt):
    t = idx.shape[1]
    return pl.pallas_call(
        _plan_kernel,
        grid=(1,),
        in_specs=[_const_spec(idx.shape), _const_spec(rank.shape), _const_spec(cnt.shape)],
        out_specs=[_const_spec((TOP_K, t)), _const_spec((t // SC_WINDOW * TOP_K, SC_WINDOW)),
                   _const_spec((N_EXPERTS, LANES))],
        out_shape=[jax.ShapeDtypeStruct((TOP_K, t), I32),
                   jax.ShapeDtypeStruct((t // SC_WINDOW * TOP_K, SC_WINDOW), I32),
                   jax.ShapeDtypeStruct((N_EXPERTS, LANES), I32)],
        compiler_params=_params(("arbitrary",)),
        name="plan",
    )(idx, rank, cnt)


def _sc_scatter_rows(rows, win_slots, n_slots):
    t, d = rows.shape
    workers = SC_CORES * SC_SUBCORES
    per_w = t // workers
    nwin = per_w // SC_WINDOW
    assert per_w * workers == t and nwin * SC_WINDOW == per_w and nwin % 2 == 0
    idx = win_slots.reshape(workers, nwin * TOP_K, SC_WINDOW)
    mesh = plsc.VectorSubcoreMesh(core_axis_name="core", subcore_axis_name="subcore",
                                  num_cores=SC_CORES, num_subcores=SC_SUBCORES)

    @functools.partial(
        pl.kernel, out_type=jax.ShapeDtypeStruct((n_slots, d), rows.dtype), mesh=mesh, name="sc_scatter_rows",
        scratch_types=[pltpu.VMEM((nwin * TOP_K, SC_WINDOW), I32), pltpu.VMEM((2, SC_WINDOW, d), rows.dtype),
                       pltpu.SemaphoreType.DMA((2,)), pltpu.SemaphoreType.DMA((2,))])
    def scatter(rows_hbm, idx_hbm, out_hbm, idx_v, rows_v, lsem, ssem):
        wid = lax.axis_index("subcore") * SC_CORES + lax.axis_index("core")
        base = wid * per_w
        pltpu.sync_copy(idx_hbm.at[wid], idx_v)

        def load(w, slot):
            return pltpu.make_async_copy(rows_hbm.at[pl.ds(base + w * SC_WINDOW, SC_WINDOW)], rows_v.at[slot],
                                         lsem.at[slot])

        def send(w, k, slot):
            return pltpu.make_async_copy(rows_v.at[slot], out_hbm.at[idx_v.at[w * TOP_K + k]], ssem.at[slot])

        load(0, 0).start()

        @pl.loop(0, nwin, step=2)
        def _(w0):
            for b in range(2):
                w = w0 + b
                load(w, b).wait()

                @pl.when(w + 1 < nwin)
                def _():
                    @pl.when(w >= 1)
                    def _():
                        for k in range(TOP_K):
                            send(w - 1, k, 1 - b).wait()

                    load(w + 1, 1 - b).start()

                for k in range(TOP_K):
                    send(w, k, b).start()

        for k in range(TOP_K):
            send(nwin - 2, k, 0).wait()
        for k in range(TOP_K):
            send(nwin - 1, k, 1).wait()

    return scatter(rows, idx)


def _pack_bf16_pairs(a):
    n = a.shape[1] // 2
    bits = lax.bitcast_convert_type(a.astype(BF16).astype(F32), U32)
    return (bits[:, :n] >> 16) | (bits[:, n:] & jnp.uint32(0xFFFF0000))


def _unpack_bf16_pairs(words):
    lo = lax.bitcast_convert_type(words << 16, F32)
    hi = lax.bitcast_convert_type(words & jnp.uint32(0xFFFF0000), F32)
    return jnp.concatenate([lo, hi], axis=1)


def _experts_kernel(fb_ref, nb_ref, cnt_ref, nu_ref, xs_ref, wg_ref, wu_ref, wd_ref, y_ref,
                    wgbuf, wubuf, wdbuf, wsem, wgu_s, wd_s, xbuf, ybuf, xsem, ysem):
    e = pl.program_id(0)
    n_used = nu_ref[0]
    bm = EXPERT_BLOCK
    row_queue = 1

    def x_copy(g, slot):
        return pltpu.make_async_copy(xs_ref.at[pl.ds(g * bm, bm), :], xbuf.at[slot], xsem.at[slot])

    def y_copy(g, slot):
        return pltpu.make_async_copy(ybuf.at[slot], y_ref.at[pl.ds(g * bm, bm), :], ysem.at[slot])

    @pl.when(e == 0)
    def _():
        for g0 in range(X_AHEAD):
            @pl.when(g0 < n_used)
            def _():
                x_copy(g0, g0).start(priority=row_queue)

    def w_copies(j, slot):
        return [pltpu.make_async_copy(src.at[j], dst.at[slot], wsem.at[slot])
                for src, dst in ((wg_ref, wgbuf), (wu_ref, wubuf), (wd_ref, wdbuf))]

    @pl.when(e == 0)
    def _():
        for j in range(W_SLOTS - 1):
            for cp in w_copies(j, j):
                cp.start()

    wslot = lax.rem(e, W_SLOTS)
    for cp in w_copies(e, wslot):
        cp.wait()
    ahead_e = e + (W_SLOTS - 1)

    @pl.when(ahead_e < pl.num_programs(0))
    def _():
        for cp in w_copies(ahead_e, lax.rem(ahead_e, W_SLOTS)):
            cp.start()

    nb = nb_ref[e]

    @pl.when(nb > 0)
    def _():
        wgu_s[:, :D_EXPERT] = wgbuf[wslot].astype(BF16)
        wgu_s[:, D_EXPERT:] = wubuf[wslot].astype(BF16)
        wd_s[...] = wdbuf[wslot].astype(BF16)

    def process(b0, width):
        g0 = fb_ref[e] + b0
        xs = []
        for i in range(width):
            g = g0 + i
            x_copy(g, g & (X_SLOTS - 1)).wait()
            xs.append(xbuf[g & (X_SLOTS - 1)])
        for i in range(width):
            ahead = g0 + X_AHEAD + i

            @pl.when(ahead < n_used)
            def _():
                x_copy(ahead, ahead & (X_SLOTS - 1)).start(priority=row_queue)

        ys = []
        for i in range(width):
            x = _unpack_bf16_pairs(xs[i])
            live = lax.broadcasted_iota(I32, (bm, 1), 0) < cnt_ref[e] - (b0 + i) * bm
            xb = jnp.where(live, x, 0.0).astype(BF16)
            gu = _dot(xb, wgu_s[...])
            gt = gu[:, :D_EXPERT]
            act = (gt * jax.nn.sigmoid(gt) * gu[:, D_EXPERT:]).astype(BF16)
            ys.append(_pack_bf16_pairs(_dot(act, wd_s[...])))
        for i in range(width):
            g = g0 + i
            slot = g & (Y_SLOTS - 1)

            @pl.when(g >= Y_SLOTS)
            def _():
                y_copy(g - Y_SLOTS, slot).wait()

            ybuf[slot] = ys[i]
            y_copy(g, slot).start(priority=row_queue)

    def pair(p, carry):
        process(2 * p, 2)
        return carry

    lax.fori_loop(0, nb >> 1, pair, 0)

    @pl.when((nb & 1) == 1)
    def _():
        process(nb - 1, 1)

    @pl.when(e == pl.num_programs(0) - 1)
    def _():
        for back in range(Y_SLOTS, 0, -1):
            @pl.when(n_used >= back)
            def _():
                y_copy(n_used - back, (n_used - back) & (Y_SLOTS - 1)).wait()


def _experts(xs, first_block, n_blocks, counts, n_used, w_gate, w_up, w_down):
    n_slots, half = xs.shape
    d = 2 * half
    grid_spec = pltpu.PrefetchScalarGridSpec(
        num_scalar_prefetch=4,
        grid=(N_EXPERTS,),
        in_specs=[pl.BlockSpec(memory_space=pl.ANY)] * 4,
        out_specs=pl.BlockSpec(memory_space=pl.ANY),
        scratch_shapes=[pltpu.VMEM((W_SLOTS, d, D_EXPERT), F32), pltpu.VMEM((W_SLOTS, d, D_EXPERT), F32),
                        pltpu.VMEM((W_SLOTS, D_EXPERT, d), F32), pltpu.SemaphoreType.DMA((W_SLOTS,)),
                        pltpu.VMEM((d, 2 * D_EXPERT), BF16), pltpu.VMEM((D_EXPERT, d), BF16),
                        pltpu.VMEM((X_SLOTS, EXPERT_BLOCK, half), U32),
                        pltpu.VMEM((Y_SLOTS, EXPERT_BLOCK, half), U32),
                        pltpu.SemaphoreType.DMA((X_SLOTS,)), pltpu.SemaphoreType.DMA((Y_SLOTS,))],
    )
    return pl.pallas_call(
        _experts_kernel,
        grid_spec=grid_spec,
        out_shape=jax.ShapeDtypeStruct((n_slots, half), U32),
        compiler_params=_params(("arbitrary",)),
        name="experts",
    )(first_block, n_blocks, counts, n_used, xs, w_gate, w_up, w_down)


def _sc_gather_rows(table, indices):
    n = indices.shape[0]
    d = table.shape[1]
    workers = SC_CORES * SC_SUBCORES
    per_w = n // workers
    nch = per_w // SC_WINDOW
    assert per_w * workers == n and nch * SC_WINDOW == per_w and nch % 2 == 0
    mesh = plsc.VectorSubcoreMesh(core_axis_name="core", subcore_axis_name="subcore",
                                  num_cores=SC_CORES, num_subcores=SC_SUBCORES)

    @functools.partial(
        pl.kernel, out_type=jax.ShapeDtypeStruct((n, d), table.dtype), mesh=mesh, name="sc_gather_rows",
        scratch_types=[pltpu.VMEM((per_w,), I32), pltpu.VMEM((2, SC_WINDOW, d), table.dtype),
                       pltpu.SemaphoreType.DMA((2,)), pltpu.SemaphoreType.DMA((2,))])
    def gather(table_hbm, idx_hbm, out_hbm, idx_v, rows_v, gsem, wsem):
        base = (lax.axis_index("subcore") * SC_CORES + lax.axis_index("core")) * per_w
        pltpu.sync_copy(idx_hbm.at[pl.ds(base, per_w)], idx_v)

        def fetch(c, slot):
            return pltpu.make_async_copy(table_hbm.at[idx_v.at[pl.ds(c * SC_WINDOW, SC_WINDOW)]],
                                         rows_v.at[slot], gsem.at[slot])

        def put(c, slot):
            return pltpu.make_async_copy(rows_v.at[slot], out_hbm.at[pl.ds(base + c * SC_WINDOW, SC_WINDOW)],
                                         wsem.at[slot])

        fetch(0, 0).start()

        @pl.loop(0, nch, step=2)
        def _(c0):
            for b in range(2):
                c = c0 + b
                fetch(c, b).wait()

                @pl.when(c + 1 < nch)
                def _():
                    @pl.when(c >= 1)
                    def _():
                        put(c - 1, 1 - b).wait()

                    fetch(c + 1, 1 - b).start()

                put(c, b).start()

        put(nch - 2, 0).wait()
        put(nch - 1, 1).wait()

    return gather(table, indices)


def _combine_kernel(yt_ref, wt_ref, res_ref, g_ref, b_ref, o_ref):
    acc = res_ref[...]
    for k in range(TOP_K):
        acc = acc + _unpack_bf16_pairs(yt_ref[k]) * wt_ref[:, k:k + 1]
    o_ref[...] = _layer_norm(acc, g_ref[...], b_ref[...])


def _combine(y_tok, wts_tok, res, ln_g, ln_b):
    t, d = res.shape
    tok = pl.BlockSpec((COMBINE_TILE, d), lambda i: (i, 0))
    return pl.pallas_call(
        _combine_kernel,
        grid=(t // COMBINE_TILE,),
        in_specs=[pl.BlockSpec((TOP_K, COMBINE_TILE, d // 2), lambda i: (0, i, 0)),
                  pl.BlockSpec((COMBINE_TILE, TOP_K), lambda i: (i, 0)),
                  tok, _const_spec(ln_g.shape), _const_spec(ln_b.shape)],
        out_specs=tok,
        out_shape=jax.ShapeDtypeStruct((t, d), F32),
        compiler_params=_params(("arbitrary",)),
        name="combine",
    )(y_tok, wts_tok, res, ln_g, ln_b)


def _prepare(l, w_in, conv_w, a_log, dt_bias, gdn_norm_g, w_a, sgu_ln_g, sgu_ln_b, sgu_w, sgu_b, w_b,
             w_mem_kv, w_c, w_o, ln1_g, ln1_b, router_w, router_bias, ws_gate, ws_up, ws_down, ln2_g, ln2_b):
    wi = w_in[l]
    d = D_MODEL
    bf = lambda a: a.astype(BF16)
    row = lambda a: a.reshape(1, -1).astype(F32)
    w_ab = jnp.zeros((d, LANES), F32).at[:, :2 * GDN_HEADS].set(wi[:, _C_AB:_C_UV])
    prm = jnp.zeros((SUBLANES, LANES), F32).at[0, :GDN_HEADS].set(a_log[l]).at[1, :GDN_HEADS].set(dt_bias[l])
    sgu_bias = jnp.repeat(sgu_b[l].T, SGU_WIDTH // SGU_GROUPS, axis=1)
    return {
        "w_qkv": bf(wi[:, _C_QKV:_C_Z]), "w_z": bf(wi[:, _C_Z:_C_AB]), "w_ab": bf(w_ab),
        "w_uv": bf(wi[:, _C_UV:_C_XQ]), "w_xq": bf(wi[:, _C_XQ:_C_GATE]),
        "w_g0": bf(wi[:, _C_GATE:_C_GATE + d]), "w_g1": bf(wi[:, _C_GATE + d:_C_GATE + 2 * d]),
        "w_g2": bf(wi[:, _C_GATE + 2 * d:_C_GATE + 3 * d]),
        "conv_w": conv_w[l].astype(F32), "gdn_prm": prm, "gdn_norm_g": row(gdn_norm_g[l]),
        "w_a": bf(w_a[l]), "sgu_ln_g": row(sgu_ln_g[l]), "sgu_ln_b": row(sgu_ln_b[l]),
        "sgu_w": sgu_w[l].astype(F32), "sgu_bias": sgu_bias.astype(F32), "w_b": bf(w_b[l]),
        "w_mem_kv": bf(w_mem_kv[l]), "w_c": bf(w_c[l]), "w_o": bf(w_o[l]),
        "ln1_g": row(ln1_g[l]), "ln1_b": row(ln1_b[l]),
        "router_wt": bf(router_w[l].T), "router_bias": router_bias[l].reshape(-1, 1).astype(F32),
        "ws_gu": bf(jnp.concatenate([ws_gate[l], ws_up[l]], axis=1)), "ws_down": bf(ws_down[l]),
        "ln2_g": row(ln2_g[l]), "ln2_b": row(ln2_b[l]),
    }


def _layer(x, mem, wts, w_gate, w_up, w_down):
    bsz, s, d = x.shape
    t = bsz * s
    k, v = _mem_kv(mem, wts["w_mem_kv"])
    mbc = _mix_bc(x, wts, k, v)
    og = _gdn(x, wts)
    res, hp, idx, rw, rank, cnt = _merge(x, og, mbc, wts)
    nb = t * TOP_K // EXPERT_BLOCK + N_EXPERTS
    pos, win_slots, meta = _plan(idx, rank, cnt)
    xs = _sc_scatter_rows(hp, win_slots, nb * EXPERT_BLOCK)
    y = _experts(xs, meta[:, 0], meta[:, 1], meta[:, 2], meta[:1, 3], w_gate, w_up, w_down)
    y_tok = _sc_gather_rows(y, pos.reshape(-1)).reshape(TOP_K, t, d // 2)
    out = _combine(y_tok, rw.T, res.reshape(t, d), wts["ln2_g"], wts["ln2_b"])
    return out.reshape(bsz, s, d)


def kernel(x, mem, w_in, conv_w, a_log, dt_bias, gdn_norm_g, w_a, sgu_ln_g, sgu_ln_b, sgu_w, sgu_b, w_b,
           w_mem_kv, w_c, w_o, ln1_g, ln1_b, router_w, router_bias, w_gate, w_up, w_down, ws_gate, ws_up,
           ws_down, ln2_g, ln2_b):
    assert x.shape[1] % TILE == 0 and x.shape[2] == D_MODEL
    for l in range(DEPTH):
        wts = _prepare(l, w_in, conv_w, a_log, dt_bias, gdn_norm_g, w_a, sgu_ln_g, sgu_ln_b, sgu_w, sgu_b,
                       w_b, w_mem_kv, w_c, w_o, ln1_g, ln1_b, router_w, router_bias, ws_gate, ws_up,
                       ws_down, ln2_g, ln2_b)
        x = _layer(x, mem, wts, w_gate[l], w_up[l], w_down[l])
    return x
```

```python
import functools

import jax
import jax.numpy as jnp
from jax import lax
from jax.experimental import pallas as pl
from jax.experimental.pallas import tpu as pltpu
from jax.experimental.pallas import tpu_sc as plsc

F32 = jnp.float32
BF16 = jnp.bfloat16
I32 = jnp.int32
U32 = jnp.uint32

D_MODEL = 1024
DEPTH = 1
GDN_HEADS = 4
HEAD_DIM = 128
GDN_WIDTH = GDN_HEADS * HEAD_DIM
CONV_WIDTH = 4
SGU_GROUPS = 4
SGU_WIDTH = 512
SGU_CHUNK = 128
SGU_GROUP_DIM = SGU_WIDTH // SGU_GROUPS
N_MEM = 256
XATTN_HEADS = 4
XATTN_WIDTH = 512
XATTN_HEAD_DIM = XATTN_WIDTH // XATTN_HEADS
N_EXPERTS = 256
TOP_K = 8
N_GROUPS = 8
GROUP_SIZE = N_EXPERTS // N_GROUPS
TOPK_GROUPS = 4
D_EXPERT = 256
D_SHARED = 256
ROUTED_SCALE = 2.5
LN_EPS = 1e-5
RMS_EPS = 1e-6
DEEPNORM_ALPHA = (2.0 * DEPTH) ** 0.25

_C_QKV = 0
_C_Z = 3 * GDN_WIDTH
_C_AB = 4 * GDN_WIDTH
_C_UV = _C_AB + 2 * GDN_HEADS
_C_XQ = _C_UV + 2 * SGU_WIDTH
_C_GATE = _C_XQ + XATTN_WIDTH

LANES = 128
SUBLANES = 8
CHUNK = 128
GDN_GROUP = 2
TILE = 512
HALO = SUBLANES
EXPERT_BLOCK = 288
X_SLOTS = 8
X_AHEAD = X_SLOTS - 2
Y_SLOTS = 4
W_SLOTS = 3
SC_CORES = 2
SC_SUBCORES = 16
SC_WINDOW = 64
COMBINE_TILE = 512
VMEM_LIMIT = 56 * 1024 * 1024


def _dot(a, b):
    return jnp.dot(a, b, preferred_element_type=F32)


def _dot_nt(a, b):
    return lax.dot_general(a, b, (((1,), (1,)), ((), ())), preferred_element_type=F32)


def _split(a):
    hi = a.astype(BF16)
    lo = (a - hi.astype(F32)).astype(BF16)
    return hi, lo


def _dot3_parts(ah, al, bh, bl):
    return _dot(jnp.concatenate([ah, ah, al], axis=1), jnp.concatenate([bh, bl, bh], axis=0))


def _dot3(a, b):
    return _dot3_parts(*_split(a), *_split(b))


def _softplus(x):
    return jnp.maximum(x, 0.0) + jnp.log1p(jnp.exp(-jnp.abs(x)))


def _layer_norm(x, g, b):
    mu = jnp.mean(x, axis=-1, keepdims=True)
    xc = x - mu
    var = jnp.mean(xc * xc, axis=-1, keepdims=True)
    return xc * lax.rsqrt(var + LN_EPS) * g + b


def _const_spec(shape):
    nd = len(shape)
    return pl.BlockSpec(shape, lambda *_: (0,) * nd)


def _params(semantics):
    return pltpu.CompilerParams(dimension_semantics=semantics, vmem_limit_bytes=VMEM_LIMIT)


def _mem_kv_kernel(mem_ref, w_ref, k_ref, v_ref):
    kv = _dot(mem_ref[0].astype(BF16), w_ref[...])
    k_ref[0] = kv[:, :XATTN_WIDTH].astype(BF16)
    v_ref[0] = kv[:, XATTN_WIDTH:].astype(BF16)


def _mem_kv(mem, w_kv):
    bsz, n_mem, d = mem.shape
    out = jax.ShapeDtypeStruct((bsz, n_mem, XATTN_WIDTH), BF16)
    return pl.pallas_call(
        _mem_kv_kernel,
        grid=(bsz,),
        in_specs=[pl.BlockSpec((1, n_mem, d), lambda b: (b, 0, 0)), _const_spec(w_kv.shape)],
        out_specs=[pl.BlockSpec((1, n_mem, XATTN_WIDTH), lambda b: (b, 0, 0))] * 2,
        out_shape=[out, out],
        compiler_params=_params(("arbitrary",)),
        name="mem_kv",
    )(mem, w_kv)


def _mix_bc_kernel(x_ref, wuv_ref, wg1_ref, wb_ref, lng_ref, lnb_ref, sw_ref, sb_ref,
                   wxq_ref, k_ref, v_ref, wg2_ref, wc_ref, o_ref):
    xb = x_ref[0].astype(BF16)
    uv = _dot(xb, wuv_ref[...])
    xq = _dot(xb, wxq_ref[...])
    gate_b = _dot(xb, wg1_ref[...])
    gate_c = _dot(xb, wg2_ref[...])
    scores = []
    for h in range(XATTN_HEADS):
        qh = xq[:, h * XATTN_HEAD_DIM:(h + 1) * XATTN_HEAD_DIM].astype(BF16)
        kh = k_ref[0, :, h * XATTN_HEAD_DIM:(h + 1) * XATTN_HEAD_DIM]
        scores.append(_dot_nt(qh, kh) * (XATTN_HEAD_DIM ** -0.5))
    uv = jax.nn.gelu(uv, approximate=True)
    u = uv[:, :SGU_WIDTH]
    v = _layer_norm(uv[:, SGU_WIDTH:], lng_ref[...], lnb_ref[...])
    ii = lax.broadcasted_iota(I32, (SGU_CHUNK, SGU_CHUNK), 0)
    jj = lax.broadcasted_iota(I32, (SGU_CHUNK, SGU_CHUNK), 1)
    rows = []
    for n in range(TILE // SGU_CHUNK):
        cols = []
        for g in range(SGU_GROUPS):
            wt = jnp.where(ii >= jj, sw_ref[g], 0.0).astype(BF16)
            vg = v[n * SGU_CHUNK:(n + 1) * SGU_CHUNK, g * SGU_GROUP_DIM:(g + 1) * SGU_GROUP_DIM].astype(BF16)
            cols.append(_dot(wt, vg))
        rows.append(jnp.concatenate(cols, axis=1) + sb_ref[...])
    sg = jnp.concatenate(rows, axis=0)
    br_b = _dot((u * sg).astype(BF16), wb_ref[...])
    acc = jax.nn.sigmoid(gate_b) * br_b
    heads = []
    for h, sc in enumerate(scores):
        vh = v_ref[0, :, h * XATTN_HEAD_DIM:(h + 1) * XATTN_HEAD_DIM]
        p = jnp.exp(sc - jnp.max(sc, axis=-1, keepdims=True))
        p = p / jnp.sum(p, axis=-1, keepdims=True)
        heads.append(_dot(p.astype(BF16), vh))
    att = jnp.concatenate(heads, axis=1).astype(BF16)
    br_c = _dot(att, wc_ref[...])
    o_ref[0] = acc + jax.nn.sigmoid(gate_c) * br_c


def _mix_bc(x, wts, k, v):
    bsz, s, d = x.shape
    tok = pl.BlockSpec((1, TILE, d), lambda b, i: (b, i, 0))
    kvs = pl.BlockSpec((1, N_MEM, XATTN_WIDTH), lambda b, i: (b, 0, 0))
    names = ("w_uv", "w_g1", "w_b", "sgu_ln_g", "sgu_ln_b", "sgu_w", "sgu_bias", "w_xq")
    consts = [wts[n] for n in names]
    tail = [wts["w_g2"], wts["w_c"]]
    return pl.pallas_call(
        _mix_bc_kernel,
        grid=(bsz, s // TILE),
        in_specs=[tok] + [_const_spec(c.shape) for c in consts] + [kvs, kvs]
        + [_const_spec(c.shape) for c in tail],
        out_specs=tok,
        out_shape=jax.ShapeDtypeStruct((bsz, s, d), F32),
        compiler_params=_params(("arbitrary", "arbitrary")),
        name="mix_bc",
    )(x, *consts, k, v, *tail)


def _unit_lower_inverses(lmats, ii, jj):
    eye = (ii == jj).astype(F32)
    x = ii ^ jj
    lsplit = [_split(l) for l in lmats]
    ts = [eye - jnp.where(x < 2, l, 0.0) for l in lmats]
    zero = jnp.zeros((CHUNK, CHUNK), BF16)
    s = 2
    while s < CHUNK:
        off = (x < 2 * s) & ((ii & s) != 0) & ((jj & s) == 0)
        tsplit = [_split(t) for t in ts]
        prods = []
        for (th, tl), (lh, ll) in zip(tsplit, lsplit):
            oh = jnp.where(off, lh, zero)
            ol = jnp.where(off, ll, zero)
            prods.append(_dot3_parts(th, tl, oh, ol))
        nxt = []
        for t, p, (th, tl) in zip(ts, prods, tsplit):
            ph, plo = _split(p)
            nxt.append(t - _dot3_parts(ph, plo, th, tl))
        ts = nxt
        s *= 2
    return ts


def _gdn_kernel(x_ref, wqkv_ref, wz_ref, wab_ref, convw_ref, prm_ref, gn_ref, o_ref, hist_ref, state_ref):
    @pl.when(pl.program_id(1) == 0)
    def _():
        hist_ref[0:HALO, :] = jnp.zeros((HALO, 3 * GDN_WIDTH), F32)
        state_ref[...] = jnp.zeros(state_ref.shape, F32)

    xb = x_ref[0].astype(BF16)
    ab = _dot(xb, wab_ref[...])
    parts = []
    for p in range(3):
        cols = slice(p * GDN_WIDTH, (p + 1) * GDN_WIDTH)
        pre = _dot(xb, wqkv_ref[:, cols])
        hist_ref[HALO:HALO + TILE, cols] = pre
        conv = pre * convw_ref[CONV_WIDTH - 1:CONV_WIDTH, cols]
        for j in range(CONV_WIDTH - 1):
            shift = CONV_WIDTH - 1 - j
            conv = conv + hist_ref[HALO - shift:HALO - shift + TILE, cols] * convw_ref[j:j + 1, cols]
        hist_ref[0:HALO, cols] = pre[TILE - HALO:, :]
        parts.append(conv * jax.nn.sigmoid(conv))
    qkv = jnp.concatenate(parts, axis=1)
    z = _dot(xb, wz_ref[...])
    g = -jnp.exp(prm_ref[0:1, :]) * _softplus(ab + prm_ref[1:2, :])
    beta = jax.nn.sigmoid(ab)

    ii = lax.broadcasted_iota(I32, (CHUNK, CHUNK), 0)
    jj = lax.broadcasted_iota(I32, (CHUNK, CHUNK), 1)
    tri = (ii >= jj).astype(BF16)
    def chunk_items(c):
        items = []
        r0 = c * CHUNK
        gch = g[r0:r0 + CHUNK, :]
        g1 = gch.astype(BF16)
        r1 = gch - g1.astype(F32)
        g2 = r1.astype(BF16)
        g3 = (r1 - g2.astype(F32)).astype(BF16)
        gc = _dot(jnp.concatenate([tri, tri, tri], axis=1),
                  jnp.concatenate([g1, g2, g3], axis=0))
        gct = gc.T
        for h in range(GDN_HEADS):
            qh = qkv[r0:r0 + CHUNK, h * HEAD_DIM:(h + 1) * HEAD_DIM]
            kh = qkv[r0:r0 + CHUNK, GDN_WIDTH + h * HEAD_DIM:GDN_WIDTH + (h + 1) * HEAD_DIM]
            vh = qkv[r0:r0 + CHUNK, 2 * GDN_WIDTH + h * HEAD_DIM:2 * GDN_WIDTH + (h + 1) * HEAD_DIM]
            qn = qh * lax.rsqrt(jnp.sum(qh * qh, axis=-1, keepdims=True) + RMS_EPS) * (HEAD_DIM ** -0.5)
            kn = kh * lax.rsqrt(jnp.sum(kh * kh, axis=-1, keepdims=True) + RMS_EPS)
            gcol = gc[:, h:h + 1]
            grow = gct[h:h + 1, :]
            bcol = beta[r0:r0 + CHUNK, GDN_HEADS + h:GDN_HEADS + h + 1]
            glast = gc[CHUNK - 1:CHUNK, h:h + 1]
            decay = jnp.exp(jnp.where(ii >= jj, gcol - grow, -jnp.inf))
            knb = kn.astype(BF16)
            kk = _dot_nt(knb, knb)
            qk = _dot_nt(qn.astype(BF16), knb)
            egc = jnp.exp(gcol)
            items.append(dict(
                c=c, h=h, lmat=jnp.where(ii > jj, kk * bcol * decay, 0.0),
                rhs=jnp.concatenate([vh * bcol, kn * (bcol * egc)], axis=1),
                qd=(qn * egc).astype(BF16), kdt=(kn * jnp.exp(glast - gcol)).T.astype(BF16),
                a=(qk * decay).astype(BF16), gl=jnp.exp(glast)))
        return items

    states = [state_ref[h] for h in range(GDN_HEADS)]
    for c0 in range(0, TILE // CHUNK, GDN_GROUP):
        items = [it for c in range(c0, c0 + GDN_GROUP) for it in chunk_items(c)]
        tinvs = _unit_lower_inverses([it["lmat"] for it in items], ii, jj)
        sols = [_dot3(tinv, it["rhs"]) for tinv, it in zip(tinvs, items)]
        for it, sol in zip(items, sols):
            h = it["h"]
            rows = slice(it["c"] * CHUNK, (it["c"] + 1) * CHUNK)
            lanes = slice(h * HEAD_DIM, (h + 1) * HEAD_DIM)
            st = states[h]
            wq = jnp.concatenate([sol[:, HEAD_DIM:].astype(BF16), it["qd"]], axis=0)
            ws = _dot(wq, st.astype(BF16))
            vb = (sol[:, :HEAD_DIM] - ws[:CHUNK]).astype(BF16)
            o = ws[CHUNK:] + _dot(it["a"], vb)
            states[h] = st * it["gl"] + _dot(it["kdt"], vb)
            zz = z[rows, lanes]
            o = o * lax.rsqrt(jnp.mean(o * o, axis=-1, keepdims=True) + RMS_EPS) * gn_ref[...]
            o_ref[0, rows, lanes] = (o * (zz * jax.nn.sigmoid(zz))).astype(BF16)
    for h in range(GDN_HEADS):
        state_ref[h] = states[h]


def _gdn(x, wts):
    bsz, s, d = x.shape
    tok = pl.BlockSpec((1, TILE, d), lambda b, i: (b, i, 0))
    hw = pl.BlockSpec((1, TILE, GDN_WIDTH), lambda b, i: (b, i, 0))
    consts = [wts[n] for n in ("w_qkv", "w_z", "w_ab", "conv_w", "gdn_prm", "gdn_norm_g")]
    return pl.pallas_call(
        _gdn_kernel,
        grid=(bsz, s // TILE),
        in_specs=[tok] + [_const_spec(c.shape) for c in consts],
        out_specs=hw,
        out_shape=jax.ShapeDtypeStruct((bsz, s, GDN_WIDTH), BF16),
        scratch_shapes=[pltpu.VMEM((HALO + TILE, 3 * GDN_WIDTH), F32),
                        pltpu.VMEM((GDN_HEADS, HEAD_DIM, HEAD_DIM), F32)],
        compiler_params=_params(("arbitrary", "arbitrary")),
        name="gdn",
    )(x, *consts)


def _merge_kernel(x_ref, og_ref, mbc_ref, wa_ref, wg0_ref, wo_ref, ln1g_ref, ln1b_ref,
                  rwt_ref, rb_ref, wsgu_ref, wsd_ref,
                  res_ref, hp_ref, idx_ref, wt_ref, rank_ref, cnt_ref, carry_ref):
    first = (pl.program_id(0) == 0) & (pl.program_id(1) == 0)

    @pl.when(first)
    def _():
        carry_ref[...] = jnp.zeros(carry_ref.shape, F32)

    x = x_ref[0]
    xb = x.astype(BF16)
    br_a = _dot(og_ref[0], wa_ref[...])
    merged = jax.nn.sigmoid(_dot(xb, wg0_ref[...])) * br_a + mbc_ref[0]
    y = _dot(merged.astype(BF16), wo_ref[...])
    h = _layer_norm(DEEPNORM_ALPHA * x + y, ln1g_ref[...], ln1b_ref[...])
    hb = h.astype(BF16)
    hp_ref[...] = _pack_bf16_pairs(h)

    gu = _dot(hb, wsgu_ref[...])
    gt = gu[:, :D_SHARED]
    act = gt * jax.nn.sigmoid(gt) * gu[:, D_SHARED:]
    res_ref[0] = DEEPNORM_ALPHA * h + _dot(act.astype(BF16), wsd_ref[...])

    scores = jax.nn.sigmoid(_dot_nt(rwt_ref[...], hb))
    biased = scores + rb_ref[...]
    g3 = biased.reshape(N_GROUPS, GROUP_SIZE, TILE)
    m1 = jnp.max(g3, axis=1)
    m1b = m1[:, None, :]
    n_top = jnp.sum((g3 == m1b).astype(F32), axis=1)
    m2 = jnp.max(jnp.where(g3 < m1b, g3, -jnp.inf), axis=1)
    gs = m1 + jnp.where(n_top >= 2.0, m1, m2)
    gidx = lax.broadcasted_iota(I32, (N_GROUPS, TILE), 0)
    beaten = jnp.zeros((N_GROUPS, TILE), F32)
    for g in range(N_GROUPS):
        row = gs[g:g + 1, :]
        beaten = beaten + ((row > gs) | ((row == gs) & (g < gidx))).astype(F32)
    sel = (beaten < float(TOPK_GROUPS)).astype(F32)
    sel_e = jnp.broadcast_to(sel[:, None, :], (N_GROUPS, GROUP_SIZE, TILE)).reshape(N_EXPERTS, TILE)
    masked = jnp.where(sel_e > 0.5, biased, -jnp.inf)
    eidx = lax.broadcasted_iota(I32, (N_EXPERTS, TILE), 0)
    chosen = jnp.zeros((N_EXPERTS, TILE), F32)
    picks, pick_scores = [], []
    for _k in range(TOP_K):
        mx = jnp.max(masked, axis=0, keepdims=True)
        pick = jnp.min(jnp.where(masked == mx, eidx, N_EXPERTS), axis=0, keepdims=True)
        hot = eidx == pick
        picks.append(pick)
        pick_scores.append(jnp.sum(jnp.where(hot, scores, 0.0), axis=0, keepdims=True))
        chosen = jnp.where(hot, 1.0, chosen)
        masked = jnp.where(hot, -jnp.inf, masked)
    total = pick_scores[0]
    for sck in pick_scores[1:]:
        total = total + sck
    idx_ref[...] = jnp.concatenate(picks, axis=0)
    wt_ref[...] = jnp.concatenate([sck / total * ROUTED_SCALE for sck in pick_scores], axis=0)

    ti = lax.broadcasted_iota(I32, (TILE, TILE), 0)
    tj = lax.broadcasted_iota(I32, (TILE, TILE), 1)
    before = _dot(chosen.astype(BF16), (ti < tj).astype(BF16))
    carry = carry_ref[...]
    before = before + jnp.concatenate([carry] * (TILE // LANES), axis=1)
    rank_ref[...] = jnp.concatenate(
        [jnp.sum(jnp.where(eidx == p, before, 0.0), axis=0, keepdims=True) for p in picks],
        axis=0).astype(I32)
    carry = carry + _dot(chosen.astype(BF16), jnp.ones((TILE, LANES), BF16))
    carry_ref[...] = carry
    cnt_ref[...] = carry.astype(I32)


def _merge(x, og, mbc, wts):
    bsz, s, d = x.shape
    t = bsz * s
    nt = s // TILE
    tok = pl.BlockSpec((1, TILE, d), lambda b, i: (b, i, 0))
    ogs = pl.BlockSpec((1, TILE, GDN_WIDTH), lambda b, i: (b, i, 0))
    flat = lambda b, i: (b * nt + i, 0)
    lane = lambda b, i: (0, b * nt + i)
    consts = [wts[n] for n in ("w_a", "w_g0", "w_o", "ln1_g", "ln1_b", "router_wt", "router_bias",
                               "ws_gu", "ws_down")]
    return pl.pallas_call(
        _merge_kernel,
        grid=(bsz, nt),
        in_specs=[tok, ogs, tok] + [_const_spec(c.shape) for c in consts],
        out_specs=[tok,
                   pl.BlockSpec((TILE, d // 2), flat),
                   pl.BlockSpec((TOP_K, TILE), lane),
                   pl.BlockSpec((TOP_K, TILE), lane),
                   pl.BlockSpec((TOP_K, TILE), lane),
                   _const_spec((N_EXPERTS, LANES))],
        out_shape=[jax.ShapeDtypeStruct((bsz, s, d), F32),
                   jax.ShapeDtypeStruct((t, d // 2), U32),
                   jax.ShapeDtypeStruct((TOP_K, t), I32),
                   jax.ShapeDtypeStruct((TOP_K, t), F32),
                   jax.ShapeDtypeStruct((TOP_K, t), I32),
                   jax.ShapeDtypeStruct((N_EXPERTS, LANES), I32)],
        scratch_shapes=[pltpu.VMEM((N_EXPERTS, LANES), F32)],
        compiler_params=_params(("arbitrary", "arbitrary")),
        name="merge",
    )(x, og, mbc, *consts)


def _plan_kernel(idx_ref, rank_ref, cnt_ref, pos_ref, win_ref, meta_ref):
    e = N_EXPERTS
    cnt = cnt_ref[...]
    padded = cnt + (EXPERT_BLOCK - 1)
    nblk = jnp.floor(padded.astype(F32) * (1.0 / EXPERT_BLOCK)).astype(I32)
    nblk = jnp.where(nblk * EXPERT_BLOCK > padded, nblk - 1, nblk)
    nblk = jnp.where((nblk + 1) * EXPERT_BLOCK <= padded, nblk + 1, nblk)
    ei = lax.broadcasted_iota(I32, (e, e), 0)
    ej = lax.broadcasted_iota(I32, (e, e), 1)
    incl = _dot((ei >= ej).astype(BF16), nblk.astype(F32).astype(BF16)).astype(I32)
    excl = incl - nblk
    start = (excl * EXPERT_BLOCK).astype(F32)
    lane = lax.broadcasted_iota(I32, (e, LANES), 1)
    n_used = jnp.broadcast_to(incl[e - 1:e, :], (e, LANES))
    meta_ref[...] = jnp.where(lane == 0, excl, jnp.where(lane == 1, nblk, jnp.where(lane == 2, cnt, n_used)))

    e128 = lax.broadcasted_iota(I32, (e, LANES), 0)

    def body(i, carry):
        off = pl.multiple_of(i * LANES, LANES)
        ids = idx_ref[:, pl.ds(off, LANES)]
        rows = [jnp.sum(jnp.where(e128 == ids[k:k + 1, :], start, 0.0), axis=0, keepdims=True)
                for k in range(TOP_K)]
        p = jnp.concatenate(rows, axis=0).astype(I32) + rank_ref[:, pl.ds(off, LANES)]
        pos_ref[:, pl.ds(off, LANES)] = p
        for half in range(LANES // SC_WINDOW):
            row0 = pl.multiple_of((i * (LANES // SC_WINDOW) + half) * TOP_K, TOP_K)
            win_ref[pl.ds(row0, TOP_K), :] = p[:, half * SC_WINDOW:(half + 1) * SC_WINDOW]
        return carry

    lax.fori_loop(0, idx_ref.shape[1] // LANES, body, 0)


def _plan(idx, rank, cnt):
    t = idx.shape[1]
    return pl.pallas_call(
        _plan_kernel,
        grid=(1,),
        in_specs=[_const_spec(idx.shape), _const_spec(rank.shape), _const_spec(cnt.shape)],
        out_specs=[_const_spec((TOP_K, t)), _const_spec((t // SC_WINDOW * TOP_K, SC_WINDOW)),
                   _const_spec((N_EXPERTS, LANES))],
        out_shape=[jax.ShapeDtypeStruct((TOP_K, t), I32),
                   jax.ShapeDtypeStruct((t // SC_WINDOW * TOP_K, SC_WINDOW), I32),
                   jax.ShapeDtypeStruct((N_EXPERTS, LANES), I32)],
        compiler_params=_params(("arbitrary",)),
        name="plan",
    )(idx, rank, cnt)


def _sc_scatter_rows(rows, win_slots, n_slots):
    t, d = rows.shape
    workers = SC_CORES * SC_SUBCORES
    per_w = t // workers
    nwin = per_w // SC_WINDOW
    assert per_w * workers == t and nwin * SC_WINDOW == per_w and nwin % 2 == 0
    idx = win_slots.reshape(workers, nwin * TOP_K, SC_WINDOW)
    mesh = plsc.VectorSubcoreMesh(core_axis_name="core", subcore_axis_name="subcore",
                                  num_cores=SC_CORES, num_subcores=SC_SUBCORES)

    @functools.partial(
        pl.kernel, out_type=jax.ShapeDtypeStruct((n_slots, d), rows.dtype), mesh=mesh, name="sc_scatter_rows",
        scratch_types=[pltpu.VMEM((nwin * TOP_K, SC_WINDOW), I32), pltpu.VMEM((2, SC_WINDOW, d), rows.dtype),
                       pltpu.SemaphoreType.DMA((2,)), pltpu.SemaphoreType.DMA((2,))])
    def scatter(rows_hbm, idx_hbm, out_hbm, idx_v, rows_v, lsem, ssem):
        wid = lax.axis_index("subcore") * SC_CORES + lax.axis_index("core")
        base = wid * per_w
        pltpu.sync_copy(idx_hbm.at[wid], idx_v)

        def load(w, slot):
            return pltpu.make_async_copy(rows_hbm.at[pl.ds(base + w * SC_WINDOW, SC_WINDOW)], rows_v.at[slot],
                                         lsem.at[slot])

        def send(w, k, slot):
            return pltpu.make_async_copy(rows_v.at[slot], out_hbm.at[idx_v.at[w * TOP_K + k]], ssem.at[slot])

        load(0, 0).start()

        @pl.loop(0, nwin, step=2)
        def _(w0):
            for b in range(2):
                w = w0 + b
                load(w, b).wait()

                @pl.when(w + 1 < nwin)
                def _():
                    @pl.when(w >= 1)
                    def _():
                        for k in range(TOP_K):
                            send(w - 1, k, 1 - b).wait()

                    load(w + 1, 1 - b).start()

                for k in range(TOP_K):
                    send(w, k, b).start()

        for k in range(TOP_K):
            send(nwin - 2, k, 0).wait()
        for k in range(TOP_K):
            send(nwin - 1, k, 1).wait()

    return scatter(rows, idx)


def _pack_bf16_pairs(a):
    n = a.shape[1] // 2
    bits = lax.bitcast_convert_type(a.astype(BF16).astype(F32), U32)
    return (bits[:, :n] >> 16) | (bits[:, n:] & jnp.uint32(0xFFFF0000))


def _unpack_bf16_pairs(words):
    lo = lax.bitcast_convert_type(words << 16, F32)
    hi = lax.bitcast_convert_type(words & jnp.uint32(0xFFFF0000), F32)
    return jnp.concatenate([lo, hi], axis=1)


def _experts_kernel(fb_ref, nb_ref, cnt_ref, nu_ref, xs_ref, wg_ref, wu_ref, wd_ref, y_ref,
                    wgbuf, wubuf, wdbuf, wsem, wgu_s, wd_s, xbuf, ybuf, xsem, ysem):
    e = pl.program_id(0)
    n_used = nu_ref[0]
    bm = EXPERT_BLOCK
    row_queue = 1

    def x_copy(g, slot):
        return pltpu.make_async_copy(xs_ref.at[pl.ds(g * bm, bm), :], xbuf.at[slot], xsem.at[slot])

    def y_copy(g, slot):
        return pltpu.make_async_copy(ybuf.at[slot], y_ref.at[pl.ds(g * bm, bm), :], ysem.at[slot])

    @pl.when(e == 0)
    def _():
        for g0 in range(X_AHEAD):
            @pl.when(g0 < n_used)
            def _():
                x_copy(g0, g0).start(priority=row_queue)

    def w_copies(j, slot):
        return [pltpu.make_async_copy(src.at[j], dst.at[slot], wsem.at[slot])
                for src, dst in ((wg_ref, wgbuf), (wu_ref, wubuf), (wd_ref, wdbuf))]

    @pl.when(e == 0)
    def _():
        for j in range(W_SLOTS - 1):
            for cp in w_copies(j, j):
                cp.start()

    wslot = lax.rem(e, W_SLOTS)
    for cp in w_copies(e, wslot):
        cp.wait()
    ahead_e = e + (W_SLOTS - 1)

    @pl.when(ahead_e < pl.num_programs(0))
    def _():
        for cp in w_copies(ahead_e, lax.rem(ahead_e, W_SLOTS)):
            cp.start()

    nb = nb_ref[e]

    @pl.when(nb > 0)
    def _():
        wgu_s[:, :D_EXPERT] = wgbuf[wslot].astype(BF16)
        wgu_s[:, D_EXPERT:] = wubuf[wslot].astype(BF16)
        wd_s[...] = wdbuf[wslot].astype(BF16)

    def process(b0, width):
        g0 = fb_ref[e] + b0
        xs = []
        for i in range(width):
            g = g0 + i
            x_copy(g, g & (X_SLOTS - 1)).wait()
            xs.append(xbuf[g & (X_SLOTS - 1)])
        for i in range(width):
            ahead = g0 + X_AHEAD + i

            @pl.when(ahead < n_used)
            def _():
                x_copy(ahead, ahead & (X_SLOTS - 1)).start(priority=row_queue)

        ys = []
        for i in range(width):
            x = _unpack_bf16_pairs(xs[i])
            live = lax.broadcasted_iota(I32, (bm, 1), 0) < cnt_ref[e] - (b0 + i) * bm
            xb = jnp.where(live, x, 0.0).astype(BF16)
            gu = _dot(xb, wgu_s[...])
            gt = gu[:, :D_EXPERT]
            act = (gt * jax.nn.sigmoid(gt) * gu[:, D_EXPERT:]).astype(BF16)
            ys.append(_pack_bf16_pairs(_dot(act, wd_s[...])))
        for i in range(width):
            g = g0 + i
            slot = g & (Y_SLOTS - 1)

            @pl.when(g >= Y_SLOTS)
            def _():
                y_copy(g - Y_SLOTS, slot).wait()

            ybuf[slot] = ys[i]
            y_copy(g, slot).start(priority=row_queue)

    def pair(p, carry):
        process(2 * p, 2)
        return carry

    lax.fori_loop(0, nb >> 1, pair, 0)

    @pl.when((nb & 1) == 1)
    def _():
        process(nb - 1, 1)

    @pl.when(e == pl.num_programs(0) - 1)
    def _():
        for back in range(Y_SLOTS, 0, -1):
            @pl.when(n_used >= back)
            def _():
                y_copy(n_used - back, (n_used - back) & (Y_SLOTS - 1)).wait()


def _experts(xs, first_block, n_blocks, counts, n_used, w_gate, w_up, w_down):
    n_slots, half = xs.shape
    d = 2 * half
    grid_spec = pltpu.PrefetchScalarGridSpec(
        num_scalar_prefetch=4,
        grid=(N_EXPERTS,),
        in_specs=[pl.BlockSpec(memory_space=pl.ANY)] * 4,
        out_specs=pl.BlockSpec(memory_space=pl.ANY),
        scratch_shapes=[pltpu.VMEM((W_SLOTS, d, D_EXPERT), F32), pltpu.VMEM((W_SLOTS, d, D_EXPERT), F32),
                        pltpu.VMEM((W_SLOTS, D_EXPERT, d), F32), pltpu.SemaphoreType.DMA((W_SLOTS,)),
                        pltpu.VMEM((d, 2 * D_EXPERT), BF16), pltpu.VMEM((D_EXPERT, d), BF16),
                        pltpu.VMEM((X_SLOTS, EXPERT_BLOCK, half), U32),
                        pltpu.VMEM((Y_SLOTS, EXPERT_BLOCK, half), U32),
                        pltpu.SemaphoreType.DMA((X_SLOTS,)), pltpu.SemaphoreType.DMA((Y_SLOTS,))],
    )
    return pl.pallas_call(
        _experts_kernel,
        grid_spec=grid_spec,
        out_shape=jax.ShapeDtypeStruct((n_slots, half), U32),
        compiler_params=_params(("arbitrary",)),
        name="experts",
    )(first_block, n_blocks, counts, n_used, xs, w_gate, w_up, w_down)


def _sc_gather_rows(table, indices):
    n = indices.shape[0]
    d = table.shape[1]
    workers = SC_CORES * SC_SUBCORES
    per_w = n // workers
    nch = per_w // SC_WINDOW
    assert per_w * workers == n and nch * SC_WINDOW == per_w and nch % 2 == 0
    mesh = plsc.VectorSubcoreMesh(core_axis_name="core", subcore_axis_name="subcore",
                                  num_cores=SC_CORES, num_subcores=SC_SUBCORES)

    @functools.partial(
        pl.kernel, out_type=jax.ShapeDtypeStruct((n, d), table.dtype), mesh=mesh, name="sc_gather_rows",
        scratch_types=[pltpu.VMEM((per_w,), I32), pltpu.VMEM((2, SC_WINDOW, d), table.dtype),
                       pltpu.SemaphoreType.DMA((2,)), pltpu.SemaphoreType.DMA((2,))])
    def gather(table_hbm, idx_hbm, out_hbm, idx_v, rows_v, gsem, wsem):
        base = (lax.axis_index("subcore") * SC_CORES + lax.axis_index("core")) * per_w
        pltpu.sync_copy(idx_hbm.at[pl.ds(base, per_w)], idx_v)

        def fetch(c, slot):
            return pltpu.make_async_copy(table_hbm.at[idx_v.at[pl.ds(c * SC_WINDOW, SC_WINDOW)]],
                                         rows_v.at[slot], gsem.at[slot])

        def put(c, slot):
            return pltpu.make_async_copy(rows_v.at[slot], out_hbm.at[pl.ds(base + c * SC_WINDOW, SC_WINDOW)],
                                         wsem.at[slot])

        fetch(0, 0).start()

        @pl.loop(0, nch, step=2)
        def _(c0):
            for b in range(2):
                c = c0 + b
                fetch(c, b).wait()

                @pl.when(c + 1 < nch)
                def _():
                    @pl.when(c >= 1)
                    def _():
                        put(c - 1, 1 - b).wait()

                    fetch(c + 1, 1 - b).start()

                put(c, b).start()

        put(nch - 2, 0).wait()
        put(nch - 1, 1).wait()

    return gather(table, indices)


def _combine_kernel(yt_ref, wt_ref, res_ref, g_ref, b_ref, o_ref):
    acc = res_ref[...]
    for k in range(TOP_K):
        acc = acc + _unpack_bf16_pairs(yt_ref[k]) * wt_ref[:, k:k + 1]
    o_ref[...] = _layer_norm(acc, g_ref[...], b_ref[...])


def _combine(y_tok, wts_tok, res, ln_g, ln_b):
    t, d = res.shape
    tok = pl.BlockSpec((COMBINE_TILE, d), lambda i: (i, 0))
    return pl.pallas_call(
        _combine_kernel,
        grid=(t // COMBINE_TILE,),
        in_specs=[pl.BlockSpec((TOP_K, COMBINE_TILE, d // 2), lambda i: (0, i, 0)),
                  pl.BlockSpec((COMBINE_TILE, TOP_K), lambda i: (i, 0)),
                  tok, _const_spec(ln_g.shape), _const_spec(ln_b.shape)],
        out_specs=tok,
        out_shape=jax.ShapeDtypeStruct((t, d), F32),
        compiler_params=_params(("arbitrary",)),
        name="combine",
    )(y_tok, wts_tok, res, ln_g, ln_b)


def _prepare(l, w_in, conv_w, a_log, dt_bias, gdn_norm_g, w_a, sgu_ln_g, sgu_ln_b, sgu_w, sgu_b, w_b,
             w_mem_kv, w_c, w_o, ln1_g, ln1_b, router_w, router_bias, ws_gate, ws_up, ws_down, ln2_g, ln2_b):
    wi = w_in[l]
    d = D_MODEL
    bf = lambda a: a.astype(BF16)
    row = lambda a: a.reshape(1, -1).astype(F32)
    w_ab = jnp.zeros((d, LANES), F32).at[:, :2 * GDN_HEADS].set(wi[:, _C_AB:_C_UV])
    prm = jnp.zeros((SUBLANES, LANES), F32).at[0, :GDN_HEADS].set(a_log[l]).at[1, :GDN_HEADS].set(dt_bias[l])
    sgu_bias = jnp.repeat(sgu_b[l].T, SGU_WIDTH // SGU_GROUPS, axis=1)
    return {
        "w_qkv": bf(wi[:, _C_QKV:_C_Z]), "w_z": bf(wi[:, _C_Z:_C_AB]), "w_ab": bf(w_ab),
        "w_uv": bf(wi[:, _C_UV:_C_XQ]), "w_xq": bf(wi[:, _C_XQ:_C_GATE]),
        "w_g0": bf(wi[:, _C_GATE:_C_GATE + d]), "w_g1": bf(wi[:, _C_GATE + d:_C_GATE + 2 * d]),
        "w_g2": bf(wi[:, _C_GATE + 2 * d:_C_GATE + 3 * d]),
        "conv_w": conv_w[l].astype(F32), "gdn_prm": prm, "gdn_norm_g": row(gdn_norm_g[l]),
        "w_a": bf(w_a[l]), "sgu_ln_g": row(sgu_ln_g[l]), "sgu_ln_b": row(sgu_ln_b[l]),
        "sgu_w": sgu_w[l].astype(F32), "sgu_bias": sgu_bias.astype(F32), "w_b": bf(w_b[l]),
        "w_mem_kv": bf(w_mem_kv[l]), "w_c": bf(w_c[l]), "w_o": bf(w_o[l]),
        "ln1_g": row(ln1_g[l]), "ln1_b": row(ln1_b[l]),
        "router_wt": bf(router_w[l].T), "router_bias": router_bias[l].reshape(-1, 1).astype(F32),
        "ws_gu": bf(jnp.concatenate([ws_gate[l], ws_up[l]], axis=1)), "ws_down": bf(ws_down[l]),
        "ln2_g": row(ln2_g[l]), "ln2_b": row(ln2_b[l]),
    }


def _layer(x, mem, wts, w_gate, w_up, w_down):
    bsz, s, d = x.shape
    t = bsz * s
    k, v = _mem_kv(mem, wts["w_mem_kv"])
    mbc = _mix_bc(x, wts, k, v)
    og = _gdn(x, wts)
    res, hp, idx, rw, rank, cnt = _merge(x, og, mbc, wts)
    nb = t * TOP_K // EXPERT_BLOCK + N_EXPERTS
    pos, win_slots, meta = _plan(idx, rank, cnt)
    xs = _sc_scatter_rows(hp, win_slots, nb * EXPERT_BLOCK)
    y = _experts(xs, meta[:, 0], meta[:, 1], meta[:, 2], meta[:1, 3], w_gate, w_up, w_down)
    y_tok = _sc_gather_rows(y, pos.reshape(-1)).reshape(TOP_K, t, d // 2)
    out = _combine(y_tok, rw.T, res.reshape(t, d), wts["ln2_g"], wts["ln2_b"])
    return out.reshape(bsz, s, d)


def kernel(x, mem, w_in, conv_w, a_log, dt_bias, gdn_norm_g, w_a, sgu_ln_g, sgu_ln_b, sgu_w, sgu_b, w_b,
           w_mem_kv, w_c, w_o, ln1_g, ln1_b, router_w, router_bias, w_gate, w_up, w_down, ws_gate, ws_up,
           ws_down, ln2_g, ln2_b):
    assert x.shape[1] % TILE == 0 and x.shape[2] == D_MODEL
    for l in range(DEPTH):
        wts = _prepare(l, w_in, conv_w, a_log, dt_bias, gdn_norm_g, w_a, sgu_ln_g, sgu_ln_b, sgu_w, sgu_b,
                       w_b, w_mem_kv, w_c, w_o, ln1_g, ln1_b, router_w, router_bias, ws_gate, ws_up,
                       ws_down, ln2_g, ln2_b)
        x = _layer(x, mem, wts, w_gate[l], w_up[l], w_down[l])
    return x
```

```python
import functools

import jax
import jax.numpy as jnp
from jax import lax
from jax.experimental import pallas as pl
from jax.experimental.pallas import tpu as pltpu
from jax.experimental.pallas import tpu_sc as plsc

F32 = jnp.float32
BF16 = jnp.bfloat16
I32 = jnp.int32
U32 = jnp.uint32

D_MODEL = 1024
DEPTH = 1
GDN_HEADS = 4
HEAD_DIM = 128
GDN_WIDTH = GDN_HEADS * HEAD_DIM
CONV_WIDTH = 4
SGU_GROUPS = 4
SGU_WIDTH = 512
SGU_CHUNK = 128
SGU_GROUP_DIM = SGU_WIDTH // SGU_GROUPS
N_MEM = 256
XATTN_HEADS = 4
XATTN_WIDTH = 512
XATTN_HEAD_DIM = XATTN_WIDTH // XATTN_HEADS
N_EXPERTS = 256
TOP_K = 8
N_GROUPS = 8
GROUP_SIZE = N_EXPERTS // N_GROUPS
TOPK_GROUPS = 4
D_EXPERT = 256
D_SHARED = 256
ROUTED_SCALE = 2.5
LN_EPS = 1e-5
RMS_EPS = 1e-6
DEEPNORM_ALPHA = (2.0 * DEPTH) ** 0.25

_C_QKV = 0
_C_Z = 3 * GDN_WIDTH
_C_AB = 4 * GDN_WIDTH
_C_UV = _C_AB + 2 * GDN_HEADS
_C_XQ = _C_UV + 2 * SGU_WIDTH
_C_GATE = _C_XQ + XATTN_WIDTH

LANES = 128
SUBLANES = 8
CHUNK = 128
GDN_GROUP = 2
TILE = 512
HALO = SUBLANES
EXPERT_BLOCK = 288
X_SLOTS = 8
X_AHEAD = X_SLOTS - 2
Y_SLOTS = 4
W_SLOTS = 5
SC_CORES = 2
SC_SUBCORES = 16
SC_WINDOW = 64
COMBINE_TILE = 512
VMEM_LIMIT = 56 * 1024 * 1024


def _dot(a, b):
    return jnp.dot(a, b, preferred_element_type=F32)


def _dot_nt(a, b):
    return lax.dot_general(a, b, (((1,), (1,)), ((), ())), preferred_element_type=F32)


def _split(a):
    hi = a.astype(BF16)
    lo = (a - hi.astype(F32)).astype(BF16)
    return hi, lo


def _dot3_parts(ah, al, bh, bl):
    return _dot(jnp.concatenate([ah, ah, al], axis=1), jnp.concatenate([bh, bl, bh], axis=0))


def _dot3(a, b):
    return _dot3_parts(*_split(a), *_split(b))


def _softplus(x):
    return jnp.maximum(x, 0.0) + jnp.log1p(jnp.exp(-jnp.abs(x)))


def _layer_norm(x, g, b):
    mu = jnp.mean(x, axis=-1, keepdims=True)
    xc = x - mu
    var = jnp.mean(xc * xc, axis=-1, keepdims=True)
    return xc * lax.rsqrt(var + LN_EPS) * g + b


def _const_spec(shape):
    nd = len(shape)
    return pl.BlockSpec(shape, lambda *_: (0,) * nd)


def _params(semantics):
    return pltpu.CompilerParams(dimension_semantics=semantics, vmem_limit_bytes=VMEM_LIMIT)


def _mem_kv_kernel(mem_ref, w_ref, k_ref, v_ref):
    kv = _dot(mem_ref[0].astype(BF16), w_ref[...])
    k_ref[0] = kv[:, :XATTN_WIDTH].astype(BF16)
    v_ref[0] = kv[:, XATTN_WIDTH:].astype(BF16)


def _mem_kv(mem, w_kv):
    bsz, n_mem, d = mem.shape
    out = jax.ShapeDtypeStruct((bsz, n_mem, XATTN_WIDTH), BF16)
    return pl.pallas_call(
        _mem_kv_kernel,
        grid=(bsz,),
        in_specs=[pl.BlockSpec((1, n_mem, d), lambda b: (b, 0, 0)), _const_spec(w_kv.shape)],
        out_specs=[pl.BlockSpec((1, n_mem, XATTN_WIDTH), lambda b: (b, 0, 0))] * 2,
        out_shape=[out, out],
        compiler_params=_params(("arbitrary",)),
        name="mem_kv",
    )(mem, w_kv)


def _mix_bc_kernel(x_ref, wuv_ref, wg1_ref, wb_ref, lng_ref, lnb_ref, sw_ref, sb_ref,
                   wxq_ref, k_ref, v_ref, wg2_ref, wc_ref, o_ref):
    xb = x_ref[0].astype(BF16)
    uv = _dot(xb, wuv_ref[...])
    xq = _dot(xb, wxq_ref[...])
    gate_b = _dot(xb, wg1_ref[...])
    gate_c = _dot(xb, wg2_ref[...])
    scores = []
    for h in range(XATTN_HEADS):
        qh = xq[:, h * XATTN_HEAD_DIM:(h + 1) * XATTN_HEAD_DIM].astype(BF16)
        kh = k_ref[0, :, h * XATTN_HEAD_DIM:(h + 1) * XATTN_HEAD_DIM]
        scores.append(_dot_nt(qh, kh) * (XATTN_HEAD_DIM ** -0.5))
    uv = jax.nn.gelu(uv, approximate=True)
    u = uv[:, :SGU_WIDTH]
    v = _layer_norm(uv[:, SGU_WIDTH:], lng_ref[...], lnb_ref[...])
    ii = lax.broadcasted_iota(I32, (SGU_CHUNK, SGU_CHUNK), 0)
    jj = lax.broadcasted_iota(I32, (SGU_CHUNK, SGU_CHUNK), 1)
    rows = []
    for n in range(TILE // SGU_CHUNK):
        cols = []
        for g in range(SGU_GROUPS):
            wt = jnp.where(ii >= jj, sw_ref[g], 0.0).astype(BF16)
            vg = v[n * SGU_CHUNK:(n + 1) * SGU_CHUNK, g * SGU_GROUP_DIM:(g + 1) * SGU_GROUP_DIM].astype(BF16)
            cols.append(_dot(wt, vg))
        rows.append(jnp.concatenate(cols, axis=1) + sb_ref[...])
    sg = jnp.concatenate(rows, axis=0)
    br_b = _dot((u * sg).astype(BF16), wb_ref[...])
    acc = jax.nn.sigmoid(gate_b) * br_b
    heads = []
    for h, sc in enumerate(scores):
        vh = v_ref[0, :, h * XATTN_HEAD_DIM:(h + 1) * XATTN_HEAD_DIM]
        p = jnp.exp(sc - jnp.max(sc, axis=-1, keepdims=True))
        p = p / jnp.sum(p, axis=-1, keepdims=True)
        heads.append(_dot(p.astype(BF16), vh))
    att = jnp.concatenate(heads, axis=1).astype(BF16)
    br_c = _dot(att, wc_ref[...])
    o_ref[0] = acc + jax.nn.sigmoid(gate_c) * br_c


def _mix_bc(x, wts, k, v):
    bsz, s, d = x.shape
    tok = pl.BlockSpec((1, TILE, d), lambda b, i: (b, i, 0))
    kvs = pl.BlockSpec((1, N_MEM, XATTN_WIDTH), lambda b, i: (b, 0, 0))
    names = ("w_uv", "w_g1", "w_b", "sgu_ln_g", "sgu_ln_b", "sgu_w", "sgu_bias", "w_xq")
    consts = [wts[n] for n in names]
    tail = [wts["w_g2"], wts["w_c"]]
    return pl.pallas_call(
        _mix_bc_kernel,
        grid=(bsz, s // TILE),
        in_specs=[tok] + [_const_spec(c.shape) for c in consts] + [kvs, kvs]
        + [_const_spec(c.shape) for c in tail],
        out_specs=tok,
        out_shape=jax.ShapeDtypeStruct((bsz, s, d), F32),
        compiler_params=_params(("arbitrary", "arbitrary")),
        name="mix_bc",
    )(x, *consts, k, v, *tail)


def _unit_lower_inverses(lmats, ii, jj):
    eye = (ii == jj).astype(F32)
    x = ii ^ jj
    lsplit = [_split(l) for l in lmats]
    ts = [eye - jnp.where(x < 2, l, 0.0) for l in lmats]
    zero = jnp.zeros((CHUNK, CHUNK), BF16)
    s = 2
    while s < CHUNK:
        off = (x < 2 * s) & ((ii & s) != 0) & ((jj & s) == 0)
        tsplit = [_split(t) for t in ts]
        prods = []
        for (th, tl), (lh, ll) in zip(tsplit, lsplit):
            oh = jnp.where(off, lh, zero)
            ol = jnp.where(off, ll, zero)
            prods.append(_dot3_parts(th, tl, oh, ol))
        nxt = []
        for t, p, (th, tl) in zip(ts, prods, tsplit):
            ph, plo = _split(p)
            nxt.append(t - _dot3_parts(ph, plo, th, tl))
        ts = nxt
        s *= 2
    return ts


def _gdn_kernel(x_ref, wqkv_ref, wz_ref, wab_ref, convw_ref, prm_ref, gn_ref, o_ref, hist_ref, state_ref):
    @pl.when(pl.program_id(1) == 0)
    def _():
        hist_ref[0:HALO, :] = jnp.zeros((HALO, 3 * GDN_WIDTH), F32)
        state_ref[...] = jnp.zeros(state_ref.shape, F32)

    xb = x_ref[0].astype(BF16)
    ab = _dot(xb, wab_ref[...])
    parts = []
    for p in range(3):
        cols = slice(p * GDN_WIDTH, (p + 1) * GDN_WIDTH)
        pre = _dot(xb, wqkv_ref[:, cols])
        hist_ref[HALO:HALO + TILE, cols] = pre
        conv = pre * convw_ref[CONV_WIDTH - 1:CONV_WIDTH, cols]
        for j in range(CONV_WIDTH - 1):
            shift = CONV_WIDTH - 1 - j
            conv = conv + hist_ref[HALO - shift:HALO - shift + TILE, cols] * convw_ref[j:j + 1, cols]
        hist_ref[0:HALO, cols] = pre[TILE - HALO:, :]
        parts.append(conv * jax.nn.sigmoid(conv))
    qkv = jnp.concatenate(parts, axis=1)
    z = _dot(xb, wz_ref[...])
    g = -jnp.exp(prm_ref[0:1, :]) * _softplus(ab + prm_ref[1:2, :])
    beta = jax.nn.sigmoid(ab)

    ii = lax.broadcasted_iota(I32, (CHUNK, CHUNK), 0)
    jj = lax.broadcasted_iota(I32, (CHUNK, CHUNK), 1)
    tri = (ii >= jj).astype(BF16)
    def chunk_items(c):
        items = []
        r0 = c * CHUNK
        gch = g[r0:r0 + CHUNK, :]
        g1 = gch.astype(BF16)
        r1 = gch - g1.astype(F32)
        g2 = r1.astype(BF16)
        g3 = (r1 - g2.astype(F32)).astype(BF16)
        gc = _dot(jnp.concatenate([tri, tri, tri], axis=1),
                  jnp.concatenate([g1, g2, g3], axis=0))
        gct = gc.T
        for h in range(GDN_HEADS):
            qh = qkv[r0:r0 + CHUNK, h * HEAD_DIM:(h + 1) * HEAD_DIM]
            kh = qkv[r0:r0 + CHUNK, GDN_WIDTH + h * HEAD_DIM:GDN_WIDTH + (h + 1) * HEAD_DIM]
            vh = qkv[r0:r0 + CHUNK, 2 * GDN_WIDTH + h * HEAD_DIM:2 * GDN_WIDTH + (h + 1) * HEAD_DIM]
            qn = qh * lax.rsqrt(jnp.sum(qh * qh, axis=-1, keepdims=True) + RMS_EPS) * (HEAD_DIM ** -0.5)
            kn = kh * lax.rsqrt(jnp.sum(kh * kh, axis=-1, keepdims=True) + RMS_EPS)
            gcol = gc[:, h:h + 1]
            grow = gct[h:h + 1, :]
            bcol = beta[r0:r0 + CHUNK, GDN_HEADS + h:GDN_HEADS + h + 1]
            glast = gc[CHUNK - 1:CHUNK, h:h + 1]
            decay = jnp.exp(jnp.where(ii >= jj, gcol - grow, -jnp.inf))
            knb = kn.astype(BF16)
            kk = _dot_nt(knb, knb)
            qk = _dot_nt(qn.astype(BF16), knb)
            egc = jnp.exp(gcol)
            items.append(dict(
                c=c, h=h, lmat=jnp.where(ii > jj, kk * bcol * decay, 0.0),
                rhs=jnp.concatenate([vh * bcol, kn * (bcol * egc)], axis=1),
                qd=(qn * egc).astype(BF16), kdt=(kn * jnp.exp(glast - gcol)).T.astype(BF16),
                a=(qk * decay).astype(BF16), gl=jnp.exp(glast)))
        return items

    states = [state_ref[h] for h in range(GDN_HEADS)]
    for c0 in range(0, TILE // CHUNK, GDN_GROUP):
        items = [it for c in range(c0, c0 + GDN_GROUP) for it in chunk_items(c)]
        tinvs = _unit_lower_inverses([it["lmat"] for it in items], ii, jj)
        sols = [_dot3(tinv, it["rhs"]) for tinv, it in zip(tinvs, items)]
        for it, sol in zip(items, sols):
            h = it["h"]
            rows = slice(it["c"] * CHUNK, (it["c"] + 1) * CHUNK)
            lanes = slice(h * HEAD_DIM, (h + 1) * HEAD_DIM)
            st = states[h]
            wq = jnp.concatenate([sol[:, HEAD_DIM:].astype(BF16), it["qd"]], axis=0)
            ws = _dot(wq, st.astype(BF16))
            vb = (sol[:, :HEAD_DIM] - ws[:CHUNK]).astype(BF16)
            o = ws[CHUNK:] + _dot(it["a"], vb)
            states[h] = st * it["gl"] + _dot(it["kdt"], vb)
            zz = z[rows, lanes]
            o = o * lax.rsqrt(jnp.mean(o * o, axis=-1, keepdims=True) + RMS_EPS) * gn_ref[...]
            o_ref[0, rows, lanes] = (o * (zz * jax.nn.sigmoid(zz))).astype(BF16)
    for h in range(GDN_HEADS):
        state_ref[h] = states[h]


def _gdn(x, wts):
    bsz, s, d = x.shape
    tok = pl.BlockSpec((1, TILE, d), lambda b, i: (b, i, 0))
    hw = pl.BlockSpec((1, TILE, GDN_WIDTH), lambda b, i: (b, i, 0))
    consts = [wts[n] for n in ("w_qkv", "w_z", "w_ab", "conv_w", "gdn_prm", "gdn_norm_g")]
    return pl.pallas_call(
        _gdn_kernel,
        grid=(bsz, s // TILE),
        in_specs=[tok] + [_const_spec(c.shape) for c in consts],
        out_specs=hw,
        out_shape=jax.ShapeDtypeStruct((bsz, s, GDN_WIDTH), BF16),
        scratch_shapes=[pltpu.VMEM((HALO + TILE, 3 * GDN_WIDTH), F32),
                        pltpu.VMEM((GDN_HEADS, HEAD_DIM, HEAD_DIM), F32)],
        compiler_params=_params(("arbitrary", "arbitrary")),
        name="gdn",
    )(x, *consts)


def _merge_kernel(x_ref, og_ref, mbc_ref, wa_ref, wg0_ref, wo_ref, ln1g_ref, ln1b_ref,
                  rwt_ref, rb_ref, wsgu_ref, wsd_ref,
                  res_ref, hp_ref, idx_ref, wt_ref, rank_ref, cnt_ref, carry_ref):
    first = (pl.program_id(0) == 0) & (pl.program_id(1) == 0)

    @pl.when(first)
    def _():
        carry_ref[...] = jnp.zeros(carry_ref.shape, F32)

    x = x_ref[0]
    xb = x.astype(BF16)
    br_a = _dot(og_ref[0], wa_ref[...])
    merged = jax.nn.sigmoid(_dot(xb, wg0_ref[...])) * br_a + mbc_ref[0]
    y = _dot(merged.astype(BF16), wo_ref[...])
    h = _layer_norm(DEEPNORM_ALPHA * x + y, ln1g_ref[...], ln1b_ref[...])
    hb = h.astype(BF16)
    hp_ref[...] = _pack_bf16_pairs(h)

    gu = _dot(hb, wsgu_ref[...])
    gt = gu[:, :D_SHARED]
    act = gt * jax.nn.sigmoid(gt) * gu[:, D_SHARED:]
    res_ref[0] = DEEPNORM_ALPHA * h + _dot(act.astype(BF16), wsd_ref[...])

    scores = jax.nn.sigmoid(_dot_nt(rwt_ref[...], hb))
    biased = scores + rb_ref[...]
    g3 = biased.reshape(N_GROUPS, GROUP_SIZE, TILE)
    m1 = jnp.max(g3, axis=1)
    m1b = m1[:, None, :]
    n_top = jnp.sum((g3 == m1b).astype(F32), axis=1)
    m2 = jnp.max(jnp.where(g3 < m1b, g3, -jnp.inf), axis=1)
    gs = m1 + jnp.where(n_top >= 2.0, m1, m2)
    gidx = lax.broadcasted_iota(I32, (N_GROUPS, TILE), 0)
    beaten = jnp.zeros((N_GROUPS, TILE), F32)
    for g in range(N_GROUPS):
        row = gs[g:g + 1, :]
        beaten = beaten + ((row > gs) | ((row == gs) & (g < gidx))).astype(F32)
    sel = (beaten < float(TOPK_GROUPS)).astype(F32)
    sel_e = jnp.broadcast_to(sel[:, None, :], (N_GROUPS, GROUP_SIZE, TILE)).reshape(N_EXPERTS, TILE)
    masked = jnp.where(sel_e > 0.5, biased, -jnp.inf)
    eidx = lax.broadcasted_iota(I32, (N_EXPERTS, TILE), 0)
    chosen = jnp.zeros((N_EXPERTS, TILE), F32)
    picks, pick_scores = [], []
    for _k in range(TOP_K):
        mx = jnp.max(masked, axis=0, keepdims=True)
        pick = jnp.min(jnp.where(masked == mx, eidx, N_EXPERTS), axis=0, keepdims=True)
        hot = eidx == pick
        picks.append(pick)
        pick_scores.append(jnp.sum(jnp.where(hot, scores, 0.0), axis=0, keepdims=True))
        chosen = jnp.where(hot, 1.0, chosen)
        masked = jnp.where(hot, -jnp.inf, masked)
    total = pick_scores[0]
    for sck in pick_scores[1:]:
        total = total + sck
    idx_ref[...] = jnp.concatenate(picks, axis=0)
    wt_ref[...] = jnp.concatenate([sck / total * ROUTED_SCALE for sck in pick_scores], axis=0)

    ti = lax.broadcasted_iota(I32, (TILE, TILE), 0)
    tj = lax.broadcasted_iota(I32, (TILE, TILE), 1)
    before = _dot(chosen.astype(BF16), (ti < tj).astype(BF16))
    carry = carry_ref[...]
    before = before + jnp.concatenate([carry] * (TILE // LANES), axis=1)
    rank_ref[...] = jnp.concatenate(
        [jnp.sum(jnp.where(eidx == p, before, 0.0), axis=0, keepdims=True) for p in picks],
        axis=0).astype(I32)
    carry = carry + _dot(chosen.astype(BF16), jnp.ones((TILE, LANES), BF16))
    carry_ref[...] = carry
    cnt_ref[...] = carry.astype(I32)


def _merge(x, og, mbc, wts):
    bsz, s, d = x.shape
    t = bsz * s
    nt = s // TILE
    tok = pl.BlockSpec((1, TILE, d), lambda b, i: (b, i, 0))
    ogs = pl.BlockSpec((1, TILE, GDN_WIDTH), lambda b, i: (b, i, 0))
    flat = lambda b, i: (b * nt + i, 0)
    lane = lambda b, i: (0, b * nt + i)
    consts = [wts[n] for n in ("w_a", "w_g0", "w_o", "ln1_g", "ln1_b", "router_wt", "router_bias",
                               "ws_gu", "ws_down")]
    return pl.pallas_call(
        _merge_kernel,
        grid=(bsz, nt),
        in_specs=[tok, ogs, tok] + [_const_spec(c.shape) for c in consts],
        out_specs=[tok,
                   pl.BlockSpec((TILE, d // 2), flat),
                   pl.BlockSpec((TOP_K, TILE), lane),
                   pl.BlockSpec((TOP_K, TILE), lane),
                   pl.BlockSpec((TOP_K, TILE), lane),
                   _const_spec((N_EXPERTS, LANES))],
        out_shape=[jax.ShapeDtypeStruct((bsz, s, d), F32),
                   jax.ShapeDtypeStruct((t, d // 2), U32),
                   jax.ShapeDtypeStruct((TOP_K, t), I32),
                   jax.ShapeDtypeStruct((TOP_K, t), F32),
                   jax.ShapeDtypeStruct((TOP_K, t), I32),
                   jax.ShapeDtypeStruct((N_EXPERTS, LANES), I32)],
        scratch_shapes=[pltpu.VMEM((N_EXPERTS, LANES), F32)],
        compiler_params=_params(("arbitrary", "arbitrary")),
        name="merge",
    )(x, og, mbc, *consts)


def _plan_kernel(idx_ref, rank_ref, cnt_ref, pos_ref, win_ref, meta_ref):
    e = N_EXPERTS
    cnt = cnt_ref[...]
    padded = cnt + (EXPERT_BLOCK - 1)
    nblk = jnp.floor(padded.astype(F32) * (1.0 / EXPERT_BLOCK)).astype(I32)
    nblk = jnp.where(nblk * EXPERT_BLOCK > padded, nblk - 1, nblk)
    nblk = jnp.where((nblk + 1) * EXPERT_BLOCK <= padded, nblk + 1, nblk)
    ei = lax.broadcasted_iota(I32, (e, e), 0)
    ej = lax.broadcasted_iota(I32, (e, e), 1)
    incl = _dot((ei >= ej).astype(BF16), nblk.astype(F32).astype(BF16)).astype(I32)
    excl = incl - nblk
    start = (excl * EXPERT_BLOCK).astype(F32)
    lane = lax.broadcasted_iota(I32, (e, LANES), 1)
    n_used = jnp.broadcast_to(incl[e - 1:e, :], (e, LANES))
    meta_ref[...] = jnp.where(lane == 0, excl, jnp.where(lane == 1, nblk, jnp.where(lane == 2, cnt, n_used)))

    e128 = lax.broadcasted_iota(I32, (e, LANES), 0)

    def body(i, carry):
        off = pl.multiple_of(i * LANES, LANES)
        ids = idx_ref[:, pl.ds(off, LANES)]
        rows = [jnp.sum(jnp.where(e128 == ids[k:k + 1, :], start, 0.0), axis=0, keepdims=True)
                for k in range(TOP_K)]
        p = jnp.concatenate(rows, axis=0).astype(I32) + rank_ref[:, pl.ds(off, LANES)]
        pos_ref[:, pl.ds(off, LANES)] = p
        for half in range(LANES // SC_WINDOW):
            row0 = pl.multiple_of((i * (LANES // SC_WINDOW) + half) * TOP_K, TOP_K)
            win_ref[pl.ds(row0, TOP_K), :] = p[:, half * SC_WINDOW:(half + 1) * SC_WINDOW]
        return carry

    lax.fori_loop(0, idx_ref.shape[1] // LANES, body, 0)


def _plan(idx, rank, cnt):
    t = idx.shape[1]
    return pl.pallas_call(
        _plan_kernel,
        grid=(1,),
        in_specs=[_const_spec(idx.shape), _const_spec(rank.shape), _const_spec(cnt.shape)],
        out_specs=[_const_spec((TOP_K, t)), _const_spec((t // SC_WINDOW * TOP_K, SC_WINDOW)),
                   _const_spec((N_EXPERTS, LANES))],
        out_shape=[jax.ShapeDtypeStruct((TOP_K, t), I32),
                   jax.ShapeDtypeStruct((t // SC_WINDOW * TOP_K, SC_WINDOW), I32),
                   jax.ShapeDtypeStruct((N_EXPERTS, LANES), I32)],
        compiler_params=_params(("arbitrary",)),
        name="plan",
    )(idx, rank, cnt)


def _sc_scatter_rows(rows, win_slots, n_slots):
    t, d = rows.shape
    workers = SC_CORES * SC_SUBCORES
    per_w = t // workers
    nwin = per_w // SC_WINDOW
    assert per_w * workers == t and nwin * SC_WINDOW == per_w and nwin % 2 == 0
    idx = win_slots.reshape(workers, nwin * TOP_K, SC_WINDOW)
    mesh = plsc.VectorSubcoreMesh(core_axis_name="core", subcore_axis_name="subcore",
                                  num_cores=SC_CORES, num_subcores=SC_SUBCORES)

    @functools.partial(
        pl.kernel, out_type=jax.ShapeDtypeStruct((n_slots, d), rows.dtype), mesh=mesh, name="sc_scatter_rows",
        scratch_types=[pltpu.VMEM((nwin * TOP_K, SC_WINDOW), I32), pltpu.VMEM((2, SC_WINDOW, d), rows.dtype),
                       pltpu.SemaphoreType.DMA((2,)), pltpu.SemaphoreType.DMA((2,))])
    def scatter(rows_hbm, idx_hbm, out_hbm, idx_v, rows_v, lsem, ssem):
        wid = lax.axis_index("subcore") * SC_CORES + lax.axis_index("core")
        base = wid * per_w
        pltpu.sync_copy(idx_hbm.at[wid], idx_v)

        def load(w, slot):
            return pltpu.make_async_copy(rows_hbm.at[pl.ds(base + w * SC_WINDOW, SC_WINDOW)], rows_v.at[slot],
                                         lsem.at[slot])

        def send(w, k, slot):
            return pltpu.make_async_copy(rows_v.at[slot], out_hbm.at[idx_v.at[w * TOP_K + k]], ssem.at[slot])

        load(0, 0).start()

        @pl.loop(0, nwin, step=2)
        def _(w0):
            for b in range(2):
                w = w0 + b
                load(w, b).wait()

                @pl.when(w + 1 < nwin)
                def _():
                    @pl.when(w >= 1)
                    def _():
                        for k in range(TOP_K):
                            send(w - 1, k, 1 - b).wait()

                    load(w + 1, 1 - b).start()

                for k in range(TOP_K):
                    send(w, k, b).start()

        for k in range(TOP_K):
            send(nwin - 2, k, 0).wait()
        for k in range(TOP_K):
            send(nwin - 1, k, 1).wait()

    return scatter(rows, idx)


def _pack_bf16_pairs(a):
    n = a.shape[1] // 2
    bits = lax.bitcast_convert_type(a.astype(BF16).astype(F32), U32)
    return (bits[:, :n] >> 16) | (bits[:, n:] & jnp.uint32(0xFFFF0000))


def _unpack_bf16_pairs(words):
    lo = lax.bitcast_convert_type(words << 16, F32)
    hi = lax.bitcast_convert_type(words & jnp.uint32(0xFFFF0000), F32)
    return jnp.concatenate([lo, hi], axis=1)


def _experts_kernel(fb_ref, nb_ref, cnt_ref, nu_ref, xs_ref, wg_ref, wu_ref, wd_ref, y_ref,
                    wgbuf, wubuf, wdbuf, wsem, wgu_s, wd_s, xbuf, ybuf, xsem, ysem):
    e = pl.program_id(0)
    n_used = nu_ref[0]
    bm = EXPERT_BLOCK
    row_queue = 1

    def x_copy(g, slot):
        return pltpu.make_async_copy(xs_ref.at[pl.ds(g * bm, bm), :], xbuf.at[slot], xsem.at[slot])

    def y_copy(g, slot):
        return pltpu.make_async_copy(ybuf.at[slot], y_ref.at[pl.ds(g * bm, bm), :], ysem.at[slot])

    @pl.when(e == 0)
    def _():
        for g0 in range(X_AHEAD):
            @pl.when(g0 < n_used)
            def _():
                x_copy(g0, g0).start(priority=row_queue)

    def w_copies(j, slot):
        return [pltpu.make_async_copy(src.at[j], dst.at[slot], wsem.at[slot])
                for src, dst in ((wg_ref, wgbuf), (wu_ref, wubuf), (wd_ref, wdbuf))]

    @pl.when(e == 0)
    def _():
        for j in range(W_SLOTS - 1):
            for cp in w_copies(j, j):
                cp.start()

    wslot = lax.rem(e, W_SLOTS)
    for cp in w_copies(e, wslot):
        cp.wait()
    ahead_e = e + (W_SLOTS - 1)

    @pl.when(ahead_e < pl.num_programs(0))
    def _():
        for cp in w_copies(ahead_e, lax.rem(ahead_e, W_SLOTS)):
            cp.start()

    nb = nb_ref[e]

    @pl.when(nb > 0)
    def _():
        wgu_s[:, :D_EXPERT] = wgbuf[wslot].astype(BF16)
        wgu_s[:, D_EXPERT:] = wubuf[wslot].astype(BF16)
        wd_s[...] = wdbuf[wslot].astype(BF16)

    def process(b0, width):
        g0 = fb_ref[e] + b0
        xs = []
        for i in range(width):
            g = g0 + i
            x_copy(g, g & (X_SLOTS - 1)).wait()
            xs.append(xbuf[g & (X_SLOTS - 1)])
        for i in range(width):
            ahead = g0 + X_AHEAD + i

            @pl.when(ahead < n_used)
            def _():
                x_copy(ahead, ahead & (X_SLOTS - 1)).start(priority=row_queue)

        ys = []
        for i in range(width):
            x = _unpack_bf16_pairs(xs[i])
            live = lax.broadcasted_iota(I32, (bm, 1), 0) < cnt_ref[e] - (b0 + i) * bm
            xb = jnp.where(live, x, 0.0).astype(BF16)
            gu = _dot(xb, wgu_s[...])
            gt = gu[:, :D_EXPERT]
            act = (gt * jax.nn.sigmoid(gt) * gu[:, D_EXPERT:]).astype(BF16)
            ys.append(_pack_bf16_pairs(_dot(act, wd_s[...])))
        for i in range(width):
            g = g0 + i
            slot = g & (Y_SLOTS - 1)

            @pl.when(g >= Y_SLOTS)
            def _():
                y_copy(g - Y_SLOTS, slot).wait()

            ybuf[slot] = ys[i]
            y_copy(g, slot).start(priority=row_queue)

    def pair(p, carry):
        process(2 * p, 2)
        return carry

    lax.fori_loop(0, nb >> 1, pair, 0)

    @pl.when((nb & 1) == 1)
    def _():
        process(nb - 1, 1)

    @pl.when(e == pl.num_programs(0) - 1)
    def _():
        for back in range(Y_SLOTS, 0, -1):
            @pl.when(n_used >= back)
            def _():
                y_copy(n_used - back, (n_used - back) & (Y_SLOTS - 1)).wait()


def _experts(xs, first_block, n_blocks, counts, n_used, w_gate, w_up, w_down):
    n_slots, half = xs.shape
    d = 2 * half
    grid_spec = pltpu.PrefetchScalarGridSpec(
        num_scalar_prefetch=4,
        grid=(N_EXPERTS,),
        in_specs=[pl.BlockSpec(memory_space=pl.ANY)] * 4,
        out_specs=pl.BlockSpec(memory_space=pl.ANY),
        scratch_shapes=[pltpu.VMEM((W_SLOTS, d, D_EXPERT), F32), pltpu.VMEM((W_SLOTS, d, D_EXPERT), F32),
                        pltpu.VMEM((W_SLOTS, D_EXPERT, d), F32), pltpu.SemaphoreType.DMA((W_SLOTS,)),
                        pltpu.VMEM((d, 2 * D_EXPERT), BF16), pltpu.VMEM((D_EXPERT, d), BF16),
                        pltpu.VMEM((X_SLOTS, EXPERT_BLOCK, half), U32),
                        pltpu.VMEM((Y_SLOTS, EXPERT_BLOCK, half), U32),
                        pltpu.SemaphoreType.DMA((X_SLOTS,)), pltpu.SemaphoreType.DMA((Y_SLOTS,))],
    )
    return pl.pallas_call(
        _experts_kernel,
        grid_spec=grid_spec,
        out_shape=jax.ShapeDtypeStruct((n_slots, half), U32),
        compiler_params=_params(("arbitrary",)),
        name="experts",
    )(first_block, n_blocks, counts, n_used, xs, w_gate, w_up, w_down)


def _sc_gather_rows(table, indices):
    n = indices.shape[0]
    d = table.shape[1]
    workers = SC_CORES * SC_SUBCORES
    per_w = n // workers
    nch = per_w // SC_WINDOW
    assert per_w * workers == n and nch * SC_WINDOW == per_w and nch % 2 == 0
    mesh = plsc.VectorSubcoreMesh(core_axis_name="core", subcore_axis_name="subcore",
                                  num_cores=SC_CORES, num_subcores=SC_SUBCORES)

    @functools.partial(
        pl.kernel, out_type=jax.ShapeDtypeStruct((n, d), table.dtype), mesh=mesh, name="sc_gather_rows",
        scratch_types=[pltpu.VMEM((per_w,), I32), pltpu.VMEM((2, SC_WINDOW, d), table.dtype),
                       pltpu.SemaphoreType.DMA((2,)), pltpu.SemaphoreType.DMA((2,))])
    def gather(table_hbm, idx_hbm, out_hbm, idx_v, rows_v, gsem, wsem):
        base = (lax.axis_index("subcore") * SC_CORES + lax.axis_index("core")) * per_w
        pltpu.sync_copy(idx_hbm.at[pl.ds(base, per_w)], idx_v)

        def fetch(c, slot):
            return pltpu.make_async_copy(table_hbm.at[idx_v.at[pl.ds(c * SC_WINDOW, SC_WINDOW)]],
                                         rows_v.at[slot], gsem.at[slot])

        def put(c, slot):
            return pltpu.make_async_copy(rows_v.at[slot], out_hbm.at[pl.ds(base + c * SC_WINDOW, SC_WINDOW)],
                                         wsem.at[slot])

        fetch(0, 0).start()

        @pl.loop(0, nch, step=2)
        def _(c0):
            for b in range(2):
                c = c0 + b
                fetch(c, b).wait()

                @pl.when(c + 1 < nch)
                def _():
                    @pl.when(c >= 1)
                    def _():
                        put(c - 1, 1 - b).wait()

                    fetch(c + 1, 1 - b).start()

                put(c, b).start()

        put(nch - 2, 0).wait()
        put(nch - 1, 1).wait()

    return gather(table, indices)


def _combine_kernel(yt_ref, wt_ref, res_ref, g_ref, b_ref, o_ref):
    acc = res_ref[...]
    for k in range(TOP_K):
        acc = acc + _unpack_bf16_pairs(yt_ref[k]) * wt_ref[:, k:k + 1]
    o_ref[...] = _layer_norm(acc, g_ref[...], b_ref[...])


def _combine(y_tok, wts_tok, res, ln_g, ln_b):
    t, d = res.shape
    tok = pl.BlockSpec((COMBINE_TILE, d), lambda i: (i, 0))
    return pl.pallas_call(
        _combine_kernel,
        grid=(t // COMBINE_TILE,),
        in_specs=[pl.BlockSpec((TOP_K, COMBINE_TILE, d // 2), lambda i: (0, i, 0)),
                  pl.BlockSpec((COMBINE_TILE, TOP_K), lambda i: (i, 0)),
                  tok, _const_spec(ln_g.shape), _const_spec(ln_b.shape)],
        out_specs=tok,
        out_shape=jax.ShapeDtypeStruct((t, d), F32),
        compiler_params=_params(("arbitrary",)),
        name="combine",
    )(y_tok, wts_tok, res, ln_g, ln_b)


def _prepare(l, w_in, conv_w, a_log, dt_bias, gdn_norm_g, w_a, sgu_ln_g, sgu_ln_b, sgu_w, sgu_b, w_b,
             w_mem_kv, w_c, w_o, ln1_g, ln1_b, router_w, router_bias, ws_gate, ws_up, ws_down, ln2_g, ln2_b):
    wi = w_in[l]
    d = D_MODEL
    bf = lambda a: a.astype(BF16)
    row = lambda a: a.reshape(1, -1).astype(F32)
    w_ab = jnp.zeros((d, LANES), F32).at[:, :2 * GDN_HEADS].set(wi[:, _C_AB:_C_UV])
    prm = jnp.zeros((SUBLANES, LANES), F32).at[0, :GDN_HEADS].set(a_log[l]).at[1, :GDN_HEADS].set(dt_bias[l])
    sgu_bias = jnp.repeat(sgu_b[l].T, SGU_WIDTH // SGU_GROUPS, axis=1)
    return {
        "w_qkv": bf(wi[:, _C_QKV:_C_Z]), "w_z": bf(wi[:, _C_Z:_C_AB]), "w_ab": bf(w_ab),
        "w_uv": bf(wi[:, _C_UV:_C_XQ]), "w_xq": bf(wi[:, _C_XQ:_C_GATE]),
        "w_g0": bf(wi[:, _C_GATE:_C_GATE + d]), "w_g1": bf(wi[:, _C_GATE + d:_C_GATE + 2 * d]),
        "w_g2": bf(wi[:, _C_GATE + 2 * d:_C_GATE + 3 * d]),
        "conv_w": conv_w[l].astype(F32), "gdn_prm": prm, "gdn_norm_g": row(gdn_norm_g[l]),
        "w_a": bf(w_a[l]), "sgu_ln_g": row(sgu_ln_g[l]), "sgu_ln_b": row(sgu_ln_b[l]),
        "sgu_w": sgu_w[l].astype(F32), "sgu_bias": sgu_bias.astype(F32), "w_b": bf(w_b[l]),
        "w_mem_kv": bf(w_mem_kv[l]), "w_c": bf(w_c[l]), "w_o": bf(w_o[l]),
        "ln1_g": row(ln1_g[l]), "ln1_b": row(ln1_b[l]),
        "router_wt": bf(router_w[l].T), "router_bias": router_bias[l].reshape(-1, 1).astype(F32),
        "ws_gu": bf(jnp.concatenate([ws_gate[l], ws_up[l]], axis=1)), "ws_down": bf(ws_down[l]),
        "ln2_g": row(ln2_g[l]), "ln2_b": row(ln2_b[l]),
    }


def _layer(x, mem, wts, w_gate, w_up, w_down):
    bsz, s, d = x.shape
    t = bsz * s
    k, v = _mem_kv(mem, wts["w_mem_kv"])
    mbc = _mix_bc(x, wts, k, v)
    og = _gdn(x, wts)
    res, hp, idx, rw, rank, cnt = _merge(x, og, mbc, wts)
    nb = t * TOP_K // EXPERT_BLOCK + N_EXPERTS
    pos, win_slots, meta = _plan(idx, rank, cnt)
    xs = _sc_scatter_rows(hp, win_slots, nb * EXPERT_BLOCK)
    y = _experts(xs, meta[:, 0], meta[:, 1], meta[:, 2], meta[:1, 3], w_gate, w_up, w_down)
    y_tok = _sc_gather_rows(y, pos.reshape(-1)).reshape(TOP_K, t, d // 2)
    out = _combine(y_tok, rw.T, res.reshape(t, d), wts["ln2_g"], wts["ln2_b"])
    return out.reshape(bsz, s, d)


def kernel(x, mem, w_in, conv_w, a_log, dt_bias, gdn_norm_g, w_a, sgu_ln_g, sgu_ln_b, sgu_w, sgu_b, w_b,
           w_mem_kv, w_c, w_o, ln1_g, ln1_b, router_w, router_bias, w_gate, w_up, w_down, ws_gate, ws_up,
           ws_down, ln2_g, ln2_b):
    assert x.shape[1] % TILE == 0 and x.shape[2] == D_MODEL
    for l in range(DEPTH):
        wts = _prepare(l, w_in, conv_w, a_log, dt_bias, gdn_norm_g, w_a, sgu_ln_g, sgu_ln_b, sgu_w, sgu_b,
                       w_b, w_mem_kv, w_c, w_o, ln1_g, ln1_b, router_w, router_bias, ws_gate, ws_up,
                       ws_down, ln2_g, ln2_b)
        x = _layer(x, mem, wts, w_gate[l], w_up[l], w_down[l])
    return x
```
